```python
import math
import jax, jax.numpy as jnp
from jax import lax
import numpy as np

D_MODEL = 2048
BATCH = 8
SEQ = 8192
DEPTH = 1

N_HEADS = 32
HEAD_DIM = 64
N_KV_HEADS = 4
GROUP = N_HEADS // N_KV_HEADS
D_ATTN = N_HEADS * HEAD_DIM
D_KV = N_KV_HEADS * HEAD_DIM
WINDOW = 128
BLOCK = 128
NUM_BUCKETS = 32
MAX_DISTANCE = 128
D_RNN = 2560
N_RNN_BLOCKS = 20
RNN_BLOCK = D_RNN // N_RNN_BLOCKS
RNN_CONV = 4
RG_C = 8.0
D_FF = 3 * D_MODEL
FFN_CONV = 3
IN_SPLITS = (D_ATTN, D_KV, D_KV, D_RNN, D_RNN, D_MODEL, D_MODEL)
D_IN = D_ATTN + 2 * D_KV + 2 * D_RNN + 2 * D_MODEL
EPS = 1e-6
NEG_INF = -1e30

kernel_name = "hybrid_rglru_swa_sink_convffn_adaln"


def rmsnorm(x, g):
    xf = x.astype(jnp.float32)
    y = xf * lax.rsqrt(jnp.mean(xf * xf, axis=-1, keepdims=True) + EPS)
    return (y * g.astype(jnp.float32)).astype(x.dtype)


def causal_dwconv(x, w, b):
    k = w.shape[0]
    y = lax.conv_general_dilated(
        x, w[:, None, :].astype(x.dtype), window_strides=(1,), padding=[(k - 1, 0)],
        dimension_numbers=("NWC", "WIO", "NWC"), feature_group_count=x.shape[-1])
    return y + b


def t5_bucket(dist):
    max_exact = NUM_BUCKETS // 2
    d = jnp.maximum(dist, 1).astype(jnp.float32)
    large = max_exact + (jnp.log(d / max_exact) / math.log(MAX_DISTANCE / max_exact)
                         * (NUM_BUCKETS - max_exact)).astype(jnp.int32)
    large = jnp.minimum(large, NUM_BUCKETS - 1)
    return jnp.where(dist < max_exact, dist, large)


def band_bias_and_mask(rel_bias, n_blocks):
    qi = jnp.arange(BLOCK)[:, None]
    kj = jnp.arange(2 * BLOCK)[None, :]
    dist = qi + BLOCK - kj
    bucket = t5_bucket(jnp.maximum(dist, 0))
    bias = jnp.transpose(rel_bias[bucket], (2, 0, 1)).astype(jnp.float32)
    in_window = (dist >= 0) & (dist < WINDOW)
    key_exists = (jnp.arange(n_blocks)[:, None, None] > 0) | (kj >= BLOCK)[None]
    mask = in_window[None] & key_exists
    return bias.reshape(N_KV_HEADS, GROUP, BLOCK, 2 * BLOCK), mask


def swa_attention(q, k, v, sinks, bias, mask):
    b, s, _ = q.shape
    nb = s // BLOCK
    qb = q.reshape(b, nb, BLOCK, N_KV_HEADS, GROUP, HEAD_DIM)
    kb = k.reshape(b, nb, BLOCK, N_KV_HEADS, HEAD_DIM)
    vb = v.reshape(b, nb, BLOCK, N_KV_HEADS, HEAD_DIM)
    pad = ((0, 0), (1, 0), (0, 0), (0, 0), (0, 0))
    k_band = jnp.concatenate([jnp.pad(kb, pad)[:, :-1], kb], axis=2)
    v_band = jnp.concatenate([jnp.pad(vb, pad)[:, :-1], vb], axis=2)
    scores = jnp.einsum("bnqhgd,bnkhd->bnhgqk", qb, k_band,
                        preferred_element_type=jnp.float32) * (HEAD_DIM ** -0.5)
    scores = jnp.where(mask[None, :, None, None], scores + bias[None, None], NEG_INF)
    sink = sinks.astype(jnp.float32).reshape(N_KV_HEADS, GROUP)[None, None, :, :, None, None]
    m = jnp.maximum(jnp.max(scores, axis=-1, keepdims=True), sink)
    p = jnp.exp(scores - m)
    probs = p / (jnp.sum(p, axis=-1, keepdims=True) + jnp.exp(sink - m))
    out = jnp.einsum("bnhgqk,bnkhd->bnqhgd", probs.astype(v.dtype), v_band)
    return out.reshape(b, s, D_ATTN)


def _lin_combine(e1, e2):
    a1, b1 = e1
    a2, b2 = e2
    return (a1 * a2, a2 * b1 + b2)


def rg_lru(x, w_a, b_a, w_i, b_i, lam):
    b, s, _ = x.shape
    xb = x.reshape(b, s, N_RNN_BLOCKS, RNN_BLOCK)
    r = jax.nn.sigmoid((jnp.einsum("bsni,nij->bsnj", xb, w_a).reshape(b, s, D_RNN) + b_a).astype(jnp.float32))
    i = jax.nn.sigmoid((jnp.einsum("bsni,nij->bsnj", xb, w_i).reshape(b, s, D_RNN) + b_i).astype(jnp.float32))
    log_a = -RG_C * r * jax.nn.softplus(-lam.astype(jnp.float32))
    a = jnp.exp(log_a)
    mult = jnp.sqrt(-jnp.expm1(2.0 * log_a))
    mult = jnp.where(jnp.arange(s)[None, :, None] == 0, 1.0, mult)
    inp = mult * i * x.astype(jnp.float32)
    _, h = lax.associative_scan(_lin_combine, (a, inp), axis=1)
    return h.astype(x.dtype)


def _fwd_setup_inputs(seed: int = 0) -> dict:
    key = jax.random.key(seed)
    ks = jax.random.split(key, 26)
    L = DEPTH

    def nrm(k, shape, scale):
        return jax.random.normal(k, shape, jnp.float32) * scale

    a_init = jax.random.uniform(ks[14], (L, D_RNN), jnp.float32, 0.9, 0.999) ** (1.0 / RG_C)
    return {
        "x": nrm(ks[0], (BATCH, SEQ, D_MODEL), 1.0),
        "c": nrm(ks[1], (BATCH, D_MODEL), 1.0),
        "w_ada": nrm(ks[2], (L, D_MODEL, 6 * D_MODEL), D_MODEL ** -0.5),
        "b_ada": nrm(ks[3], (L, 6 * D_MODEL), 0.01),
        "norm1": 1.0 + nrm(ks[4], (L, D_MODEL), 0.02),
        "w_in": nrm(ks[5], (L, D_MODEL, D_IN), D_MODEL ** -0.5),
        "rnn_conv_w": nrm(ks[6], (L, RNN_CONV, D_RNN), RNN_CONV ** -0.5),
        "rnn_conv_b": nrm(ks[7], (L, D_RNN), 0.01),
        "w_rg_a": nrm(ks[8], (L, N_RNN_BLOCKS, RNN_BLOCK, RNN_BLOCK), RNN_BLOCK ** -0.5),
        "b_rg_a": nrm(ks[9], (L, D_RNN), 0.1),
        "w_rg_i": nrm(ks[10], (L, N_RNN_BLOCKS, RNN_BLOCK, RNN_BLOCK), RNN_BLOCK ** -0.5),
        "b_rg_i": nrm(ks[11], (L, D_RNN), 0.1),
        "rg_lambda": jnp.log(a_init) - jnp.log1p(-a_init),
        "w_o_rnn": nrm(ks[12], (L, D_RNN, D_MODEL), D_RNN ** -0.5),
        "w_o_attn": nrm(ks[13], (L, D_ATTN, D_MODEL), D_ATTN ** -0.5),
        "attn_sinks": nrm(ks[15], (L, N_HEADS), 0.5),
        "rel_bias": nrm(ks[16], (NUM_BUCKETS, N_HEADS), 0.2),
        "w_out": nrm(ks[17], (L, D_MODEL, D_MODEL), D_MODEL ** -0.5),
        "norm2": 1.0 + nrm(ks[18], (L, D_MODEL), 0.02),
        "w_up": nrm(ks[19], (L, D_MODEL, 2 * D_FF), D_MODEL ** -0.5),
        "ffn_conv_w": nrm(ks[20], (L, FFN_CONV, 2 * D_FF), FFN_CONV ** -0.5),
        "ffn_conv_b": nrm(ks[21], (L, 2 * D_FF), 0.01),
        "w_down": nrm(ks[22], (L, D_FF, D_MODEL), D_FF ** -0.5),
        "norm_f": 1.0 + nrm(ks[23], (D_MODEL,), 0.02),
    }


def _fwd_reference(x, c, w_ada, b_ada, norm1, w_in, rnn_conv_w, rnn_conv_b, w_rg_a, b_rg_a,
              w_rg_i, b_rg_i, rg_lambda, w_o_rnn, w_o_attn, attn_sinks, rel_bias, w_out,
              norm2, w_up, ffn_conv_w, ffn_conv_b, w_down, norm_f):
    n_blocks = x.shape[1] // BLOCK
    bias, mask = band_bias_and_mask(rel_bias, n_blocks)
    split_idx = np.cumsum(IN_SPLITS)[:-1].tolist()
    h = x
    for l in range(DEPTH):
        mod = jax.nn.silu(c) @ w_ada[l] + b_ada[l]
        shift1, scale1, gate1, shift2, scale2, gate2 = [m[:, None, :] for m in jnp.split(mod, 6, axis=-1)]

        u = rmsnorm(h, norm1[l]) * (1.0 + scale1) + shift1
        proj = u @ w_in[l]
        q, k, v, xr, gr, ga_logit, gr_logit = jnp.split(proj, split_idx, axis=-1)
        y_attn = swa_attention(q, k, v, attn_sinks[l], bias, mask) @ w_o_attn[l]
        xr = causal_dwconv(xr, rnn_conv_w[l], rnn_conv_b[l])
        hr = rg_lru(xr, w_rg_a[l], b_rg_a[l], w_rg_i[l], b_rg_i[l], rg_lambda[l])
        y_rnn = (hr * jax.nn.gelu(gr)) @ w_o_rnn[l]
        merged = jax.nn.sigmoid(ga_logit) * y_attn + jax.nn.sigmoid(gr_logit) * y_rnn
        h = h + gate1 * (merged @ w_out[l])

        u2 = rmsnorm(h, norm2[l]) * (1.0 + scale2) + shift2
        up = causal_dwconv(u2 @ w_up[l], ffn_conv_w[l], ffn_conv_b[l])
        g, val = jnp.split(up, 2, axis=-1)
        h = h + gate2 * ((jax.nn.gelu(g) * val) @ w_down[l])
    return rmsnorm(h, norm_f)


import jax as _jax
import jax.numpy as _jnp

TWIN_FORMAT = 'train_step'
FWD_PARAMS = ['x', 'c', 'w_ada', 'b_ada', 'norm1', 'w_in', 'rnn_conv_w', 'rnn_conv_b', 'w_rg_a', 'b_rg_a', 'w_rg_i', 'b_rg_i', 'rg_lambda', 'w_o_rnn', 'w_o_attn', 'attn_sinks', 'rel_bias', 'w_out', 'norm2', 'w_up', 'ffn_conv_w', 'ffn_conv_b', 'w_down', 'norm_f']
TWIN_WEIGHTS = ['w_ada', 'b_ada', 'norm1', 'w_in', 'rnn_conv_w', 'rnn_conv_b', 'w_rg_a', 'b_rg_a', 'w_rg_i', 'b_rg_i', 'rg_lambda', 'w_o_rnn', 'w_o_attn', 'attn_sinks', 'rel_bias', 'w_out', 'norm2', 'w_up', 'ffn_conv_w', 'ffn_conv_b', 'w_down', 'norm_f']
TWIN_DIFF_INPUT = 'x'
TWIN_INPUTS = ['x', 'c', 'w_ada', 'b_ada', 'norm1', 'w_in', 'rnn_conv_w', 'rnn_conv_b', 'w_rg_a', 'b_rg_a', 'w_rg_i', 'b_rg_i', 'rg_lambda', 'w_o_rnn', 'w_o_attn', 'attn_sinks', 'rel_bias', 'w_out', 'norm2', 'w_up', 'ffn_conv_w', 'ffn_conv_b', 'w_down', 'norm_f', 'loss_target', 'm_w_ada', 'm_b_ada', 'm_norm1', 'm_w_in', 'm_rnn_conv_w', 'm_rnn_conv_b', 'm_w_rg_a', 'm_b_rg_a', 'm_w_rg_i', 'm_b_rg_i', 'm_rg_lambda', 'm_w_o_rnn', 'm_w_o_attn', 'm_attn_sinks', 'm_rel_bias', 'm_w_out', 'm_norm2', 'm_w_up', 'm_ffn_conv_w', 'm_ffn_conv_b', 'm_w_down', 'm_norm_f', 'v_w_ada', 'v_b_ada', 'v_norm1', 'v_w_in', 'v_rnn_conv_w', 'v_rnn_conv_b', 'v_w_rg_a', 'v_b_rg_a', 'v_w_rg_i', 'v_b_rg_i', 'v_rg_lambda', 'v_w_o_rnn', 'v_w_o_attn', 'v_attn_sinks', 'v_rel_bias', 'v_w_out', 'v_norm2', 'v_w_up', 'v_ffn_conv_w', 'v_ffn_conv_b', 'v_w_down', 'v_norm_f']
TWIN_OUTPUTS = ['loss', 'grad_x', 'grad_w_ada', 'grad_b_ada', 'grad_norm1', 'grad_w_in', 'grad_rnn_conv_w', 'grad_rnn_conv_b', 'grad_w_rg_a', 'grad_b_rg_a', 'grad_w_rg_i', 'grad_b_rg_i', 'grad_rg_lambda', 'grad_w_o_rnn', 'grad_w_o_attn', 'grad_attn_sinks', 'grad_rel_bias', 'grad_w_out', 'grad_norm2', 'grad_w_up', 'grad_ffn_conv_w', 'grad_ffn_conv_b', 'grad_w_down', 'grad_norm_f', 'delta_w_ada', 'delta_b_ada', 'delta_norm1', 'delta_w_in', 'delta_rnn_conv_w', 'delta_rnn_conv_b', 'delta_w_rg_a', 'delta_b_rg_a', 'delta_w_rg_i', 'delta_b_rg_i', 'delta_rg_lambda', 'delta_w_o_rnn', 'delta_w_o_attn', 'delta_attn_sinks', 'delta_rel_bias', 'delta_w_out', 'delta_norm2', 'delta_w_up', 'delta_ffn_conv_w', 'delta_ffn_conv_b', 'delta_w_down', 'delta_norm_f', 'new_m_w_ada', 'new_m_b_ada', 'new_m_norm1', 'new_m_w_in', 'new_m_rnn_conv_w', 'new_m_rnn_conv_b', 'new_m_w_rg_a', 'new_m_b_rg_a', 'new_m_w_rg_i', 'new_m_b_rg_i', 'new_m_rg_lambda', 'new_m_w_o_rnn', 'new_m_w_o_attn', 'new_m_attn_sinks', 'new_m_rel_bias', 'new_m_w_out', 'new_m_norm2', 'new_m_w_up', 'new_m_ffn_conv_w', 'new_m_ffn_conv_b', 'new_m_w_down', 'new_m_norm_f', 'new_v_w_ada', 'new_v_b_ada', 'new_v_norm1', 'new_v_w_in', 'new_v_rnn_conv_w', 'new_v_rnn_conv_b', 'new_v_w_rg_a', 'new_v_b_rg_a', 'new_v_w_rg_i', 'new_v_b_rg_i', 'new_v_rg_lambda', 'new_v_w_o_rnn', 'new_v_w_o_attn', 'new_v_attn_sinks', 'new_v_rel_bias', 'new_v_w_out', 'new_v_norm2', 'new_v_w_up', 'new_v_ffn_conv_w', 'new_v_ffn_conv_b', 'new_v_w_down', 'new_v_norm_f']
TWIN_LEAF_KINDS = {'loss': 'loss', 'grad_x': 'grad_x', 'grad_w_ada': 'grad_w', 'grad_b_ada': 'grad_w', 'grad_norm1': 'grad_w', 'grad_w_in': 'grad_w', 'grad_rnn_conv_w': 'grad_w', 'grad_rnn_conv_b': 'grad_w', 'grad_w_rg_a': 'grad_w', 'grad_b_rg_a': 'grad_w', 'grad_w_rg_i': 'grad_w', 'grad_b_rg_i': 'grad_w', 'grad_rg_lambda': 'grad_w', 'grad_w_o_rnn': 'grad_w', 'grad_w_o_attn': 'grad_w', 'grad_attn_sinks': 'grad_w', 'grad_rel_bias': 'grad_w', 'grad_w_out': 'grad_w', 'grad_norm2': 'grad_w', 'grad_w_up': 'grad_w', 'grad_ffn_conv_w': 'grad_w', 'grad_ffn_conv_b': 'grad_w', 'grad_w_down': 'grad_w', 'grad_norm_f': 'grad_w', 'delta_w_ada': 'delta_w', 'delta_b_ada': 'delta_w', 'delta_norm1': 'delta_w', 'delta_w_in': 'delta_w', 'delta_rnn_conv_w': 'delta_w', 'delta_rnn_conv_b': 'delta_w', 'delta_w_rg_a': 'delta_w', 'delta_b_rg_a': 'delta_w', 'delta_w_rg_i': 'delta_w', 'delta_b_rg_i': 'delta_w', 'delta_rg_lambda': 'delta_w', 'delta_w_o_rnn': 'delta_w', 'delta_w_o_attn': 'delta_w', 'delta_attn_sinks': 'delta_w', 'delta_rel_bias': 'delta_w', 'delta_w_out': 'delta_w', 'delta_norm2': 'delta_w', 'delta_w_up': 'delta_w', 'delta_ffn_conv_w': 'delta_w', 'delta_ffn_conv_b': 'delta_w', 'delta_w_down': 'delta_w', 'delta_norm_f': 'delta_w', 'new_m_w_ada': 'new_m', 'new_m_b_ada': 'new_m', 'new_m_norm1': 'new_m', 'new_m_w_in': 'new_m', 'new_m_rnn_conv_w': 'new_m', 'new_m_rnn_conv_b': 'new_m', 'new_m_w_rg_a': 'new_m', 'new_m_b_rg_a': 'new_m', 'new_m_w_rg_i': 'new_m', 'new_m_b_rg_i': 'new_m', 'new_m_rg_lambda': 'new_m', 'new_m_w_o_rnn': 'new_m', 'new_m_w_o_attn': 'new_m', 'new_m_attn_sinks': 'new_m', 'new_m_rel_bias': 'new_m', 'new_m_w_out': 'new_m', 'new_m_norm2': 'new_m', 'new_m_w_up': 'new_m', 'new_m_ffn_conv_w': 'new_m', 'new_m_ffn_conv_b': 'new_m', 'new_m_w_down': 'new_m', 'new_m_norm_f': 'new_m', 'new_v_w_ada': 'new_v', 'new_v_b_ada': 'new_v', 'new_v_norm1': 'new_v', 'new_v_w_in': 'new_v', 'new_v_rnn_conv_w': 'new_v', 'new_v_rnn_conv_b': 'new_v', 'new_v_w_rg_a': 'new_v', 'new_v_b_rg_a': 'new_v', 'new_v_w_rg_i': 'new_v', 'new_v_b_rg_i': 'new_v', 'new_v_rg_lambda': 'new_v', 'new_v_w_o_rnn': 'new_v', 'new_v_w_o_attn': 'new_v', 'new_v_attn_sinks': 'new_v', 'new_v_rel_bias': 'new_v', 'new_v_w_out': 'new_v', 'new_v_norm2': 'new_v', 'new_v_w_up': 'new_v', 'new_v_ffn_conv_w': 'new_v', 'new_v_ffn_conv_b': 'new_v', 'new_v_w_down': 'new_v', 'new_v_norm_f': 'new_v'}


def _forward(args):
    return _fwd_reference(*[args[k] for k in FWD_PARAMS])


def _output_shape():
    def fwd():
        inp = _fwd_setup_inputs(0)
        return _fwd_reference(*[inp[k] for k in FWD_PARAMS])
    out = _jax.eval_shape(fwd)
    return out.shape, out.dtype

N_MICROBATCH = 1
ADAM_LR = 0.001
ADAM_B1 = 0.9
ADAM_B2 = 0.999
ADAM_EPS = 1e-08
ADAM_WD = 0.01
ADAM_STEP = 10
PER_EXAMPLE_BATCH_AXIS = {'x': 0, 'c': 0, 'loss_target': 0}
SHARED_INPUTS = []
_WEIGHT_DTYPES = {'w_ada': _jnp.float32, 'b_ada': _jnp.float32, 'norm1': _jnp.float32, 'w_in': _jnp.float32, 'rnn_conv_w': _jnp.float32, 'rnn_conv_b': _jnp.float32, 'w_rg_a': _jnp.float32, 'b_rg_a': _jnp.float32, 'w_rg_i': _jnp.float32, 'b_rg_i': _jnp.float32, 'rg_lambda': _jnp.float32, 'w_o_rnn': _jnp.float32, 'w_o_attn': _jnp.float32, 'attn_sinks': _jnp.float32, 'rel_bias': _jnp.float32, 'w_out': _jnp.float32, 'norm2': _jnp.float32, 'w_up': _jnp.float32, 'ffn_conv_w': _jnp.float32, 'ffn_conv_b': _jnp.float32, 'w_down': _jnp.float32, 'norm_f': _jnp.float32}
MOMENT_SCALE = {'w_ada': 1.018425e-01, 'b_ada': 1.901179e-01, 'norm1': 1.224352e-01, 'w_in': 7.982300e-02, 'rnn_conv_w': 1.243123e-01, 'rnn_conv_b': 2.155315e-01, 'w_rg_a': 1.452286e-02, 'b_rg_a': 2.420167e-02, 'w_rg_i': 3.097339e-02, 'b_rg_i': 4.991997e-02, 'rg_lambda': 6.912672e-02, 'w_o_rnn': 1.332285e-01, 'w_o_attn': 2.520057e-02, 'attn_sinks': 8.536122e-03, 'rel_bias': 1.407207e-02, 'w_out': 1.325390e-01, 'norm2': 8.013552e-02, 'w_up': 3.664751e-02, 'ffn_conv_w': 3.687129e-02, 'ffn_conv_b': 2.734557e-02, 'w_down': 6.252174e-02, 'norm_f': 3.270946e+01}


def _to_microbatches(a, axis):
    t = _jnp.moveaxis(a, axis, 0)
    t = t.reshape((N_MICROBATCH, t.shape[0] // N_MICROBATCH) + t.shape[1:])
    return _jnp.moveaxis(t, 1, axis + 1)


def setup_inputs(seed: int = 0) -> dict:
    inp = _fwd_setup_inputs(seed)
    key = _jax.random.fold_in(_jax.random.key(seed), 7919)
    shape, _ = _output_shape()
    out = dict(inp)
    out["loss_target"] = _jax.random.normal(_jax.random.fold_in(key, 0), shape, _jnp.float32)
    for i, name in enumerate(TWIN_WEIGHTS):
        w = inp[name].astype(_jnp.float32)
        if MOMENT_SCALE is None:
            s = _jnp.sqrt(_jnp.mean(_jnp.square(w)) + 1e-30)
        else:
            s = MOMENT_SCALE[name]
        km, kv = _jax.random.split(_jax.random.fold_in(key, i + 1))
        out[name] = w
        out["m_" + name] = s * _jax.random.normal(km, w.shape, _jnp.float32)
        out["v_" + name] = (s * s) * _jax.random.uniform(kv, w.shape, _jnp.float32, 0.5, 1.5)
    if N_MICROBATCH > 1:
        for name, axis in PER_EXAMPLE_BATCH_AXIS.items():
            out[name] = _to_microbatches(out[name], axis)
    return {'x': out['x'], 'c': out['c'], 'w_ada': out['w_ada'], 'b_ada': out['b_ada'], 'norm1': out['norm1'], 'w_in': out['w_in'], 'rnn_conv_w': out['rnn_conv_w'], 'rnn_conv_b': out['rnn_conv_b'], 'w_rg_a': out['w_rg_a'], 'b_rg_a': out['b_rg_a'], 'w_rg_i': out['w_rg_i'], 'b_rg_i': out['b_rg_i'], 'rg_lambda': out['rg_lambda'], 'w_o_rnn': out['w_o_rnn'], 'w_o_attn': out['w_o_attn'], 'attn_sinks': out['attn_sinks'], 'rel_bias': out['rel_bias'], 'w_out': out['w_out'], 'norm2': out['norm2'], 'w_up': out['w_up'], 'ffn_conv_w': out['ffn_conv_w'], 'ffn_conv_b': out['ffn_conv_b'], 'w_down': out['w_down'], 'norm_f': out['norm_f'], 'loss_target': out['loss_target'], 'm_w_ada': out['m_w_ada'], 'm_b_ada': out['m_b_ada'], 'm_norm1': out['m_norm1'], 'm_w_in': out['m_w_in'], 'm_rnn_conv_w': out['m_rnn_conv_w'], 'm_rnn_conv_b': out['m_rnn_conv_b'], 'm_w_rg_a': out['m_w_rg_a'], 'm_b_rg_a': out['m_b_rg_a'], 'm_w_rg_i': out['m_w_rg_i'], 'm_b_rg_i': out['m_b_rg_i'], 'm_rg_lambda': out['m_rg_lambda'], 'm_w_o_rnn': out['m_w_o_rnn'], 'm_w_o_attn': out['m_w_o_attn'], 'm_attn_sinks': out['m_attn_sinks'], 'm_rel_bias': out['m_rel_bias'], 'm_w_out': out['m_w_out'], 'm_norm2': out['m_norm2'], 'm_w_up': out['m_w_up'], 'm_ffn_conv_w': out['m_ffn_conv_w'], 'm_ffn_conv_b': out['m_ffn_conv_b'], 'm_w_down': out['m_w_down'], 'm_norm_f': out['m_norm_f'], 'v_w_ada': out['v_w_ada'], 'v_b_ada': out['v_b_ada'], 'v_norm1': out['v_norm1'], 'v_w_in': out['v_w_in'], 'v_rnn_conv_w': out['v_rnn_conv_w'], 'v_rnn_conv_b': out['v_rnn_conv_b'], 'v_w_rg_a': out['v_w_rg_a'], 'v_b_rg_a': out['v_b_rg_a'], 'v_w_rg_i': out['v_w_rg_i'], 'v_b_rg_i': out['v_b_rg_i'], 'v_rg_lambda': out['v_rg_lambda'], 'v_w_o_rnn': out['v_w_o_rnn'], 'v_w_o_attn': out['v_w_o_attn'], 'v_attn_sinks': out['v_attn_sinks'], 'v_rel_bias': out['v_rel_bias'], 'v_w_out': out['v_w_out'], 'v_norm2': out['v_norm2'], 'v_w_up': out['v_w_up'], 'v_ffn_conv_w': out['v_ffn_conv_w'], 'v_ffn_conv_b': out['v_ffn_conv_b'], 'v_w_down': out['v_w_down'], 'v_norm_f': out['v_norm_f']}


def _loss(weights, diff, rest, loss_target):
    with _jax.named_scope("forward"):
        args = {**rest, TWIN_DIFF_INPUT: diff, **{k: w.astype(_WEIGHT_DTYPES[k]) for k, w in weights.items()}}
        y = _forward(args)
    with _jax.named_scope("loss_head"):
        err = _jnp.square(y.astype(_jnp.float32) - loss_target)
        return 0.5 * _jnp.sum(_jnp.mean(err, axis=-1)) if err.ndim else 0.5 * err


def _adamw(w, g, m, v):
    m = ADAM_B1 * m + (1.0 - ADAM_B1) * g
    v = ADAM_B2 * v + (1.0 - ADAM_B2) * _jnp.square(g)
    m_hat = m / (1.0 - ADAM_B1 ** ADAM_STEP)
    v_hat = v / (1.0 - ADAM_B2 ** ADAM_STEP)
    delta = -ADAM_LR * (m_hat / (_jnp.sqrt(v_hat) + ADAM_EPS) + ADAM_WD * w)
    return delta, m, v


def reference(x, c, w_ada, b_ada, norm1, w_in, rnn_conv_w, rnn_conv_b, w_rg_a, b_rg_a, w_rg_i, b_rg_i, rg_lambda, w_o_rnn, w_o_attn, attn_sinks, rel_bias, w_out, norm2, w_up, ffn_conv_w, ffn_conv_b, w_down, norm_f, loss_target, m_w_ada, m_b_ada, m_norm1, m_w_in, m_rnn_conv_w, m_rnn_conv_b, m_w_rg_a, m_b_rg_a, m_w_rg_i, m_b_rg_i, m_rg_lambda, m_w_o_rnn, m_w_o_attn, m_attn_sinks, m_rel_bias, m_w_out, m_norm2, m_w_up, m_ffn_conv_w, m_ffn_conv_b, m_w_down, m_norm_f, v_w_ada, v_b_ada, v_norm1, v_w_in, v_rnn_conv_w, v_rnn_conv_b, v_w_rg_a, v_b_rg_a, v_w_rg_i, v_b_rg_i, v_rg_lambda, v_w_o_rnn, v_w_o_attn, v_attn_sinks, v_rel_bias, v_w_out, v_norm2, v_w_up, v_ffn_conv_w, v_ffn_conv_b, v_w_down, v_norm_f):
    given = dict(x=x, c=c, w_ada=w_ada, b_ada=b_ada, norm1=norm1, w_in=w_in, rnn_conv_w=rnn_conv_w, rnn_conv_b=rnn_conv_b, w_rg_a=w_rg_a, b_rg_a=b_rg_a, w_rg_i=w_rg_i, b_rg_i=b_rg_i, rg_lambda=rg_lambda, w_o_rnn=w_o_rnn, w_o_attn=w_o_attn, attn_sinks=attn_sinks, rel_bias=rel_bias, w_out=w_out, norm2=norm2, w_up=w_up, ffn_conv_w=ffn_conv_w, ffn_conv_b=ffn_conv_b, w_down=w_down, norm_f=norm_f, loss_target=loss_target, m_w_ada=m_w_ada, m_b_ada=m_b_ada, m_norm1=m_norm1, m_w_in=m_w_in, m_rnn_conv_w=m_rnn_conv_w, m_rnn_conv_b=m_rnn_conv_b, m_w_rg_a=m_w_rg_a, m_b_rg_a=m_b_rg_a, m_w_rg_i=m_w_rg_i, m_b_rg_i=m_b_rg_i, m_rg_lambda=m_rg_lambda, m_w_o_rnn=m_w_o_rnn, m_w_o_attn=m_w_o_attn, m_attn_sinks=m_attn_sinks, m_rel_bias=m_rel_bias, m_w_out=m_w_out, m_norm2=m_norm2, m_w_up=m_w_up, m_ffn_conv_w=m_ffn_conv_w, m_ffn_conv_b=m_ffn_conv_b, m_w_down=m_w_down, m_norm_f=m_norm_f, v_w_ada=v_w_ada, v_b_ada=v_b_ada, v_norm1=v_norm1, v_w_in=v_w_in, v_rnn_conv_w=v_rnn_conv_w, v_rnn_conv_b=v_rnn_conv_b, v_w_rg_a=v_w_rg_a, v_b_rg_a=v_b_rg_a, v_w_rg_i=v_w_rg_i, v_b_rg_i=v_b_rg_i, v_rg_lambda=v_rg_lambda, v_w_o_rnn=v_w_o_rnn, v_w_o_attn=v_w_o_attn, v_attn_sinks=v_attn_sinks, v_rel_bias=v_rel_bias, v_w_out=v_w_out, v_norm2=v_norm2, v_w_up=v_w_up, v_ffn_conv_w=v_ffn_conv_w, v_ffn_conv_b=v_ffn_conv_b, v_w_down=v_w_down, v_norm_f=v_norm_f)
    weights = {n: given[n] for n in TWIN_WEIGHTS}
    shared = {n: given[n] for n in SHARED_INPUTS}
    per_example = {n: given[n] for n in ['x', 'c']}
    grad_fn = _jax.value_and_grad(_loss, argnums=(0, 1))

    def one_microbatch(ex, loss_target):
        ex = dict(ex)
        diff = ex.pop(TWIN_DIFF_INPUT)
        return grad_fn(weights, diff, {**shared, **ex}, loss_target)

    if N_MICROBATCH == 1:
        loss, (grad_w, grad_x) = one_microbatch(per_example, given["loss_target"])
    else:
        def body(carry, xs):
            loss_sum, grad_sum = carry
            l_k, (gw_k, gx_k) = one_microbatch(xs[0], xs[1])
            with _jax.named_scope("update"):
                return (loss_sum + l_k, _jax.tree.map(_jnp.add, grad_sum, gw_k)), gx_k

        init = (_jnp.zeros((), _jnp.float32), _jax.tree.map(_jnp.zeros_like, weights))
        (loss, grad_w), grad_x = _jax.lax.scan(body, init, (per_example, given["loss_target"]))
    with _jax.named_scope("update"):
        delta_w, new_m, new_v = {}, {}, {}
        for n in TWIN_WEIGHTS:
            delta_w[n], new_m[n], new_v[n] = _adamw(weights[n], grad_w[n], given["m_" + n], given["v_" + n])
    return (loss, grad_x, *[grad_w[n] for n in TWIN_WEIGHTS], *[delta_w[n] for n in TWIN_WEIGHTS],
            *[new_m[n] for n in TWIN_WEIGHTS], *[new_v[n] for n in TWIN_WEIGHTS])
```

```python
import functools
import math

import numpy as np
import jax
import jax.numpy as jnp
from jax import lax
from jax.experimental import pallas as pl
from jax.experimental.pallas import tpu as pltpu

F32, BF16 = jnp.float32, jnp.bfloat16

N_DEV = 8
D_MODEL = 2048
N_HEADS, HEAD_DIM, N_KV = 32, 64, 4
GROUP = N_HEADS // N_KV
D_ATTN, D_KV = N_HEADS * HEAD_DIM, N_KV * HEAD_DIM
BLOCK = 128
NUM_BUCKETS, MAX_DISTANCE = 32, 128
D_RNN, N_RNN_BLOCKS, RNN_BLOCK = 2560, 20, 128
RNN_CONV, FFN_CONV = 4, 3
RG_C = 8.0
D_FF = 3 * D_MODEL
D_IN = D_ATTN + 2 * D_KV + 2 * D_RNN + 2 * D_MODEL
EPS = 1e-6
NEG_INF = -1e30
ADAM_LR, ADAM_B1, ADAM_B2, ADAM_EPS, ADAM_WD, ADAM_STEP = 0.001, 0.9, 0.999, 1e-08, 0.01, 10

LANES = 128
HALO = 16
VMEM_LIMIT = 56 * 1024 * 1024
MESH = pl.DeviceIdType.MESH

Q0, K0, V0, XR0, GR0, GA0, GL0 = 0, 2048, 2304, 2560, 5120, 7680, 9728


def _cparams(n_axes):
    return pltpu.CompilerParams(dimension_semantics=("arbitrary",) * n_axes, vmem_limit_bytes=VMEM_LIMIT)


def _gelu(x):
    k = math.sqrt(2.0 / math.pi)
    return 0.5 * x * (1.0 + jnp.tanh(k * (x + 0.044715 * x * x * x)))


def _gelu_and_grad(x):
    k = math.sqrt(2.0 / math.pi)
    t = jnp.tanh(k * (x + 0.044715 * x * x * x))
    g = 0.5 * x * (1.0 + t)
    dg = 0.5 * (1.0 + t) + 0.5 * x * (1.0 - t * t) * k * (1.0 + 3.0 * 0.044715 * x * x)
    return g, dg


def _sigmoid(x):
    return 1.0 / (1.0 + jnp.exp(-x))


def _shift_down(prev, x, j):
    if j == 0:
        return x
    xe = jnp.concatenate([prev, x], axis=0)
    return pltpu.roll(xe, j, axis=0)[HALO:, :]


def _shift_up(x, nxt, j):
    if j == 0:
        return x
    xe = jnp.concatenate([x, nxt], axis=0)
    n = xe.shape[0]
    return pltpu.roll(xe, n - j, axis=0)[: x.shape[0], :]


def _my_coords():
    return lax.axis_index("x"), lax.axis_index("y"), lax.axis_index("c")


def _peer(x, y, c, k):
    kx, ky, kc = (k >> 2) & 1, (k >> 1) & 1, k & 1
    px, py, pc = (x + kx) % 2, (y + ky) % 2, (c + kc) % 2
    return (px, py, pc), 4 * px + 2 * py + pc


def exchange(items, name):
    n = len(items)
    scat = [s for _, s in items]
    out_shapes = []
    for arr, s in items:
        shp = arr.shape[1:] if s else arr.shape
        out_shapes.append(jax.ShapeDtypeStruct((N_DEV,) + tuple(shp), arr.dtype))

    def body(*refs):
        srcs, dsts = refs[:n], refs[n:2 * n]
        send_sems, recv_sems, loc_sems = refs[2 * n:]
        x, y, c = _my_coords()
        me = 4 * x + 2 * y + c
        copies = []
        for a in range(n):
            mine = srcs[a].at[me] if scat[a] else srcs[a]
            loc = pltpu.make_async_copy(mine, dsts[a].at[me], loc_sems.at[a])
            loc.start()
            copies.append(loc)
        for k in range(1, N_DEV):
            peer, p = _peer(x, y, c, k)
            for a in range(n):
                src = srcs[a].at[p] if scat[a] else srcs[a]
                cp = pltpu.make_async_remote_copy(
                    src_ref=src, dst_ref=dsts[a].at[me], send_sem=send_sems.at[a, k - 1],
                    recv_sem=recv_sems.at[a, k - 1], device_id=peer, device_id_type=MESH)
                cp.start()
                copies.append(cp)
        for cp in copies:
            cp.wait()

    any_spec = pl.BlockSpec(memory_space=pl.ANY)
    return pl.pallas_call(
        body, name=name, out_shape=tuple(out_shapes),
        in_specs=[any_spec] * n, out_specs=tuple([any_spec] * n),
        scratch_shapes=[pltpu.SemaphoreType.DMA((n, N_DEV - 1)), pltpu.SemaphoreType.DMA((n, N_DEV - 1)),
                        pltpu.SemaphoreType.DMA((n,))],
    )(*[a for a, _ in items])


def mod_forward(c, w_ada, b_ada):
    d, ncol = w_ada.shape

    def body(c_ref, w_ref, b_ref, mod_ref, call_ref, cols_ref, s1, r1, s2, r2):
        x, y, c_ = _my_coords()
        me = 4 * x + 2 * y + c_
        call_ref[me] = c_ref[0]
        sends = []
        for k in range(1, N_DEV):
            peer, p = _peer(x, y, c_, k)
            cp = pltpu.make_async_remote_copy(src_ref=c_ref.at[0], dst_ref=call_ref.at[me], send_sem=s1.at[k - 1],
                                              recv_sem=r1.at[k - 1], device_id=peer, device_id_type=MESH)
            cp.start()
            sends.append(cp)
        for cp in sends:
            cp.wait()
        rows = lax.broadcasted_iota(jnp.int32, (N_DEV, d), 0)
        cmat = jnp.zeros((N_DEV, d), F32)
        for b in range(N_DEV):
            cmat = jnp.where(rows == b, call_ref[b], cmat)
        cs = cmat * _sigmoid(cmat)
        bias = b_ref[:, pl.ds(pl.multiple_of(me * ncol, LANES), ncol)]
        mc = jnp.dot(cs, w_ref[...], preferred_element_type=F32, precision=lax.Precision.HIGHEST) + bias
        for b in range(N_DEV):
            cols_ref[b] = mc[b:b + 1, :]
        mod_ref[me] = cols_ref[me]
        sends = []
        for k in range(1, N_DEV):
            peer, p = _peer(x, y, c_, k)
            cp = pltpu.make_async_remote_copy(src_ref=cols_ref.at[p], dst_ref=mod_ref.at[me], send_sem=s2.at[k - 1],
                                              recv_sem=r2.at[k - 1], device_id=peer, device_id_type=MESH)
            cp.start()
            sends.append(cp)
        for cp in sends:
            cp.wait()

    vm = pl.BlockSpec(memory_space=pltpu.VMEM)
    return pl.pallas_call(
        body, name="mod_forward",
        out_shape=(jax.ShapeDtypeStruct((N_DEV, 1, ncol), F32), jax.ShapeDtypeStruct((N_DEV, 1, d), F32)),
        in_specs=[vm, vm, vm], out_specs=(vm, vm),
        scratch_shapes=[pltpu.VMEM((N_DEV, 1, ncol), F32)] + [pltpu.SemaphoreType.DMA((N_DEV - 1,))] * 4,
        compiler_params=pltpu.CompilerParams(vmem_limit_bytes=VMEM_LIMIT),
    )(c, w_ada, b_ada)


def matmul(a, b, mode, out_dtype, name, tm, tn, tk, res=None, gate=None):
    if mode == "nn":
        (m, kk), (_, n) = a.shape, b.shape
    elif mode == "nt":
        (m, kk), (n, _) = a.shape, b.shape
    else:
        (kk, m), (_, n) = a.shape, b.shape
    tm, tn, tk = min(tm, m), min(tn, n), min(tk, kk)
    assert m % tm == 0 and n % tn == 0 and kk % tk == 0, (name, m, n, kk, tm, tn, tk)
    if mode == "nn":
        a_spec = pl.BlockSpec((tm, tk), lambda j, i, k: (i, k))
        b_spec = pl.BlockSpec((tk, tn), lambda j, i, k: (k, j))
        dims = (((1,), (0,)), ((), ()))
    elif mode == "nt":
        a_spec = pl.BlockSpec((tm, tk), lambda j, i, k: (i, k))
        b_spec = pl.BlockSpec((tn, tk), lambda j, i, k: (j, k))
        dims = (((1,), (1,)), ((), ()))
    else:
        a_spec = pl.BlockSpec((tk, tm), lambda j, i, k: (k, i))
        b_spec = pl.BlockSpec((tk, tn), lambda j, i, k: (k, j))
        dims = (((0,), (0,)), ((), ()))
    nk = kk // tk
    fused = res is not None
    o_spec = pl.BlockSpec((tm, tn), lambda j, i, k: (i, j))

    def body(*refs):
        if fused:
            a_ref, b_ref, res_ref, gate_ref, o_ref, o2_ref = refs[:6]
        else:
            a_ref, b_ref, o_ref = refs[:3]
        acc_ref = refs[-1] if nk > 1 else None

        def finish(acc):
            if fused:
                o_ref[...] = res_ref[...] + gate_ref[...] * acc
                o2_ref[...] = acc.astype(o2_ref.dtype)
            else:
                o_ref[...] = acc.astype(o_ref.dtype)

        prod = lax.dot_general(a_ref[...], b_ref[...], dims, preferred_element_type=F32)
        if nk == 1:
            finish(prod)
        else:
            k = pl.program_id(2)

            @pl.when(k == 0)
            def _():
                acc_ref[...] = prod

            @pl.when(k > 0)
            def _():
                acc_ref[...] += prod

            @pl.when(k == nk - 1)
            def _():
                finish(acc_ref[...])

    in_specs, args = [a_spec, b_spec], [a, b]
    if fused:
        in_specs += [o_spec, pl.BlockSpec((1, tn), lambda j, i, k: (0, j))]
        args += [res, gate]
        out_shape = (jax.ShapeDtypeStruct((m, n), F32), jax.ShapeDtypeStruct((m, n), out_dtype))
        out_specs = (o_spec, o_spec)
    else:
        out_shape = jax.ShapeDtypeStruct((m, n), out_dtype)
        out_specs = o_spec
    return pl.pallas_call(
        body, name=name, grid=(n // tn, m // tm, nk), in_specs=in_specs, out_specs=out_specs, out_shape=out_shape,
        scratch_shapes=[pltpu.VMEM((tm, tn), F32)] if nk > 1 else [], compiler_params=_cparams(3),
    )(*args)


def _row_tile(s, want):
    t = min(want, s)
    assert s % t == 0 and t % HALO == 0
    return t


def prenorm(x, nw, scale, shift, name):
    s, d = x.shape
    tm = _row_tile(s, 512)

    def body(x_ref, nw_ref, sc_ref, sh_ref, o_ref):
        xv = x_ref[...]
        r = lax.rsqrt(jnp.mean(xv * xv, axis=-1, keepdims=True) + EPS)
        o_ref[...] = ((xv * r) * nw_ref[...] * (1.0 + sc_ref[...]) + sh_ref[...]).astype(BF16)

    row = pl.BlockSpec((tm, d), lambda i: (i, 0))
    vec = pl.BlockSpec((1, d), lambda i: (0, 0))
    return pl.pallas_call(body, name=name, grid=(s // tm,), in_specs=[row, vec, vec, vec], out_specs=row,
                          out_shape=jax.ShapeDtypeStruct((s, d), BF16), compiler_params=_cparams(1))(x, nw, scale, shift)


def norm_backward(du, xin, dres, nw, scale, name, mo=None, gate=None):
    s, d = xin.shape
    tm = _row_tile(s, 256)
    gated = mo is not None

    def body(*refs):
        if gated:
            du_ref, x_ref, dr_ref, nw_ref, sc_ref, mo_ref, g_ref, dx_ref, dsh_ref, dsc_ref, dnw_ref, dmo_ref, dg_ref = refs
        else:
            du_ref, x_ref, dr_ref, nw_ref, sc_ref, dx_ref, dsh_ref, dsc_ref, dnw_ref = refs
        i = pl.program_id(0)
        xv = x_ref[...]
        r = lax.rsqrt(jnp.mean(xv * xv, axis=-1, keepdims=True) + EPS)
        xn = xv * r
        duv = du_ref[...].astype(F32)
        nwv, scv = nw_ref[...], sc_ref[...]
        dxn = duv * (nwv * (1.0 + scv))
        dx = dr_ref[...] + r * (dxn - xn * jnp.mean(dxn * xn, axis=-1, keepdims=True))
        dx_ref[...] = dx
        sums = [jnp.sum(duv, axis=0, keepdims=True), jnp.sum(duv * xn * nwv, axis=0, keepdims=True),
                jnp.sum(duv * xn * (1.0 + scv), axis=0, keepdims=True)]
        accs = [dsh_ref, dsc_ref, dnw_ref]
        if gated:
            dmo_ref[...] = (dx * g_ref[...]).astype(BF16)
            sums.append(jnp.sum(dx * mo_ref[...].astype(F32), axis=0, keepdims=True))
            accs.append(dg_ref)

        @pl.when(i == 0)
        def _():
            for acc, sm in zip(accs, sums):
                acc[...] = sm

        @pl.when(i > 0)
        def _():
            for acc, sm in zip(accs, sums):
                acc[...] += sm

    row = pl.BlockSpec((tm, d), lambda i: (i, 0))
    vec = pl.BlockSpec((1, d), lambda i: (0, 0))
    vshape = jax.ShapeDtypeStruct((1, d), F32)
    in_specs, args = [row, row, row, vec, vec], [du, xin, dres, nw, scale]
    out_specs, out_shape = [row, vec, vec, vec], [jax.ShapeDtypeStruct((s, d), F32), vshape, vshape, vshape]
    if gated:
        in_specs += [row, vec]
        args += [mo, gate]
        out_specs += [row, vec]
        out_shape += [jax.ShapeDtypeStruct((s, d), BF16), vshape]
    return pl.pallas_call(body, name=name, grid=(s // tm,), in_specs=in_specs, out_specs=tuple(out_specs),
                          out_shape=tuple(out_shape), compiler_params=_cparams(1))(*args)


def merge_forward(proj, y_attn, y_rnn):
    s, d = y_attn.shape
    tm, cw = _row_tile(s, 1024), 512

    def body(ga_ref, gl_ref, ya_ref, yr_ref, o_ref):
        o_ref[...] = (_sigmoid(ga_ref[...].astype(F32)) * ya_ref[...].astype(F32)
                      + _sigmoid(gl_ref[...].astype(F32)) * yr_ref[...].astype(F32)).astype(BF16)

    def at(off):
        return pl.BlockSpec((tm, cw), lambda j, i: (i, off // cw + j))

    return pl.pallas_call(body, name="merge_forward", grid=(d // cw, s // tm),
                          in_specs=[at(GA0), at(GL0), at(0), at(0)], out_specs=at(0),
                          out_shape=jax.ShapeDtypeStruct((s, d), BF16), compiler_params=_cparams(2))(proj, proj, y_attn, y_rnn)


def merge_backward(dmerged, proj, y_attn, y_rnn):
    s, d = y_attn.shape
    tm, cw = _row_tile(s, 1024), 512

    def body(dm_ref, ga_ref, gl_ref, ya_ref, yr_ref, dga_ref, dgl_ref, dya_ref, dyr_ref):
        dm = dm_ref[...].astype(F32)
        sa, sl = _sigmoid(ga_ref[...].astype(F32)), _sigmoid(gl_ref[...].astype(F32))
        dga_ref[...] = (dm * ya_ref[...].astype(F32) * sa * (1.0 - sa)).astype(BF16)
        dgl_ref[...] = (dm * yr_ref[...].astype(F32) * sl * (1.0 - sl)).astype(BF16)
        dya_ref[...] = (dm * sa).astype(BF16)
        dyr_ref[...] = (dm * sl).astype(BF16)

    def at(off):
        return pl.BlockSpec((tm, cw), lambda j, i: (i, off // cw + j))

    o = jax.ShapeDtypeStruct((s, d), BF16)
    return pl.pallas_call(body, name="merge_backward", grid=(d // cw, s // tm),
                          in_specs=[at(0), at(GA0), at(GL0), at(0), at(0)], out_specs=(at(0),) * 4,
                          out_shape=(o, o, o, o), compiler_params=_cparams(2))(dmerged, proj, proj, y_attn, y_rnn)


def _prev_spec(tm, cw, off_blocks):
    r = tm // HALO
    return pl.BlockSpec((HALO, cw), lambda j, i: (jnp.maximum(i * r - 1, 0), off_blocks + j))


def ffn_act_forward(upp, cw_full, cb_full):
    s, f2 = upp.shape
    f = f2 // 2
    tm, cw = _row_tile(s, 512), 1536
    nj = f // cw

    def body(g_ref, gp_ref, v_ref, vp_ref, wg_ref, wv_ref, bg_ref, bv_ref, o_ref):
        i = pl.program_id(1)
        first = i == 0

        def conv(x_ref, p_ref, w_ref, b_ref):
            xv = x_ref[...].astype(F32)
            pv = jnp.where(first, 0.0, p_ref[...].astype(F32))
            acc = b_ref[...] + w_ref[FFN_CONV - 1:FFN_CONV, :] * xv
            for k in range(FFN_CONV - 1):
                acc = acc + w_ref[k:k + 1, :] * _shift_down(pv, xv, FFN_CONV - 1 - k)
            return acc

        g = conv(g_ref, gp_ref, wg_ref, bg_ref)
        v = conv(v_ref, vp_ref, wv_ref, bv_ref)
        o_ref[...] = (_gelu(g) * v).astype(BF16)

    def tile(ob):
        return pl.BlockSpec((tm, cw), lambda j, i: (i, ob + j))

    def par(rows, ob):
        return pl.BlockSpec((rows, cw), lambda j, i: (0, ob + j))

    return pl.pallas_call(
        body, name="ffn_act_forward", grid=(nj, s // tm),
        in_specs=[tile(0), _prev_spec(tm, cw, 0), tile(nj), _prev_spec(tm, cw, nj),
                  par(FFN_CONV, 0), par(FFN_CONV, nj), par(1, 0), par(1, nj)],
        out_specs=tile(0), out_shape=jax.ShapeDtypeStruct((s, f), BF16), compiler_params=_cparams(2),
    )(upp, upp, upp, upp, cw_full, cw_full, cb_full, cb_full)


def ffn_act_backward(dact, upp, cw_full, cb_full):
    s, f2 = upp.shape
    f = f2 // 2
    tm, cw = _row_tile(s, 256), 1536
    nj = f // cw

    def body(da_ref, g_ref, gp_ref, v_ref, vp_ref, wg_ref, wv_ref, bg_ref, bv_ref, dup_ref, db_ref, dw_ref):
        j, i = pl.program_id(0), pl.program_id(1)
        first = i == 0

        def shifted(x_ref, p_ref):
            xv = x_ref[...].astype(F32)
            pv = jnp.where(first, 0.0, p_ref[...].astype(F32))
            return [_shift_down(pv, xv, FFN_CONV - 1 - k) for k in range(FFN_CONV)]

        def conv(sh, w_ref, b_ref):
            acc = b_ref[...]
            for k in range(FFN_CONV):
                acc = acc + w_ref[k:k + 1, :] * sh[k]
            return acc

        gs, vs = shifted(g_ref, gp_ref), shifted(v_ref, vp_ref)
        g, v = conv(gs, wg_ref, bg_ref), conv(vs, wv_ref, bv_ref)
        da = da_ref[...].astype(F32)
        ge, dge = _gelu_and_grad(g)

        def emit(dup, sh):
            dup_ref[...] = dup.astype(BF16)
            sums_b = jnp.sum(dup, axis=0, keepdims=True)
            sums_w = jnp.concatenate([jnp.sum(dup * sh[k], axis=0, keepdims=True) for k in range(FFN_CONV)], axis=0)

            @pl.when(first)
            def _():
                db_ref[...] = sums_b
                dw_ref[...] = sums_w

            @pl.when(i > 0)
            def _():
                db_ref[...] += sums_b
                dw_ref[...] += sums_w

        @pl.when(j < nj)
        def _():
            emit(da * v * dge, gs)

        @pl.when(j >= nj)
        def _():
            emit(da * ge, vs)

    def tile(fn):
        return pl.BlockSpec((tm, cw), lambda j, i: (i, fn(j)))

    def prev(fn):
        r = tm // HALO
        return pl.BlockSpec((HALO, cw), lambda j, i: (jnp.maximum(i * r - 1, 0), fn(j)))

    def par(rows, fn):
        return pl.BlockSpec((rows, cw), lambda j, i: (0, fn(j)))

    gcol = lambda j: j % nj
    vcol = lambda j: nj + j % nj
    same = lambda j: j
    return pl.pallas_call(
        body, name="ffn_act_backward", grid=(2 * nj, s // tm),
        in_specs=[tile(gcol), tile(gcol), prev(gcol), tile(vcol), prev(vcol),
                  par(FFN_CONV, gcol), par(FFN_CONV, vcol), par(1, gcol), par(1, vcol)],
        out_specs=(tile(same), par(1, same), par(FFN_CONV, same)),
        out_shape=(jax.ShapeDtypeStruct((s, f2), BF16), jax.ShapeDtypeStruct((1, f2), F32),
                   jax.ShapeDtypeStruct((FFN_CONV, f2), F32)),
        compiler_params=_cparams(2),
    )(dact, upp, upp, upp, upp, cw_full, cw_full, cb_full, cb_full)


def conv_backward(dup, cw_full):
    s, f2 = dup.shape
    tm, cw = _row_tile(s, 512), 1536
    nt = s // tm
    r = tm // HALO

    def body(x_ref, n_ref, w_ref, o_ref):
        i = pl.program_id(1)
        xv = x_ref[...].astype(F32)
        nv = jnp.where(i == nt - 1, 0.0, n_ref[...].astype(F32))
        acc = w_ref[FFN_CONV - 1:FFN_CONV, :] * xv
        for k in range(FFN_CONV - 1):
            acc = acc + w_ref[k:k + 1, :] * _shift_up(xv, nv, FFN_CONV - 1 - k)
        o_ref[...] = acc.astype(BF16)

    tile = pl.BlockSpec((tm, cw), lambda j, i: (i, j))
    nxt = pl.BlockSpec((HALO, cw), lambda j, i: (jnp.minimum((i + 1) * r, s // HALO - 1), j))
    par = pl.BlockSpec((FFN_CONV, cw), lambda j, i: (0, j))
    return pl.pallas_call(body, name="ffn_conv_backward", grid=(f2 // cw, nt), in_specs=[tile, nxt, par],
                          out_specs=tile, out_shape=jax.ShapeDtypeStruct((s, f2), BF16),
                          compiler_params=_cparams(2))(dup, dup, cw_full)


def loss_head(h2, target, dn, norm_f, gate2):
    s, d = h2.shape
    tm = _row_tile(s, 256)

    def body(h_ref, t_ref, dn_ref, nf_ref, g_ref, dh_ref, ddn_ref, loss_ref, dnf_ref, dg_ref):
        i = pl.program_id(0)
        hv = h_ref[...]
        r = lax.rsqrt(jnp.mean(hv * hv, axis=-1, keepdims=True) + EPS)
        yh = hv * r
        nf = nf_ref[...]
        err = yh * nf - t_ref[...]
        dy = err * (1.0 / d)
        dyh = dy * nf
        dh = r * (dyh - yh * jnp.mean(dyh * yh, axis=-1, keepdims=True))
        dh_ref[...] = dh
        ddn_ref[...] = (dh * g_ref[...]).astype(BF16)
        sums = [jnp.sum(err * err, axis=0, keepdims=True) * (0.5 / d), jnp.sum(dy * yh, axis=0, keepdims=True),
                jnp.sum(dh * dn_ref[...].astype(F32), axis=0, keepdims=True)]
        accs = [loss_ref, dnf_ref, dg_ref]

        @pl.when(i == 0)
        def _():
            for acc, sm in zip(accs, sums):
                acc[...] = sm

        @pl.when(i > 0)
        def _():
            for acc, sm in zip(accs, sums):
                acc[...] += sm

    row = pl.BlockSpec((tm, d), lambda i: (i, 0))
    vec = pl.BlockSpec((1, d), lambda i: (0, 0))
    v = jax.ShapeDtypeStruct((1, d), F32)
    return pl.pallas_call(
        body, name="loss_head", grid=(s // tm,), in_specs=[row, row, row, vec, vec], out_specs=(row, row, vec, vec, vec),
        out_shape=(jax.ShapeDtypeStruct((s, d), F32), jax.ShapeDtypeStruct((s, d), BF16), v, v, v),
        compiler_params=_cparams(1))(h2, target, dn, norm_f, gate2)


def _t5_buckets():
    qi = np.arange(BLOCK)[:, None]
    kj = np.arange(2 * BLOCK)[None, :]
    dist = qi + BLOCK - kj
    dd = np.maximum(dist, 0)
    max_exact = NUM_BUCKETS // 2
    dflt = np.maximum(dd, 1).astype(np.float32)
    large = max_exact + (np.log(dflt / max_exact) / math.log(MAX_DISTANCE / max_exact)
                         * (NUM_BUCKETS - max_exact)).astype(np.int32)
    large = np.minimum(large, NUM_BUCKETS - 1)
    bucket = np.where(dd < max_exact, dd, large).astype(np.int32)
    in_window = (dist >= 0) & (dist < BLOCK)
    return bucket, in_window


def band_bias(rel_bias):
    bucket, in_window = _t5_buckets()
    bucket = jnp.asarray(np.where(in_window, bucket, -1).astype(np.int32))

    def body(rb_ref, bk_ref, o_ref):
        bk = bk_ref[...]
        for h in range(N_HEADS):
            acc = jnp.full((BLOCK, 2 * BLOCK), NEG_INF, F32)
            for b in range(NUM_BUCKETS):
                acc = jnp.where(bk == b, rb_ref[b, h], acc)
            o_ref[h] = acc

    return pl.pallas_call(
        body, name="band_bias", out_shape=jax.ShapeDtypeStruct((N_HEADS, BLOCK, 2 * BLOCK), F32),
        in_specs=[pl.BlockSpec(memory_space=pltpu.SMEM), pl.BlockSpec(memory_space=pltpu.VMEM)],
        out_specs=pl.BlockSpec(memory_space=pltpu.VMEM))(rel_bias, bucket)


def rel_bias_grad(dbias):
    bucket, in_window = _t5_buckets()
    bucket = jnp.asarray(np.where(in_window, bucket, -1).astype(np.int32))

    def body(db_ref, bk_ref, o_ref):
        bk = bk_ref[...]
        rows = lax.broadcasted_iota(jnp.int32, (NUM_BUCKETS, LANES), 0)
        lanes = lax.broadcasted_iota(jnp.int32, (NUM_BUCKETS, LANES), 1)
        acc = jnp.zeros((NUM_BUCKETS, LANES), F32)
        for h in range(N_HEADS):
            dv = db_ref[h]
            for b in range(NUM_BUCKETS):
                sm = jnp.sum(jnp.where(bk == b, dv, 0.0))
                acc = jnp.where((rows == b) & (lanes == h), sm, acc)
        o_ref[...] = acc

    vm = pl.BlockSpec(memory_space=pltpu.VMEM)
    return pl.pallas_call(body, name="rel_bias_grad", out_shape=jax.ShapeDtypeStruct((NUM_BUCKETS, LANES), F32),
                          in_specs=[vm, vm], out_specs=vm)(dbias, bucket)


HP = 2
Q_PER_HP = D_ATTN // HP
H_PER_HP = N_HEADS // HP


def _attn_setup(hh, q_ref, kp_ref, kc_ref, vp_ref, vc_ref, bias_ref, sink_ref, hp, n):
    lane = lax.broadcasted_iota(jnp.int32, (2 * BLOCK, LANES), 1)
    own = (lane >= HEAD_DIM) if hh == 1 else (lane < HEAD_DIM)
    kband = jnp.concatenate([kp_ref[...], kc_ref[...]], axis=0)
    vband = jnp.concatenate([vp_ref[...], vc_ref[...]], axis=0)
    kk = jnp.where(own, kband, pltpu.roll(kband, HEAD_DIM, axis=1))
    vv = jnp.where(own, vband, pltpu.roll(vband, HEAD_DIM, axis=1))
    qlane = lax.broadcasted_iota(jnp.int32, (BLOCK, LANES), 1)
    lo = qlane < HEAD_DIM
    parts = []
    for s_ in range(GROUP // 2):
        slab = q_ref[:, hh * (GROUP * HEAD_DIM) + s_ * LANES: hh * (GROUP * HEAD_DIM) + (s_ + 1) * LANES]
        parts.append(jnp.where(lo, slab, jnp.zeros_like(slab)))
        parts.append(jnp.where(lo, jnp.zeros_like(slab), slab))
    qs = jnp.concatenate(parts, axis=0)
    sc = lax.dot_general(qs, kk, (((1,), (1,)), ((), ())), preferred_element_type=F32) * (HEAD_DIM ** -0.5)
    sc = sc + jnp.concatenate([bias_ref[hh * GROUP + g] for g in range(GROUP)], axis=0)
    kcol = lax.broadcasted_iota(jnp.int32, sc.shape, 1)
    sc = jnp.where((n == 0) & (kcol < BLOCK), NEG_INF, sc)
    sink = jnp.concatenate([jnp.full((BLOCK, 1), sink_ref[hp * H_PER_HP + hh * GROUP + g], F32) for g in range(GROUP)], axis=0)
    m = jnp.maximum(jnp.max(sc, axis=-1, keepdims=True), sink)
    p = jnp.exp(sc - m)
    es = jnp.exp(sink - m)
    inv = 1.0 / (jnp.sum(p, axis=-1, keepdims=True) + es)
    return qs, kk, vv, p * inv, es * inv


def _unstack(o, dtype):
    lane = lax.broadcasted_iota(jnp.int32, (BLOCK, LANES), 1)
    lo = lane < HEAD_DIM
    slabs = []
    for s_ in range(GROUP // 2):
        ev = o[(2 * s_) * BLOCK:(2 * s_ + 1) * BLOCK]
        od = o[(2 * s_ + 1) * BLOCK:(2 * s_ + 2) * BLOCK]
        slabs.append(jnp.where(lo, ev, od).astype(dtype))
    return slabs


def attention_forward(proj, bias, sinks):
    s = proj.shape[0]
    nb = s // BLOCK
    kb, vb = K0 // LANES, V0 // LANES

    def body(q_ref, kp_ref, kc_ref, vp_ref, vc_ref, bias_ref, sink_ref, o_ref):
        hp, n = pl.program_id(0), pl.program_id(1)
        for hh in range(2):
            qs, kk, vv, probs, _ = _attn_setup(hh, q_ref, kp_ref, kc_ref, vp_ref, vc_ref, bias_ref, sink_ref, hp, n)
            o = jnp.dot(probs.astype(BF16), vv, preferred_element_type=F32)
            for s_, slab in enumerate(_unstack(o, BF16)):
                c0 = hh * (GROUP * HEAD_DIM) + s_ * LANES
                o_ref[:, c0:c0 + LANES] = slab

    qspec = pl.BlockSpec((BLOCK, Q_PER_HP), lambda hp, n: (n, hp))

    def kv(base, prev):
        if prev:
            return pl.BlockSpec((BLOCK, LANES), lambda hp, n: (jnp.maximum(n - 1, 0), base + hp))
        return pl.BlockSpec((BLOCK, LANES), lambda hp, n: (n, base + hp))

    return pl.pallas_call(
        body, name="attention_forward", grid=(HP, nb),
        in_specs=[qspec, kv(kb, True), kv(kb, False), kv(vb, True), kv(vb, False),
                  pl.BlockSpec((H_PER_HP, BLOCK, 2 * BLOCK), lambda hp, n: (hp, 0, 0)),
                  pl.BlockSpec(memory_space=pltpu.SMEM)],
        out_specs=qspec, out_shape=jax.ShapeDtypeStruct((s, D_ATTN), BF16), compiler_params=_cparams(2),
    )(proj, proj, proj, proj, proj, bias, sinks)


def attention_backward(proj, datt, bias, sinks):
    s = proj.shape[0]
    nb = s // BLOCK
    kb, vb = K0 // LANES, V0 // LANES

    def body(q_ref, kp_ref, kc_ref, vp_ref, vc_ref, do_ref, bias_ref, sink_ref,
             dq_ref, dk_ref, dv_ref, dbias_ref, dsink_ref, kcar_ref, vcar_ref):
        hp, n = pl.program_id(0), pl.program_id(1)

        @pl.when(n == 0)
        def _():
            kcar_ref[...] = jnp.zeros_like(kcar_ref)
            vcar_ref[...] = jnp.zeros_like(vcar_ref)
            dbias_ref[...] = jnp.zeros_like(dbias_ref)
            dsink_ref[...] = jnp.zeros_like(dsink_ref)

        @pl.when(n < nb)
        def _():
            lane2 = lax.broadcasted_iota(jnp.int32, (2 * BLOCK, LANES), 1)
            qlane = lax.broadcasted_iota(jnp.int32, (BLOCK, LANES), 1)
            lo = qlane < HEAD_DIM
            dk_band = jnp.zeros((2 * BLOCK, LANES), F32)
            dv_band = jnp.zeros((2 * BLOCK, LANES), F32)
            for hh in range(2):
                qs, kk, vv, probs, psink = _attn_setup(hh, q_ref, kp_ref, kc_ref, vp_ref, vc_ref, bias_ref, sink_ref, hp, n)
                parts = []
                for s_ in range(GROUP // 2):
                    c0 = hh * (GROUP * HEAD_DIM) + s_ * LANES
                    slab = do_ref[:, c0:c0 + LANES]
                    parts.append(jnp.where(lo, slab, jnp.zeros_like(slab)))
                    parts.append(jnp.where(lo, jnp.zeros_like(slab), slab))
                dos = jnp.concatenate(parts, axis=0)
                dp = lax.dot_general(dos, vv, (((1,), (1,)), ((), ())), preferred_element_type=F32)
                dsum = jnp.sum(probs * dp, axis=-1, keepdims=True)
                ds = probs * (dp - dsum)
                for g in range(GROUP):
                    dbias_ref[hh * GROUP + g] += ds[g * BLOCK:(g + 1) * BLOCK, :]
                dsk = -psink * dsum
                rows = [jnp.full((1, LANES), jnp.sum(dsk[g * BLOCK:(g + 1) * BLOCK, :]), F32) for g in range(GROUP)]
                dsink_ref[hh * GROUP:(hh + 1) * GROUP, :] += jnp.concatenate(rows, axis=0)
                dsb = (ds * (HEAD_DIM ** -0.5)).astype(BF16)
                pb = probs.astype(BF16)
                dq = jnp.dot(dsb, kk, preferred_element_type=F32)
                for s_, slab in enumerate(_unstack(dq, BF16)):
                    c0 = hh * (GROUP * HEAD_DIM) + s_ * LANES
                    dq_ref[:, c0:c0 + LANES] = slab
                dkh = lax.dot_general(dsb, qs, (((0,), (0,)), ((), ())), preferred_element_type=F32)
                dvh = lax.dot_general(pb, dos, (((0,), (0,)), ((), ())), preferred_element_type=F32)
                own = (lane2 >= HEAD_DIM) if hh == 1 else (lane2 < HEAD_DIM)
                dk_band = dk_band + jnp.where(own, dkh + pltpu.roll(dkh, HEAD_DIM, axis=1), 0.0)
                dv_band = dv_band + jnp.where(own, dvh + pltpu.roll(dvh, HEAD_DIM, axis=1), 0.0)
            dk_ref[...] = (kcar_ref[...] + dk_band[:BLOCK]).astype(BF16)
            dv_ref[...] = (vcar_ref[...] + dv_band[:BLOCK]).astype(BF16)
            kcar_ref[...] = dk_band[BLOCK:]
            vcar_ref[...] = dv_band[BLOCK:]

        @pl.when(n == nb)
        def _():
            dk_ref[...] = kcar_ref[...].astype(BF16)
            dv_ref[...] = vcar_ref[...].astype(BF16)

    qspec = pl.BlockSpec((BLOCK, Q_PER_HP), lambda hp, n: (jnp.minimum(n, nb - 1), hp))

    def kv(base, prev):
        if prev:
            return pl.BlockSpec((BLOCK, LANES), lambda hp, n: (jnp.maximum(jnp.minimum(n, nb - 1) - 1, 0), base + hp))
        return pl.BlockSpec((BLOCK, LANES), lambda hp, n: (jnp.minimum(n, nb - 1), base + hp))

    dkv_spec = pl.BlockSpec((BLOCK, LANES), lambda hp, n: (jnp.maximum(n - 1, 0), hp))
    return pl.pallas_call(
        body, name="attention_backward", grid=(HP, nb + 1),
        in_specs=[qspec, kv(kb, True), kv(kb, False), kv(vb, True), kv(vb, False), qspec,
                  pl.BlockSpec((H_PER_HP, BLOCK, 2 * BLOCK), lambda hp, n: (hp, 0, 0)),
                  pl.BlockSpec(memory_space=pltpu.SMEM)],
        out_specs=(qspec, dkv_spec, dkv_spec,
                   pl.BlockSpec((H_PER_HP, BLOCK, 2 * BLOCK), lambda hp, n: (hp, 0, 0)),
                   pl.BlockSpec((H_PER_HP, LANES), lambda hp, n: (hp, 0))),
        out_shape=(jax.ShapeDtypeStruct((s, D_ATTN), BF16), jax.ShapeDtypeStruct((s, D_KV), BF16),
                   jax.ShapeDtypeStruct((s, D_KV), BF16), jax.ShapeDtypeStruct((N_HEADS, BLOCK, 2 * BLOCK), F32),
                   jax.ShapeDtypeStruct((N_HEADS, LANES), F32)),
        scratch_shapes=[pltpu.VMEM((BLOCK, LANES), F32), pltpu.VMEM((BLOCK, LANES), F32)],
        compiler_params=_cparams(2),
    )(proj, proj, proj, proj, proj, datt, bias, sinks)


def _neg_expm1(x):
    series = -(x * (1.0 + x * (1.0 / 2 + x * (1.0 / 6 + x * (1.0 / 24 + x * (1.0 / 120 + x * (1.0 / 720)))))))
    return jnp.where(x > -0.25, series, 1.0 - jnp.exp(x))


def _softplus_neg(lam):
    u = jnp.exp(-jnp.abs(lam))
    w = 1.0 + u
    log1p = jnp.where(w == 1.0, u, jnp.log(w) * u / jnp.where(w == 1.0, 1.0, w - 1.0))
    sp = jnp.maximum(-lam, 0.0) + log1p
    return sp, -_sigmoid(-lam)


def _rnn_gates(x_ref, xp_ref, cw_ref, cb_ref, wa_ref, ba_ref, wi_ref, bi_ref, lam_ref, first, row0):
    xv = x_ref[...].astype(F32)
    pv = jnp.where(first, 0.0, xp_ref[...].astype(F32))
    xs = [_shift_down(pv, xv, RNN_CONV - 1 - k) for k in range(RNN_CONV)]
    xc = cb_ref[...]
    for k in range(RNN_CONV):
        xc = xc + cw_ref[k:k + 1, :] * xs[k]
    xcb = xc.astype(BF16)
    ra = _sigmoid(jnp.dot(xcb, wa_ref[...], preferred_element_type=F32) + ba_ref[...])
    ri = _sigmoid(jnp.dot(xcb, wi_ref[...], preferred_element_type=F32) + bi_ref[...])
    sp, dsp = _softplus_neg(lam_ref[...])
    la = (-RG_C) * ra * sp
    a = jnp.exp(la)
    t = row0 + lax.broadcasted_iota(jnp.int32, xv.shape, 0)
    start = t == 0
    mult = jnp.where(start, 1.0, jnp.sqrt(_neg_expm1(2.0 * la)))
    return xs, xc, xcb, ra, ri, sp, dsp, a, mult, start


def _rnn_specs(t_rows, s, rev):
    nt = s // t_rows
    r = t_rows // HALO
    xb, gb = XR0 // LANES, GR0 // LANES
    ti = (lambda i: nt - 1 - i) if rev else (lambda i: i)
    tile = lambda base: pl.BlockSpec((t_rows, LANES), lambda n, i: (ti(i), base + n))
    prev = lambda base: pl.BlockSpec((HALO, LANES), lambda n, i: (jnp.maximum(ti(i) * r - 1, 0), base + n))
    par = lambda rows: pl.BlockSpec((rows, LANES), lambda n, i: (0, n))
    mat = pl.BlockSpec((None, RNN_BLOCK, RNN_BLOCK), lambda n, i: (n, 0, 0))
    return nt, ti, tile, prev, par, mat, xb, gb


def rnn_forward(proj, cw, cb, wa, ba, wi, bi, lam):
    s = proj.shape[0]
    t_rows = _row_tile(s, 512)
    nt, ti, tile, prev, par, mat, xb, gb = _rnn_specs(t_rows, s, False)

    def body(x_ref, xp_ref, g_ref, cw_ref, cb_ref, wa_ref, ba_ref, wi_ref, bi_ref, lam_ref, z_ref, h_ref, car_ref):
        i = pl.program_id(1)
        first = i == 0
        _, xc, _, _, ri, _, _, a, mult, _ = _rnn_gates(x_ref, xp_ref, cw_ref, cb_ref, wa_ref, ba_ref, wi_ref, bi_ref,
                                                       lam_ref, first, i * t_rows)
        aa, bb = a, mult * ri * xc
        rows = lax.broadcasted_iota(jnp.int32, aa.shape, 0)
        d = 1
        while d < t_rows:
            keep = rows >= d
            a_s, b_s = pltpu.roll(aa, d, axis=0), pltpu.roll(bb, d, axis=0)
            bb = jnp.where(keep, aa * b_s + bb, bb)
            aa = jnp.where(keep, aa * a_s, aa)
            d *= 2
        carry = jnp.where(first, 0.0, car_ref[0:1, :])
        h = aa * carry + bb
        car_ref[...] = jnp.broadcast_to(h[t_rows - 1:t_rows, :], car_ref.shape)
        h_ref[...] = h.astype(BF16)
        z_ref[...] = (h * _gelu(g_ref[...].astype(F32))).astype(BF16)

    o = jax.ShapeDtypeStruct((s, D_RNN), BF16)
    out_tile = pl.BlockSpec((t_rows, LANES), lambda n, i: (i, n))
    return pl.pallas_call(
        body, name="rnn_forward", grid=(N_RNN_BLOCKS, nt),
        in_specs=[tile(xb), prev(xb), tile(gb), par(RNN_CONV), par(1), mat, par(1), mat, par(1), par(1)],
        out_specs=(out_tile, out_tile), out_shape=(o, o), scratch_shapes=[pltpu.VMEM((8, LANES), F32)],
        compiler_params=_cparams(2),
    )(proj, proj, proj, cw, cb, wa, ba, wi, bi, lam)


def rnn_backward(proj, h, dz, cw, cb, wa, ba, wi, bi, lam):
    s = proj.shape[0]
    t_rows = _row_tile(s, 512)
    nt, ti, tile, prev, par, mat, xb, gb = _rnn_specs(t_rows, s, True)
    r = t_rows // HALO

    def body(x_ref, xp_ref, g_ref, h_ref, hp_ref, dz_ref, cw_ref, cb_ref, wa_ref, ba_ref, wi_ref, bi_ref, lam_ref,
             dx_ref, dg_ref, dwa_ref, dwi_ref, sm_ref, gcar_ref, xcar_ref):
        i = pl.program_id(1)
        it = nt - 1 - i
        first, last = it == 0, it == nt - 1
        xs, xc, xcb, ra, ri, sp, dsp, a, mult, start = _rnn_gates(
            x_ref, xp_ref, cw_ref, cb_ref, wa_ref, ba_ref, wi_ref, bi_ref, lam_ref, first, it * t_rows)
        hf = h_ref[...].astype(F32)
        hprev = _shift_down(jnp.where(first, 0.0, hp_ref[...].astype(F32)), hf, 1)
        ge, dge = _gelu_and_grad(g_ref[...].astype(F32))
        dz = dz_ref[...].astype(F32)
        dg_ref[...] = (dz * hf * dge).astype(BF16)
        rows = lax.broadcasted_iota(jnp.int32, hf.shape, 0)
        tail = rows == t_rows - 1
        carry = jnp.where(last, 0.0, gcar_ref[0:1, :])
        bb = dz * ge + jnp.where(tail, carry, 0.0)
        aa = jnp.where(tail, 0.0, pltpu.roll(a, t_rows - 1, axis=0))
        d = 1
        while d < t_rows:
            keep = rows < t_rows - d
            a_s, b_s = pltpu.roll(aa, t_rows - d, axis=0), pltpu.roll(bb, t_rows - d, axis=0)
            bb = jnp.where(keep, bb + aa * b_s, bb)
            aa = jnp.where(keep, aa * a_s, aa)
            d *= 2
        gg = bb
        gcar_ref[...] = jnp.broadcast_to(a[0:1, :] * gg[0:1, :], gcar_ref.shape)
        da = gg * hprev
        dmult = jnp.where(start, 0.0, gg * ri * xc)
        dri = gg * mult * xc
        dxc = gg * mult * ri
        safe_mult = jnp.where(start, 1.0, mult)
        dla = da * a - dmult * (a * a) / safe_mult
        dra = dla * ((-RG_C) * sp)
        dlam = jnp.sum(dla * ((-RG_C) * ra), axis=0, keepdims=True) * dsp
        dpa = dra * ra * (1.0 - ra)
        dpi = dri * ri * (1.0 - ri)
        dpab, dpib = dpa.astype(BF16), dpi.astype(BF16)
        nt_dims = (((1,), (1,)), ((), ()))
        tn_dims = (((0,), (0,)), ((), ()))
        dxc = dxc + lax.dot_general(dpab, wa_ref[...], nt_dims, preferred_element_type=F32) \
            + lax.dot_general(dpib, wi_ref[...], nt_dims, preferred_element_type=F32)
        dwa = lax.dot_general(xcb, dpab, tn_dims, preferred_element_type=F32)
        dwi = lax.dot_general(xcb, dpib, tn_dims, preferred_element_type=F32)
        nxt = jnp.where(last, 0.0, xcar_ref[...])
        dx = cw_ref[RNN_CONV - 1:RNN_CONV, :] * dxc
        for k in range(RNN_CONV - 1):
            dx = dx + cw_ref[k:k + 1, :] * _shift_up(dxc, nxt, RNN_CONV - 1 - k)
        dx_ref[...] = dx.astype(BF16)
        xcar_ref[...] = dxc[0:HALO, :]
        small = jnp.concatenate(
            [jnp.sum(dpa, axis=0, keepdims=True), jnp.sum(dpi, axis=0, keepdims=True), dlam,
             jnp.sum(dxc, axis=0, keepdims=True)]
            + [jnp.sum(dxc * xs[k], axis=0, keepdims=True) for k in range(RNN_CONV)], axis=0)

        @pl.when(i == 0)
        def _():
            dwa_ref[...] = dwa
            dwi_ref[...] = dwi
            sm_ref[...] = small

        @pl.when(i > 0)
        def _():
            dwa_ref[...] += dwa
            dwi_ref[...] += dwi
            sm_ref[...] += small

    o = jax.ShapeDtypeStruct((s, D_RNN), BF16)
    plain = pl.BlockSpec((t_rows, LANES), lambda n, i: (ti(i), n))
    plain_prev = pl.BlockSpec((HALO, LANES), lambda n, i: (jnp.maximum(ti(i) * r - 1, 0), n))
    return pl.pallas_call(
        body, name="rnn_backward", grid=(N_RNN_BLOCKS, nt),
        in_specs=[tile(xb), prev(xb), tile(gb), plain, plain_prev, plain,
                  par(RNN_CONV), par(1), mat, par(1), mat, par(1), par(1)],
        out_specs=(plain, plain, mat, mat, pl.BlockSpec((None, 8, LANES), lambda n, i: (n, 0, 0))),
        out_shape=(o, o, jax.ShapeDtypeStruct((N_RNN_BLOCKS, RNN_BLOCK, RNN_BLOCK), F32),
                   jax.ShapeDtypeStruct((N_RNN_BLOCKS, RNN_BLOCK, RNN_BLOCK), F32),
                   jax.ShapeDtypeStruct((N_RNN_BLOCKS, 8, LANES), F32)),
        scratch_shapes=[pltpu.VMEM((8, LANES), F32), pltpu.VMEM((HALO, LANES), F32)],
        compiler_params=_cparams(2),
    )(proj, proj, proj, h, h, dz, cw, cb, wa, ba, wi, bi, lam)


def adamw(parts, w, m, v, name, rows=256):
    p, r, c = parts.shape
    tr = min(rows, r)
    assert r % tr == 0

    def body(p_ref, w_ref, m_ref, v_ref, g_ref, d_ref, nm_ref, nv_ref):
        g = p_ref[0].astype(F32)
        for q in range(1, p):
            g = g + p_ref[q].astype(F32)
        nm = ADAM_B1 * m_ref[...] + (1.0 - ADAM_B1) * g
        nv = ADAM_B2 * v_ref[...] + (1.0 - ADAM_B2) * (g * g)
        mh = nm / (1.0 - ADAM_B1 ** ADAM_STEP)
        vh = nv / (1.0 - ADAM_B2 ** ADAM_STEP)
        g_ref[...] = g
        d_ref[...] = (-ADAM_LR) * (mh / (jnp.sqrt(vh) + ADAM_EPS) + ADAM_WD * w_ref[...])
        nm_ref[...] = nm
        nv_ref[...] = nv

    pspec = pl.BlockSpec((p, tr, c), lambda i: (0, i, 0))
    spec = pl.BlockSpec((tr, c), lambda i: (i, 0))
    o = jax.ShapeDtypeStruct((r, c), F32)
    return pl.pallas_call(body, name=name, grid=(r // tr,), in_specs=[pspec, spec, spec, spec],
                          out_specs=(spec,) * 4, out_shape=(o, o, o, o), compiler_params=_cparams(1))(parts, w, m, v)


def ada_weight_grad(c_t, dmod):
    d, nb = c_t.shape
    c = dmod.shape[1]
    tr = 512

    def body(c_ref, dm_ref, o_ref):
        cv = c_ref[...]
        cs = cv * _sigmoid(cv)
        acc = cs[:, 0:1] * dm_ref[0:1, :]
        for b in range(1, nb):
            acc = acc + cs[:, b:b + 1] * dm_ref[b:b + 1, :]
        o_ref[...] = acc

    return pl.pallas_call(body, name="ada_weight_grad", grid=(d // tr,),
                          in_specs=[pl.BlockSpec((tr, nb), lambda i: (i, 0)), pl.BlockSpec((nb, c), lambda i: (0, 0))],
                          out_specs=pl.BlockSpec((tr, c), lambda i: (i, 0)),
                          out_shape=jax.ShapeDtypeStruct((d, c), F32), compiler_params=_cparams(1))(c_t, dmod)


def _rows128(a):
    flat = a.reshape(-1).astype(F32)
    pad = (-flat.shape[0]) % (8 * LANES)
    if pad:
        flat = jnp.concatenate([flat, jnp.zeros((pad,), F32)])
    return flat.reshape(-1, LANES)


def kernel(x, c, w_ada, b_ada, norm1, w_in, rnn_conv_w, rnn_conv_b, w_rg_a, b_rg_a, w_rg_i, b_rg_i, rg_lambda, w_o_rnn, w_o_attn, attn_sinks, rel_bias, w_out, norm2, w_up, ffn_conv_w, ffn_conv_b, w_down, norm_f, loss_target, m_w_ada, m_b_ada, m_norm1, m_w_in, m_rnn_conv_w, m_rnn_conv_b, m_w_rg_a, m_b_rg_a, m_w_rg_i, m_b_rg_i, m_rg_lambda, m_w_o_rnn, m_w_o_attn, m_attn_sinks, m_rel_bias, m_w_out, m_norm2, m_w_up, m_ffn_conv_w, m_ffn_conv_b, m_w_down, m_norm_f, v_w_ada, v_b_ada, v_norm1, v_w_in, v_rnn_conv_w, v_rnn_conv_b, v_w_rg_a, v_b_rg_a, v_w_rg_i, v_b_rg_i, v_rg_lambda, v_w_o_rnn, v_w_o_attn, v_attn_sinks, v_rel_bias, v_w_out, v_norm2, v_w_up, v_ffn_conv_w, v_ffn_conv_b, v_w_down, v_norm_f):
    me = 4 * lax.axis_index("x") + 2 * lax.axis_index("y") + lax.axis_index("c")
    xs = x[0]
    tgt = loss_target[0]
    s, d = xs.shape

    mod, c_all = mod_forward(c.reshape(1, 1, d), w_ada[0], b_ada)
    mod = mod.reshape(6, d)
    shift1, scale1, gate1, shift2, scale2, gate2 = [mod[i:i + 1] for i in range(6)]

    gathered = exchange(
        [(w_in[0].astype(BF16), False), (w_up[0].astype(BF16), False), (w_o_rnn[0].astype(BF16), False),
         (w_o_attn[0].astype(BF16), False), (w_out[0].astype(BF16), False), (w_down[0].astype(BF16), False),
         (rnn_conv_w[0], False), (ffn_conv_w[0], False)], "gather_weights")
    w_in_f = jnp.transpose(gathered[0], (1, 0, 2)).reshape(d, D_IN)
    w_up_f = jnp.transpose(gathered[1], (1, 0, 2)).reshape(d, 2 * D_FF)
    w_or_f = gathered[2].reshape(D_RNN, d)
    w_oa_f = gathered[3].reshape(D_ATTN, d)
    w_out_f = gathered[4].reshape(d, d)
    w_down_f = gathered[5].reshape(D_FF, d)
    rcw = jnp.transpose(gathered[6], (1, 0, 2)).reshape(RNN_CONV, D_RNN)
    fcw = jnp.transpose(gathered[7], (1, 0, 2)).reshape(FFN_CONV, 2 * D_FF)
    wa_b, wi_b = w_rg_a[0].astype(BF16), w_rg_i[0].astype(BF16)
    sinks = attn_sinks[0]

    u = prenorm(xs, norm1, scale1, shift1, "prenorm1")
    proj = matmul(u, w_in_f, "nn", BF16, "mm_in", 1024, 512, 2048)
    bias = band_bias(rel_bias)
    att = attention_forward(proj, bias, sinks)
    z, hr = rnn_forward(proj, rcw, rnn_conv_b, wa_b, b_rg_a, wi_b, b_rg_i, rg_lambda)
    y_attn = matmul(att, w_oa_f, "nn", BF16, "mm_o_attn", 1024, 1024, 2048)
    y_rnn = matmul(z, w_or_f, "nn", BF16, "mm_o_rnn", 1024, 1024, 2560)
    merged = merge_forward(proj, y_attn, y_rnn)
    h1, mo = matmul(merged, w_out_f, "nn", BF16, "mm_out", 512, 1024, 2048, res=xs, gate=gate1)
    u2 = prenorm(h1, norm2, scale2, shift2, "prenorm2")
    upp = matmul(u2, w_up_f, "nn", BF16, "mm_up", 1024, 512, 2048)
    act = ffn_act_forward(upp, fcw, ffn_conv_b)
    h2, dn = matmul(act, w_down_f, "nn", BF16, "mm_down", 512, 1024, 2048, res=h1, gate=gate2)

    dh2, d_dn, loss_cols, d_norm_f, d_gate2 = loss_head(h2, tgt, dn, norm_f.reshape(1, d), gate2)
    loss = lax.psum(jnp.sum(loss_cols), ("x", "y", "c"))

    d_act = matmul(d_dn, w_down_f, "nt", BF16, "mm_down_dx", 1024, 1536, 2048)
    g_w_down = matmul(act, d_dn, "tn", BF16, "mm_down_dw", 1536, 1024, 1024)
    d_up, d_fcb, d_fcw = ffn_act_backward(d_act, upp, fcw, ffn_conv_b)
    d_upp = conv_backward(d_up, fcw)
    d_u2 = matmul(d_upp, w_up_f, "nt", BF16, "mm_up_dx", 1024, 1024, 3072)
    g_w_up = matmul(u2, d_upp, "tn", BF16, "mm_up_dw", 2048, 512, 1024)
    dh1, d_shift2, d_scale2, d_norm2, d_mo, d_gate1 = norm_backward(d_u2, h1, dh2, norm2, scale2, "norm2_backward",
                                                                    mo=mo, gate=gate1)
    d_merged = matmul(d_mo, w_out_f, "nt", BF16, "mm_out_dx", 1024, 1024, 2048)
    g_w_out = matmul(merged, d_mo, "tn", BF16, "mm_out_dw", 2048, 1024, 1024)
    d_ga, d_gl, d_ya, d_yr = merge_backward(d_merged, proj, y_attn, y_rnn)
    d_att = matmul(d_ya, w_oa_f, "nt", BF16, "mm_o_attn_dx", 1024, 1024, 2048)
    g_w_oa = matmul(att, d_ya, "tn", BF16, "mm_o_attn_dw", 2048, 1024, 1024)
    d_z = matmul(d_yr, w_or_f, "nt", BF16, "mm_o_rnn_dx", 1024, 1280, 2048)
    g_w_or = matmul(z, d_yr, "tn", BF16, "mm_o_rnn_dw", 1280, 1024, 1024)
    d_xr, d_gr, d_wa, d_wi, d_rsmall = rnn_backward(proj, hr, d_z, rcw, rnn_conv_b, wa_b, b_rg_a, wi_b, b_rg_i, rg_lambda)
    d_q, d_k, d_v, d_bias, d_sink = attention_backward(proj, d_att, bias, sinks)
    d_rel = rel_bias_grad(d_bias)
    d_proj = jnp.concatenate([d_q, d_k, d_v, d_xr, d_gr, d_ga, d_gl], axis=1)
    d_u = matmul(d_proj, w_in_f, "nt", BF16, "mm_in_dx", 1024, 1024, 2944)
    g_w_in = matmul(u, d_proj, "tn", BF16, "mm_in_dw", 2048, 512, 1024)
    grad_x, d_shift1, d_scale1, d_norm1 = norm_backward(d_u, xs, dh1, norm1, scale1, "norm1_backward")

    d_mod = jnp.concatenate([d_shift1, d_scale1, d_gate1, d_shift2, d_scale2, d_gate2], axis=1)
    def small_of(t):
        return [_rows128(q) for q in t]

    def rsmall_of(ba_, bi_, lam_, cb_):
        return jnp.stack([ba_[0].reshape(N_RNN_BLOCKS, LANES), bi_[0].reshape(N_RNN_BLOCKS, LANES),
                          lam_[0].reshape(N_RNN_BLOCKS, LANES), cb_[0].reshape(N_RNN_BLOCKS, LANES)]
                         + [jnp.zeros((N_RNN_BLOCKS, LANES), F32)] * 4, axis=1)

    sink_g = d_sink[:, 0]
    rel_g = d_rel[:, :N_HEADS]
    g_list = [d_mod, d_norm1, d_norm2, d_norm_f, d_rsmall, d_wa, d_wi, sink_g, rel_g, d_fcb, d_fcw]
    w_list = [b_ada, norm1, norm2, norm_f, rsmall_of(b_rg_a, b_rg_i, rg_lambda, rnn_conv_b), w_rg_a, w_rg_i,
              attn_sinks, rel_bias, ffn_conv_b, jnp.zeros_like(d_fcw)]
    m_list = [m_b_ada, m_norm1, m_norm2, m_norm_f, rsmall_of(m_b_rg_a, m_b_rg_i, m_rg_lambda, m_rnn_conv_b), m_w_rg_a,
              m_w_rg_i, m_attn_sinks, m_rel_bias, m_ffn_conv_b, jnp.zeros_like(d_fcw)]
    v_list = [v_b_ada, v_norm1, v_norm2, v_norm_f, rsmall_of(v_b_rg_a, v_b_rg_i, v_rg_lambda, v_rnn_conv_b), v_w_rg_a,
              v_w_rg_i, v_attn_sinks, v_rel_bias, v_ffn_conv_b, jnp.ones_like(d_fcw)]
    sizes = [_rows128(q).shape[0] for q in g_list]
    offs = np.concatenate([[0], np.cumsum(sizes)]).tolist()
    pack = lambda t: jnp.concatenate(small_of(t), axis=0)
    g_all = exchange([(pack(g_list), False)], "gather_small_grads")[0]
    sg, sd, sm_, sv = adamw(g_all, pack(w_list), pack(m_list), pack(v_list), "adamw_small", rows=728)

    def seg(packed, idx, like):
        n_el = int(np.prod(like.shape))
        return packed[offs[idx]:offs[idx + 1]].reshape(-1)[:n_el].reshape(like.shape)

    def unpack(packed):
        rs = seg(packed, 4, d_rsmall)
        out = dict(
            b_ada=seg(packed, 0, b_ada), norm1=seg(packed, 1, norm1), norm2=seg(packed, 2, norm2),
            norm_f=seg(packed, 3, norm_f), b_rg_a=rs[:, 0].reshape(1, D_RNN), b_rg_i=rs[:, 1].reshape(1, D_RNN),
            rg_lambda=rs[:, 2].reshape(1, D_RNN), rnn_conv_b=rs[:, 3].reshape(1, D_RNN),
            w_rg_a=seg(packed, 5, w_rg_a), w_rg_i=seg(packed, 6, w_rg_i), attn_sinks=seg(packed, 7, attn_sinks),
            rel_bias=seg(packed, 8, rel_bias), ffn_conv_b=seg(packed, 9, ffn_conv_b))
        out["rnn_conv_w_full"] = jnp.transpose(rs[:, 4:8], (1, 0, 2)).reshape(RNN_CONV, D_RNN)
        out["ffn_conv_w_full"] = seg(packed, 10, d_fcw)
        return out

    small = [unpack(t) for t in (sg, sd, sm_, sv)]

    rcw_cols = D_RNN // N_DEV
    fcw_cols = 2 * D_FF // N_DEV
    g_rcw = lax.dynamic_slice(small[0]["rnn_conv_w_full"], (0, me * rcw_cols), (RNN_CONV, rcw_cols))
    g_fcw = lax.dynamic_slice(small[0]["ffn_conv_w_full"], (0, me * fcw_cols), (FFN_CONV, fcw_cols))
    r_rcw = adamw(g_rcw[None], rnn_conv_w[0], m_rnn_conv_w[0], v_rnn_conv_w[0], "adamw_rnn_conv_w")
    r_fcw = adamw(g_fcw[None], ffn_conv_w[0], m_ffn_conv_w[0], v_ffn_conv_w[0], "adamw_ffn_conv_w")

    ada_cols = 6 * d // N_DEV
    dmod_all = g_all[:, offs[0]:offs[1]].reshape(N_DEV, 6 * d)
    dmod_cols = lax.dynamic_slice(dmod_all, (0, me * ada_cols), (N_DEV, ada_cols))
    g_ada = ada_weight_grad(jnp.transpose(c_all.reshape(N_DEV, d)), dmod_cols)
    r_ada = adamw(g_ada[None], w_ada[0], m_w_ada[0], v_w_ada[0], "adamw_w_ada")

    g_in_blk = jnp.transpose(g_w_in.reshape(d, N_DEV, D_IN // N_DEV), (1, 0, 2))
    g_up_blk = jnp.transpose(g_w_up.reshape(d, N_DEV, 2 * D_FF // N_DEV), (1, 0, 2))
    parts = exchange(
        [(g_in_blk, True), (g_up_blk, True), (g_w_or.reshape(N_DEV, D_RNN // N_DEV, d), True),
         (g_w_oa.reshape(N_DEV, D_ATTN // N_DEV, d), True), (g_w_out.reshape(N_DEV, d // N_DEV, d), True),
         (g_w_down.reshape(N_DEV, D_FF // N_DEV, d), True)], "scatter_grads")
    r_in = adamw(parts[0], w_in[0], m_w_in[0], v_w_in[0], "adamw_w_in")
    r_up = adamw(parts[1], w_up[0], m_w_up[0], v_w_up[0], "adamw_w_up")
    r_or = adamw(parts[2], w_o_rnn[0], m_w_o_rnn[0], v_w_o_rnn[0], "adamw_w_o_rnn", rows=160)
    r_oa = adamw(parts[3], w_o_attn[0], m_w_o_attn[0], v_w_o_attn[0], "adamw_w_o_attn")
    r_out = adamw(parts[4], w_out[0], m_w_out[0], v_w_out[0], "adamw_w_out")
    r_down = adamw(parts[5], w_down[0], m_w_down[0], v_w_down[0], "adamw_w_down")

    def res(kind):
        sm = small[kind]
        return [r_ada[kind][None], sm["b_ada"], sm["norm1"], r_in[kind][None], r_rcw[kind][None], sm["rnn_conv_b"],
                sm["w_rg_a"], sm["b_rg_a"], sm["w_rg_i"], sm["b_rg_i"], sm["rg_lambda"], r_or[kind][None],
                r_oa[kind][None], sm["attn_sinks"], sm["rel_bias"], r_out[kind][None], sm["norm2"], r_up[kind][None],
                r_fcw[kind][None], sm["ffn_conv_b"], r_down[kind][None], sm["norm_f"]]

    return (loss, grad_x[None], *res(0), *res(1), *res(2), *res(3))
```

```python
import functools
import math

import numpy as np
import jax
import jax.numpy as jnp
from jax import lax
from jax.experimental import pallas as pl
from jax.experimental.pallas import tpu as pltpu

F32, BF16 = jnp.float32, jnp.bfloat16

N_DEV = 8
D_MODEL = 2048
N_HEADS, HEAD_DIM, N_KV = 32, 64, 4
GROUP = N_HEADS // N_KV
D_ATTN, D_KV = N_HEADS * HEAD_DIM, N_KV * HEAD_DIM
BLOCK = 128
NUM_BUCKETS, MAX_DISTANCE = 32, 128
D_RNN, N_RNN_BLOCKS, RNN_BLOCK = 2560, 20, 128
RNN_CONV, FFN_CONV = 4, 3
RG_C = 8.0
D_FF = 3 * D_MODEL
D_IN = D_ATTN + 2 * D_KV + 2 * D_RNN + 2 * D_MODEL
EPS = 1e-6
NEG_INF = -1e30
ADAM_LR, ADAM_B1, ADAM_B2, ADAM_EPS, ADAM_WD, ADAM_STEP = 0.001, 0.9, 0.999, 1e-08, 0.01, 10

LANES = 128
HALO = 16
VMEM_LIMIT = 56 * 1024 * 1024
MESH = pl.DeviceIdType.MESH

Q0, K0, V0, XR0, GR0, GA0, GL0 = 0, 2048, 2304, 2560, 5120, 7680, 9728


def _cparams(n_axes):
    return pltpu.CompilerParams(dimension_semantics=("arbitrary",) * n_axes, vmem_limit_bytes=VMEM_LIMIT)


def _gelu(x):
    k = math.sqrt(2.0 / math.pi)
    return 0.5 * x * (1.0 + jnp.tanh(k * (x + 0.044715 * x * x * x)))


def _gelu_and_grad(x):
    k = math.sqrt(2.0 / math.pi)
    t = jnp.tanh(k * (x + 0.044715 * x * x * x))
    g = 0.5 * x * (1.0 + t)
    dg = 0.5 * (1.0 + t) + 0.5 * x * (1.0 - t * t) * k * (1.0 + 3.0 * 0.044715 * x * x)
    return g, dg


def _sigmoid(x):
    return 1.0 / (1.0 + jnp.exp(-x))


def _shift_down(prev, x, j):
    if j == 0:
        return x
    xe = jnp.concatenate([prev, x], axis=0)
    return pltpu.roll(xe, j, axis=0)[HALO:, :]


def _shift_up(x, nxt, j):
    if j == 0:
        return x
    xe = jnp.concatenate([x, nxt], axis=0)
    n = xe.shape[0]
    return pltpu.roll(xe, n - j, axis=0)[: x.shape[0], :]


def _my_coords():
    return lax.axis_index("x"), lax.axis_index("y"), lax.axis_index("c")


def _peer(x, y, c, k):
    kx, ky, kc = (k >> 2) & 1, (k >> 1) & 1, k & 1
    px, py, pc = (x + kx) % 2, (y + ky) % 2, (c + kc) % 2
    return (px, py, pc), 4 * px + 2 * py + pc


def _exchange_shapes(items):
    return [jax.ShapeDtypeStruct((N_DEV,) + tuple(arr.shape[1:] if s else arr.shape), arr.dtype) for arr, s in items]


def _exchange_sems(n):
    return [pltpu.SemaphoreType.DMA((n, N_DEV - 1)), pltpu.SemaphoreType.DMA((n, N_DEV - 1)),
            pltpu.SemaphoreType.DMA((n,))]


def _exchange_copies(srcs, dsts, scat, send_sems, recv_sems, loc_sems):
    x, y, c = _my_coords()
    me = 4 * x + 2 * y + c
    copies = []
    for a in range(len(srcs)):
        mine = srcs[a].at[me] if scat[a] else srcs[a]
        copies.append(pltpu.make_async_copy(mine, dsts[a].at[me], loc_sems.at[a]))
    for k in range(1, N_DEV):
        peer, p = _peer(x, y, c, k)
        for a in range(len(srcs)):
            src = srcs[a].at[p] if scat[a] else srcs[a]
            copies.append(pltpu.make_async_remote_copy(
                src_ref=src, dst_ref=dsts[a].at[me], send_sem=send_sems.at[a, k - 1],
                recv_sem=recv_sems.at[a, k - 1], device_id=peer, device_id_type=MESH))
    return copies


def exchange(items, name):
    n = len(items)
    scat = [s for _, s in items]

    def body(*refs):
        copies = _exchange_copies(refs[:n], refs[n:2 * n], scat, *refs[2 * n:])
        for cp in copies:
            cp.start()
        for cp in copies:
            cp.wait()

    any_spec = pl.BlockSpec(memory_space=pl.ANY)
    return pl.pallas_call(
        body, name=name, out_shape=tuple(_exchange_shapes(items)),
        in_specs=[any_spec] * n, out_specs=tuple([any_spec] * n), scratch_shapes=_exchange_sems(n),
    )(*[a for a, _ in items])


def mod_forward(c, w_ada, b_ada):
    d, ncol = w_ada.shape

    def body(c_ref, w_ref, b_ref, mod_ref, call_ref, cols_ref, s1, r1, s2, r2):
        x, y, c_ = _my_coords()
        me = 4 * x + 2 * y + c_
        call_ref[me] = c_ref[0]
        sends = []
        for k in range(1, N_DEV):
            peer, p = _peer(x, y, c_, k)
            cp = pltpu.make_async_remote_copy(src_ref=c_ref.at[0], dst_ref=call_ref.at[me], send_sem=s1.at[k - 1],
                                              recv_sem=r1.at[k - 1], device_id=peer, device_id_type=MESH)
            cp.start()
            sends.append(cp)
        for cp in sends:
            cp.wait()
        rows = lax.broadcasted_iota(jnp.int32, (N_DEV, d), 0)
        cmat = jnp.zeros((N_DEV, d), F32)
        for b in range(N_DEV):
            cmat = jnp.where(rows == b, call_ref[b], cmat)
        cs = cmat * _sigmoid(cmat)
        bias = b_ref[:, pl.ds(pl.multiple_of(me * ncol, LANES), ncol)]
        mc = jnp.dot(cs, w_ref[...], preferred_element_type=F32, precision=lax.Precision.HIGHEST) + bias
        for b in range(N_DEV):
            cols_ref[b] = mc[b:b + 1, :]
        mod_ref[me] = cols_ref[me]
        sends = []
        for k in range(1, N_DEV):
            peer, p = _peer(x, y, c_, k)
            cp = pltpu.make_async_remote_copy(src_ref=cols_ref.at[p], dst_ref=mod_ref.at[me], send_sem=s2.at[k - 1],
                                              recv_sem=r2.at[k - 1], device_id=peer, device_id_type=MESH)
            cp.start()
            sends.append(cp)
        for cp in sends:
            cp.wait()

    vm = pl.BlockSpec(memory_space=pltpu.VMEM)
    return pl.pallas_call(
        body, name="mod_forward",
        out_shape=(jax.ShapeDtypeStruct((N_DEV, 1, ncol), F32), jax.ShapeDtypeStruct((N_DEV, 1, d), F32)),
        in_specs=[vm, vm, vm], out_specs=(vm, vm),
        scratch_shapes=[pltpu.VMEM((N_DEV, 1, ncol), F32)] + [pltpu.SemaphoreType.DMA((N_DEV - 1,))] * 4,
        compiler_params=pltpu.CompilerParams(vmem_limit_bytes=VMEM_LIMIT),
    )(c, w_ada, b_ada)


def matmul(a, b, mode, out_dtype, name, tm, tn, tk, res=None, gate=None, comm=None):
    if mode == "nn":
        (m, kk), (_, n) = a.shape, b.shape
    elif mode == "nt":
        (m, kk), (n, _) = a.shape, b.shape
    else:
        (kk, m), (_, n) = a.shape, b.shape
    tm, tn, tk = min(tm, m), min(tn, n), min(tk, kk)
    assert m % tm == 0 and n % tn == 0 and kk % tk == 0, (name, m, n, kk, tm, tn, tk)
    if mode == "nn":
        a_spec = pl.BlockSpec((tm, tk), lambda j, i, k: (i, k))
        b_spec = pl.BlockSpec((tk, tn), lambda j, i, k: (k, j))
        dims = (((1,), (0,)), ((), ()))
    elif mode == "nt":
        a_spec = pl.BlockSpec((tm, tk), lambda j, i, k: (i, k))
        b_spec = pl.BlockSpec((tn, tk), lambda j, i, k: (j, k))
        dims = (((1,), (1,)), ((), ()))
    else:
        a_spec = pl.BlockSpec((tk, tm), lambda j, i, k: (k, i))
        b_spec = pl.BlockSpec((tk, tn), lambda j, i, k: (k, j))
        dims = (((0,), (0,)), ((), ()))
    nj, ni, nk = n // tn, m // tm, kk // tk
    fused = res is not None
    items = list(comm or [])
    nc = len(items)
    scat = [s_ for _, s_ in items]
    n_in = (4 if fused else 2) + nc
    n_out = (2 if fused else 1) + nc
    o_spec = pl.BlockSpec((tm, tn), lambda j, i, k: (i, j))

    def body(*refs):
        ins, outs, scratch = refs[:n_in], refs[n_in:n_in + n_out], refs[n_in + n_out:]
        a_ref, b_ref = ins[:2]
        o_ref = outs[0]
        acc_ref = scratch[0] if nk > 1 else None
        j, i, k = pl.program_id(0), pl.program_id(1), pl.program_id(2)

        def copies():
            return _exchange_copies(ins[n_in - nc:], outs[n_out - nc:], scat, *scratch[len(scratch) - 3:])

        if nc:
            @pl.when((j == 0) & (i == 0) & (k == 0))
            def _():
                for cp in copies():
                    cp.start()

        def finish(acc):
            if fused:
                o_ref[...] = ins[2][...] + ins[3][...] * acc
                outs[1][...] = acc.astype(outs[1].dtype)
            else:
                o_ref[...] = acc.astype(o_ref.dtype)

        prod = lax.dot_general(a_ref[...], b_ref[...], dims, preferred_element_type=F32)
        if nk == 1:
            finish(prod)
        else:
            @pl.when(k == 0)
            def _():
                acc_ref[...] = prod

            @pl.when(k > 0)
            def _():
                acc_ref[...] += prod

            @pl.when(k == nk - 1)
            def _():
                finish(acc_ref[...])

        if nc:
            @pl.when((j == nj - 1) & (i == ni - 1) & (k == nk - 1))
            def _():
                for cp in copies():
                    cp.wait()

    any_spec = pl.BlockSpec(memory_space=pl.ANY)
    in_specs, args = [a_spec, b_spec], [a, b]
    if fused:
        in_specs += [o_spec, pl.BlockSpec((1, tn), lambda j, i, k: (0, j))]
        args += [res, gate]
        out_shape = [jax.ShapeDtypeStruct((m, n), F32), jax.ShapeDtypeStruct((m, n), out_dtype)]
        out_specs = [o_spec, o_spec]
    else:
        out_shape = [jax.ShapeDtypeStruct((m, n), out_dtype)]
        out_specs = [o_spec]
    in_specs += [any_spec] * nc
    args += [arr for arr, _ in items]
    out_shape += _exchange_shapes(items)
    out_specs += [any_spec] * nc
    scratch = ([pltpu.VMEM((tm, tn), F32)] if nk > 1 else []) + (_exchange_sems(nc) if nc else [])
    outs = pl.pallas_call(
        body, name=name, grid=(nj, ni, nk), in_specs=in_specs, out_specs=tuple(out_specs), out_shape=tuple(out_shape),
        scratch_shapes=scratch, compiler_params=_cparams(3),
    )(*args)
    return outs[0] if len(outs) == 1 else outs


def _row_tile(s, want):
    t = min(want, s)
    assert s % t == 0 and t % HALO == 0
    return t


def prenorm(x, nw, scale, shift, name):
    s, d = x.shape
    tm = _row_tile(s, 512)

    def body(x_ref, nw_ref, sc_ref, sh_ref, o_ref):
        xv = x_ref[...]
        r = lax.rsqrt(jnp.mean(xv * xv, axis=-1, keepdims=True) + EPS)
        o_ref[...] = ((xv * r) * nw_ref[...] * (1.0 + sc_ref[...]) + sh_ref[...]).astype(BF16)

    row = pl.BlockSpec((tm, d), lambda i: (i, 0))
    vec = pl.BlockSpec((1, d), lambda i: (0, 0))
    return pl.pallas_call(body, name=name, grid=(s // tm,), in_specs=[row, vec, vec, vec], out_specs=row,
                          out_shape=jax.ShapeDtypeStruct((s, d), BF16), compiler_params=_cparams(1))(x, nw, scale, shift)


def norm_backward(du, xin, dres, nw, scale, name, mo=None, gate=None):
    s, d = xin.shape
    tm = _row_tile(s, 256)
    gated = mo is not None

    def body(*refs):
        if gated:
            du_ref, x_ref, dr_ref, nw_ref, sc_ref, mo_ref, g_ref, dx_ref, dsh_ref, dsc_ref, dnw_ref, dmo_ref, dg_ref = refs
        else:
            du_ref, x_ref, dr_ref, nw_ref, sc_ref, dx_ref, dsh_ref, dsc_ref, dnw_ref = refs
        i = pl.program_id(0)
        xv = x_ref[...]
        r = lax.rsqrt(jnp.mean(xv * xv, axis=-1, keepdims=True) + EPS)
        xn = xv * r
        duv = du_ref[...].astype(F32)
        nwv, scv = nw_ref[...], sc_ref[...]
        dxn = duv * (nwv * (1.0 + scv))
        dx = dr_ref[...] + r * (dxn - xn * jnp.mean(dxn * xn, axis=-1, keepdims=True))
        dx_ref[...] = dx
        sums = [jnp.sum(duv, axis=0, keepdims=True), jnp.sum(duv * xn * nwv, axis=0, keepdims=True),
                jnp.sum(duv * xn * (1.0 + scv), axis=0, keepdims=True)]
        accs = [dsh_ref, dsc_ref, dnw_ref]
        if gated:
            dmo_ref[...] = (dx * g_ref[...]).astype(BF16)
            sums.append(jnp.sum(dx * mo_ref[...].astype(F32), axis=0, keepdims=True))
            accs.append(dg_ref)

        @pl.when(i == 0)
        def _():
            for acc, sm in zip(accs, sums):
                acc[...] = sm

        @pl.when(i > 0)
        def _():
            for acc, sm in zip(accs, sums):
                acc[...] += sm

    row = pl.BlockSpec((tm, d), lambda i: (i, 0))
    vec = pl.BlockSpec((1, d), lambda i: (0, 0))
    vshape = jax.ShapeDtypeStruct((1, d), F32)
    in_specs, args = [row, row, row, vec, vec], [du, xin, dres, nw, scale]
    out_specs, out_shape = [row, vec, vec, vec], [jax.ShapeDtypeStruct((s, d), F32), vshape, vshape, vshape]
    if gated:
        in_specs += [row, vec]
        args += [mo, gate]
        out_specs += [row, vec]
        out_shape += [jax.ShapeDtypeStruct((s, d), BF16), vshape]
    return pl.pallas_call(body, name=name, grid=(s // tm,), in_specs=in_specs, out_specs=tuple(out_specs),
                          out_shape=tuple(out_shape), compiler_params=_cparams(1))(*args)


def merge_forward(proj, y_attn, y_rnn):
    s, d = y_attn.shape
    tm, cw = _row_tile(s, 1024), 512

    def body(ga_ref, gl_ref, ya_ref, yr_ref, o_ref):
        o_ref[...] = (_sigmoid(ga_ref[...].astype(F32)) * ya_ref[...].astype(F32)
                      + _sigmoid(gl_ref[...].astype(F32)) * yr_ref[...].astype(F32)).astype(BF16)

    def at(off):
        return pl.BlockSpec((tm, cw), lambda j, i: (i, off // cw + j))

    return pl.pallas_call(body, name="merge_forward", grid=(d // cw, s // tm),
                          in_specs=[at(GA0), at(GL0), at(0), at(0)], out_specs=at(0),
                          out_shape=jax.ShapeDtypeStruct((s, d), BF16), compiler_params=_cparams(2))(proj, proj, y_attn, y_rnn)


def merge_backward(dmerged, proj, y_attn, y_rnn):
    s, d = y_attn.shape
    tm, cw = _row_tile(s, 1024), 512

    def body(dm_ref, ga_ref, gl_ref, ya_ref, yr_ref, dga_ref, dgl_ref, dya_ref, dyr_ref):
        dm = dm_ref[...].astype(F32)
        sa, sl = _sigmoid(ga_ref[...].astype(F32)), _sigmoid(gl_ref[...].astype(F32))
        dga_ref[...] = (dm * ya_ref[...].astype(F32) * sa * (1.0 - sa)).astype(BF16)
        dgl_ref[...] = (dm * yr_ref[...].astype(F32) * sl * (1.0 - sl)).astype(BF16)
        dya_ref[...] = (dm * sa).astype(BF16)
        dyr_ref[...] = (dm * sl).astype(BF16)

    def at(off):
        return pl.BlockSpec((tm, cw), lambda j, i: (i, off // cw + j))

    o = jax.ShapeDtypeStruct((s, d), BF16)
    return pl.pallas_call(body, name="merge_backward", grid=(d // cw, s // tm),
                          in_specs=[at(0), at(GA0), at(GL0), at(0), at(0)], out_specs=(at(0),) * 4,
                          out_shape=(o, o, o, o), compiler_params=_cparams(2))(dmerged, proj, proj, y_attn, y_rnn)


def _prev_spec(tm, cw, off_blocks):
    r = tm // HALO
    return pl.BlockSpec((HALO, cw), lambda j, i: (jnp.maximum(i * r - 1, 0), off_blocks + j))


def ffn_act_forward(upp, cw_full, cb_full):
    s, f2 = upp.shape
    f = f2 // 2
    tm, cw = _row_tile(s, 512), 1536
    nj = f // cw

    def body(g_ref, gp_ref, v_ref, vp_ref, wg_ref, wv_ref, bg_ref, bv_ref, o_ref, og_ref, ov_ref):
        i = pl.program_id(1)
        first = i == 0

        def conv(x_ref, p_ref, w_ref, b_ref):
            xv = x_ref[...].astype(F32)
            pv = jnp.where(first, 0.0, p_ref[...].astype(F32))
            acc = b_ref[...] + w_ref[FFN_CONV - 1:FFN_CONV, :] * xv
            for k in range(FFN_CONV - 1):
                acc = acc + w_ref[k:k + 1, :] * _shift_down(pv, xv, FFN_CONV - 1 - k)
            return acc

        g = conv(g_ref, gp_ref, wg_ref, bg_ref)
        v = conv(v_ref, vp_ref, wv_ref, bv_ref)
        og_ref[...] = g.astype(BF16)
        ov_ref[...] = v.astype(BF16)
        o_ref[...] = (_gelu(g) * v).astype(BF16)

    def tile(ob):
        return pl.BlockSpec((tm, cw), lambda j, i: (i, ob + j))

    def par(rows, ob):
        return pl.BlockSpec((rows, cw), lambda j, i: (0, ob + j))

    o = jax.ShapeDtypeStruct((s, f), BF16)
    return pl.pallas_call(
        body, name="ffn_act_forward", grid=(nj, s // tm),
        in_specs=[tile(0), _prev_spec(tm, cw, 0), tile(nj), _prev_spec(tm, cw, nj),
                  par(FFN_CONV, 0), par(FFN_CONV, nj), par(1, 0), par(1, nj)],
        out_specs=(tile(0), tile(0), tile(0)), out_shape=(o, o, o), compiler_params=_cparams(2),
    )(upp, upp, upp, upp, cw_full, cw_full, cb_full, cb_full)


ROW_CHUNK = 16
LANE_CHUNK = 512


def ffn_act_backward(dact, up_g, up_v):
    s, f = dact.shape
    tm, cw = _row_tile(s, 512), 1536
    nr = tm // ROW_CHUNK

    def body(da_ref, g_ref, v_ref, dg_ref, dv_ref, sg_ref, sv_ref, acc_ref):
        i = pl.program_id(1)
        acc_ref[...] = jnp.zeros_like(acc_ref)

        def chunk(r, carry):
            rows = pl.ds(pl.multiple_of(r * ROW_CHUNK, ROW_CHUNK), ROW_CHUNK)
            for c0 in range(0, cw, LANE_CHUNK):
                cols = pl.ds(c0, LANE_CHUNK)
                da = da_ref[rows, cols].astype(F32)
                ge, dge = _gelu_and_grad(g_ref[rows, cols].astype(F32))
                dg = da * v_ref[rows, cols].astype(F32) * dge
                dv = da * ge
                dg_ref[rows, cols] = dg.astype(BF16)
                dv_ref[rows, cols] = dv.astype(BF16)
                acc_ref[0, :, cols] += dg[0:8] + dg[8:16]
                acc_ref[1, :, cols] += dv[0:8] + dv[8:16]
            return carry

        lax.fori_loop(0, nr, chunk, 0)
        sg = jnp.sum(acc_ref[0], axis=0, keepdims=True)
        sv = jnp.sum(acc_ref[1], axis=0, keepdims=True)

        @pl.when(i == 0)
        def _():
            sg_ref[...] = sg
            sv_ref[...] = sv

        @pl.when(i > 0)
        def _():
            sg_ref[...] += sg
            sv_ref[...] += sv

    tile = pl.BlockSpec((tm, cw), lambda j, i: (i, j))
    vec = pl.BlockSpec((1, cw), lambda j, i: (0, j))
    o = jax.ShapeDtypeStruct((s, f), BF16)
    v1 = jax.ShapeDtypeStruct((1, f), F32)
    return pl.pallas_call(
        body, name="ffn_act_backward", grid=(f // cw, s // tm), in_specs=[tile, tile, tile],
        out_specs=(tile, tile, vec, vec), out_shape=(o, o, v1, v1),
        scratch_shapes=[pltpu.VMEM((2, 8, cw), F32)], compiler_params=_cparams(2),
    )(dact, up_g, up_v)


def conv_backward(d_g, d_v, upp, cw_full):
    s, f = d_g.shape
    f2 = 2 * f
    tm, cw = _row_tile(s, 512), 1536
    nt, nj = s // tm, f // cw
    r_halo = tm // HALO
    nr = tm // ROW_CHUNK

    def body(xg_ref, ng_ref, xv_ref, nv_ref, u_ref, w_ref, o_ref, dw_ref, acc_ref):
        j, i = pl.program_id(0), pl.program_id(1)
        acc_ref[...] = jnp.zeros_like(acc_ref)

        def run(x_ref, n_ref):
            def chunk(r, carry):
                rows = pl.ds(pl.multiple_of(r * ROW_CHUNK, ROW_CHUNK), ROW_CHUNK)
                nrows = pl.ds(pl.multiple_of(jnp.minimum(r + 1, nr - 1) * ROW_CHUNK, ROW_CHUNK), ROW_CHUNK)
                for c0 in range(0, cw, LANE_CHUNK):
                    cols = pl.ds(c0, LANE_CHUNK)
                    cur = x_ref[rows, cols].astype(F32)
                    halo = jnp.where(i == nt - 1, 0.0, n_ref[:, cols].astype(F32))
                    nxt = jnp.where(r == nr - 1, halo, x_ref[nrows, cols].astype(F32))
                    uv = u_ref[rows, cols].astype(F32)
                    acc = None
                    for k in range(FFN_CONV):
                        sh = _shift_up(cur, nxt, FFN_CONV - 1 - k)
                        term = w_ref[k:k + 1, cols] * sh
                        acc = term if acc is None else acc + term
                        pr = uv * sh
                        acc_ref[k, :, cols] += pr[0:8] + pr[8:16]
                    o_ref[rows, cols] = acc.astype(BF16)
                return carry

            lax.fori_loop(0, nr, chunk, 0)

        @pl.when(j < nj)
        def _():
            run(xg_ref, ng_ref)

        @pl.when(j >= nj)
        def _():
            run(xv_ref, nv_ref)

        sums = jnp.concatenate([jnp.sum(acc_ref[k], axis=0, keepdims=True) for k in range(FFN_CONV)], axis=0)

        @pl.when(i == 0)
        def _():
            dw_ref[...] = sums

        @pl.when(i > 0)
        def _():
            dw_ref[...] += sums

    half = lambda j: j % nj
    tile_h = pl.BlockSpec((tm, cw), lambda j, i: (i, half(j)))
    next_h = pl.BlockSpec((HALO, cw), lambda j, i: (jnp.minimum((i + 1) * r_halo, s // HALO - 1), half(j)))
    tile = pl.BlockSpec((tm, cw), lambda j, i: (i, j))
    par = pl.BlockSpec((FFN_CONV, cw), lambda j, i: (0, j))
    return pl.pallas_call(
        body, name="ffn_conv_backward", grid=(2 * nj, nt), in_specs=[tile_h, next_h, tile_h, next_h, tile, par],
        out_specs=(tile, par), out_shape=(jax.ShapeDtypeStruct((s, f2), BF16), jax.ShapeDtypeStruct((FFN_CONV, f2), F32)),
        scratch_shapes=[pltpu.VMEM((FFN_CONV, 8, cw), F32)], compiler_params=_cparams(2),
    )(d_g, d_g, d_v, d_v, upp, cw_full)


def loss_head(h2, target, dn, norm_f, gate2):
    s, d = h2.shape
    tm = _row_tile(s, 256)

    def body(h_ref, t_ref, dn_ref, nf_ref, g_ref, dh_ref, ddn_ref, loss_ref, dnf_ref, dg_ref):
        i = pl.program_id(0)
        hv = h_ref[...]
        r = lax.rsqrt(jnp.mean(hv * hv, axis=-1, keepdims=True) + EPS)
        yh = hv * r
        nf = nf_ref[...]
        err = yh * nf - t_ref[...]
        dy = err * (1.0 / d)
        dyh = dy * nf
        dh = r * (dyh - yh * jnp.mean(dyh * yh, axis=-1, keepdims=True))
        dh_ref[...] = dh
        ddn_ref[...] = (dh * g_ref[...]).astype(BF16)
        sums = [jnp.sum(err * err, axis=0, keepdims=True) * (0.5 / d), jnp.sum(dy * yh, axis=0, keepdims=True),
                jnp.sum(dh * dn_ref[...].astype(F32), axis=0, keepdims=True)]
        accs = [loss_ref, dnf_ref, dg_ref]

        @pl.when(i == 0)
        def _():
            for acc, sm in zip(accs, sums):
                acc[...] = sm

        @pl.when(i > 0)
        def _():
            for acc, sm in zip(accs, sums):
                acc[...] += sm

    row = pl.BlockSpec((tm, d), lambda i: (i, 0))
    vec = pl.BlockSpec((1, d), lambda i: (0, 0))
    v = jax.ShapeDtypeStruct((1, d), F32)
    return pl.pallas_call(
        body, name="loss_head", grid=(s // tm,), in_specs=[row, row, row, vec, vec], out_specs=(row, row, vec, vec, vec),
        out_shape=(jax.ShapeDtypeStruct((s, d), F32), jax.ShapeDtypeStruct((s, d), BF16), v, v, v),
        compiler_params=_cparams(1))(h2, target, dn, norm_f, gate2)


def _t5_buckets():
    qi = np.arange(BLOCK)[:, None]
    kj = np.arange(2 * BLOCK)[None, :]
    dist = qi + BLOCK - kj
    dd = np.maximum(dist, 0)
    max_exact = NUM_BUCKETS // 2
    dflt = np.maximum(dd, 1).astype(np.float32)
    large = max_exact + (np.log(dflt / max_exact) / math.log(MAX_DISTANCE / max_exact)
                         * (NUM_BUCKETS - max_exact)).astype(np.int32)
    large = np.minimum(large, NUM_BUCKETS - 1)
    bucket = np.where(dd < max_exact, dd, large).astype(np.int32)
    in_window = (dist >= 0) & (dist < BLOCK)
    return bucket, in_window


def band_bias(rel_bias):
    bucket, in_window = _t5_buckets()
    bucket = jnp.asarray(np.where(in_window, bucket, -1).astype(np.int32))

    def body(rb_ref, bk_ref, o_ref):
        bk = bk_ref[...]
        for h in range(N_HEADS):
            acc = jnp.full((BLOCK, 2 * BLOCK), NEG_INF, F32)
            for b in range(NUM_BUCKETS):
                acc = jnp.where(bk == b, rb_ref[b, h], acc)
            o_ref[h] = acc

    return pl.pallas_call(
        body, name="band_bias", out_shape=jax.ShapeDtypeStruct((N_HEADS, BLOCK, 2 * BLOCK), F32),
        in_specs=[pl.BlockSpec(memory_space=pltpu.SMEM), pl.BlockSpec(memory_space=pltpu.VMEM)],
        out_specs=pl.BlockSpec(memory_space=pltpu.VMEM))(rel_bias, bucket)


def rel_bias_grad(dbias):
    bucket, in_window = _t5_buckets()
    bucket = jnp.asarray(np.where(in_window, bucket, -1).astype(np.int32))

    def body(db_ref, bk_ref, o_ref):
        bk = bk_ref[...]
        rows = lax.broadcasted_iota(jnp.int32, (NUM_BUCKETS, LANES), 0)
        lanes = lax.broadcasted_iota(jnp.int32, (NUM_BUCKETS, LANES), 1)
        acc = jnp.zeros((NUM_BUCKETS, LANES), F32)
        for h in range(N_HEADS):
            dv = db_ref[h]
            for b in range(NUM_BUCKETS):
                sm = jnp.sum(jnp.where(bk == b, dv, 0.0))
                acc = jnp.where((rows == b) & (lanes == h), sm, acc)
        o_ref[...] = acc

    vm = pl.BlockSpec(memory_space=pltpu.VMEM)
    return pl.pallas_call(body, name="rel_bias_grad", out_shape=jax.ShapeDtypeStruct((NUM_BUCKETS, LANES), F32),
                          in_specs=[vm, vm], out_specs=vm)(dbias, bucket)


HP = 2
Q_PER_HP = D_ATTN // HP
H_PER_HP = N_HEADS // HP


def _attn_setup(hh, q_ref, kp_ref, kc_ref, vp_ref, vc_ref, bias_ref, sink_ref, hp, n):
    lane = lax.broadcasted_iota(jnp.int32, (2 * BLOCK, LANES), 1)
    own = (lane >= HEAD_DIM) if hh == 1 else (lane < HEAD_DIM)
    kband = jnp.concatenate([kp_ref[...], kc_ref[...]], axis=0)
    vband = jnp.concatenate([vp_ref[...], vc_ref[...]], axis=0)
    kk = jnp.where(own, kband, pltpu.roll(kband, HEAD_DIM, axis=1))
    vv = jnp.where(own, vband, pltpu.roll(vband, HEAD_DIM, axis=1))
    qlane = lax.broadcasted_iota(jnp.int32, (BLOCK, LANES), 1)
    lo = qlane < HEAD_DIM
    parts = []
    for s_ in range(GROUP // 2):
        slab = q_ref[:, hh * (GROUP * HEAD_DIM) + s_ * LANES: hh * (GROUP * HEAD_DIM) + (s_ + 1) * LANES]
        parts.append(jnp.where(lo, slab, jnp.zeros_like(slab)))
        parts.append(jnp.where(lo, jnp.zeros_like(slab), slab))
    qs = jnp.concatenate(parts, axis=0)
    sc = lax.dot_general(qs, kk, (((1,), (1,)), ((), ())), preferred_element_type=F32) * (HEAD_DIM ** -0.5)
    sc = sc + jnp.concatenate([bias_ref[hh * GROUP + g] for g in range(GROUP)], axis=0)
    kcol = lax.broadcasted_iota(jnp.int32, sc.shape, 1)
    sc = jnp.where((n == 0) & (kcol < BLOCK), NEG_INF, sc)
    sink = jnp.concatenate([jnp.full((BLOCK, 1), sink_ref[hp * H_PER_HP + hh * GROUP + g], F32) for g in range(GROUP)], axis=0)
    m = jnp.maximum(jnp.max(sc, axis=-1, keepdims=True), sink)
    p = jnp.exp(sc - m)
    es = jnp.exp(sink - m)
    inv = 1.0 / (jnp.sum(p, axis=-1, keepdims=True) + es)
    return qs, kk, vv, p * inv, es * inv


def _unstack(o, dtype):
    lane = lax.broadcasted_iota(jnp.int32, (BLOCK, LANES), 1)
    lo = lane < HEAD_DIM
    slabs = []
    for s_ in range(GROUP // 2):
        ev = o[(2 * s_) * BLOCK:(2 * s_ + 1) * BLOCK]
        od = o[(2 * s_ + 1) * BLOCK:(2 * s_ + 2) * BLOCK]
        slabs.append(jnp.where(lo, ev, od).astype(dtype))
    return slabs


def attention_forward(proj, bias, sinks, comm):
    s = proj.shape[0]
    nb = s // BLOCK
    kb, vb = K0 // LANES, V0 // LANES
    nc = len(comm)
    scat = [s_ for _, s_ in comm]

    def body(*refs):
        q_ref, kp_ref, kc_ref, vp_ref, vc_ref, bias_ref, sink_ref = refs[:7]
        srcs, o_ref, dsts, sems = refs[7:7 + nc], refs[7 + nc], refs[8 + nc:8 + 2 * nc], refs[8 + 2 * nc:]
        hp, n = pl.program_id(0), pl.program_id(1)

        @pl.when((hp == 0) & (n == 0))
        def _():
            for cp in _exchange_copies(srcs, dsts, scat, *sems):
                cp.start()

        for hh in range(2):
            qs, kk, vv, probs, _ = _attn_setup(hh, q_ref, kp_ref, kc_ref, vp_ref, vc_ref, bias_ref, sink_ref, hp, n)
            o = jnp.dot(probs.astype(BF16), vv, preferred_element_type=F32)
            for s_, slab in enumerate(_unstack(o, BF16)):
                c0 = hh * (GROUP * HEAD_DIM) + s_ * LANES
                o_ref[:, c0:c0 + LANES] = slab

        @pl.when((hp == HP - 1) & (n == nb - 1))
        def _():
            for cp in _exchange_copies(srcs, dsts, scat, *sems):
                cp.wait()

    qspec = pl.BlockSpec((BLOCK, Q_PER_HP), lambda hp, n: (n, hp))
    any_spec = pl.BlockSpec(memory_space=pl.ANY)

    def kv(base, prev):
        if prev:
            return pl.BlockSpec((BLOCK, LANES), lambda hp, n: (jnp.maximum(n - 1, 0), base + hp))
        return pl.BlockSpec((BLOCK, LANES), lambda hp, n: (n, base + hp))

    return pl.pallas_call(
        body, name="attention_forward", grid=(HP, nb),
        in_specs=[qspec, kv(kb, True), kv(kb, False), kv(vb, True), kv(vb, False),
                  pl.BlockSpec((H_PER_HP, BLOCK, 2 * BLOCK), lambda hp, n: (hp, 0, 0)),
                  pl.BlockSpec(memory_space=pltpu.SMEM)] + [any_spec] * nc,
        out_specs=(qspec,) + (any_spec,) * nc,
        out_shape=(jax.ShapeDtypeStruct((s, D_ATTN), BF16),) + tuple(_exchange_shapes(comm)),
        scratch_shapes=_exchange_sems(nc), compiler_params=_cparams(2),
    )(proj, proj, proj, proj, proj, bias, sinks, *[arr for arr, _ in comm])


def attention_backward(proj, datt, bias, sinks, comm):
    s = proj.shape[0]
    nb = s // BLOCK
    kb, vb = K0 // LANES, V0 // LANES
    nc = len(comm)
    scat = [s_ for _, s_ in comm]

    def body(*refs):
        q_ref, kp_ref, kc_ref, vp_ref, vc_ref, do_ref, bias_ref, sink_ref = refs[:8]
        srcs = refs[8:8 + nc]
        dq_ref, dk_ref, dv_ref, dbias_ref, dsink_ref = refs[8 + nc:13 + nc]
        dsts = refs[13 + nc:13 + 2 * nc]
        kcar_ref, vcar_ref = refs[13 + 2 * nc:15 + 2 * nc]
        sems = refs[15 + 2 * nc:]
        hp, n = pl.program_id(0), pl.program_id(1)

        @pl.when((hp == 0) & (n == 0))
        def _():
            for cp in _exchange_copies(srcs, dsts, scat, *sems):
                cp.start()

        @pl.when(n == 0)
        def _():
            kcar_ref[...] = jnp.zeros_like(kcar_ref)
            vcar_ref[...] = jnp.zeros_like(vcar_ref)
            dbias_ref[...] = jnp.zeros_like(dbias_ref)
            dsink_ref[...] = jnp.zeros_like(dsink_ref)

        @pl.when(n < nb)
        def _():
            lane2 = lax.broadcasted_iota(jnp.int32, (2 * BLOCK, LANES), 1)
            qlane = lax.broadcasted_iota(jnp.int32, (BLOCK, LANES), 1)
            lo = qlane < HEAD_DIM
            dk_band = jnp.zeros((2 * BLOCK, LANES), F32)
            dv_band = jnp.zeros((2 * BLOCK, LANES), F32)
            for hh in range(2):
                qs, kk, vv, probs, psink = _attn_setup(hh, q_ref, kp_ref, kc_ref, vp_ref, vc_ref, bias_ref, sink_ref, hp, n)
                parts = []
                for s_ in range(GROUP // 2):
                    c0 = hh * (GROUP * HEAD_DIM) + s_ * LANES
                    slab = do_ref[:, c0:c0 + LANES]
                    parts.append(jnp.where(lo, slab, jnp.zeros_like(slab)))
                    parts.append(jnp.where(lo, jnp.zeros_like(slab), slab))
                dos = jnp.concatenate(parts, axis=0)
                dp = lax.dot_general(dos, vv, (((1,), (1,)), ((), ())), preferred_element_type=F32)
                dsum = jnp.sum(probs * dp, axis=-1, keepdims=True)
                ds = probs * (dp - dsum)
                for g in range(GROUP):
                    dbias_ref[hh * GROUP + g] += ds[g * BLOCK:(g + 1) * BLOCK, :]
                dsk = -psink * dsum
                rows = [jnp.full((1, LANES), jnp.sum(dsk[g * BLOCK:(g + 1) * BLOCK, :]), F32) for g in range(GROUP)]
                dsink_ref[hh * GROUP:(hh + 1) * GROUP, :] += jnp.concatenate(rows, axis=0)
                dsb = (ds * (HEAD_DIM ** -0.5)).astype(BF16)
                pb = probs.astype(BF16)
                dq = jnp.dot(dsb, kk, preferred_element_type=F32)
                for s_, slab in enumerate(_unstack(dq, BF16)):
                    c0 = hh * (GROUP * HEAD_DIM) + s_ * LANES
                    dq_ref[:, c0:c0 + LANES] = slab
                dkh = lax.dot_general(dsb, qs, (((0,), (0,)), ((), ())), preferred_element_type=F32)
                dvh = lax.dot_general(pb, dos, (((0,), (0,)), ((), ())), preferred_element_type=F32)
                own = (lane2 >= HEAD_DIM) if hh == 1 else (lane2 < HEAD_DIM)
                dk_band = dk_band + jnp.where(own, dkh + pltpu.roll(dkh, HEAD_DIM, axis=1), 0.0)
                dv_band = dv_band + jnp.where(own, dvh + pltpu.roll(dvh, HEAD_DIM, axis=1), 0.0)
            dk_ref[...] = (kcar_ref[...] + dk_band[:BLOCK]).astype(BF16)
            dv_ref[...] = (vcar_ref[...] + dv_band[:BLOCK]).astype(BF16)
            kcar_ref[...] = dk_band[BLOCK:]
            vcar_ref[...] = dv_band[BLOCK:]

        @pl.when(n == nb)
        def _():
            dk_ref[...] = kcar_ref[...].astype(BF16)
            dv_ref[...] = vcar_ref[...].astype(BF16)

        @pl.when((hp == HP - 1) & (n == nb))
        def _():
            for cp in _exchange_copies(srcs, dsts, scat, *sems):
                cp.wait()

    qspec = pl.BlockSpec((BLOCK, Q_PER_HP), lambda hp, n: (jnp.minimum(n, nb - 1), hp))
    any_spec = pl.BlockSpec(memory_space=pl.ANY)

    def kv(base, prev):
        if prev:
            return pl.BlockSpec((BLOCK, LANES), lambda hp, n: (jnp.maximum(jnp.minimum(n, nb - 1) - 1, 0), base + hp))
        return pl.BlockSpec((BLOCK, LANES), lambda hp, n: (jnp.minimum(n, nb - 1), base + hp))

    dkv_spec = pl.BlockSpec((BLOCK, LANES), lambda hp, n: (jnp.maximum(n - 1, 0), hp))
    return pl.pallas_call(
        body, name="attention_backward", grid=(HP, nb + 1),
        in_specs=[qspec, kv(kb, True), kv(kb, False), kv(vb, True), kv(vb, False), qspec,
                  pl.BlockSpec((H_PER_HP, BLOCK, 2 * BLOCK), lambda hp, n: (hp, 0, 0)),
                  pl.BlockSpec(memory_space=pltpu.SMEM)] + [any_spec] * nc,
        out_specs=(qspec, dkv_spec, dkv_spec,
                   pl.BlockSpec((H_PER_HP, BLOCK, 2 * BLOCK), lambda hp, n: (hp, 0, 0)),
                   pl.BlockSpec((H_PER_HP, LANES), lambda hp, n: (hp, 0))) + (any_spec,) * nc,
        out_shape=(jax.ShapeDtypeStruct((s, D_ATTN), BF16), jax.ShapeDtypeStruct((s, D_KV), BF16),
                   jax.ShapeDtypeStruct((s, D_KV), BF16), jax.ShapeDtypeStruct((N_HEADS, BLOCK, 2 * BLOCK), F32),
                   jax.ShapeDtypeStruct((N_HEADS, LANES), F32)) + tuple(_exchange_shapes(comm)),
        scratch_shapes=[pltpu.VMEM((BLOCK, LANES), F32), pltpu.VMEM((BLOCK, LANES), F32)] + _exchange_sems(nc),
        compiler_params=_cparams(2),
    )(proj, proj, proj, proj, proj, datt, bias, sinks, *[arr for arr, _ in comm])


def _neg_expm1(x):
    series = -(x * (1.0 + x * (1.0 / 2 + x * (1.0 / 6 + x * (1.0 / 24 + x * (1.0 / 120 + x * (1.0 / 720)))))))
    return jnp.where(x > -0.25, series, 1.0 - jnp.exp(x))


def _softplus_neg(lam):
    u = jnp.exp(-jnp.abs(lam))
    w = 1.0 + u
    log1p = jnp.where(w == 1.0, u, jnp.log(w) * u / jnp.where(w == 1.0, 1.0, w - 1.0))
    sp = jnp.maximum(-lam, 0.0) + log1p
    return sp, -_sigmoid(-lam)


def _rnn_gates(x_ref, xp_ref, cw_ref, cb_ref, wa_ref, ba_ref, wi_ref, bi_ref, lam_ref, first, row0):
    xv = x_ref[...].astype(F32)
    pv = jnp.where(first, 0.0, xp_ref[...].astype(F32))
    xs = [_shift_down(pv, xv, RNN_CONV - 1 - k) for k in range(RNN_CONV)]
    xc = cb_ref[...]
    for k in range(RNN_CONV):
        xc = xc + cw_ref[k:k + 1, :] * xs[k]
    xcb = xc.astype(BF16)
    ra = _sigmoid(jnp.dot(xcb, wa_ref[...], preferred_element_type=F32) + ba_ref[...])
    ri = _sigmoid(jnp.dot(xcb, wi_ref[...], preferred_element_type=F32) + bi_ref[...])
    sp, dsp = _softplus_neg(lam_ref[...])
    la = (-RG_C) * ra * sp
    a = jnp.exp(la)
    t = row0 + lax.broadcasted_iota(jnp.int32, xv.shape, 0)
    start = t == 0
    mult = jnp.where(start, 1.0, jnp.sqrt(_neg_expm1(2.0 * la)))
    return xs, xc, xcb, ra, ri, sp, dsp, a, mult, start


def _rnn_specs(t_rows, s, rev):
    nt = s // t_rows
    r = t_rows // HALO
    xb, gb = XR0 // LANES, GR0 // LANES
    ti = (lambda i: nt - 1 - i) if rev else (lambda i: i)
    tile = lambda base: pl.BlockSpec((t_rows, LANES), lambda n, i: (ti(i), base + n))
    prev = lambda base: pl.BlockSpec((HALO, LANES), lambda n, i: (jnp.maximum(ti(i) * r - 1, 0), base + n))
    par = lambda rows: pl.BlockSpec((rows, LANES), lambda n, i: (0, n))
    mat = pl.BlockSpec((None, RNN_BLOCK, RNN_BLOCK), lambda n, i: (n, 0, 0))
    return nt, ti, tile, prev, par, mat, xb, gb


def rnn_forward(proj, cw, cb, wa, ba, wi, bi, lam):
    s = proj.shape[0]
    t_rows = _row_tile(s, 512)
    nt, ti, tile, prev, par, mat, xb, gb = _rnn_specs(t_rows, s, False)

    def body(x_ref, xp_ref, g_ref, cw_ref, cb_ref, wa_ref, ba_ref, wi_ref, bi_ref, lam_ref, z_ref, h_ref, car_ref):
        i = pl.program_id(1)
        first = i == 0
        _, xc, _, _, ri, _, _, a, mult, _ = _rnn_gates(x_ref, xp_ref, cw_ref, cb_ref, wa_ref, ba_ref, wi_ref, bi_ref,
                                                       lam_ref, first, i * t_rows)
        aa, bb = a, mult * ri * xc
        rows = lax.broadcasted_iota(jnp.int32, aa.shape, 0)
        d = 1
        while d < t_rows:
            keep = rows >= d
            a_s, b_s = pltpu.roll(aa, d, axis=0), pltpu.roll(bb, d, axis=0)
            bb = jnp.where(keep, aa * b_s + bb, bb)
            aa = jnp.where(keep, aa * a_s, aa)
            d *= 2
        carry = jnp.where(first, 0.0, car_ref[0:1, :])
        h = aa * carry + bb
        car_ref[...] = jnp.broadcast_to(h[t_rows - 1:t_rows, :], car_ref.shape)
        h_ref[...] = h.astype(BF16)
        z_ref[...] = (h * _gelu(g_ref[...].astype(F32))).astype(BF16)

    o = jax.ShapeDtypeStruct((s, D_RNN), BF16)
    out_tile = pl.BlockSpec((t_rows, LANES), lambda n, i: (i, n))
    return pl.pallas_call(
        body, name="rnn_forward", grid=(N_RNN_BLOCKS, nt),
        in_specs=[tile(xb), prev(xb), tile(gb), par(RNN_CONV), par(1), mat, par(1), mat, par(1), par(1)],
        out_specs=(out_tile, out_tile), out_shape=(o, o), scratch_shapes=[pltpu.VMEM((8, LANES), F32)],
        compiler_params=_cparams(2),
    )(proj, proj, proj, cw, cb, wa, ba, wi, bi, lam)


def rnn_backward(proj, h, dz, cw, cb, wa, ba, wi, bi, lam):
    s = proj.shape[0]
    t_rows = _row_tile(s, 512)
    nt, ti, tile, prev, par, mat, xb, gb = _rnn_specs(t_rows, s, True)
    r = t_rows // HALO

    def body(x_ref, xp_ref, g_ref, h_ref, hp_ref, dz_ref, cw_ref, cb_ref, wa_ref, ba_ref, wi_ref, bi_ref, lam_ref,
             dx_ref, dg_ref, dwa_ref, dwi_ref, sm_ref, gcar_ref, xcar_ref):
        i = pl.program_id(1)
        it = nt - 1 - i
        first, last = it == 0, it == nt - 1
        xs, xc, xcb, ra, ri, sp, dsp, a, mult, start = _rnn_gates(
            x_ref, xp_ref, cw_ref, cb_ref, wa_ref, ba_ref, wi_ref, bi_ref, lam_ref, first, it * t_rows)
        hf = h_ref[...].astype(F32)
        hprev = _shift_down(jnp.where(first, 0.0, hp_ref[...].astype(F32)), hf, 1)
        ge, dge = _gelu_and_grad(g_ref[...].astype(F32))
        dz = dz_ref[...].astype(F32)
        dg_ref[...] = (dz * hf * dge).astype(BF16)
        rows = lax.broadcasted_iota(jnp.int32, hf.shape, 0)
        tail = rows == t_rows - 1
        carry = jnp.where(last, 0.0, gcar_ref[0:1, :])
        bb = dz * ge + jnp.where(tail, carry, 0.0)
        aa = jnp.where(tail, 0.0, pltpu.roll(a, t_rows - 1, axis=0))
        d = 1
        while d < t_rows:
            keep = rows < t_rows - d
            a_s, b_s = pltpu.roll(aa, t_rows - d, axis=0), pltpu.roll(bb, t_rows - d, axis=0)
            bb = jnp.where(keep, bb + aa * b_s, bb)
            aa = jnp.where(keep, aa * a_s, aa)
            d *= 2
        gg = bb
        gcar_ref[...] = jnp.broadcast_to(a[0:1, :] * gg[0:1, :], gcar_ref.shape)
        da = gg * hprev
        dmult = jnp.where(start, 0.0, gg * ri * xc)
        dri = gg * mult * xc
        dxc = gg * mult * ri
        safe_mult = jnp.where(start, 1.0, mult)
        dla = da * a - dmult * (a * a) / safe_mult
        dra = dla * ((-RG_C) * sp)
        dlam = jnp.sum(dla * ((-RG_C) * ra), axis=0, keepdims=True) * dsp
        dpa = dra * ra * (1.0 - ra)
        dpi = dri * ri * (1.0 - ri)
        dpab, dpib = dpa.astype(BF16), dpi.astype(BF16)
        nt_dims = (((1,), (1,)), ((), ()))
        tn_dims = (((0,), (0,)), ((), ()))
        dxc = dxc + lax.dot_general(dpab, wa_ref[...], nt_dims, preferred_element_type=F32) \
            + lax.dot_general(dpib, wi_ref[...], nt_dims, preferred_element_type=F32)
        dwa = lax.dot_general(xcb, dpab, tn_dims, preferred_element_type=F32)
        dwi = lax.dot_general(xcb, dpib, tn_dims, preferred_element_type=F32)
        nxt = jnp.where(last, 0.0, xcar_ref[...])
        dx = cw_ref[RNN_CONV - 1:RNN_CONV, :] * dxc
        for k in range(RNN_CONV - 1):
            dx = dx + cw_ref[k:k + 1, :] * _shift_up(dxc, nxt, RNN_CONV - 1 - k)
        dx_ref[...] = dx.astype(BF16)
        xcar_ref[...] = dxc[0:HALO, :]
        small = jnp.concatenate(
            [jnp.sum(dpa, axis=0, keepdims=True), jnp.sum(dpi, axis=0, keepdims=True), dlam,
             jnp.sum(dxc, axis=0, keepdims=True)]
            + [jnp.sum(dxc * xs[k], axis=0, keepdims=True) for k in range(RNN_CONV)], axis=0)

        @pl.when(i == 0)
        def _():
            dwa_ref[...] = dwa
            dwi_ref[...] = dwi
            sm_ref[...] = small

        @pl.when(i > 0)
        def _():
            dwa_ref[...] += dwa
            dwi_ref[...] += dwi
            sm_ref[...] += small

    o = jax.ShapeDtypeStruct((s, D_RNN), BF16)
    plain = pl.BlockSpec((t_rows, LANES), lambda n, i: (ti(i), n))
    plain_prev = pl.BlockSpec((HALO, LANES), lambda n, i: (jnp.maximum(ti(i) * r - 1, 0), n))
    return pl.pallas_call(
        body, name="rnn_backward", grid=(N_RNN_BLOCKS, nt),
        in_specs=[tile(xb), prev(xb), tile(gb), plain, plain_prev, plain,
                  par(RNN_CONV), par(1), mat, par(1), mat, par(1), par(1)],
        out_specs=(plain, plain, mat, mat, pl.BlockSpec((None, 8, LANES), lambda n, i: (n, 0, 0))),
        out_shape=(o, o, jax.ShapeDtypeStruct((N_RNN_BLOCKS, RNN_BLOCK, RNN_BLOCK), F32),
                   jax.ShapeDtypeStruct((N_RNN_BLOCKS, RNN_BLOCK, RNN_BLOCK), F32),
                   jax.ShapeDtypeStruct((N_RNN_BLOCKS, 8, LANES), F32)),
        scratch_shapes=[pltpu.VMEM((8, LANES), F32), pltpu.VMEM((HALO, LANES), F32)],
        compiler_params=_cparams(2),
    )(proj, proj, proj, h, h, dz, cw, cb, wa, ba, wi, bi, lam)


def adamw(parts, w, m, v, name, rows=256):
    p, r, c = parts.shape
    tr = min(rows, r)
    assert r % tr == 0

    def body(p_ref, w_ref, m_ref, v_ref, g_ref, d_ref, nm_ref, nv_ref):
        g = p_ref[0].astype(F32)
        for q in range(1, p):
            g = g + p_ref[q].astype(F32)
        nm = ADAM_B1 * m_ref[...] + (1.0 - ADAM_B1) * g
        nv = ADAM_B2 * v_ref[...] + (1.0 - ADAM_B2) * (g * g)
        mh = nm / (1.0 - ADAM_B1 ** ADAM_STEP)
        vh = nv / (1.0 - ADAM_B2 ** ADAM_STEP)
        g_ref[...] = g
        d_ref[...] = (-ADAM_LR) * (mh / (jnp.sqrt(vh) + ADAM_EPS) + ADAM_WD * w_ref[...])
        nm_ref[...] = nm
        nv_ref[...] = nv

    pspec = pl.BlockSpec((p, tr, c), lambda i: (0, i, 0))
    spec = pl.BlockSpec((tr, c), lambda i: (i, 0))
    o = jax.ShapeDtypeStruct((r, c), F32)
    return pl.pallas_call(body, name=name, grid=(r // tr,), in_specs=[pspec, spec, spec, spec],
                          out_specs=(spec,) * 4, out_shape=(o, o, o, o), compiler_params=_cparams(1))(parts, w, m, v)


def ada_weight_grad(c_t, dmod):
    d, nb = c_t.shape
    c = dmod.shape[1]
    tr = 512

    def body(c_ref, dm_ref, o_ref):
        cv = c_ref[...]
        cs = cv * _sigmoid(cv)
        acc = cs[:, 0:1] * dm_ref[0:1, :]
        for b in range(1, nb):
            acc = acc + cs[:, b:b + 1] * dm_ref[b:b + 1, :]
        o_ref[...] = acc

    return pl.pallas_call(body, name="ada_weight_grad", grid=(d // tr,),
                          in_specs=[pl.BlockSpec((tr, nb), lambda i: (i, 0)), pl.BlockSpec((nb, c), lambda i: (0, 0))],
                          out_specs=pl.BlockSpec((tr, c), lambda i: (i, 0)),
                          out_shape=jax.ShapeDtypeStruct((d, c), F32), compiler_params=_cparams(1))(c_t, dmod)


def _rows128(a):
    flat = a.reshape(-1).astype(F32)
    pad = (-flat.shape[0]) % (8 * LANES)
    if pad:
        flat = jnp.concatenate([flat, jnp.zeros((pad,), F32)])
    return flat.reshape(-1, LANES)


def kernel(x, c, w_ada, b_ada, norm1, w_in, rnn_conv_w, rnn_conv_b, w_rg_a, b_rg_a, w_rg_i, b_rg_i, rg_lambda, w_o_rnn, w_o_attn, attn_sinks, rel_bias, w_out, norm2, w_up, ffn_conv_w, ffn_conv_b, w_down, norm_f, loss_target, m_w_ada, m_b_ada, m_norm1, m_w_in, m_rnn_conv_w, m_rnn_conv_b, m_w_rg_a, m_b_rg_a, m_w_rg_i, m_b_rg_i, m_rg_lambda, m_w_o_rnn, m_w_o_attn, m_attn_sinks, m_rel_bias, m_w_out, m_norm2, m_w_up, m_ffn_conv_w, m_ffn_conv_b, m_w_down, m_norm_f, v_w_ada, v_b_ada, v_norm1, v_w_in, v_rnn_conv_w, v_rnn_conv_b, v_w_rg_a, v_b_rg_a, v_w_rg_i, v_b_rg_i, v_rg_lambda, v_w_o_rnn, v_w_o_attn, v_attn_sinks, v_rel_bias, v_w_out, v_norm2, v_w_up, v_ffn_conv_w, v_ffn_conv_b, v_w_down, v_norm_f):
    me = 4 * lax.axis_index("x") + 2 * lax.axis_index("y") + lax.axis_index("c")
    xs = x[0]
    tgt = loss_target[0]
    s, d = xs.shape
    bf = lambda w: w[0].astype(BF16)

    mod, c_all = mod_forward(c.reshape(1, 1, d), w_ada[0], b_ada)
    mod = mod.reshape(6, d)
    shift1, scale1, gate1, shift2, scale2, gate2 = [mod[i:i + 1] for i in range(6)]

    g_in, g_rcw, g_fcw = exchange([(bf(w_in), False), (rnn_conv_w[0], False), (ffn_conv_w[0], False)], "gather_w_in")
    w_in_f = jnp.transpose(g_in, (1, 0, 2)).reshape(d, D_IN)
    rcw = jnp.transpose(g_rcw, (1, 0, 2)).reshape(RNN_CONV, D_RNN)
    fcw = jnp.transpose(g_fcw, (1, 0, 2)).reshape(FFN_CONV, 2 * D_FF)
    wa_b, wi_b = bf(w_rg_a), bf(w_rg_i)
    sinks = attn_sinks[0]

    u = prenorm(xs, norm1, scale1, shift1, "prenorm1")
    proj, g_oa, g_or, g_out, g_down = matmul(
        u, w_in_f, "nn", BF16, "mm_in", 1024, 512, 2048,
        comm=[(bf(w_o_attn), False), (bf(w_o_rnn), False), (bf(w_out), False), (bf(w_down), False)])
    w_oa_f, w_or_f = g_oa.reshape(D_ATTN, d), g_or.reshape(D_RNN, d)
    w_out_f, w_down_f = g_out.reshape(d, d), g_down.reshape(D_FF, d)
    bias = band_bias(rel_bias)
    att, g_up = attention_forward(proj, bias, sinks, [(bf(w_up), False)])
    w_up_f = jnp.transpose(g_up, (1, 0, 2)).reshape(d, 2 * D_FF)
    z, hr = rnn_forward(proj, rcw, rnn_conv_b, wa_b, b_rg_a, wi_b, b_rg_i, rg_lambda)
    y_attn = matmul(att, w_oa_f, "nn", BF16, "mm_o_attn", 1024, 1024, 2048)
    y_rnn = matmul(z, w_or_f, "nn", BF16, "mm_o_rnn", 1024, 1024, 2560)
    merged = merge_forward(proj, y_attn, y_rnn)
    h1, mo = matmul(merged, w_out_f, "nn", BF16, "mm_out", 512, 1024, 2048, res=xs, gate=gate1)
    u2 = prenorm(h1, norm2, scale2, shift2, "prenorm2")
    upp = matmul(u2, w_up_f, "nn", BF16, "mm_up", 1024, 512, 2048)
    act, up_g, up_v = ffn_act_forward(upp, fcw, ffn_conv_b)
    h2, dn = matmul(act, w_down_f, "nn", BF16, "mm_down", 512, 1024, 2048, res=h1, gate=gate2)

    dh2, d_dn, loss_cols, d_norm_f, d_gate2 = loss_head(h2, tgt, dn, norm_f.reshape(1, d), gate2)
    loss = lax.psum(jnp.sum(loss_cols), ("x", "y", "c"))

    d_act = matmul(d_dn, w_down_f, "nt", BF16, "mm_down_dx", 1024, 1536, 2048)
    g_w_down = matmul(act, d_dn, "tn", BF16, "mm_down_dw", 1536, 1024, 1024)
    d_g, d_v, d_fcb_g, d_fcb_v = ffn_act_backward(d_act, up_g, up_v)
    d_upp, d_fcw = conv_backward(d_g, d_v, upp, fcw)
    d_fcb = jnp.concatenate([d_fcb_g, d_fcb_v], axis=1)
    d_u2, p_down = matmul(d_upp, w_up_f, "nt", BF16, "mm_up_dx", 1024, 1024, 3072,
                          comm=[(g_w_down.reshape(N_DEV, D_FF // N_DEV, d), True)])
    g_w_up = matmul(u2, d_upp, "tn", BF16, "mm_up_dw", 2048, 512, 1024)
    g_up_blk = jnp.transpose(g_w_up.reshape(d, N_DEV, 2 * D_FF // N_DEV), (1, 0, 2))
    dh1, d_shift2, d_scale2, d_norm2, d_mo, d_gate1 = norm_backward(d_u2, h1, dh2, norm2, scale2, "norm2_backward",
                                                                    mo=mo, gate=gate1)
    d_merged = matmul(d_mo, w_out_f, "nt", BF16, "mm_out_dx", 1024, 1024, 2048)
    g_w_out = matmul(merged, d_mo, "tn", BF16, "mm_out_dw", 2048, 1024, 1024)
    d_ga, d_gl, d_ya, d_yr = merge_backward(d_merged, proj, y_attn, y_rnn)
    d_att = matmul(d_ya, w_oa_f, "nt", BF16, "mm_o_attn_dx", 1024, 1024, 2048)
    g_w_oa = matmul(att, d_ya, "tn", BF16, "mm_o_attn_dw", 2048, 1024, 1024)
    d_z = matmul(d_yr, w_or_f, "nt", BF16, "mm_o_rnn_dx", 1024, 1280, 2048)
    g_w_or = matmul(z, d_yr, "tn", BF16, "mm_o_rnn_dw", 1280, 1024, 1024)
    d_xr, d_gr, d_wa, d_wi, d_rsmall = rnn_backward(proj, hr, d_z, rcw, rnn_conv_b, wa_b, b_rg_a, wi_b, b_rg_i, rg_lambda)
    d_q, d_k, d_v_, d_bias, d_sink, p_up = attention_backward(proj, d_att, bias, sinks, [(g_up_blk, True)])
    d_rel = rel_bias_grad(d_bias)
    d_proj = jnp.concatenate([d_q, d_k, d_v_, d_xr, d_gr, d_ga, d_gl], axis=1)

    def rsmall_of(ba_, bi_, lam_, cb_):
        return jnp.stack([ba_[0].reshape(N_RNN_BLOCKS, LANES), bi_[0].reshape(N_RNN_BLOCKS, LANES),
                          lam_[0].reshape(N_RNN_BLOCKS, LANES), cb_[0].reshape(N_RNN_BLOCKS, LANES)]
                         + [jnp.zeros((N_RNN_BLOCKS, LANES), F32)] * 4, axis=1)

    pack = lambda t: jnp.concatenate([_rows128(q) for q in t], axis=0)
    late_n = 2
    g_early = [d_norm2, d_norm_f, d_rsmall, d_wa, d_wi, d_sink[:, 0], d_rel[:, :N_HEADS], d_fcb, d_fcw]

    g_w_in, p_out, p_oa, p_or, small_early = matmul(
        u, d_proj, "tn", BF16, "mm_in_dw", 2048, 512, 1024,
        comm=[(g_w_out.reshape(N_DEV, d // N_DEV, d), True), (g_w_oa.reshape(N_DEV, D_ATTN // N_DEV, d), True),
              (g_w_or.reshape(N_DEV, D_RNN // N_DEV, d), True), (pack(g_early), False)])
    g_in_blk = jnp.transpose(g_w_in.reshape(d, N_DEV, D_IN // N_DEV), (1, 0, 2))
    d_u, p_in = matmul(d_proj, w_in_f, "nt", BF16, "mm_in_dx", 1024, 1024, 2944, comm=[(g_in_blk, True)])
    grad_x, d_shift1, d_scale1, d_norm1 = norm_backward(d_u, xs, dh1, norm1, scale1, "norm1_backward")
    d_mod = jnp.concatenate([d_shift1, d_scale1, d_gate1, d_shift2, d_scale2, d_gate2], axis=1)
    small_late = exchange([(pack([d_mod, d_norm1]), False)], "gather_late_grads")[0]

    g_list = [d_mod, d_norm1] + g_early
    w_list = [b_ada, norm1, norm2, norm_f, rsmall_of(b_rg_a, b_rg_i, rg_lambda, rnn_conv_b), w_rg_a, w_rg_i,
              attn_sinks, rel_bias, ffn_conv_b, jnp.zeros_like(d_fcw)]
    m_list = [m_b_ada, m_norm1, m_norm2, m_norm_f, rsmall_of(m_b_rg_a, m_b_rg_i, m_rg_lambda, m_rnn_conv_b), m_w_rg_a,
              m_w_rg_i, m_attn_sinks, m_rel_bias, m_ffn_conv_b, jnp.zeros_like(d_fcw)]
    v_list = [v_b_ada, v_norm1, v_norm2, v_norm_f, rsmall_of(v_b_rg_a, v_b_rg_i, v_rg_lambda, v_rnn_conv_b), v_w_rg_a,
              v_w_rg_i, v_attn_sinks, v_rel_bias, v_ffn_conv_b, jnp.ones_like(d_fcw)]
    sizes = [_rows128(q).shape[0] for q in g_list]
    offs = np.concatenate([[0], np.cumsum(sizes)]).tolist()
    r_late = offs[late_n]
    late = adamw(small_late, pack(w_list[:late_n]), pack(m_list[:late_n]), pack(v_list[:late_n]), "adamw_small_late",
                 rows=r_late)
    early = adamw(small_early, pack(w_list[late_n:]), pack(m_list[late_n:]), pack(v_list[late_n:]), "adamw_small_early",
                  rows=(offs[-1] - r_late) // 7)

    def seg(packed, idx, like):
        n_el = int(np.prod(like.shape))
        return packed[offs[idx]:offs[idx + 1]].reshape(-1)[:n_el].reshape(like.shape)

    def unpack(kind):
        packed = jnp.concatenate([late[kind], early[kind]], axis=0)
        rs = seg(packed, 4, d_rsmall)
        out = dict(
            b_ada=seg(packed, 0, b_ada), norm1=seg(packed, 1, norm1), norm2=seg(packed, 2, norm2),
            norm_f=seg(packed, 3, norm_f), b_rg_a=rs[:, 0].reshape(1, D_RNN), b_rg_i=rs[:, 1].reshape(1, D_RNN),
            rg_lambda=rs[:, 2].reshape(1, D_RNN), rnn_conv_b=rs[:, 3].reshape(1, D_RNN),
            w_rg_a=seg(packed, 5, w_rg_a), w_rg_i=seg(packed, 6, w_rg_i), attn_sinks=seg(packed, 7, attn_sinks),
            rel_bias=seg(packed, 8, rel_bias), ffn_conv_b=seg(packed, 9, ffn_conv_b))
        out["rnn_conv_w_full"] = jnp.transpose(rs[:, 4:8], (1, 0, 2)).reshape(RNN_CONV, D_RNN)
        out["ffn_conv_w_full"] = seg(packed, 10, d_fcw)
        return out

    small = [unpack(kind) for kind in range(4)]

    rcw_cols = D_RNN // N_DEV
    fcw_cols = 2 * D_FF // N_DEV
    g_rcw_ = lax.dynamic_slice(small[0]["rnn_conv_w_full"], (0, me * rcw_cols), (RNN_CONV, rcw_cols))
    g_fcw_ = lax.dynamic_slice(small[0]["ffn_conv_w_full"], (0, me * fcw_cols), (FFN_CONV, fcw_cols))
    r_rcw = adamw(g_rcw_[None], rnn_conv_w[0], m_rnn_conv_w[0], v_rnn_conv_w[0], "adamw_rnn_conv_w")
    r_fcw = adamw(g_fcw_[None], ffn_conv_w[0], m_ffn_conv_w[0], v_ffn_conv_w[0], "adamw_ffn_conv_w")

    ada_cols = 6 * d // N_DEV
    dmod_all = small_late[:, offs[0]:offs[1]].reshape(N_DEV, 6 * d)
    dmod_cols = lax.dynamic_slice(dmod_all, (0, me * ada_cols), (N_DEV, ada_cols))
    g_ada = ada_weight_grad(jnp.transpose(c_all.reshape(N_DEV, d)), dmod_cols)
    r_ada = adamw(g_ada[None], w_ada[0], m_w_ada[0], v_w_ada[0], "adamw_w_ada")

    r_in = adamw(p_in, w_in[0], m_w_in[0], v_w_in[0], "adamw_w_in")
    r_up = adamw(p_up, w_up[0], m_w_up[0], v_w_up[0], "adamw_w_up")
    r_or = adamw(p_or, w_o_rnn[0], m_w_o_rnn[0], v_w_o_rnn[0], "adamw_w_o_rnn", rows=160)
    r_oa = adamw(p_oa, w_o_attn[0], m_w_o_attn[0], v_w_o_attn[0], "adamw_w_o_attn")
    r_out = adamw(p_out, w_out[0], m_w_out[0], v_w_out[0], "adamw_w_out")
    r_down = adamw(p_down, w_down[0], m_w_down[0], v_w_down[0], "adamw_w_down")

    def res(kind):
        sm = small[kind]
        return [r_ada[kind][None], sm["b_ada"], sm["norm1"], r_in[kind][None], r_rcw[kind][None], sm["rnn_conv_b"],
                sm["w_rg_a"], sm["b_rg_a"], sm["w_rg_i"], sm["b_rg_i"], sm["rg_lambda"], r_or[kind][None],
                r_oa[kind][None], sm["attn_sinks"], sm["rel_bias"], r_out[kind][None], sm["norm2"], r_up[kind][None],
                r_fcw[kind][None], sm["ffn_conv_b"], r_down[kind][None], sm["norm_f"]]

    return (loss, grad_x[None], *res(0), *res(1), *res(2), *res(3))
```

```python
import functools
import math

import numpy as np
import jax
import jax.numpy as jnp
from jax import lax
from jax.experimental import pallas as pl
from jax.experimental.pallas import tpu as pltpu

F32, BF16 = jnp.float32, jnp.bfloat16

N_DEV = 8
D_MODEL = 2048
N_HEADS, HEAD_DIM, N_KV = 32, 64, 4
GROUP = N_HEADS // N_KV
D_ATTN, D_KV = N_HEADS * HEAD_DIM, N_KV * HEAD_DIM
BLOCK = 128
NUM_BUCKETS, MAX_DISTANCE = 32, 128
D_RNN, N_RNN_BLOCKS, RNN_BLOCK = 2560, 20, 128
RNN_CONV, FFN_CONV = 4, 3
RG_C = 8.0
D_FF = 3 * D_MODEL
D_IN = D_ATTN + 2 * D_KV + 2 * D_RNN + 2 * D_MODEL
EPS = 1e-6
NEG_INF = -1e30
ADAM_LR, ADAM_B1, ADAM_B2, ADAM_EPS, ADAM_WD, ADAM_STEP = 0.001, 0.9, 0.999, 1e-08, 0.01, 10

LANES = 128
HALO = 16
VMEM_LIMIT = 56 * 1024 * 1024
MESH = pl.DeviceIdType.MESH

Q0, K0, V0, XR0, GR0, GA0, GL0 = 0, 2048, 2304, 2560, 5120, 7680, 9728


def _cparams(n_axes):
    return pltpu.CompilerParams(dimension_semantics=("arbitrary",) * n_axes, vmem_limit_bytes=VMEM_LIMIT)


def _gelu(x):
    k = math.sqrt(2.0 / math.pi)
    return 0.5 * x * (1.0 + jnp.tanh(k * (x + 0.044715 * x * x * x)))


def _gelu_and_grad(x):
    k = math.sqrt(2.0 / math.pi)
    t = jnp.tanh(k * (x + 0.044715 * x * x * x))
    g = 0.5 * x * (1.0 + t)
    dg = 0.5 * (1.0 + t) + 0.5 * x * (1.0 - t * t) * k * (1.0 + 3.0 * 0.044715 * x * x)
    return g, dg


def _sigmoid(x):
    return 1.0 / (1.0 + jnp.exp(-x))


def _shift_down(prev, x, j):
    if j == 0:
        return x
    xe = jnp.concatenate([prev, x], axis=0)
    return pltpu.roll(xe, j, axis=0)[HALO:, :]


def _shift_up(x, nxt, j):
    if j == 0:
        return x
    xe = jnp.concatenate([x, nxt], axis=0)
    n = xe.shape[0]
    return pltpu.roll(xe, n - j, axis=0)[: x.shape[0], :]


def _my_coords():
    return lax.axis_index("x"), lax.axis_index("y"), lax.axis_index("c")


def _peer(x, y, c, k):
    kx, ky, kc = (k >> 2) & 1, (k >> 1) & 1, k & 1
    px, py, pc = (x + kx) % 2, (y + ky) % 2, (c + kc) % 2
    return (px, py, pc), 4 * px + 2 * py + pc


def _exchange_shapes(items):
    return [jax.ShapeDtypeStruct((N_DEV,) + tuple(arr.shape[1:] if s else arr.shape), arr.dtype) for arr, s in items]


def _exchange_sems(n):
    return [pltpu.SemaphoreType.DMA((n, N_DEV - 1)), pltpu.SemaphoreType.DMA((n, N_DEV - 1)),
            pltpu.SemaphoreType.DMA((n,))]


def _exchange_copies(srcs, dsts, scat, send_sems, recv_sems, loc_sems):
    x, y, c = _my_coords()
    me = 4 * x + 2 * y + c
    copies = []
    for a in range(len(srcs)):
        mine = srcs[a].at[me] if scat[a] else srcs[a]
        copies.append(pltpu.make_async_copy(mine, dsts[a].at[me], loc_sems.at[a]))
    for k in range(1, N_DEV):
        peer, p = _peer(x, y, c, k)
        for a in range(len(srcs)):
            src = srcs[a].at[p] if scat[a] else srcs[a]
            copies.append(pltpu.make_async_remote_copy(
                src_ref=src, dst_ref=dsts[a].at[me], send_sem=send_sems.at[a, k - 1],
                recv_sem=recv_sems.at[a, k - 1], device_id=peer, device_id_type=MESH))
    return copies


def exchange(items, name):
    n = len(items)
    scat = [s for _, s in items]

    def body(*refs):
        copies = _exchange_copies(refs[:n], refs[n:2 * n], scat, *refs[2 * n:])
        for cp in copies:
            cp.start()
        for cp in copies:
            cp.wait()

    any_spec = pl.BlockSpec(memory_space=pl.ANY)
    return pl.pallas_call(
        body, name=name, out_shape=tuple(_exchange_shapes(items)),
        in_specs=[any_spec] * n, out_specs=tuple([any_spec] * n), scratch_shapes=_exchange_sems(n),
    )(*[a for a, _ in items])


def mod_forward(c, w_ada, b_ada):
    d, ncol = w_ada.shape

    def body(c_ref, w_ref, b_ref, mod_ref, call_ref, cols_ref, s1, r1, s2, r2):
        x, y, c_ = _my_coords()
        me = 4 * x + 2 * y + c_
        call_ref[me] = c_ref[0]
        sends = []
        for k in range(1, N_DEV):
            peer, p = _peer(x, y, c_, k)
            cp = pltpu.make_async_remote_copy(src_ref=c_ref.at[0], dst_ref=call_ref.at[me], send_sem=s1.at[k - 1],
                                              recv_sem=r1.at[k - 1], device_id=peer, device_id_type=MESH)
            cp.start()
            sends.append(cp)
        for cp in sends:
            cp.wait()
        rows = lax.broadcasted_iota(jnp.int32, (N_DEV, d), 0)
        cmat = jnp.zeros((N_DEV, d), F32)
        for b in range(N_DEV):
            cmat = jnp.where(rows == b, call_ref[b], cmat)
        cs = cmat * _sigmoid(cmat)
        bias = b_ref[:, pl.ds(pl.multiple_of(me * ncol, LANES), ncol)]
        mc = jnp.dot(cs, w_ref[...], preferred_element_type=F32, precision=lax.Precision.HIGHEST) + bias
        for b in range(N_DEV):
            cols_ref[b] = mc[b:b + 1, :]
        mod_ref[me] = cols_ref[me]
        sends = []
        for k in range(1, N_DEV):
            peer, p = _peer(x, y, c_, k)
            cp = pltpu.make_async_remote_copy(src_ref=cols_ref.at[p], dst_ref=mod_ref.at[me], send_sem=s2.at[k - 1],
                                              recv_sem=r2.at[k - 1], device_id=peer, device_id_type=MESH)
            cp.start()
            sends.append(cp)
        for cp in sends:
            cp.wait()

    vm = pl.BlockSpec(memory_space=pltpu.VMEM)
    return pl.pallas_call(
        body, name="mod_forward",
        out_shape=(jax.ShapeDtypeStruct((N_DEV, 1, ncol), F32), jax.ShapeDtypeStruct((N_DEV, 1, d), F32)),
        in_specs=[vm, vm, vm], out_specs=(vm, vm),
        scratch_shapes=[pltpu.VMEM((N_DEV, 1, ncol), F32)] + [pltpu.SemaphoreType.DMA((N_DEV - 1,))] * 4,
        compiler_params=pltpu.CompilerParams(vmem_limit_bytes=VMEM_LIMIT),
    )(c, w_ada, b_ada)


def matmul(a, b, mode, out_dtype, name, tm, tn, tk, res=None, gate=None, comm=None):
    if mode == "nn":
        (m, kk), (_, n) = a.shape, b.shape
    elif mode == "nt":
        (m, kk), (n, _) = a.shape, b.shape
    else:
        (kk, m), (_, n) = a.shape, b.shape
    tm, tn, tk = min(tm, m), min(tn, n), min(tk, kk)
    assert m % tm == 0 and n % tn == 0 and kk % tk == 0, (name, m, n, kk, tm, tn, tk)
    if mode == "nn":
        a_spec = pl.BlockSpec((tm, tk), lambda j, i, k: (i, k))
        b_spec = pl.BlockSpec((tk, tn), lambda j, i, k: (k, j))
        dims = (((1,), (0,)), ((), ()))
    elif mode == "nt":
        a_spec = pl.BlockSpec((tm, tk), lambda j, i, k: (i, k))
        b_spec = pl.BlockSpec((tn, tk), lambda j, i, k: (j, k))
        dims = (((1,), (1,)), ((), ()))
    else:
        a_spec = pl.BlockSpec((tk, tm), lambda j, i, k: (k, i))
        b_spec = pl.BlockSpec((tk, tn), lambda j, i, k: (k, j))
        dims = (((0,), (0,)), ((), ()))
    nj, ni, nk = n // tn, m // tm, kk // tk
    fused = res is not None
    items = list(comm or [])
    nc = len(items)
    scat = [s_ for _, s_ in items]
    n_in = (4 if fused else 2) + nc
    n_out = (2 if fused else 1) + nc
    o_spec = pl.BlockSpec((tm, tn), lambda j, i, k: (i, j))

    def body(*refs):
        ins, outs, scratch = refs[:n_in], refs[n_in:n_in + n_out], refs[n_in + n_out:]
        a_ref, b_ref = ins[:2]
        o_ref = outs[0]
        acc_ref = scratch[0] if nk > 1 else None
        j, i, k = pl.program_id(0), pl.program_id(1), pl.program_id(2)

        def copies():
            return _exchange_copies(ins[n_in - nc:], outs[n_out - nc:], scat, *scratch[len(scratch) - 3:])

        if nc:
            @pl.when((j == 0) & (i == 0) & (k == 0))
            def _():
                for cp in copies():
                    cp.start()

        def finish(acc):
            if fused:
                o_ref[...] = ins[2][...] + ins[3][...] * acc
                outs[1][...] = acc.astype(outs[1].dtype)
            else:
                o_ref[...] = acc.astype(o_ref.dtype)

        prod = lax.dot_general(a_ref[...], b_ref[...], dims, preferred_element_type=F32)
        if nk == 1:
            finish(prod)
        else:
            @pl.when(k == 0)
            def _():
                acc_ref[...] = prod

            @pl.when(k > 0)
            def _():
                acc_ref[...] += prod

            @pl.when(k == nk - 1)
            def _():
                finish(acc_ref[...])

        if nc:
            @pl.when((j == nj - 1) & (i == ni - 1) & (k == nk - 1))
            def _():
                for cp in copies():
                    cp.wait()

    any_spec = pl.BlockSpec(memory_space=pl.ANY)
    in_specs, args = [a_spec, b_spec], [a, b]
    if fused:
        in_specs += [o_spec, pl.BlockSpec((1, tn), lambda j, i, k: (0, j))]
        args += [res, gate]
        out_shape = [jax.ShapeDtypeStruct((m, n), F32), jax.ShapeDtypeStruct((m, n), out_dtype)]
        out_specs = [o_spec, o_spec]
    else:
        out_shape = [jax.ShapeDtypeStruct((m, n), out_dtype)]
        out_specs = [o_spec]
    in_specs += [any_spec] * nc
    args += [arr for arr, _ in items]
    out_shape += _exchange_shapes(items)
    out_specs += [any_spec] * nc
    scratch = ([pltpu.VMEM((tm, tn), F32)] if nk > 1 else []) + (_exchange_sems(nc) if nc else [])
    outs = pl.pallas_call(
        body, name=name, grid=(nj, ni, nk), in_specs=in_specs, out_specs=tuple(out_specs), out_shape=tuple(out_shape),
        scratch_shapes=scratch, compiler_params=_cparams(3),
    )(*args)
    return outs[0] if len(outs) == 1 else outs


def _row_tile(s, want):
    t = min(want, s)
    assert s % t == 0 and t % HALO == 0
    return t


def prenorm(x, nw, scale, shift, name):
    s, d = x.shape
    tm = _row_tile(s, 512)

    def body(x_ref, nw_ref, sc_ref, sh_ref, o_ref):
        xv = x_ref[...]
        r = lax.rsqrt(jnp.mean(xv * xv, axis=-1, keepdims=True) + EPS)
        o_ref[...] = ((xv * r) * nw_ref[...] * (1.0 + sc_ref[...]) + sh_ref[...]).astype(BF16)

    row = pl.BlockSpec((tm, d), lambda i: (i, 0))
    vec = pl.BlockSpec((1, d), lambda i: (0, 0))
    return pl.pallas_call(body, name=name, grid=(s // tm,), in_specs=[row, vec, vec, vec], out_specs=row,
                          out_shape=jax.ShapeDtypeStruct((s, d), BF16), compiler_params=_cparams(1))(x, nw, scale, shift)


def norm_backward(du, xin, dres, nw, scale, name, mo=None, gate=None):
    s, d = xin.shape
    tm = _row_tile(s, 256)
    gated = mo is not None

    def body(*refs):
        if gated:
            du_ref, x_ref, dr_ref, nw_ref, sc_ref, mo_ref, g_ref, dx_ref, dsh_ref, dsc_ref, dnw_ref, dmo_ref, dg_ref = refs
        else:
            du_ref, x_ref, dr_ref, nw_ref, sc_ref, dx_ref, dsh_ref, dsc_ref, dnw_ref = refs
        i = pl.program_id(0)
        xv = x_ref[...]
        r = lax.rsqrt(jnp.mean(xv * xv, axis=-1, keepdims=True) + EPS)
        xn = xv * r
        duv = du_ref[...].astype(F32)
        nwv, scv = nw_ref[...], sc_ref[...]
        dxn = duv * (nwv * (1.0 + scv))
        dx = dr_ref[...] + r * (dxn - xn * jnp.mean(dxn * xn, axis=-1, keepdims=True))
        dx_ref[...] = dx
        sums = [jnp.sum(duv, axis=0, keepdims=True), jnp.sum(duv * xn * nwv, axis=0, keepdims=True),
                jnp.sum(duv * xn * (1.0 + scv), axis=0, keepdims=True)]
        accs = [dsh_ref, dsc_ref, dnw_ref]
        if gated:
            dmo_ref[...] = (dx * g_ref[...]).astype(BF16)
            sums.append(jnp.sum(dx * mo_ref[...].astype(F32), axis=0, keepdims=True))
            accs.append(dg_ref)

        @pl.when(i == 0)
        def _():
            for acc, sm in zip(accs, sums):
                acc[...] = sm

        @pl.when(i > 0)
        def _():
            for acc, sm in zip(accs, sums):
                acc[...] += sm

    row = pl.BlockSpec((tm, d), lambda i: (i, 0))
    vec = pl.BlockSpec((1, d), lambda i: (0, 0))
    vshape = jax.ShapeDtypeStruct((1, d), F32)
    in_specs, args = [row, row, row, vec, vec], [du, xin, dres, nw, scale]
    out_specs, out_shape = [row, vec, vec, vec], [jax.ShapeDtypeStruct((s, d), F32), vshape, vshape, vshape]
    if gated:
        in_specs += [row, vec]
        args += [mo, gate]
        out_specs += [row, vec]
        out_shape += [jax.ShapeDtypeStruct((s, d), BF16), vshape]
    return pl.pallas_call(body, name=name, grid=(s // tm,), in_specs=in_specs, out_specs=tuple(out_specs),
                          out_shape=tuple(out_shape), compiler_params=_cparams(1))(*args)


def merge_forward(proj, y_attn, y_rnn):
    s, d = y_attn.shape
    tm, cw = _row_tile(s, 1024), 512

    def body(ga_ref, gl_ref, ya_ref, yr_ref, o_ref):
        o_ref[...] = (_sigmoid(ga_ref[...].astype(F32)) * ya_ref[...].astype(F32)
                      + _sigmoid(gl_ref[...].astype(F32)) * yr_ref[...].astype(F32)).astype(BF16)

    def at(off):
        return pl.BlockSpec((tm, cw), lambda j, i: (i, off // cw + j))

    return pl.pallas_call(body, name="merge_forward", grid=(d // cw, s // tm),
                          in_specs=[at(GA0), at(GL0), at(0), at(0)], out_specs=at(0),
                          out_shape=jax.ShapeDtypeStruct((s, d), BF16), compiler_params=_cparams(2))(proj, proj, y_attn, y_rnn)


def merge_backward(dmerged, proj, y_attn, y_rnn):
    s, d = y_attn.shape
    tm, cw = _row_tile(s, 1024), 512

    def body(dm_ref, ga_ref, gl_ref, ya_ref, yr_ref, dga_ref, dgl_ref, dya_ref, dyr_ref):
        dm = dm_ref[...].astype(F32)
        sa, sl = _sigmoid(ga_ref[...].astype(F32)), _sigmoid(gl_ref[...].astype(F32))
        dga_ref[...] = (dm * ya_ref[...].astype(F32) * sa * (1.0 - sa)).astype(BF16)
        dgl_ref[...] = (dm * yr_ref[...].astype(F32) * sl * (1.0 - sl)).astype(BF16)
        dya_ref[...] = (dm * sa).astype(BF16)
        dyr_ref[...] = (dm * sl).astype(BF16)

    def at(off):
        return pl.BlockSpec((tm, cw), lambda j, i: (i, off // cw + j))

    o = jax.ShapeDtypeStruct((s, d), BF16)
    return pl.pallas_call(body, name="merge_backward", grid=(d // cw, s // tm),
                          in_specs=[at(0), at(GA0), at(GL0), at(0), at(0)], out_specs=(at(0),) * 4,
                          out_shape=(o, o, o, o), compiler_params=_cparams(2))(dmerged, proj, proj, y_attn, y_rnn)


def _prev_spec(tm, cw, off_blocks):
    r = tm // HALO
    return pl.BlockSpec((HALO, cw), lambda j, i: (jnp.maximum(i * r - 1, 0), off_blocks + j))


def ffn_act_forward(upp, cw_full, cb_full):
    s, f2 = upp.shape
    f = f2 // 2
    tm, cw = _row_tile(s, 512), 1536
    nj = f // cw

    def body(g_ref, gp_ref, v_ref, vp_ref, wg_ref, wv_ref, bg_ref, bv_ref, o_ref, og_ref, ov_ref):
        i = pl.program_id(1)
        first = i == 0

        def conv(x_ref, p_ref, w_ref, b_ref):
            xv = x_ref[...].astype(F32)
            pv = jnp.where(first, 0.0, p_ref[...].astype(F32))
            acc = b_ref[...] + w_ref[FFN_CONV - 1:FFN_CONV, :] * xv
            for k in range(FFN_CONV - 1):
                acc = acc + w_ref[k:k + 1, :] * _shift_down(pv, xv, FFN_CONV - 1 - k)
            return acc

        g = conv(g_ref, gp_ref, wg_ref, bg_ref)
        v = conv(v_ref, vp_ref, wv_ref, bv_ref)
        og_ref[...] = g.astype(BF16)
        ov_ref[...] = v.astype(BF16)
        o_ref[...] = (_gelu(g) * v).astype(BF16)

    def tile(ob):
        return pl.BlockSpec((tm, cw), lambda j, i: (i, ob + j))

    def par(rows, ob):
        return pl.BlockSpec((rows, cw), lambda j, i: (0, ob + j))

    o = jax.ShapeDtypeStruct((s, f), BF16)
    return pl.pallas_call(
        body, name="ffn_act_forward", grid=(nj, s // tm),
        in_specs=[tile(0), _prev_spec(tm, cw, 0), tile(nj), _prev_spec(tm, cw, nj),
                  par(FFN_CONV, 0), par(FFN_CONV, nj), par(1, 0), par(1, nj)],
        out_specs=(tile(0), tile(0), tile(0)), out_shape=(o, o, o), compiler_params=_cparams(2),
    )(upp, upp, upp, upp, cw_full, cw_full, cb_full, cb_full)


ROW_CHUNK = 16
LANE_CHUNK = 512


def ffn_act_backward(dact, up_g, up_v):
    s, f = dact.shape
    tm, cw = _row_tile(s, 512), 1536
    nr = tm // ROW_CHUNK

    def body(da_ref, g_ref, v_ref, dg_ref, dv_ref, sg_ref, sv_ref, acc_ref):
        i = pl.program_id(1)
        acc_ref[...] = jnp.zeros_like(acc_ref)

        def chunk(r, carry):
            rows = pl.ds(pl.multiple_of(r * ROW_CHUNK, ROW_CHUNK), ROW_CHUNK)
            for c0 in range(0, cw, LANE_CHUNK):
                cols = pl.ds(c0, LANE_CHUNK)
                da = da_ref[rows, cols].astype(F32)
                ge, dge = _gelu_and_grad(g_ref[rows, cols].astype(F32))
                dg = da * v_ref[rows, cols].astype(F32) * dge
                dv = da * ge
                dg_ref[rows, cols] = dg.astype(BF16)
                dv_ref[rows, cols] = dv.astype(BF16)
                acc_ref[0, :, cols] += dg[0:8] + dg[8:16]
                acc_ref[1, :, cols] += dv[0:8] + dv[8:16]
            return carry

        lax.fori_loop(0, nr, chunk, 0)
        sg = jnp.sum(acc_ref[0], axis=0, keepdims=True)
        sv = jnp.sum(acc_ref[1], axis=0, keepdims=True)

        @pl.when(i == 0)
        def _():
            sg_ref[...] = sg
            sv_ref[...] = sv

        @pl.when(i > 0)
        def _():
            sg_ref[...] += sg
            sv_ref[...] += sv

    tile = pl.BlockSpec((tm, cw), lambda j, i: (i, j))
    vec = pl.BlockSpec((1, cw), lambda j, i: (0, j))
    o = jax.ShapeDtypeStruct((s, f), BF16)
    v1 = jax.ShapeDtypeStruct((1, f), F32)
    return pl.pallas_call(
        body, name="ffn_act_backward", grid=(f // cw, s // tm), in_specs=[tile, tile, tile],
        out_specs=(tile, tile, vec, vec), out_shape=(o, o, v1, v1),
        scratch_shapes=[pltpu.VMEM((2, 8, cw), F32)], compiler_params=_cparams(2),
    )(dact, up_g, up_v)


def conv_backward(d_g, d_v, upp, cw_full):
    s, f = d_g.shape
    f2 = 2 * f
    tm, cw = _row_tile(s, 512), 1536
    nt, nj = s // tm, f // cw
    r_halo = tm // HALO
    nr = tm // ROW_CHUNK

    def body(xg_ref, ng_ref, xv_ref, nv_ref, u_ref, w_ref, o_ref, dw_ref, acc_ref):
        j, i = pl.program_id(0), pl.program_id(1)
        acc_ref[...] = jnp.zeros_like(acc_ref)

        def run(x_ref, n_ref):
            def chunk(r, carry):
                rows = pl.ds(pl.multiple_of(r * ROW_CHUNK, ROW_CHUNK), ROW_CHUNK)
                nrows = pl.ds(pl.multiple_of(jnp.minimum(r + 1, nr - 1) * ROW_CHUNK, ROW_CHUNK), ROW_CHUNK)
                for c0 in range(0, cw, LANE_CHUNK):
                    cols = pl.ds(c0, LANE_CHUNK)
                    cur = x_ref[rows, cols].astype(F32)
                    halo = jnp.where(i == nt - 1, 0.0, n_ref[:, cols].astype(F32))
                    nxt = jnp.where(r == nr - 1, halo, x_ref[nrows, cols].astype(F32))
                    uv = u_ref[rows, cols].astype(F32)
                    acc = None
                    for k in range(FFN_CONV):
                        sh = _shift_up(cur, nxt, FFN_CONV - 1 - k)
                        term = w_ref[k:k + 1, cols] * sh
                        acc = term if acc is None else acc + term
                        pr = uv * sh
                        acc_ref[k, :, cols] += pr[0:8] + pr[8:16]
                    o_ref[rows, cols] = acc.astype(BF16)
                return carry

            lax.fori_loop(0, nr, chunk, 0)

        @pl.when(j < nj)
        def _():
            run(xg_ref, ng_ref)

        @pl.when(j >= nj)
        def _():
            run(xv_ref, nv_ref)

        sums = jnp.concatenate([jnp.sum(acc_ref[k], axis=0, keepdims=True) for k in range(FFN_CONV)], axis=0)

        @pl.when(i == 0)
        def _():
            dw_ref[...] = sums

        @pl.when(i > 0)
        def _():
            dw_ref[...] += sums

    half = lambda j: j % nj
    tile_h = pl.BlockSpec((tm, cw), lambda j, i: (i, half(j)))
    next_h = pl.BlockSpec((HALO, cw), lambda j, i: (jnp.minimum((i + 1) * r_halo, s // HALO - 1), half(j)))
    tile = pl.BlockSpec((tm, cw), lambda j, i: (i, j))
    par = pl.BlockSpec((FFN_CONV, cw), lambda j, i: (0, j))
    return pl.pallas_call(
        body, name="ffn_conv_backward", grid=(2 * nj, nt), in_specs=[tile_h, next_h, tile_h, next_h, tile, par],
        out_specs=(tile, par), out_shape=(jax.ShapeDtypeStruct((s, f2), BF16), jax.ShapeDtypeStruct((FFN_CONV, f2), F32)),
        scratch_shapes=[pltpu.VMEM((FFN_CONV, 8, cw), F32)], compiler_params=_cparams(2),
    )(d_g, d_g, d_v, d_v, upp, cw_full)


def loss_head(h2, target, dn, norm_f, gate2):
    s, d = h2.shape
    tm = _row_tile(s, 256)

    def body(h_ref, t_ref, dn_ref, nf_ref, g_ref, dh_ref, ddn_ref, loss_ref, dnf_ref, dg_ref):
        i = pl.program_id(0)
        hv = h_ref[...]
        r = lax.rsqrt(jnp.mean(hv * hv, axis=-1, keepdims=True) + EPS)
        yh = hv * r
        nf = nf_ref[...]
        err = yh * nf - t_ref[...]
        dy = err * (1.0 / d)
        dyh = dy * nf
        dh = r * (dyh - yh * jnp.mean(dyh * yh, axis=-1, keepdims=True))
        dh_ref[...] = dh
        ddn_ref[...] = (dh * g_ref[...]).astype(BF16)
        sums = [jnp.sum(err * err, axis=0, keepdims=True) * (0.5 / d), jnp.sum(dy * yh, axis=0, keepdims=True),
                jnp.sum(dh * dn_ref[...].astype(F32), axis=0, keepdims=True)]
        accs = [loss_ref, dnf_ref, dg_ref]

        @pl.when(i == 0)
        def _():
            for acc, sm in zip(accs, sums):
                acc[...] = sm

        @pl.when(i > 0)
        def _():
            for acc, sm in zip(accs, sums):
                acc[...] += sm

    row = pl.BlockSpec((tm, d), lambda i: (i, 0))
    vec = pl.BlockSpec((1, d), lambda i: (0, 0))
    v = jax.ShapeDtypeStruct((1, d), F32)
    return pl.pallas_call(
        body, name="loss_head", grid=(s // tm,), in_specs=[row, row, row, vec, vec], out_specs=(row, row, vec, vec, vec),
        out_shape=(jax.ShapeDtypeStruct((s, d), F32), jax.ShapeDtypeStruct((s, d), BF16), v, v, v),
        compiler_params=_cparams(1))(h2, target, dn, norm_f, gate2)


def _t5_buckets():
    qi = np.arange(BLOCK)[:, None]
    kj = np.arange(2 * BLOCK)[None, :]
    dist = qi + BLOCK - kj
    dd = np.maximum(dist, 0)
    max_exact = NUM_BUCKETS // 2
    dflt = np.maximum(dd, 1).astype(np.float32)
    large = max_exact + (np.log(dflt / max_exact) / math.log(MAX_DISTANCE / max_exact)
                         * (NUM_BUCKETS - max_exact)).astype(np.int32)
    large = np.minimum(large, NUM_BUCKETS - 1)
    bucket = np.where(dd < max_exact, dd, large).astype(np.int32)
    in_window = (dist >= 0) & (dist < BLOCK)
    return bucket, in_window


def band_bias(rel_bias):
    bucket, in_window = _t5_buckets()
    bucket_t = jnp.asarray(np.where(in_window, bucket, -1).astype(np.int32).T)

    def body(rb_ref, bk_ref, o_ref):
        bk = bk_ref[...]
        for h in range(N_HEADS):
            acc = jnp.full((2 * BLOCK, BLOCK), NEG_INF, F32)
            for b in range(NUM_BUCKETS):
                acc = jnp.where(bk == b, rb_ref[b, h], acc)
            o_ref[h] = acc

    return pl.pallas_call(
        body, name="band_bias", out_shape=jax.ShapeDtypeStruct((N_HEADS, 2 * BLOCK, BLOCK), F32),
        in_specs=[pl.BlockSpec(memory_space=pltpu.SMEM), pl.BlockSpec(memory_space=pltpu.VMEM)],
        out_specs=pl.BlockSpec(memory_space=pltpu.VMEM))(rel_bias, bucket_t)


def rel_bias_grad(dbias):
    bucket, in_window = _t5_buckets()
    bucket_t = jnp.asarray(np.where(in_window, bucket, -1).astype(np.int32).T)

    def body(db_ref, bk_ref, o_ref):
        bk = bk_ref[...]
        rows = lax.broadcasted_iota(jnp.int32, (NUM_BUCKETS, LANES), 0)
        lanes = lax.broadcasted_iota(jnp.int32, (NUM_BUCKETS, LANES), 1)
        acc = jnp.zeros((NUM_BUCKETS, LANES), F32)
        for h in range(N_HEADS):
            dv = db_ref[h]
            for b in range(NUM_BUCKETS):
                sm = jnp.sum(jnp.where(bk == b, dv, 0.0))
                acc = jnp.where((rows == b) & (lanes == h), sm, acc)
        o_ref[...] = acc

    vm = pl.BlockSpec(memory_space=pltpu.VMEM)
    return pl.pallas_call(body, name="rel_bias_grad", out_shape=jax.ShapeDtypeStruct((NUM_BUCKETS, LANES), F32),
                          in_specs=[vm, vm], out_specs=vm)(dbias, bucket_t)


HP = 2
Q_PER_HP = D_ATTN // HP
H_PER_HP = N_HEADS // HP
NT_DIMS = (((1,), (1,)), ((), ()))
TN_DIMS = (((0,), (0,)), ((), ()))


def _stack_heads(ref, hh):
    lane = lax.broadcasted_iota(jnp.int32, (BLOCK, LANES), 1)
    lo = lane < HEAD_DIM
    parts = []
    for s_ in range(GROUP // 2):
        c0 = hh * (GROUP * HEAD_DIM) + s_ * LANES
        slab = ref[:, c0:c0 + LANES]
        parts.append(jnp.where(lo, slab, jnp.zeros_like(slab)))
        parts.append(jnp.where(lo, jnp.zeros_like(slab), slab))
    return jnp.concatenate(parts, axis=0)


def _attn_probs(hh, q_ref, kp_ref, kc_ref, vp_ref, vc_ref, bias_ref, sink_ref, hp, n):
    lane = lax.broadcasted_iota(jnp.int32, (2 * BLOCK, LANES), 1)
    own = (lane >= HEAD_DIM) if hh == 1 else (lane < HEAD_DIM)
    kband = jnp.concatenate([kp_ref[...], kc_ref[...]], axis=0)
    vband = jnp.concatenate([vp_ref[...], vc_ref[...]], axis=0)
    kk = jnp.where(own, kband, pltpu.roll(kband, HEAD_DIM, axis=1))
    vv = jnp.where(own, vband, pltpu.roll(vband, HEAD_DIM, axis=1))
    qs = _stack_heads(q_ref, hh)
    sc = lax.dot_general(kk, qs, NT_DIMS, preferred_element_type=F32) * (HEAD_DIM ** -0.5)
    sc = sc + jnp.concatenate([bias_ref[hh * GROUP + g] for g in range(GROUP)], axis=1)
    krow = lax.broadcasted_iota(jnp.int32, sc.shape, 0)
    sc = jnp.where((n == 0) & (krow < BLOCK), NEG_INF, sc)
    sink = jnp.concatenate([jnp.full((1, BLOCK), sink_ref[hp * H_PER_HP + hh * GROUP + g], F32) for g in range(GROUP)], axis=1)
    m = jnp.maximum(jnp.max(sc, axis=0, keepdims=True), sink)
    p = jnp.exp(sc - m)
    es = jnp.exp(sink - m)
    inv = 1.0 / (jnp.sum(p, axis=0, keepdims=True) + es)
    return qs, kk, vv, p * inv, es * inv


def _unstack(o, dtype):
    lane = lax.broadcasted_iota(jnp.int32, (BLOCK, LANES), 1)
    lo = lane < HEAD_DIM
    slabs = []
    for s_ in range(GROUP // 2):
        ev = o[(2 * s_) * BLOCK:(2 * s_ + 1) * BLOCK]
        od = o[(2 * s_ + 1) * BLOCK:(2 * s_ + 2) * BLOCK]
        slabs.append(jnp.where(lo, ev, od).astype(dtype))
    return slabs


def attention_forward(proj, bias, sinks, comm):
    s = proj.shape[0]
    nb = s // BLOCK
    kb, vb = K0 // LANES, V0 // LANES
    nc = len(comm)
    scat = [s_ for _, s_ in comm]

    def body(*refs):
        q_ref, kp_ref, kc_ref, vp_ref, vc_ref, bias_ref, sink_ref = refs[:7]
        srcs, o_ref, dsts, sems = refs[7:7 + nc], refs[7 + nc], refs[8 + nc:8 + 2 * nc], refs[8 + 2 * nc:]
        hp, n = pl.program_id(0), pl.program_id(1)

        @pl.when((hp == 0) & (n == 0))
        def _():
            for cp in _exchange_copies(srcs, dsts, scat, *sems):
                cp.start()

        for hh in range(2):
            qs, kk, vv, probs, _ = _attn_probs(hh, q_ref, kp_ref, kc_ref, vp_ref, vc_ref, bias_ref, sink_ref, hp, n)
            o = lax.dot_general(probs.astype(BF16), vv, TN_DIMS, preferred_element_type=F32)
            for s_, slab in enumerate(_unstack(o, BF16)):
                c0 = hh * (GROUP * HEAD_DIM) + s_ * LANES
                o_ref[:, c0:c0 + LANES] = slab

        @pl.when((hp == HP - 1) & (n == nb - 1))
        def _():
            for cp in _exchange_copies(srcs, dsts, scat, *sems):
                cp.wait()

    qspec = pl.BlockSpec((BLOCK, Q_PER_HP), lambda hp, n: (n, hp))
    any_spec = pl.BlockSpec(memory_space=pl.ANY)

    def kv(base, prev):
        if prev:
            return pl.BlockSpec((BLOCK, LANES), lambda hp, n: (jnp.maximum(n - 1, 0), base + hp))
        return pl.BlockSpec((BLOCK, LANES), lambda hp, n: (n, base + hp))

    return pl.pallas_call(
        body, name="attention_forward", grid=(HP, nb),
        in_specs=[qspec, kv(kb, True), kv(kb, False), kv(vb, True), kv(vb, False),
                  pl.BlockSpec((H_PER_HP, 2 * BLOCK, BLOCK), lambda hp, n: (hp, 0, 0)),
                  pl.BlockSpec(memory_space=pltpu.SMEM)] + [any_spec] * nc,
        out_specs=(qspec,) + (any_spec,) * nc,
        out_shape=(jax.ShapeDtypeStruct((s, D_ATTN), BF16),) + tuple(_exchange_shapes(comm)),
        scratch_shapes=_exchange_sems(nc), compiler_params=_cparams(2),
    )(proj, proj, proj, proj, proj, bias, sinks, *[arr for arr, _ in comm])


def attention_backward(proj, datt, bias, sinks, comm):
    s = proj.shape[0]
    nb = s // BLOCK
    kb, vb = K0 // LANES, V0 // LANES
    nc = len(comm)
    scat = [s_ for _, s_ in comm]

    def body(*refs):
        q_ref, kp_ref, kc_ref, vp_ref, vc_ref, do_ref, bias_ref, sink_ref = refs[:8]
        srcs = refs[8:8 + nc]
        dq_ref, dk_ref, dv_ref, dbias_ref, dsink_ref = refs[8 + nc:13 + nc]
        dsts = refs[13 + nc:13 + 2 * nc]
        kcar_ref, vcar_ref, sacc_ref = refs[13 + 2 * nc:16 + 2 * nc]
        sems = refs[16 + 2 * nc:]
        hp, n = pl.program_id(0), pl.program_id(1)

        @pl.when((hp == 0) & (n == 0))
        def _():
            for cp in _exchange_copies(srcs, dsts, scat, *sems):
                cp.start()

        @pl.when(n == 0)
        def _():
            kcar_ref[...] = jnp.zeros_like(kcar_ref)
            vcar_ref[...] = jnp.zeros_like(vcar_ref)
            dbias_ref[...] = jnp.zeros_like(dbias_ref)
            sacc_ref[...] = jnp.zeros_like(sacc_ref)

        @pl.when(n < nb)
        def _():
            lane2 = lax.broadcasted_iota(jnp.int32, (2 * BLOCK, LANES), 1)
            dk_band = jnp.zeros((2 * BLOCK, LANES), F32)
            dv_band = jnp.zeros((2 * BLOCK, LANES), F32)
            for hh in range(2):
                qs, kk, vv, probs, psink = _attn_probs(hh, q_ref, kp_ref, kc_ref, vp_ref, vc_ref, bias_ref, sink_ref, hp, n)
                dos = _stack_heads(do_ref, hh)
                dp = lax.dot_general(vv, dos, NT_DIMS, preferred_element_type=F32)
                dsum = jnp.sum(probs * dp, axis=0, keepdims=True)
                ds = probs * (dp - dsum)
                for g in range(GROUP):
                    dbias_ref[hh * GROUP + g] += ds[:, g * BLOCK:(g + 1) * BLOCK]
                sacc_ref[hh:hh + 1, :] += -psink * dsum
                dsb = (ds * (HEAD_DIM ** -0.5)).astype(BF16)
                pb = probs.astype(BF16)
                dq = lax.dot_general(dsb, kk, TN_DIMS, preferred_element_type=F32)
                for s_, slab in enumerate(_unstack(dq, BF16)):
                    c0 = hh * (GROUP * HEAD_DIM) + s_ * LANES
                    dq_ref[:, c0:c0 + LANES] = slab
                dkh = jnp.dot(dsb, qs, preferred_element_type=F32)
                dvh = jnp.dot(pb, dos, preferred_element_type=F32)
                own = (lane2 >= HEAD_DIM) if hh == 1 else (lane2 < HEAD_DIM)
                dk_band = dk_band + jnp.where(own, dkh + pltpu.roll(dkh, HEAD_DIM, axis=1), 0.0)
                dv_band = dv_band + jnp.where(own, dvh + pltpu.roll(dvh, HEAD_DIM, axis=1), 0.0)
            dk_ref[...] = (kcar_ref[...] + dk_band[:BLOCK]).astype(BF16)
            dv_ref[...] = (vcar_ref[...] + dv_band[:BLOCK]).astype(BF16)
            kcar_ref[...] = dk_band[BLOCK:]
            vcar_ref[...] = dv_band[BLOCK:]

        @pl.when(n == nb)
        def _():
            dk_ref[...] = kcar_ref[...].astype(BF16)
            dv_ref[...] = vcar_ref[...].astype(BF16)
            rows = [jnp.full((1, LANES), jnp.sum(sacc_ref[hh:hh + 1, g * BLOCK:(g + 1) * BLOCK]), F32)
                    for hh in range(2) for g in range(GROUP)]
            dsink_ref[...] = jnp.concatenate(rows, axis=0)

        @pl.when((hp == HP - 1) & (n == nb))
        def _():
            for cp in _exchange_copies(srcs, dsts, scat, *sems):
                cp.wait()

    qspec = pl.BlockSpec((BLOCK, Q_PER_HP), lambda hp, n: (jnp.minimum(n, nb - 1), hp))
    any_spec = pl.BlockSpec(memory_space=pl.ANY)

    def kv(base, prev):
        if prev:
            return pl.BlockSpec((BLOCK, LANES), lambda hp, n: (jnp.maximum(jnp.minimum(n, nb - 1) - 1, 0), base + hp))
        return pl.BlockSpec((BLOCK, LANES), lambda hp, n: (jnp.minimum(n, nb - 1), base + hp))

    dkv_spec = pl.BlockSpec((BLOCK, LANES), lambda hp, n: (jnp.maximum(n - 1, 0), hp))
    return pl.pallas_call(
        body, name="attention_backward", grid=(HP, nb + 1),
        in_specs=[qspec, kv(kb, True), kv(kb, False), kv(vb, True), kv(vb, False), qspec,
                  pl.BlockSpec((H_PER_HP, 2 * BLOCK, BLOCK), lambda hp, n: (hp, 0, 0)),
                  pl.BlockSpec(memory_space=pltpu.SMEM)] + [any_spec] * nc,
        out_specs=(qspec, dkv_spec, dkv_spec,
                   pl.BlockSpec((H_PER_HP, 2 * BLOCK, BLOCK), lambda hp, n: (hp, 0, 0)),
                   pl.BlockSpec((H_PER_HP, LANES), lambda hp, n: (hp, 0))) + (any_spec,) * nc,
        out_shape=(jax.ShapeDtypeStruct((s, D_ATTN), BF16), jax.ShapeDtypeStruct((s, D_KV), BF16),
                   jax.ShapeDtypeStruct((s, D_KV), BF16), jax.ShapeDtypeStruct((N_HEADS, 2 * BLOCK, BLOCK), F32),
                   jax.ShapeDtypeStruct((N_HEADS, LANES), F32)) + tuple(_exchange_shapes(comm)),
        scratch_shapes=[pltpu.VMEM((BLOCK, LANES), F32), pltpu.VMEM((BLOCK, LANES), F32),
                        pltpu.VMEM((8, GROUP * BLOCK), F32)] + _exchange_sems(nc),
        compiler_params=_cparams(2),
    )(proj, proj, proj, proj, proj, datt, bias, sinks, *[arr for arr, _ in comm])


def _neg_expm1(x):
    series = -(x * (1.0 + x * (1.0 / 2 + x * (1.0 / 6 + x * (1.0 / 24 + x * (1.0 / 120 + x * (1.0 / 720)))))))
    return jnp.where(x > -0.25, series, 1.0 - jnp.exp(x))


def _softplus_neg(lam):
    u = jnp.exp(-jnp.abs(lam))
    w = 1.0 + u
    log1p = jnp.where(w == 1.0, u, jnp.log(w) * u / jnp.where(w == 1.0, 1.0, w - 1.0))
    sp = jnp.maximum(-lam, 0.0) + log1p
    return sp, -_sigmoid(-lam)


def _rnn_gates(x_ref, xp_ref, cw_ref, cb_ref, wa_ref, ba_ref, wi_ref, bi_ref, lam_ref, first, row0):
    xv = x_ref[...].astype(F32)
    pv = jnp.where(first, 0.0, xp_ref[...].astype(F32))
    xs = [_shift_down(pv, xv, RNN_CONV - 1 - k) for k in range(RNN_CONV)]
    xc = cb_ref[...]
    for k in range(RNN_CONV):
        xc = xc + cw_ref[k:k + 1, :] * xs[k]
    xcb = xc.astype(BF16)
    ra = _sigmoid(jnp.dot(xcb, wa_ref[...], preferred_element_type=F32) + ba_ref[...])
    ri = _sigmoid(jnp.dot(xcb, wi_ref[...], preferred_element_type=F32) + bi_ref[...])
    sp, dsp = _softplus_neg(lam_ref[...])
    la = (-RG_C) * ra * sp
    a = jnp.exp(la)
    t = row0 + lax.broadcasted_iota(jnp.int32, xv.shape, 0)
    start = t == 0
    mult = jnp.where(start, 1.0, jnp.sqrt(_neg_expm1(2.0 * la)))
    return xs, xc, xcb, ra, ri, sp, dsp, a, mult, start


def _rnn_specs(t_rows, s, rev):
    nt = s // t_rows
    r = t_rows // HALO
    xb, gb = XR0 // LANES, GR0 // LANES
    ti = (lambda i: nt - 1 - i) if rev else (lambda i: i)
    tile = lambda base: pl.BlockSpec((t_rows, LANES), lambda n, i: (ti(i), base + n))
    prev = lambda base: pl.BlockSpec((HALO, LANES), lambda n, i: (jnp.maximum(ti(i) * r - 1, 0), base + n))
    par = lambda rows: pl.BlockSpec((rows, LANES), lambda n, i: (0, n))
    mat = pl.BlockSpec((None, RNN_BLOCK, RNN_BLOCK), lambda n, i: (n, 0, 0))
    return nt, ti, tile, prev, par, mat, xb, gb


def rnn_forward(proj, cw, cb, wa, ba, wi, bi, lam):
    s = proj.shape[0]
    t_rows = _row_tile(s, 512)
    nt, ti, tile, prev, par, mat, xb, gb = _rnn_specs(t_rows, s, False)

    def body(x_ref, xp_ref, g_ref, cw_ref, cb_ref, wa_ref, ba_ref, wi_ref, bi_ref, lam_ref, z_ref, h_ref, car_ref):
        i = pl.program_id(1)
        first = i == 0
        _, xc, _, _, ri, _, _, a, mult, _ = _rnn_gates(x_ref, xp_ref, cw_ref, cb_ref, wa_ref, ba_ref, wi_ref, bi_ref,
                                                       lam_ref, first, i * t_rows)
        aa, bb = a, mult * ri * xc
        rows = lax.broadcasted_iota(jnp.int32, aa.shape, 0)
        d = 1
        while d < t_rows:
            keep = rows >= d
            a_s, b_s = pltpu.roll(aa, d, axis=0), pltpu.roll(bb, d, axis=0)
            bb = jnp.where(keep, aa * b_s + bb, bb)
            aa = jnp.where(keep, aa * a_s, aa)
            d *= 2
        carry = jnp.where(first, 0.0, car_ref[0:1, :])
        h = aa * carry + bb
        car_ref[...] = jnp.broadcast_to(h[t_rows - 1:t_rows, :], car_ref.shape)
        h_ref[...] = h.astype(BF16)
        z_ref[...] = (h * _gelu(g_ref[...].astype(F32))).astype(BF16)

    o = jax.ShapeDtypeStruct((s, D_RNN), BF16)
    out_tile = pl.BlockSpec((t_rows, LANES), lambda n, i: (i, n))
    return pl.pallas_call(
        body, name="rnn_forward", grid=(N_RNN_BLOCKS, nt),
        in_specs=[tile(xb), prev(xb), tile(gb), par(RNN_CONV), par(1), mat, par(1), mat, par(1), par(1)],
        out_specs=(out_tile, out_tile), out_shape=(o, o), scratch_shapes=[pltpu.VMEM((8, LANES), F32)],
        compiler_params=_cparams(2),
    )(proj, proj, proj, cw, cb, wa, ba, wi, bi, lam)


def rnn_backward(proj, h, dz, cw, cb, wa, ba, wi, bi, lam):
    s = proj.shape[0]
    t_rows = _row_tile(s, 512)
    nt, ti, tile, prev, par, mat, xb, gb = _rnn_specs(t_rows, s, True)
    r = t_rows // HALO

    def body(x_ref, xp_ref, g_ref, h_ref, hp_ref, dz_ref, cw_ref, cb_ref, wa_ref, ba_ref, wi_ref, bi_ref, lam_ref,
             dx_ref, dg_ref, dwa_ref, dwi_ref, sm_ref, gcar_ref, xcar_ref):
        i = pl.program_id(1)
        it = nt - 1 - i
        first, last = it == 0, it == nt - 1
        xs, xc, xcb, ra, ri, sp, dsp, a, mult, start = _rnn_gates(
            x_ref, xp_ref, cw_ref, cb_ref, wa_ref, ba_ref, wi_ref, bi_ref, lam_ref, first, it * t_rows)
        hf = h_ref[...].astype(F32)
        hprev = _shift_down(jnp.where(first, 0.0, hp_ref[...].astype(F32)), hf, 1)
        ge, dge = _gelu_and_grad(g_ref[...].astype(F32))
        dz = dz_ref[...].astype(F32)
        dg_ref[...] = (dz * hf * dge).astype(BF16)
        rows = lax.broadcasted_iota(jnp.int32, hf.shape, 0)
        tail = rows == t_rows - 1
        carry = jnp.where(last, 0.0, gcar_ref[0:1, :])
        bb = dz * ge + jnp.where(tail, carry, 0.0)
        aa = jnp.where(tail, 0.0, pltpu.roll(a, t_rows - 1, axis=0))
        d = 1
        while d < t_rows:
            keep = rows < t_rows - d
            a_s, b_s = pltpu.roll(aa, t_rows - d, axis=0), pltpu.roll(bb, t_rows - d, axis=0)
            bb = jnp.where(keep, bb + aa * b_s, bb)
            aa = jnp.where(keep, aa * a_s, aa)
            d *= 2
        gg = bb
        gcar_ref[...] = jnp.broadcast_to(a[0:1, :] * gg[0:1, :], gcar_ref.shape)
        da = gg * hprev
        dmult = jnp.where(start, 0.0, gg * ri * xc)
        dri = gg * mult * xc
        dxc = gg * mult * ri
        safe_mult = jnp.where(start, 1.0, mult)
        dla = da * a - dmult * (a * a) / safe_mult
        dra = dla * ((-RG_C) * sp)
        dlam = jnp.sum(dla * ((-RG_C) * ra), axis=0, keepdims=True) * dsp
        dpa = dra * ra * (1.0 - ra)
        dpi = dri * ri * (1.0 - ri)
        dpab, dpib = dpa.astype(BF16), dpi.astype(BF16)
        nt_dims = (((1,), (1,)), ((), ()))
        tn_dims = (((0,), (0,)), ((), ()))
        dxc = dxc + lax.dot_general(dpab, wa_ref[...], nt_dims, preferred_element_type=F32) \
            + lax.dot_general(dpib, wi_ref[...], nt_dims, preferred_element_type=F32)
        dwa = lax.dot_general(xcb, dpab, tn_dims, preferred_element_type=F32)
        dwi = lax.dot_general(xcb, dpib, tn_dims, preferred_element_type=F32)
        nxt = jnp.where(last, 0.0, xcar_ref[...])
        dx = cw_ref[RNN_CONV - 1:RNN_CONV, :] * dxc
        for k in range(RNN_CONV - 1):
            dx = dx + cw_ref[k:k + 1, :] * _shift_up(dxc, nxt, RNN_CONV - 1 - k)
        dx_ref[...] = dx.astype(BF16)
        xcar_ref[...] = dxc[0:HALO, :]
        small = jnp.concatenate(
            [jnp.sum(dpa, axis=0, keepdims=True), jnp.sum(dpi, axis=0, keepdims=True), dlam,
             jnp.sum(dxc, axis=0, keepdims=True)]
            + [jnp.sum(dxc * xs[k], axis=0, keepdims=True) for k in range(RNN_CONV)], axis=0)

        @pl.when(i == 0)
        def _():
            dwa_ref[...] = dwa
            dwi_ref[...] = dwi
            sm_ref[...] = small

        @pl.when(i > 0)
        def _():
            dwa_ref[...] += dwa
            dwi_ref[...] += dwi
            sm_ref[...] += small

    o = jax.ShapeDtypeStruct((s, D_RNN), BF16)
    plain = pl.BlockSpec((t_rows, LANES), lambda n, i: (ti(i), n))
    plain_prev = pl.BlockSpec((HALO, LANES), lambda n, i: (jnp.maximum(ti(i) * r - 1, 0), n))
    return pl.pallas_call(
        body, name="rnn_backward", grid=(N_RNN_BLOCKS, nt),
        in_specs=[tile(xb), prev(xb), tile(gb), plain, plain_prev, plain,
                  par(RNN_CONV), par(1), mat, par(1), mat, par(1), par(1)],
        out_specs=(plain, plain, mat, mat, pl.BlockSpec((None, 8, LANES), lambda n, i: (n, 0, 0))),
        out_shape=(o, o, jax.ShapeDtypeStruct((N_RNN_BLOCKS, RNN_BLOCK, RNN_BLOCK), F32),
                   jax.ShapeDtypeStruct((N_RNN_BLOCKS, RNN_BLOCK, RNN_BLOCK), F32),
                   jax.ShapeDtypeStruct((N_RNN_BLOCKS, 8, LANES), F32)),
        scratch_shapes=[pltpu.VMEM((8, LANES), F32), pltpu.VMEM((HALO, LANES), F32)],
        compiler_params=_cparams(2),
    )(proj, proj, proj, h, h, dz, cw, cb, wa, ba, wi, bi, lam)


def adamw(parts, w, m, v, name, rows=256):
    p, r, c = parts.shape
    tr = min(rows, r)
    assert r % tr == 0

    def body(p_ref, w_ref, m_ref, v_ref, g_ref, d_ref, nm_ref, nv_ref):
        g = p_ref[0].astype(F32)
        for q in range(1, p):
            g = g + p_ref[q].astype(F32)
        nm = ADAM_B1 * m_ref[...] + (1.0 - ADAM_B1) * g
        nv = ADAM_B2 * v_ref[...] + (1.0 - ADAM_B2) * (g * g)
        mh = nm / (1.0 - ADAM_B1 ** ADAM_STEP)
        vh = nv / (1.0 - ADAM_B2 ** ADAM_STEP)
        g_ref[...] = g
        d_ref[...] = (-ADAM_LR) * (mh / (jnp.sqrt(vh) + ADAM_EPS) + ADAM_WD * w_ref[...])
        nm_ref[...] = nm
        nv_ref[...] = nv

    pspec = pl.BlockSpec((p, tr, c), lambda i: (0, i, 0))
    spec = pl.BlockSpec((tr, c), lambda i: (i, 0))
    o = jax.ShapeDtypeStruct((r, c), F32)
    return pl.pallas_call(body, name=name, grid=(r // tr,), in_specs=[pspec, spec, spec, spec],
                          out_specs=(spec,) * 4, out_shape=(o, o, o, o), compiler_params=_cparams(1))(parts, w, m, v)


def ada_weight_grad(c_t, dmod):
    d, nb = c_t.shape
    c = dmod.shape[1]
    tr = 512

    def body(c_ref, dm_ref, o_ref):
        cv = c_ref[...]
        cs = cv * _sigmoid(cv)
        acc = cs[:, 0:1] * dm_ref[0:1, :]
        for b in range(1, nb):
            acc = acc + cs[:, b:b + 1] * dm_ref[b:b + 1, :]
        o_ref[...] = acc

    return pl.pallas_call(body, name="ada_weight_grad", grid=(d // tr,),
                          in_specs=[pl.BlockSpec((tr, nb), lambda i: (i, 0)), pl.BlockSpec((nb, c), lambda i: (0, 0))],
                          out_specs=pl.BlockSpec((tr, c), lambda i: (i, 0)),
                          out_shape=jax.ShapeDtypeStruct((d, c), F32), compiler_params=_cparams(1))(c_t, dmod)


def _rows128(a):
    flat = a.reshape(-1).astype(F32)
    pad = (-flat.shape[0]) % (8 * LANES)
    if pad:
        flat = jnp.concatenate([flat, jnp.zeros((pad,), F32)])
    return flat.reshape(-1, LANES)


def kernel(x, c, w_ada, b_ada, norm1, w_in, rnn_conv_w, rnn_conv_b, w_rg_a, b_rg_a, w_rg_i, b_rg_i, rg_lambda, w_o_rnn, w_o_attn, attn_sinks, rel_bias, w_out, norm2, w_up, ffn_conv_w, ffn_conv_b, w_down, norm_f, loss_target, m_w_ada, m_b_ada, m_norm1, m_w_in, m_rnn_conv_w, m_rnn_conv_b, m_w_rg_a, m_b_rg_a, m_w_rg_i, m_b_rg_i, m_rg_lambda, m_w_o_rnn, m_w_o_attn, m_attn_sinks, m_rel_bias, m_w_out, m_norm2, m_w_up, m_ffn_conv_w, m_ffn_conv_b, m_w_down, m_norm_f, v_w_ada, v_b_ada, v_norm1, v_w_in, v_rnn_conv_w, v_rnn_conv_b, v_w_rg_a, v_b_rg_a, v_w_rg_i, v_b_rg_i, v_rg_lambda, v_w_o_rnn, v_w_o_attn, v_attn_sinks, v_rel_bias, v_w_out, v_norm2, v_w_up, v_ffn_conv_w, v_ffn_conv_b, v_w_down, v_norm_f):
    me = 4 * lax.axis_index("x") + 2 * lax.axis_index("y") + lax.axis_index("c")
    xs = x[0]
    tgt = loss_target[0]
    s, d = xs.shape
    bf = lambda w: w[0].astype(BF16)

    mod, c_all = mod_forward(c.reshape(1, 1, d), w_ada[0], b_ada)
    mod = mod.reshape(6, d)
    shift1, scale1, gate1, shift2, scale2, gate2 = [mod[i:i + 1] for i in range(6)]

    g_in, g_rcw, g_fcw = exchange([(bf(w_in), False), (rnn_conv_w[0], False), (ffn_conv_w[0], False)], "gather_w_in")
    w_in_f = jnp.transpose(g_in, (1, 0, 2)).reshape(d, D_IN)
    rcw = jnp.transpose(g_rcw, (1, 0, 2)).reshape(RNN_CONV, D_RNN)
    fcw = jnp.transpose(g_fcw, (1, 0, 2)).reshape(FFN_CONV, 2 * D_FF)
    wa_b, wi_b = bf(w_rg_a), bf(w_rg_i)
    sinks = attn_sinks[0]

    u = prenorm(xs, norm1, scale1, shift1, "prenorm1")
    proj, g_oa, g_or, g_out, g_down = matmul(
        u, w_in_f, "nn", BF16, "mm_in", 1024, 512, 2048,
        comm=[(bf(w_o_attn), False), (bf(w_o_rnn), False), (bf(w_out), False), (bf(w_down), False)])
    w_oa_f, w_or_f = g_oa.reshape(D_ATTN, d), g_or.reshape(D_RNN, d)
    w_out_f, w_down_f = g_out.reshape(d, d), g_down.reshape(D_FF, d)
    bias = band_bias(rel_bias)
    att, g_up = attention_forward(proj, bias, sinks, [(bf(w_up), False)])
    w_up_f = jnp.transpose(g_up, (1, 0, 2)).reshape(d, 2 * D_FF)
    z, hr = rnn_forward(proj, rcw, rnn_conv_b, wa_b, b_rg_a, wi_b, b_rg_i, rg_lambda)
    y_attn = matmul(att, w_oa_f, "nn", BF16, "mm_o_attn", 1024, 1024, 2048)
    y_rnn = matmul(z, w_or_f, "nn", BF16, "mm_o_rnn", 1024, 1024, 2560)
    merged = merge_forward(proj, y_attn, y_rnn)
    h1, mo = matmul(merged, w_out_f, "nn", BF16, "mm_out", 512, 1024, 2048, res=xs, gate=gate1)
    u2 = prenorm(h1, norm2, scale2, shift2, "prenorm2")
    upp = matmul(u2, w_up_f, "nn", BF16, "mm_up", 1024, 1024, 2048)
    act, up_g, up_v = ffn_act_forward(upp, fcw, ffn_conv_b)
    h2, dn = matmul(act, w_down_f, "nn", BF16, "mm_down", 512, 1024, 2048, res=h1, gate=gate2)

    dh2, d_dn, loss_cols, d_norm_f, d_gate2 = loss_head(h2, tgt, dn, norm_f.reshape(1, d), gate2)
    loss = lax.psum(jnp.sum(loss_cols), ("x", "y", "c"))

    d_act = matmul(d_dn, w_down_f, "nt", BF16, "mm_down_dx", 1024, 1536, 2048)
    g_w_down = matmul(act, d_dn, "tn", BF16, "mm_down_dw", 1536, 1024, 2048)
    d_g, d_v, d_fcb_g, d_fcb_v = ffn_act_backward(d_act, up_g, up_v)
    d_upp, d_fcw = conv_backward(d_g, d_v, upp, fcw)
    d_fcb = jnp.concatenate([d_fcb_g, d_fcb_v], axis=1)
    d_u2, p_down = matmul(d_upp, w_up_f, "nt", BF16, "mm_up_dx", 1024, 1024, 3072,
                          comm=[(g_w_down.reshape(N_DEV, D_FF // N_DEV, d), True)])
    g_w_up = matmul(u2, d_upp, "tn", BF16, "mm_up_dw", 1024, 1024, 4096)
    g_up_blk = jnp.transpose(g_w_up.reshape(d, N_DEV, 2 * D_FF // N_DEV), (1, 0, 2))
    dh1, d_shift2, d_scale2, d_norm2, d_mo, d_gate1 = norm_backward(d_u2, h1, dh2, norm2, scale2, "norm2_backward",
                                                                    mo=mo, gate=gate1)
    d_merged = matmul(d_mo, w_out_f, "nt", BF16, "mm_out_dx", 1024, 1024, 2048)
    g_w_out = matmul(merged, d_mo, "tn", BF16, "mm_out_dw", 2048, 1024, 2048)
    d_ga, d_gl, d_ya, d_yr = merge_backward(d_merged, proj, y_attn, y_rnn)
    d_att = matmul(d_ya, w_oa_f, "nt", BF16, "mm_o_attn_dx", 1024, 1024, 2048)
    g_w_oa = matmul(att, d_ya, "tn", BF16, "mm_o_attn_dw", 2048, 1024, 2048)
    d_z = matmul(d_yr, w_or_f, "nt", BF16, "mm_o_rnn_dx", 1024, 1280, 2048)
    g_w_or = matmul(z, d_yr, "tn", BF16, "mm_o_rnn_dw", 1280, 1024, 2048)
    d_xr, d_gr, d_wa, d_wi, d_rsmall = rnn_backward(proj, hr, d_z, rcw, rnn_conv_b, wa_b, b_rg_a, wi_b, b_rg_i, rg_lambda)
    d_q, d_k, d_v_, d_bias, d_sink, p_up = attention_backward(proj, d_att, bias, sinks, [(g_up_blk, True)])
    d_rel = rel_bias_grad(d_bias)
    d_proj = jnp.concatenate([d_q, d_k, d_v_, d_xr, d_gr, d_ga, d_gl], axis=1)

    def rsmall_of(ba_, bi_, lam_, cb_):
        return jnp.stack([ba_[0].reshape(N_RNN_BLOCKS, LANES), bi_[0].reshape(N_RNN_BLOCKS, LANES),
                          lam_[0].reshape(N_RNN_BLOCKS, LANES), cb_[0].reshape(N_RNN_BLOCKS, LANES)]
                         + [jnp.zeros((N_RNN_BLOCKS, LANES), F32)] * 4, axis=1)

    pack = lambda t: jnp.concatenate([_rows128(q) for q in t], axis=0)
    late_n = 2
    g_early = [d_norm2, d_norm_f, d_rsmall, d_wa, d_wi, d_sink[:, 0], d_rel[:, :N_HEADS], d_fcb, d_fcw]

    g_w_in, p_out, p_oa, p_or, small_early = matmul(
        u, d_proj, "tn", BF16, "mm_in_dw", 2048, 512, 2048,
        comm=[(g_w_out.reshape(N_DEV, d // N_DEV, d), True), (g_w_oa.reshape(N_DEV, D_ATTN // N_DEV, d), True),
              (g_w_or.reshape(N_DEV, D_RNN // N_DEV, d), True), (pack(g_early), False)])
    g_in_blk = jnp.transpose(g_w_in.reshape(d, N_DEV, D_IN // N_DEV), (1, 0, 2))
    d_u, p_in = matmul(d_proj, w_in_f, "nt", BF16, "mm_in_dx", 1024, 1024, 2944, comm=[(g_in_blk, True)])
    grad_x, d_shift1, d_scale1, d_norm1 = norm_backward(d_u, xs, dh1, norm1, scale1, "norm1_backward")
    d_mod = jnp.concatenate([d_shift1, d_scale1, d_gate1, d_shift2, d_scale2, d_gate2], axis=1)
    small_late = exchange([(pack([d_mod, d_norm1]), False)], "gather_late_grads")[0]

    g_list = [d_mod, d_norm1] + g_early
    w_list = [b_ada, norm1, norm2, norm_f, rsmall_of(b_rg_a, b_rg_i, rg_lambda, rnn_conv_b), w_rg_a, w_rg_i,
              attn_sinks, rel_bias, ffn_conv_b, jnp.zeros_like(d_fcw)]
    m_list = [m_b_ada, m_norm1, m_norm2, m_norm_f, rsmall_of(m_b_rg_a, m_b_rg_i, m_rg_lambda, m_rnn_conv_b), m_w_rg_a,
              m_w_rg_i, m_attn_sinks, m_rel_bias, m_ffn_conv_b, jnp.zeros_like(d_fcw)]
    v_list = [v_b_ada, v_norm1, v_norm2, v_norm_f, rsmall_of(v_b_rg_a, v_b_rg_i, v_rg_lambda, v_rnn_conv_b), v_w_rg_a,
              v_w_rg_i, v_attn_sinks, v_rel_bias, v_ffn_conv_b, jnp.ones_like(d_fcw)]
    sizes = [_rows128(q).shape[0] for q in g_list]
    offs = np.concatenate([[0], np.cumsum(sizes)]).tolist()
    r_late = offs[late_n]
    late = adamw(small_late, pack(w_list[:late_n]), pack(m_list[:late_n]), pack(v_list[:late_n]), "adamw_small_late",
                 rows=r_late)
    early = adamw(small_early, pack(w_list[late_n:]), pack(m_list[late_n:]), pack(v_list[late_n:]), "adamw_small_early",
                  rows=(offs[-1] - r_late) // 7)

    def seg(packed, idx, like):
        n_el = int(np.prod(like.shape))
        return packed[offs[idx]:offs[idx + 1]].reshape(-1)[:n_el].reshape(like.shape)

    def unpack(kind):
        packed = jnp.concatenate([late[kind], early[kind]], axis=0)
        rs = seg(packed, 4, d_rsmall)
        out = dict(
            b_ada=seg(packed, 0, b_ada), norm1=seg(packed, 1, norm1), norm2=seg(packed, 2, norm2),
            norm_f=seg(packed, 3, norm_f), b_rg_a=rs[:, 0].reshape(1, D_RNN), b_rg_i=rs[:, 1].reshape(1, D_RNN),
            rg_lambda=rs[:, 2].reshape(1, D_RNN), rnn_conv_b=rs[:, 3].reshape(1, D_RNN),
            w_rg_a=seg(packed, 5, w_rg_a), w_rg_i=seg(packed, 6, w_rg_i), attn_sinks=seg(packed, 7, attn_sinks),
            rel_bias=seg(packed, 8, rel_bias), ffn_conv_b=seg(packed, 9, ffn_conv_b))
        out["rnn_conv_w_full"] = jnp.transpose(rs[:, 4:8], (1, 0, 2)).reshape(RNN_CONV, D_RNN)
        out["ffn_conv_w_full"] = seg(packed, 10, d_fcw)
        return out

    small = [unpack(kind) for kind in range(4)]

    rcw_cols = D_RNN // N_DEV
    fcw_cols = 2 * D_FF // N_DEV
    g_rcw_ = lax.dynamic_slice(small[0]["rnn_conv_w_full"], (0, me * rcw_cols), (RNN_CONV, rcw_cols))
    g_fcw_ = lax.dynamic_slice(small[0]["ffn_conv_w_full"], (0, me * fcw_cols), (FFN_CONV, fcw_cols))
    r_rcw = adamw(g_rcw_[None], rnn_conv_w[0], m_rnn_conv_w[0], v_rnn_conv_w[0], "adamw_rnn_conv_w")
    r_fcw = adamw(g_fcw_[None], ffn_conv_w[0], m_ffn_conv_w[0], v_ffn_conv_w[0], "adamw_ffn_conv_w")

    ada_cols = 6 * d // N_DEV
    dmod_all = small_late[:, offs[0]:offs[1]].reshape(N_DEV, 6 * d)
    dmod_cols = lax.dynamic_slice(dmod_all, (0, me * ada_cols), (N_DEV, ada_cols))
    g_ada = ada_weight_grad(jnp.transpose(c_all.reshape(N_DEV, d)), dmod_cols)
    r_ada = adamw(g_ada[None], w_ada[0], m_w_ada[0], v_w_ada[0], "adamw_w_ada")

    r_in = adamw(p_in, w_in[0], m_w_in[0], v_w_in[0], "adamw_w_in")
    r_up = adamw(p_up, w_up[0], m_w_up[0], v_w_up[0], "adamw_w_up")
    r_or = adamw(p_or, w_o_rnn[0], m_w_o_rnn[0], v_w_o_rnn[0], "adamw_w_o_rnn", rows=160)
    r_oa = adamw(p_oa, w_o_attn[0], m_w_o_attn[0], v_w_o_attn[0], "adamw_w_o_attn")
    r_out = adamw(p_out, w_out[0], m_w_out[0], v_w_out[0], "adamw_w_out")
    r_down = adamw(p_down, w_down[0], m_w_down[0], v_w_down[0], "adamw_w_down")

    def res(kind):
        sm = small[kind]
        return [r_ada[kind][None], sm["b_ada"], sm["norm1"], r_in[kind][None], r_rcw[kind][None], sm["rnn_conv_b"],
                sm["w_rg_a"], sm["b_rg_a"], sm["w_rg_i"], sm["b_rg_i"], sm["rg_lambda"], r_or[kind][None],
                r_oa[kind][None], sm["attn_sinks"], sm["rel_bias"], r_out[kind][None], sm["norm2"], r_up[kind][None],
                r_fcw[kind][None], sm["ffn_conv_b"], r_down[kind][None], sm["norm_f"]]

    return (loss, grad_x[None], *res(0), *res(1), *res(2), *res(3))
```

```python
import functools
import math

import numpy as np
import jax
import jax.numpy as jnp
from jax import lax
from jax.experimental import pallas as pl
from jax.experimental.pallas import tpu as pltpu

F32, BF16 = jnp.float32, jnp.bfloat16

N_DEV = 8
D_MODEL = 2048
N_HEADS, HEAD_DIM, N_KV = 32, 64, 4
GROUP = N_HEADS // N_KV
D_ATTN, D_KV = N_HEADS * HEAD_DIM, N_KV * HEAD_DIM
BLOCK = 128
NUM_BUCKETS, MAX_DISTANCE = 32, 128
D_RNN, N_RNN_BLOCKS, RNN_BLOCK = 2560, 20, 128
RNN_CONV, FFN_CONV = 4, 3
RG_C = 8.0
D_FF = 3 * D_MODEL
D_IN = D_ATTN + 2 * D_KV + 2 * D_RNN + 2 * D_MODEL
EPS = 1e-6
NEG_INF = -1e30
ADAM_LR, ADAM_B1, ADAM_B2, ADAM_EPS, ADAM_WD, ADAM_STEP = 0.001, 0.9, 0.999, 1e-08, 0.01, 10

LANES = 128
HALO = 16
VMEM_LIMIT = 56 * 1024 * 1024
MESH = pl.DeviceIdType.MESH
RELAY_AT_NUM, RELAY_AT_DEN = 2, 3

Q0, K0, V0, XR0, GR0, GA0, GL0 = 0, 2048, 2304, 2560, 5120, 7680, 9728


def _cparams(n_axes):
    return pltpu.CompilerParams(dimension_semantics=("arbitrary",) * n_axes, vmem_limit_bytes=VMEM_LIMIT)


def _gelu(x):
    k = math.sqrt(2.0 / math.pi)
    return 0.5 * x * (1.0 + jnp.tanh(k * (x + 0.044715 * x * x * x)))


def _gelu_and_grad(x):
    k = math.sqrt(2.0 / math.pi)
    t = jnp.tanh(k * (x + 0.044715 * x * x * x))
    g = 0.5 * x * (1.0 + t)
    dg = 0.5 * (1.0 + t) + 0.5 * x * (1.0 - t * t) * k * (1.0 + 3.0 * 0.044715 * x * x)
    return g, dg


def _sigmoid(x):
    return 1.0 / (1.0 + jnp.exp(-x))


def _shift_down(prev, x, j):
    if j == 0:
        return x
    xe = jnp.concatenate([prev, x], axis=0)
    return pltpu.roll(xe, j, axis=0)[HALO:, :]


def _shift_up(x, nxt, j):
    if j == 0:
        return x
    xe = jnp.concatenate([x, nxt], axis=0)
    n = xe.shape[0]
    return pltpu.roll(xe, n - j, axis=0)[: x.shape[0], :]


def _my_coords():
    return lax.axis_index("x"), lax.axis_index("y"), lax.axis_index("c")


def _peer(x, y, c, k):
    kx, ky, kc = (k >> 2) & 1, (k >> 1) & 1, k & 1
    px, py, pc = (x + kx) % 2, (y + ky) % 2, (c + kc) % 2
    return (px, py, pc), 4 * px + 2 * py + pc


def _exchange_shapes(items):
    return [jax.ShapeDtypeStruct((N_DEV,) + tuple(arr.shape[1:] if s else arr.shape), arr.dtype) for arr, s in items]


def _exchange_sems(n):
    return [pltpu.SemaphoreType.DMA((n, N_DEV - 1)), pltpu.SemaphoreType.DMA((n, N_DEV - 1)),
            pltpu.SemaphoreType.DMA((n,))]


def _exchange_copies(srcs, dsts, scat, send_sems, recv_sems, loc_sems):
    x, y, c = _my_coords()
    me = 4 * x + 2 * y + c
    copies = []
    for a in range(len(srcs)):
        mine = srcs[a].at[me] if scat[a] else srcs[a]
        copies.append(pltpu.make_async_copy(mine, dsts[a].at[me], loc_sems.at[a]))
    for k in range(1, N_DEV):
        peer, p = _peer(x, y, c, k)
        for a in range(len(srcs)):
            src = srcs[a].at[p] if scat[a] else srcs[a]
            copies.append(pltpu.make_async_remote_copy(
                src_ref=src, dst_ref=dsts[a].at[me], send_sem=send_sems.at[a, k - 1],
                recv_sem=recv_sems.at[a, k - 1], device_id=peer, device_id_type=MESH))
    return copies


def _exchange_steps(phase, srcs, dsts, scat, send_sems, recv_sems, loc_sems):
    if any(scat):
        if phase == 1:
            return []
        copies = _exchange_copies(srcs, dsts, scat, send_sems, recv_sems, loc_sems)
        return [cp.start if phase == 0 else cp.wait for cp in copies]
    x, y, c = _my_coords()
    me = 4 * x + 2 * y + c
    sibling, _ = _peer(x, y, c, 1)

    def remote(a, src, dst, k, to):
        return pltpu.make_async_remote_copy(src_ref=src, dst_ref=dst, send_sem=send_sems.at[a, k - 1],
                                            recv_sem=recv_sems.at[a, k - 1], device_id=to, device_id_type=MESH)

    acts = []
    for a in range(len(srcs)):
        if phase != 1:
            loc = pltpu.make_async_copy(srcs[a], dsts[a].at[me], loc_sems.at[a])
            sib = remote(a, srcs[a], dsts[a].at[me], 1, sibling)
            acts += [loc.start, sib.start] if phase == 0 else [loc.wait, sib.wait]
        for k in (2, 4, 6):
            peer, p = _peer(x, y, c, k)
            out = remote(a, srcs[a], dsts[a].at[me], k, peer)
            if phase == 0:
                acts.append(out.start)
            else:
                onward = remote(a, dsts[a].at[p], dsts[a].at[p], k + 1, sibling)
                acts += [out.wait_recv, onward.start] if phase == 1 else [out.wait_send, onward.wait]
    return acts


def exchange(items, name):
    n = len(items)
    scat = [s for _, s in items]

    def body(*refs):
        for phase in range(3):
            for act in _exchange_steps(phase, refs[:n], refs[n:2 * n], scat, *refs[2 * n:]):
                act()

    any_spec = pl.BlockSpec(memory_space=pl.ANY)
    return pl.pallas_call(
        body, name=name, out_shape=tuple(_exchange_shapes(items)),
        in_specs=[any_spec] * n, out_specs=tuple([any_spec] * n), scratch_shapes=_exchange_sems(n),
    )(*[a for a, _ in items])


def mod_forward(c, w_ada, b_ada):
    d, ncol = w_ada.shape

    def body(c_ref, w_ref, b_ref, mod_ref, call_ref, cols_ref, s1, r1, s2, r2):
        x, y, c_ = _my_coords()
        me = 4 * x + 2 * y + c_
        call_ref[me] = c_ref[0]
        sends = []
        for k in range(1, N_DEV):
            peer, p = _peer(x, y, c_, k)
            cp = pltpu.make_async_remote_copy(src_ref=c_ref.at[0], dst_ref=call_ref.at[me], send_sem=s1.at[k - 1],
                                              recv_sem=r1.at[k - 1], device_id=peer, device_id_type=MESH)
            cp.start()
            sends.append(cp)
        for cp in sends:
            cp.wait()
        rows = lax.broadcasted_iota(jnp.int32, (N_DEV, d), 0)
        cmat = jnp.zeros((N_DEV, d), F32)
        for b in range(N_DEV):
            cmat = jnp.where(rows == b, call_ref[b], cmat)
        cs = cmat * _sigmoid(cmat)
        bias = b_ref[:, pl.ds(pl.multiple_of(me * ncol, LANES), ncol)]
        mc = jnp.dot(cs, w_ref[...], preferred_element_type=F32, precision=lax.Precision.HIGHEST) + bias
        for b in range(N_DEV):
            cols_ref[b] = mc[b:b + 1, :]
        mod_ref[me] = cols_ref[me]
        sends = []
        for k in range(1, N_DEV):
            peer, p = _peer(x, y, c_, k)
            cp = pltpu.make_async_remote_copy(src_ref=cols_ref.at[p], dst_ref=mod_ref.at[me], send_sem=s2.at[k - 1],
                                              recv_sem=r2.at[k - 1], device_id=peer, device_id_type=MESH)
            cp.start()
            sends.append(cp)
        for cp in sends:
            cp.wait()

    vm = pl.BlockSpec(memory_space=pltpu.VMEM)
    return pl.pallas_call(
        body, name="mod_forward",
        out_shape=(jax.ShapeDtypeStruct((N_DEV, 1, ncol), F32), jax.ShapeDtypeStruct((N_DEV, 1, d), F32)),
        in_specs=[vm, vm, vm], out_specs=(vm, vm),
        scratch_shapes=[pltpu.VMEM((N_DEV, 1, ncol), F32)] + [pltpu.SemaphoreType.DMA((N_DEV - 1,))] * 4,
        compiler_params=pltpu.CompilerParams(vmem_limit_bytes=VMEM_LIMIT),
    )(c, w_ada, b_ada)


def matmul(a, b, mode, out_dtype, name, tm, tn, tk, res=None, gate=None, comm=None):
    if mode == "nn":
        (m, kk), (_, n) = a.shape, b.shape
    elif mode == "nt":
        (m, kk), (n, _) = a.shape, b.shape
    else:
        (kk, m), (_, n) = a.shape, b.shape
    tm, tn, tk = min(tm, m), min(tn, n), min(tk, kk)
    assert m % tm == 0 and n % tn == 0 and kk % tk == 0, (name, m, n, kk, tm, tn, tk)
    if mode == "nn":
        a_spec = pl.BlockSpec((tm, tk), lambda j, i, k: (i, k))
        b_spec = pl.BlockSpec((tk, tn), lambda j, i, k: (k, j))
        dims = (((1,), (0,)), ((), ()))
    elif mode == "nt":
        a_spec = pl.BlockSpec((tm, tk), lambda j, i, k: (i, k))
        b_spec = pl.BlockSpec((tn, tk), lambda j, i, k: (j, k))
        dims = (((1,), (1,)), ((), ()))
    else:
        a_spec = pl.BlockSpec((tk, tm), lambda j, i, k: (k, i))
        b_spec = pl.BlockSpec((tk, tn), lambda j, i, k: (k, j))
        dims = (((0,), (0,)), ((), ()))
    nj, ni, nk = n // tn, m // tm, kk // tk
    fused = res is not None
    items = list(comm or [])
    nc = len(items)
    scat = [s_ for _, s_ in items]
    n_in = (4 if fused else 2) + nc
    n_out = (2 if fused else 1) + nc
    relay_step = (RELAY_AT_NUM * nj * ni * nk) // RELAY_AT_DEN
    o_spec = pl.BlockSpec((tm, tn), lambda j, i, k: (i, j))

    def body(*refs):
        ins, outs, scratch = refs[:n_in], refs[n_in:n_in + n_out], refs[n_in + n_out:]
        a_ref, b_ref = ins[:2]
        o_ref = outs[0]
        acc_ref = scratch[0] if nk > 1 else None
        j, i, k = pl.program_id(0), pl.program_id(1), pl.program_id(2)

        def run(phase):
            for act in _exchange_steps(phase, ins[n_in - nc:], outs[n_out - nc:], scat, *scratch[len(scratch) - 3:]):
                act()

        step = (j * ni + i) * nk + k
        if nc:
            @pl.when(step == 0)
            def _():
                run(0)

            @pl.when(step == relay_step)
            def _():
                run(1)

        def finish(acc):
            if fused:
                o_ref[...] = ins[2][...] + ins[3][...] * acc
                outs[1][...] = acc.astype(outs[1].dtype)
            else:
                o_ref[...] = acc.astype(o_ref.dtype)

        prod = lax.dot_general(a_ref[...], b_ref[...], dims, preferred_element_type=F32)
        if nk == 1:
            finish(prod)
        else:
            @pl.when(k == 0)
            def _():
                acc_ref[...] = prod

            @pl.when(k > 0)
            def _():
                acc_ref[...] += prod

            @pl.when(k == nk - 1)
            def _():
                finish(acc_ref[...])

        if nc:
            @pl.when(step == nj * ni * nk - 1)
            def _():
                run(2)

    any_spec = pl.BlockSpec(memory_space=pl.ANY)
    in_specs, args = [a_spec, b_spec], [a, b]
    if fused:
        in_specs += [o_spec, pl.BlockSpec((1, tn), lambda j, i, k: (0, j))]
        args += [res, gate]
        out_shape = [jax.ShapeDtypeStruct((m, n), F32), jax.ShapeDtypeStruct((m, n), out_dtype)]
        out_specs = [o_spec, o_spec]
    else:
        out_shape = [jax.ShapeDtypeStruct((m, n), out_dtype)]
        out_specs = [o_spec]
    in_specs += [any_spec] * nc
    args += [arr for arr, _ in items]
    out_shape += _exchange_shapes(items)
    out_specs += [any_spec] * nc
    scratch = ([pltpu.VMEM((tm, tn), F32)] if nk > 1 else []) + (_exchange_sems(nc) if nc else [])
    outs = pl.pallas_call(
        body, name=name, grid=(nj, ni, nk), in_specs=in_specs, out_specs=tuple(out_specs), out_shape=tuple(out_shape),
        scratch_shapes=scratch, compiler_params=_cparams(3),
    )(*args)
    return outs[0] if len(outs) == 1 else outs


def _row_tile(s, want):
    t = min(want, s)
    assert s % t == 0 and t % HALO == 0
    return t


def prenorm(x, nw, scale, shift, name):
    s, d = x.shape
    tm = _row_tile(s, 512)

    def body(x_ref, nw_ref, sc_ref, sh_ref, o_ref):
        xv = x_ref[...]
        r = lax.rsqrt(jnp.mean(xv * xv, axis=-1, keepdims=True) + EPS)
        o_ref[...] = ((xv * r) * nw_ref[...] * (1.0 + sc_ref[...]) + sh_ref[...]).astype(BF16)

    row = pl.BlockSpec((tm, d), lambda i: (i, 0))
    vec = pl.BlockSpec((1, d), lambda i: (0, 0))
    return pl.pallas_call(body, name=name, grid=(s // tm,), in_specs=[row, vec, vec, vec], out_specs=row,
                          out_shape=jax.ShapeDtypeStruct((s, d), BF16), compiler_params=_cparams(1))(x, nw, scale, shift)


def norm_backward(du, xin, dres, nw, scale, name, mo=None, gate=None):
    s, d = xin.shape
    tm = _row_tile(s, 256)
    gated = mo is not None

    def body(*refs):
        if gated:
            du_ref, x_ref, dr_ref, nw_ref, sc_ref, mo_ref, g_ref, dx_ref, dsh_ref, dsc_ref, dnw_ref, dmo_ref, dg_ref = refs
        else:
            du_ref, x_ref, dr_ref, nw_ref, sc_ref, dx_ref, dsh_ref, dsc_ref, dnw_ref = refs
        i = pl.program_id(0)
        xv = x_ref[...]
        r = lax.rsqrt(jnp.mean(xv * xv, axis=-1, keepdims=True) + EPS)
        xn = xv * r
        duv = du_ref[...].astype(F32)
        nwv, scv = nw_ref[...], sc_ref[...]
        dxn = duv * (nwv * (1.0 + scv))
        dx = dr_ref[...] + r * (dxn - xn * jnp.mean(dxn * xn, axis=-1, keepdims=True))
        dx_ref[...] = dx
        sums = [jnp.sum(duv, axis=0, keepdims=True), jnp.sum(duv * xn * nwv, axis=0, keepdims=True),
                jnp.sum(duv * xn * (1.0 + scv), axis=0, keepdims=True)]
        accs = [dsh_ref, dsc_ref, dnw_ref]
        if gated:
            dmo_ref[...] = (dx * g_ref[...]).astype(BF16)
            sums.append(jnp.sum(dx * mo_ref[...].astype(F32), axis=0, keepdims=True))
            accs.append(dg_ref)

        @pl.when(i == 0)
        def _():
            for acc, sm in zip(accs, sums):
                acc[...] = sm

        @pl.when(i > 0)
        def _():
            for acc, sm in zip(accs, sums):
                acc[...] += sm

    row = pl.BlockSpec((tm, d), lambda i: (i, 0))
    vec = pl.BlockSpec((1, d), lambda i: (0, 0))
    vshape = jax.ShapeDtypeStruct((1, d), F32)
    in_specs, args = [row, row, row, vec, vec], [du, xin, dres, nw, scale]
    out_specs, out_shape = [row, vec, vec, vec], [jax.ShapeDtypeStruct((s, d), F32), vshape, vshape, vshape]
    if gated:
        in_specs += [row, vec]
        args += [mo, gate]
        out_specs += [row, vec]
        out_shape += [jax.ShapeDtypeStruct((s, d), BF16), vshape]
    return pl.pallas_call(body, name=name, grid=(s // tm,), in_specs=in_specs, out_specs=tuple(out_specs),
                          out_shape=tuple(out_shape), compiler_params=_cparams(1))(*args)


def merge_forward(proj, y_attn, y_rnn):
    s, d = y_attn.shape
    tm, cw = _row_tile(s, 1024), 512

    def body(ga_ref, gl_ref, ya_ref, yr_ref, o_ref):
        o_ref[...] = (_sigmoid(ga_ref[...].astype(F32)) * ya_ref[...].astype(F32)
                      + _sigmoid(gl_ref[...].astype(F32)) * yr_ref[...].astype(F32)).astype(BF16)

    def at(off):
        return pl.BlockSpec((tm, cw), lambda j, i: (i, off // cw + j))

    return pl.pallas_call(body, name="merge_forward", grid=(d // cw, s // tm),
                          in_specs=[at(GA0), at(GL0), at(0), at(0)], out_specs=at(0),
                          out_shape=jax.ShapeDtypeStruct((s, d), BF16), compiler_params=_cparams(2))(proj, proj, y_attn, y_rnn)


def merge_backward(dmerged, proj, y_attn, y_rnn):
    s, d = y_attn.shape
    tm, cw = _row_tile(s, 1024), 512

    def body(dm_ref, ga_ref, gl_ref, ya_ref, yr_ref, dga_ref, dgl_ref, dya_ref, dyr_ref):
        dm = dm_ref[...].astype(F32)
        sa, sl = _sigmoid(ga_ref[...].astype(F32)), _sigmoid(gl_ref[...].astype(F32))
        dga_ref[...] = (dm * ya_ref[...].astype(F32) * sa * (1.0 - sa)).astype(BF16)
        dgl_ref[...] = (dm * yr_ref[...].astype(F32) * sl * (1.0 - sl)).astype(BF16)
        dya_ref[...] = (dm * sa).astype(BF16)
        dyr_ref[...] = (dm * sl).astype(BF16)

    def at(off):
        return pl.BlockSpec((tm, cw), lambda j, i: (i, off // cw + j))

    o = jax.ShapeDtypeStruct((s, d), BF16)
    return pl.pallas_call(body, name="merge_backward", grid=(d // cw, s // tm),
                          in_specs=[at(0), at(GA0), at(GL0), at(0), at(0)], out_specs=(at(0),) * 4,
                          out_shape=(o, o, o, o), compiler_params=_cparams(2))(dmerged, proj, proj, y_attn, y_rnn)


def _prev_spec(tm, cw, off_blocks):
    r = tm // HALO
    return pl.BlockSpec((HALO, cw), lambda j, i: (jnp.maximum(i * r - 1, 0), off_blocks + j))


def ffn_act_forward(upp, cw_full, cb_full):
    s, f2 = upp.shape
    f = f2 // 2
    tm, cw = _row_tile(s, 512), 1536
    nj = f // cw

    def body(g_ref, gp_ref, v_ref, vp_ref, wg_ref, wv_ref, bg_ref, bv_ref, o_ref, og_ref, ov_ref):
        i = pl.program_id(1)
        first = i == 0

        def conv(x_ref, p_ref, w_ref, b_ref):
            xv = x_ref[...].astype(F32)
            pv = jnp.where(first, 0.0, p_ref[...].astype(F32))
            acc = b_ref[...] + w_ref[FFN_CONV - 1:FFN_CONV, :] * xv
            for k in range(FFN_CONV - 1):
                acc = acc + w_ref[k:k + 1, :] * _shift_down(pv, xv, FFN_CONV - 1 - k)
            return acc

        g = conv(g_ref, gp_ref, wg_ref, bg_ref)
        v = conv(v_ref, vp_ref, wv_ref, bv_ref)
        og_ref[...] = g.astype(BF16)
        ov_ref[...] = v.astype(BF16)
        o_ref[...] = (_gelu(g) * v).astype(BF16)

    def tile(ob):
        return pl.BlockSpec((tm, cw), lambda j, i: (i, ob + j))

    def par(rows, ob):
        return pl.BlockSpec((rows, cw), lambda j, i: (0, ob + j))

    o = jax.ShapeDtypeStruct((s, f), BF16)
    return pl.pallas_call(
        body, name="ffn_act_forward", grid=(nj, s // tm),
        in_specs=[tile(0), _prev_spec(tm, cw, 0), tile(nj), _prev_spec(tm, cw, nj),
                  par(FFN_CONV, 0), par(FFN_CONV, nj), par(1, 0), par(1, nj)],
        out_specs=(tile(0), tile(0), tile(0)), out_shape=(o, o, o), compiler_params=_cparams(2),
    )(upp, upp, upp, upp, cw_full, cw_full, cb_full, cb_full)


ROW_CHUNK = 16
LANE_CHUNK = 512


def ffn_act_backward(dact, up_g, up_v):
    s, f = dact.shape
    tm, cw = _row_tile(s, 512), 1536
    nr = tm // ROW_CHUNK

    def body(da_ref, g_ref, v_ref, dg_ref, dv_ref, sg_ref, sv_ref, acc_ref):
        i = pl.program_id(1)
        acc_ref[...] = jnp.zeros_like(acc_ref)

        def chunk(r, carry):
            rows = pl.ds(pl.multiple_of(r * ROW_CHUNK, ROW_CHUNK), ROW_CHUNK)
            for c0 in range(0, cw, LANE_CHUNK):
                cols = pl.ds(c0, LANE_CHUNK)
                da = da_ref[rows, cols].astype(F32)
                ge, dge = _gelu_and_grad(g_ref[rows, cols].astype(F32))
                dg = da * v_ref[rows, cols].astype(F32) * dge
                dv = da * ge
                dg_ref[rows, cols] = dg.astype(BF16)
                dv_ref[rows, cols] = dv.astype(BF16)
                acc_ref[0, :, cols] += dg[0:8] + dg[8:16]
                acc_ref[1, :, cols] += dv[0:8] + dv[8:16]
            return carry

        lax.fori_loop(0, nr, chunk, 0)
        sg = jnp.sum(acc_ref[0], axis=0, keepdims=True)
        sv = jnp.sum(acc_ref[1], axis=0, keepdims=True)

        @pl.when(i == 0)
        def _():
            sg_ref[...] = sg
            sv_ref[...] = sv

        @pl.when(i > 0)
        def _():
            sg_ref[...] += sg
            sv_ref[...] += sv

    tile = pl.BlockSpec((tm, cw), lambda j, i: (i, j))
    vec = pl.BlockSpec((1, cw), lambda j, i: (0, j))
    o = jax.ShapeDtypeStruct((s, f), BF16)
    v1 = jax.ShapeDtypeStruct((1, f), F32)
    return pl.pallas_call(
        body, name="ffn_act_backward", grid=(f // cw, s // tm), in_specs=[tile, tile, tile],
        out_specs=(tile, tile, vec, vec), out_shape=(o, o, v1, v1),
        scratch_shapes=[pltpu.VMEM((2, 8, cw), F32)], compiler_params=_cparams(2),
    )(dact, up_g, up_v)


def conv_backward(d_g, d_v, upp, cw_full):
    s, f = d_g.shape
    f2 = 2 * f
    tm, cw = _row_tile(s, 512), 1536
    nt, nj = s // tm, f // cw
    r_halo = tm // HALO
    nr = tm // ROW_CHUNK

    def body(xg_ref, ng_ref, xv_ref, nv_ref, u_ref, w_ref, o_ref, dw_ref, acc_ref):
        j, i = pl.program_id(0), pl.program_id(1)
        acc_ref[...] = jnp.zeros_like(acc_ref)

        def run(x_ref, n_ref):
            def chunk(r, carry):
                rows = pl.ds(pl.multiple_of(r * ROW_CHUNK, ROW_CHUNK), ROW_CHUNK)
                nrows = pl.ds(pl.multiple_of(jnp.minimum(r + 1, nr - 1) * ROW_CHUNK, ROW_CHUNK), ROW_CHUNK)
                for c0 in range(0, cw, LANE_CHUNK):
                    cols = pl.ds(c0, LANE_CHUNK)
                    cur = x_ref[rows, cols].astype(F32)
                    halo = jnp.where(i == nt - 1, 0.0, n_ref[:, cols].astype(F32))
                    nxt = jnp.where(r == nr - 1, halo, x_ref[nrows, cols].astype(F32))
                    uv = u_ref[rows, cols].astype(F32)
                    acc = None
                    for k in range(FFN_CONV):
                        sh = _shift_up(cur, nxt, FFN_CONV - 1 - k)
                        term = w_ref[k:k + 1, cols] * sh
                        acc = term if acc is None else acc + term
                        pr = uv * sh
                        acc_ref[k, :, cols] += pr[0:8] + pr[8:16]
                    o_ref[rows, cols] = acc.astype(BF16)
                return carry

            lax.fori_loop(0, nr, chunk, 0)

        @pl.when(j < nj)
        def _():
            run(xg_ref, ng_ref)

        @pl.when(j >= nj)
        def _():
            run(xv_ref, nv_ref)

        sums = jnp.concatenate([jnp.sum(acc_ref[k], axis=0, keepdims=True) for k in range(FFN_CONV)], axis=0)

        @pl.when(i == 0)
        def _():
            dw_ref[...] = sums

        @pl.when(i > 0)
        def _():
            dw_ref[...] += sums

    half = lambda j: j % nj
    tile_h = pl.BlockSpec((tm, cw), lambda j, i: (i, half(j)))
    next_h = pl.BlockSpec((HALO, cw), lambda j, i: (jnp.minimum((i + 1) * r_halo, s // HALO - 1), half(j)))
    tile = pl.BlockSpec((tm, cw), lambda j, i: (i, j))
    par = pl.BlockSpec((FFN_CONV, cw), lambda j, i: (0, j))
    return pl.pallas_call(
        body, name="ffn_conv_backward", grid=(2 * nj, nt), in_specs=[tile_h, next_h, tile_h, next_h, tile, par],
        out_specs=(tile, par), out_shape=(jax.ShapeDtypeStruct((s, f2), BF16), jax.ShapeDtypeStruct((FFN_CONV, f2), F32)),
        scratch_shapes=[pltpu.VMEM((FFN_CONV, 8, cw), F32)], compiler_params=_cparams(2),
    )(d_g, d_g, d_v, d_v, upp, cw_full)


def loss_head(h2, target, dn, norm_f, gate2):
    s, d = h2.shape
    tm = _row_tile(s, 256)

    def body(h_ref, t_ref, dn_ref, nf_ref, g_ref, dh_ref, ddn_ref, loss_ref, dnf_ref, dg_ref):
        i = pl.program_id(0)
        hv = h_ref[...]
        r = lax.rsqrt(jnp.mean(hv * hv, axis=-1, keepdims=True) + EPS)
        yh = hv * r
        nf = nf_ref[...]
        err = yh * nf - t_ref[...]
        dy = err * (1.0 / d)
        dyh = dy * nf
        dh = r * (dyh - yh * jnp.mean(dyh * yh, axis=-1, keepdims=True))
        dh_ref[...] = dh
        ddn_ref[...] = (dh * g_ref[...]).astype(BF16)
        sums = [jnp.sum(err * err, axis=0, keepdims=True) * (0.5 / d), jnp.sum(dy * yh, axis=0, keepdims=True),
                jnp.sum(dh * dn_ref[...].astype(F32), axis=0, keepdims=True)]
        accs = [loss_ref, dnf_ref, dg_ref]

        @pl.when(i == 0)
        def _():
            for acc, sm in zip(accs, sums):
                acc[...] = sm

        @pl.when(i > 0)
        def _():
            for acc, sm in zip(accs, sums):
                acc[...] += sm

    row = pl.BlockSpec((tm, d), lambda i: (i, 0))
    vec = pl.BlockSpec((1, d), lambda i: (0, 0))
    v = jax.ShapeDtypeStruct((1, d), F32)
    return pl.pallas_call(
        body, name="loss_head", grid=(s // tm,), in_specs=[row, row, row, vec, vec], out_specs=(row, row, vec, vec, vec),
        out_shape=(jax.ShapeDtypeStruct((s, d), F32), jax.ShapeDtypeStruct((s, d), BF16), v, v, v),
        compiler_params=_cparams(1))(h2, target, dn, norm_f, gate2)


def _t5_buckets():
    qi = np.arange(BLOCK)[:, None]
    kj = np.arange(2 * BLOCK)[None, :]
    dist = qi + BLOCK - kj
    dd = np.maximum(dist, 0)
    max_exact = NUM_BUCKETS // 2
    dflt = np.maximum(dd, 1).astype(np.float32)
    large = max_exact + (np.log(dflt / max_exact) / math.log(MAX_DISTANCE / max_exact)
                         * (NUM_BUCKETS - max_exact)).astype(np.int32)
    large = np.minimum(large, NUM_BUCKETS - 1)
    bucket = np.where(dd < max_exact, dd, large).astype(np.int32)
    in_window = (dist >= 0) & (dist < BLOCK)
    return bucket, in_window


def band_bias(rel_bias):
    bucket, in_window = _t5_buckets()
    bucket_t = jnp.asarray(np.where(in_window, bucket, -1).astype(np.int32).T)

    def body(rb_ref, bk_ref, o_ref):
        bk = bk_ref[...]
        for h in range(N_HEADS):
            acc = jnp.full((2 * BLOCK, BLOCK), NEG_INF, F32)
            for b in range(NUM_BUCKETS):
                acc = jnp.where(bk == b, rb_ref[b, h], acc)
            o_ref[h] = acc

    return pl.pallas_call(
        body, name="band_bias", out_shape=jax.ShapeDtypeStruct((N_HEADS, 2 * BLOCK, BLOCK), F32),
        in_specs=[pl.BlockSpec(memory_space=pltpu.SMEM), pl.BlockSpec(memory_space=pltpu.VMEM)],
        out_specs=pl.BlockSpec(memory_space=pltpu.VMEM))(rel_bias, bucket_t)


def rel_bias_grad(dbias):
    bucket, in_window = _t5_buckets()
    bucket_t = jnp.asarray(np.where(in_window, bucket, -1).astype(np.int32).T)

    def body(db_ref, bk_ref, o_ref):
        bk = bk_ref[...]
        rows = lax.broadcasted_iota(jnp.int32, (NUM_BUCKETS, LANES), 0)
        lanes = lax.broadcasted_iota(jnp.int32, (NUM_BUCKETS, LANES), 1)
        acc = jnp.zeros((NUM_BUCKETS, LANES), F32)
        for h in range(N_HEADS):
            dv = db_ref[h]
            for b in range(NUM_BUCKETS):
                sm = jnp.sum(jnp.where(bk == b, dv, 0.0))
                acc = jnp.where((rows == b) & (lanes == h), sm, acc)
        o_ref[...] = acc

    vm = pl.BlockSpec(memory_space=pltpu.VMEM)
    return pl.pallas_call(body, name="rel_bias_grad", out_shape=jax.ShapeDtypeStruct((NUM_BUCKETS, LANES), F32),
                          in_specs=[vm, vm], out_specs=vm)(dbias, bucket_t)


HP = 2
Q_PER_HP = D_ATTN // HP
H_PER_HP = N_HEADS // HP
NT_DIMS = (((1,), (1,)), ((), ()))
TN_DIMS = (((0,), (0,)), ((), ()))


def _stack_heads(ref, hh):
    lane = lax.broadcasted_iota(jnp.int32, (BLOCK, LANES), 1)
    lo = lane < HEAD_DIM
    parts = []
    for s_ in range(GROUP // 2):
        c0 = hh * (GROUP * HEAD_DIM) + s_ * LANES
        slab = ref[:, c0:c0 + LANES]
        parts.append(jnp.where(lo, slab, jnp.zeros_like(slab)))
        parts.append(jnp.where(lo, jnp.zeros_like(slab), slab))
    return jnp.concatenate(parts, axis=0)


def _attn_probs(hh, q_ref, kp_ref, kc_ref, vp_ref, vc_ref, bias_ref, sink_ref, hp, n):
    lane = lax.broadcasted_iota(jnp.int32, (2 * BLOCK, LANES), 1)
    own = (lane >= HEAD_DIM) if hh == 1 else (lane < HEAD_DIM)
    kband = jnp.concatenate([kp_ref[...], kc_ref[...]], axis=0)
    vband = jnp.concatenate([vp_ref[...], vc_ref[...]], axis=0)
    kk = jnp.where(own, kband, pltpu.roll(kband, HEAD_DIM, axis=1))
    vv = jnp.where(own, vband, pltpu.roll(vband, HEAD_DIM, axis=1))
    qs = _stack_heads(q_ref, hh)
    sc = lax.dot_general(kk, qs, NT_DIMS, preferred_element_type=F32) * (HEAD_DIM ** -0.5)
    sc = sc + jnp.concatenate([bias_ref[hh * GROUP + g] for g in range(GROUP)], axis=1)
    krow = lax.broadcasted_iota(jnp.int32, sc.shape, 0)
    sc = jnp.where((n == 0) & (krow < BLOCK), NEG_INF, sc)
    sink = jnp.concatenate([jnp.full((1, BLOCK), sink_ref[hp * H_PER_HP + hh * GROUP + g], F32) for g in range(GROUP)], axis=1)
    m = jnp.maximum(jnp.max(sc, axis=0, keepdims=True), sink)
    p = jnp.exp(sc - m)
    es = jnp.exp(sink - m)
    inv = 1.0 / (jnp.sum(p, axis=0, keepdims=True) + es)
    return qs, kk, vv, p * inv, es * inv


def _unstack(o, dtype):
    lane = lax.broadcasted_iota(jnp.int32, (BLOCK, LANES), 1)
    lo = lane < HEAD_DIM
    slabs = []
    for s_ in range(GROUP // 2):
        ev = o[(2 * s_) * BLOCK:(2 * s_ + 1) * BLOCK]
        od = o[(2 * s_ + 1) * BLOCK:(2 * s_ + 2) * BLOCK]
        slabs.append(jnp.where(lo, ev, od).astype(dtype))
    return slabs


def attention_forward(proj, bias, sinks, comm):
    s = proj.shape[0]
    nb = s // BLOCK
    kb, vb = K0 // LANES, V0 // LANES
    nc = len(comm)
    scat = [s_ for _, s_ in comm]

    def body(*refs):
        q_ref, kp_ref, kc_ref, vp_ref, vc_ref, bias_ref, sink_ref = refs[:7]
        srcs, o_ref, dsts, sems = refs[7:7 + nc], refs[7 + nc], refs[8 + nc:8 + 2 * nc], refs[8 + 2 * nc:]
        hp, n = pl.program_id(0), pl.program_id(1)

        step = hp * nb + n

        def run(phase):
            for act in _exchange_steps(phase, srcs, dsts, scat, *sems):
                act()

        @pl.when(step == 0)
        def _():
            run(0)

        @pl.when(step == (RELAY_AT_NUM * HP * nb) // RELAY_AT_DEN)
        def _():
            run(1)

        for hh in range(2):
            qs, kk, vv, probs, _ = _attn_probs(hh, q_ref, kp_ref, kc_ref, vp_ref, vc_ref, bias_ref, sink_ref, hp, n)
            o = lax.dot_general(probs.astype(BF16), vv, TN_DIMS, preferred_element_type=F32)
            for s_, slab in enumerate(_unstack(o, BF16)):
                c0 = hh * (GROUP * HEAD_DIM) + s_ * LANES
                o_ref[:, c0:c0 + LANES] = slab

        @pl.when(step == HP * nb - 1)
        def _():
            run(2)

    qspec = pl.BlockSpec((BLOCK, Q_PER_HP), lambda hp, n: (n, hp))
    any_spec = pl.BlockSpec(memory_space=pl.ANY)

    def kv(base, prev):
        if prev:
            return pl.BlockSpec((BLOCK, LANES), lambda hp, n: (jnp.maximum(n - 1, 0), base + hp))
        return pl.BlockSpec((BLOCK, LANES), lambda hp, n: (n, base + hp))

    return pl.pallas_call(
        body, name="attention_forward", grid=(HP, nb),
        in_specs=[qspec, kv(kb, True), kv(kb, False), kv(vb, True), kv(vb, False),
                  pl.BlockSpec((H_PER_HP, 2 * BLOCK, BLOCK), lambda hp, n: (hp, 0, 0)),
                  pl.BlockSpec(memory_space=pltpu.SMEM)] + [any_spec] * nc,
        out_specs=(qspec,) + (any_spec,) * nc,
        out_shape=(jax.ShapeDtypeStruct((s, D_ATTN), BF16),) + tuple(_exchange_shapes(comm)),
        scratch_shapes=_exchange_sems(nc), compiler_params=_cparams(2),
    )(proj, proj, proj, proj, proj, bias, sinks, *[arr for arr, _ in comm])


def attention_backward(proj, datt, bias, sinks, comm):
    s = proj.shape[0]
    nb = s // BLOCK
    kb, vb = K0 // LANES, V0 // LANES
    nc = len(comm)
    scat = [s_ for _, s_ in comm]

    def body(*refs):
        q_ref, kp_ref, kc_ref, vp_ref, vc_ref, do_ref, bias_ref, sink_ref = refs[:8]
        srcs = refs[8:8 + nc]
        dq_ref, dk_ref, dv_ref, dbias_ref, dsink_ref = refs[8 + nc:13 + nc]
        dsts = refs[13 + nc:13 + 2 * nc]
        kcar_ref, vcar_ref, sacc_ref = refs[13 + 2 * nc:16 + 2 * nc]
        sems = refs[16 + 2 * nc:]
        hp, n = pl.program_id(0), pl.program_id(1)

        @pl.when((hp == 0) & (n == 0))
        def _():
            for cp in _exchange_copies(srcs, dsts, scat, *sems):
                cp.start()

        @pl.when(n == 0)
        def _():
            kcar_ref[...] = jnp.zeros_like(kcar_ref)
            vcar_ref[...] = jnp.zeros_like(vcar_ref)
            dbias_ref[...] = jnp.zeros_like(dbias_ref)
            sacc_ref[...] = jnp.zeros_like(sacc_ref)

        @pl.when(n < nb)
        def _():
            lane2 = lax.broadcasted_iota(jnp.int32, (2 * BLOCK, LANES), 1)
            dk_band = jnp.zeros((2 * BLOCK, LANES), F32)
            dv_band = jnp.zeros((2 * BLOCK, LANES), F32)
            for hh in range(2):
                qs, kk, vv, probs, psink = _attn_probs(hh, q_ref, kp_ref, kc_ref, vp_ref, vc_ref, bias_ref, sink_ref, hp, n)
                dos = _stack_heads(do_ref, hh)
                dp = lax.dot_general(vv, dos, NT_DIMS, preferred_element_type=F32)
                dsum = jnp.sum(probs * dp, axis=0, keepdims=True)
                ds = probs * (dp - dsum)
                for g in range(GROUP):
                    dbias_ref[hh * GROUP + g] += ds[:, g * BLOCK:(g + 1) * BLOCK]
                sacc_ref[hh:hh + 1, :] += -psink * dsum
                dsb = (ds * (HEAD_DIM ** -0.5)).astype(BF16)
                pb = probs.astype(BF16)
                dq = lax.dot_general(dsb, kk, TN_DIMS, preferred_element_type=F32)
                for s_, slab in enumerate(_unstack(dq, BF16)):
                    c0 = hh * (GROUP * HEAD_DIM) + s_ * LANES
                    dq_ref[:, c0:c0 + LANES] = slab
                dkh = jnp.dot(dsb, qs, preferred_element_type=F32)
                dvh = jnp.dot(pb, dos, preferred_element_type=F32)
                own = (lane2 >= HEAD_DIM) if hh == 1 else (lane2 < HEAD_DIM)
                dk_band = dk_band + jnp.where(own, dkh + pltpu.roll(dkh, HEAD_DIM, axis=1), 0.0)
                dv_band = dv_band + jnp.where(own, dvh + pltpu.roll(dvh, HEAD_DIM, axis=1), 0.0)
            dk_ref[...] = (kcar_ref[...] + dk_band[:BLOCK]).astype(BF16)
            dv_ref[...] = (vcar_ref[...] + dv_band[:BLOCK]).astype(BF16)
            kcar_ref[...] = dk_band[BLOCK:]
            vcar_ref[...] = dv_band[BLOCK:]

        @pl.when(n == nb)
        def _():
            dk_ref[...] = kcar_ref[...].astype(BF16)
            dv_ref[...] = vcar_ref[...].astype(BF16)
            rows = [jnp.full((1, LANES), jnp.sum(sacc_ref[hh:hh + 1, g * BLOCK:(g + 1) * BLOCK]), F32)
                    for hh in range(2) for g in range(GROUP)]
            dsink_ref[...] = jnp.concatenate(rows, axis=0)

        @pl.when((hp == HP - 1) & (n == nb))
        def _():
            for cp in _exchange_copies(srcs, dsts, scat, *sems):
                cp.wait()

    qspec = pl.BlockSpec((BLOCK, Q_PER_HP), lambda hp, n: (jnp.minimum(n, nb - 1), hp))
    any_spec = pl.BlockSpec(memory_space=pl.ANY)

    def kv(base, prev):
        if prev:
            return pl.BlockSpec((BLOCK, LANES), lambda hp, n: (jnp.maximum(jnp.minimum(n, nb - 1) - 1, 0), base + hp))
        return pl.BlockSpec((BLOCK, LANES), lambda hp, n: (jnp.minimum(n, nb - 1), base + hp))

    dkv_spec = pl.BlockSpec((BLOCK, LANES), lambda hp, n: (jnp.maximum(n - 1, 0), hp))
    return pl.pallas_call(
        body, name="attention_backward", grid=(HP, nb + 1),
        in_specs=[qspec, kv(kb, True), kv(kb, False), kv(vb, True), kv(vb, False), qspec,
                  pl.BlockSpec((H_PER_HP, 2 * BLOCK, BLOCK), lambda hp, n: (hp, 0, 0)),
                  pl.BlockSpec(memory_space=pltpu.SMEM)] + [any_spec] * nc,
        out_specs=(qspec, dkv_spec, dkv_spec,
                   pl.BlockSpec((H_PER_HP, 2 * BLOCK, BLOCK), lambda hp, n: (hp, 0, 0)),
                   pl.BlockSpec((H_PER_HP, LANES), lambda hp, n: (hp, 0))) + (any_spec,) * nc,
        out_shape=(jax.ShapeDtypeStruct((s, D_ATTN), BF16), jax.ShapeDtypeStruct((s, D_KV), BF16),
                   jax.ShapeDtypeStruct((s, D_KV), BF16), jax.ShapeDtypeStruct((N_HEADS, 2 * BLOCK, BLOCK), F32),
                   jax.ShapeDtypeStruct((N_HEADS, LANES), F32)) + tuple(_exchange_shapes(comm)),
        scratch_shapes=[pltpu.VMEM((BLOCK, LANES), F32), pltpu.VMEM((BLOCK, LANES), F32),
                        pltpu.VMEM((8, GROUP * BLOCK), F32)] + _exchange_sems(nc),
        compiler_params=_cparams(2),
    )(proj, proj, proj, proj, proj, datt, bias, sinks, *[arr for arr, _ in comm])


def _neg_expm1(x):
    series = -(x * (1.0 + x * (1.0 / 2 + x * (1.0 / 6 + x * (1.0 / 24 + x * (1.0 / 120 + x * (1.0 / 720)))))))
    return jnp.where(x > -0.25, series, 1.0 - jnp.exp(x))


def _softplus_neg(lam):
    u = jnp.exp(-jnp.abs(lam))
    w = 1.0 + u
    log1p = jnp.where(w == 1.0, u, jnp.log(w) * u / jnp.where(w == 1.0, 1.0, w - 1.0))
    sp = jnp.maximum(-lam, 0.0) + log1p
    return sp, -_sigmoid(-lam)


def _rnn_gates(x_ref, xp_ref, cw_ref, cb_ref, wa_ref, ba_ref, wi_ref, bi_ref, lam_ref, first, row0):
    xv = x_ref[...].astype(F32)
    pv = jnp.where(first, 0.0, xp_ref[...].astype(F32))
    xs = [_shift_down(pv, xv, RNN_CONV - 1 - k) for k in range(RNN_CONV)]
    xc = cb_ref[...]
    for k in range(RNN_CONV):
        xc = xc + cw_ref[k:k + 1, :] * xs[k]
    xcb = xc.astype(BF16)
    ra = _sigmoid(jnp.dot(xcb, wa_ref[...], preferred_element_type=F32) + ba_ref[...])
    ri = _sigmoid(jnp.dot(xcb, wi_ref[...], preferred_element_type=F32) + bi_ref[...])
    sp, dsp = _softplus_neg(lam_ref[...])
    la = (-RG_C) * ra * sp
    a = jnp.exp(la)
    t = row0 + lax.broadcasted_iota(jnp.int32, xv.shape, 0)
    start = t == 0
    mult = jnp.where(start, 1.0, jnp.sqrt(_neg_expm1(2.0 * la)))
    return xs, xc, xcb, ra, ri, sp, dsp, a, mult, start


def _rnn_specs(t_rows, s, rev):
    nt = s // t_rows
    r = t_rows // HALO
    xb, gb = XR0 // LANES, GR0 // LANES
    ti = (lambda i: nt - 1 - i) if rev else (lambda i: i)
    tile = lambda base: pl.BlockSpec((t_rows, LANES), lambda n, i: (ti(i), base + n))
    prev = lambda base: pl.BlockSpec((HALO, LANES), lambda n, i: (jnp.maximum(ti(i) * r - 1, 0), base + n))
    par = lambda rows: pl.BlockSpec((rows, LANES), lambda n, i: (0, n))
    mat = pl.BlockSpec((None, RNN_BLOCK, RNN_BLOCK), lambda n, i: (n, 0, 0))
    return nt, ti, tile, prev, par, mat, xb, gb


def rnn_forward(proj, cw, cb, wa, ba, wi, bi, lam):
    s = proj.shape[0]
    t_rows = _row_tile(s, 512)
    nt, ti, tile, prev, par, mat, xb, gb = _rnn_specs(t_rows, s, False)

    def body(x_ref, xp_ref, g_ref, cw_ref, cb_ref, wa_ref, ba_ref, wi_ref, bi_ref, lam_ref, z_ref, h_ref, car_ref):
        i = pl.program_id(1)
        first = i == 0
        _, xc, _, _, ri, _, _, a, mult, _ = _rnn_gates(x_ref, xp_ref, cw_ref, cb_ref, wa_ref, ba_ref, wi_ref, bi_ref,
                                                       lam_ref, first, i * t_rows)
        aa, bb = a, mult * ri * xc
        rows = lax.broadcasted_iota(jnp.int32, aa.shape, 0)
        d = 1
        while d < t_rows:
            keep = rows >= d
            a_s, b_s = pltpu.roll(aa, d, axis=0), pltpu.roll(bb, d, axis=0)
            bb = jnp.where(keep, aa * b_s + bb, bb)
            aa = jnp.where(keep, aa * a_s, aa)
            d *= 2
        carry = jnp.where(first, 0.0, car_ref[0:1, :])
        h = aa * carry + bb
        car_ref[...] = jnp.broadcast_to(h[t_rows - 1:t_rows, :], car_ref.shape)
        h_ref[...] = h.astype(BF16)
        z_ref[...] = (h * _gelu(g_ref[...].astype(F32))).astype(BF16)

    o = jax.ShapeDtypeStruct((s, D_RNN), BF16)
    out_tile = pl.BlockSpec((t_rows, LANES), lambda n, i: (i, n))
    return pl.pallas_call(
        body, name="rnn_forward", grid=(N_RNN_BLOCKS, nt),
        in_specs=[tile(xb), prev(xb), tile(gb), par(RNN_CONV), par(1), mat, par(1), mat, par(1), par(1)],
        out_specs=(out_tile, out_tile), out_shape=(o, o), scratch_shapes=[pltpu.VMEM((8, LANES), F32)],
        compiler_params=_cparams(2),
    )(proj, proj, proj, cw, cb, wa, ba, wi, bi, lam)


def rnn_backward(proj, h, dz, cw, cb, wa, ba, wi, bi, lam):
    s = proj.shape[0]
    t_rows = _row_tile(s, 512)
    nt, ti, tile, prev, par, mat, xb, gb = _rnn_specs(t_rows, s, True)
    r = t_rows // HALO

    def body(x_ref, xp_ref, g_ref, h_ref, hp_ref, dz_ref, cw_ref, cb_ref, wa_ref, ba_ref, wi_ref, bi_ref, lam_ref,
             dx_ref, dg_ref, dwa_ref, dwi_ref, sm_ref, gcar_ref, xcar_ref):
        i = pl.program_id(1)
        it = nt - 1 - i
        first, last = it == 0, it == nt - 1
        xs, xc, xcb, ra, ri, sp, dsp, a, mult, start = _rnn_gates(
            x_ref, xp_ref, cw_ref, cb_ref, wa_ref, ba_ref, wi_ref, bi_ref, lam_ref, first, it * t_rows)
        hf = h_ref[...].astype(F32)
        hprev = _shift_down(jnp.where(first, 0.0, hp_ref[...].astype(F32)), hf, 1)
        ge, dge = _gelu_and_grad(g_ref[...].astype(F32))
        dz = dz_ref[...].astype(F32)
        dg_ref[...] = (dz * hf * dge).astype(BF16)
        rows = lax.broadcasted_iota(jnp.int32, hf.shape, 0)
        tail = rows == t_rows - 1
        carry = jnp.where(last, 0.0, gcar_ref[0:1, :])
        bb = dz * ge + jnp.where(tail, carry, 0.0)
        aa = jnp.where(tail, 0.0, pltpu.roll(a, t_rows - 1, axis=0))
        d = 1
        while d < t_rows:
            keep = rows < t_rows - d
            a_s, b_s = pltpu.roll(aa, t_rows - d, axis=0), pltpu.roll(bb, t_rows - d, axis=0)
            bb = jnp.where(keep, bb + aa * b_s, bb)
            aa = jnp.where(keep, aa * a_s, aa)
            d *= 2
        gg = bb
        gcar_ref[...] = jnp.broadcast_to(a[0:1, :] * gg[0:1, :], gcar_ref.shape)
        da = gg * hprev
        dmult = jnp.where(start, 0.0, gg * ri * xc)
        dri = gg * mult * xc
        dxc = gg * mult * ri
        safe_mult = jnp.where(start, 1.0, mult)
        dla = da * a - dmult * (a * a) / safe_mult
        dra = dla * ((-RG_C) * sp)
        dlam = jnp.sum(dla * ((-RG_C) * ra), axis=0, keepdims=True) * dsp
        dpa = dra * ra * (1.0 - ra)
        dpi = dri * ri * (1.0 - ri)
        dpab, dpib = dpa.astype(BF16), dpi.astype(BF16)
        nt_dims = (((1,), (1,)), ((), ()))
        tn_dims = (((0,), (0,)), ((), ()))
        dxc = dxc + lax.dot_general(dpab, wa_ref[...], nt_dims, preferred_element_type=F32) \
            + lax.dot_general(dpib, wi_ref[...], nt_dims, preferred_element_type=F32)
        dwa = lax.dot_general(xcb, dpab, tn_dims, preferred_element_type=F32)
        dwi = lax.dot_general(xcb, dpib, tn_dims, preferred_element_type=F32)
        nxt = jnp.where(last, 0.0, xcar_ref[...])
        dx = cw_ref[RNN_CONV - 1:RNN_CONV, :] * dxc
        for k in range(RNN_CONV - 1):
            dx = dx + cw_ref[k:k + 1, :] * _shift_up(dxc, nxt, RNN_CONV - 1 - k)
        dx_ref[...] = dx.astype(BF16)
        xcar_ref[...] = dxc[0:HALO, :]
        small = jnp.concatenate(
            [jnp.sum(dpa, axis=0, keepdims=True), jnp.sum(dpi, axis=0, keepdims=True), dlam,
             jnp.sum(dxc, axis=0, keepdims=True)]
            + [jnp.sum(dxc * xs[k], axis=0, keepdims=True) for k in range(RNN_CONV)], axis=0)

        @pl.when(i == 0)
        def _():
            dwa_ref[...] = dwa
            dwi_ref[...] = dwi
            sm_ref[...] = small

        @pl.when(i > 0)
        def _():
            dwa_ref[...] += dwa
            dwi_ref[...] += dwi
            sm_ref[...] += small

    o = jax.ShapeDtypeStruct((s, D_RNN), BF16)
    plain = pl.BlockSpec((t_rows, LANES), lambda n, i: (ti(i), n))
    plain_prev = pl.BlockSpec((HALO, LANES), lambda n, i: (jnp.maximum(ti(i) * r - 1, 0), n))
    return pl.pallas_call(
        body, name="rnn_backward", grid=(N_RNN_BLOCKS, nt),
        in_specs=[tile(xb), prev(xb), tile(gb), plain, plain_prev, plain,
                  par(RNN_CONV), par(1), mat, par(1), mat, par(1), par(1)],
        out_specs=(plain, plain, mat, mat, pl.BlockSpec((None, 8, LANES), lambda n, i: (n, 0, 0))),
        out_shape=(o, o, jax.ShapeDtypeStruct((N_RNN_BLOCKS, RNN_BLOCK, RNN_BLOCK), F32),
                   jax.ShapeDtypeStruct((N_RNN_BLOCKS, RNN_BLOCK, RNN_BLOCK), F32),
                   jax.ShapeDtypeStruct((N_RNN_BLOCKS, 8, LANES), F32)),
        scratch_shapes=[pltpu.VMEM((8, LANES), F32), pltpu.VMEM((HALO, LANES), F32)],
        compiler_params=_cparams(2),
    )(proj, proj, proj, h, h, dz, cw, cb, wa, ba, wi, bi, lam)


def adamw(parts, w, m, v, name, rows=256):
    p, r, c = parts.shape
    tr = min(rows, r)
    assert r % tr == 0

    def body(p_ref, w_ref, m_ref, v_ref, g_ref, d_ref, nm_ref, nv_ref):
        g = p_ref[0].astype(F32)
        for q in range(1, p):
            g = g + p_ref[q].astype(F32)
        nm = ADAM_B1 * m_ref[...] + (1.0 - ADAM_B1) * g
        nv = ADAM_B2 * v_ref[...] + (1.0 - ADAM_B2) * (g * g)
        mh = nm / (1.0 - ADAM_B1 ** ADAM_STEP)
        vh = nv / (1.0 - ADAM_B2 ** ADAM_STEP)
        g_ref[...] = g
        d_ref[...] = (-ADAM_LR) * (mh / (jnp.sqrt(vh) + ADAM_EPS) + ADAM_WD * w_ref[...])
        nm_ref[...] = nm
        nv_ref[...] = nv

    pspec = pl.BlockSpec((p, tr, c), lambda i: (0, i, 0))
    spec = pl.BlockSpec((tr, c), lambda i: (i, 0))
    o = jax.ShapeDtypeStruct((r, c), F32)
    return pl.pallas_call(body, name=name, grid=(r // tr,), in_specs=[pspec, spec, spec, spec],
                          out_specs=(spec,) * 4, out_shape=(o, o, o, o), compiler_params=_cparams(1))(parts, w, m, v)


def ada_weight_grad(c_t, dmod):
    d, nb = c_t.shape
    c = dmod.shape[1]
    tr = 512

    def body(c_ref, dm_ref, o_ref):
        cv = c_ref[...]
        cs = cv * _sigmoid(cv)
        acc = cs[:, 0:1] * dm_ref[0:1, :]
        for b in range(1, nb):
            acc = acc + cs[:, b:b + 1] * dm_ref[b:b + 1, :]
        o_ref[...] = acc

    return pl.pallas_call(body, name="ada_weight_grad", grid=(d // tr,),
                          in_specs=[pl.BlockSpec((tr, nb), lambda i: (i, 0)), pl.BlockSpec((nb, c), lambda i: (0, 0))],
                          out_specs=pl.BlockSpec((tr, c), lambda i: (i, 0)),
                          out_shape=jax.ShapeDtypeStruct((d, c), F32), compiler_params=_cparams(1))(c_t, dmod)


def _rows128(a):
    flat = a.reshape(-1).astype(F32)
    pad = (-flat.shape[0]) % (8 * LANES)
    if pad:
        flat = jnp.concatenate([flat, jnp.zeros((pad,), F32)])
    return flat.reshape(-1, LANES)


def kernel(x, c, w_ada, b_ada, norm1, w_in, rnn_conv_w, rnn_conv_b, w_rg_a, b_rg_a, w_rg_i, b_rg_i, rg_lambda, w_o_rnn, w_o_attn, attn_sinks, rel_bias, w_out, norm2, w_up, ffn_conv_w, ffn_conv_b, w_down, norm_f, loss_target, m_w_ada, m_b_ada, m_norm1, m_w_in, m_rnn_conv_w, m_rnn_conv_b, m_w_rg_a, m_b_rg_a, m_w_rg_i, m_b_rg_i, m_rg_lambda, m_w_o_rnn, m_w_o_attn, m_attn_sinks, m_rel_bias, m_w_out, m_norm2, m_w_up, m_ffn_conv_w, m_ffn_conv_b, m_w_down, m_norm_f, v_w_ada, v_b_ada, v_norm1, v_w_in, v_rnn_conv_w, v_rnn_conv_b, v_w_rg_a, v_b_rg_a, v_w_rg_i, v_b_rg_i, v_rg_lambda, v_w_o_rnn, v_w_o_attn, v_attn_sinks, v_rel_bias, v_w_out, v_norm2, v_w_up, v_ffn_conv_w, v_ffn_conv_b, v_w_down, v_norm_f):
    me = 4 * lax.axis_index("x") + 2 * lax.axis_index("y") + lax.axis_index("c")
    xs = x[0]
    tgt = loss_target[0]
    s, d = xs.shape
    bf = lambda w: w[0].astype(BF16)

    mod, c_all = mod_forward(c.reshape(1, 1, d), w_ada[0], b_ada)
    mod = mod.reshape(6, d)
    shift1, scale1, gate1, shift2, scale2, gate2 = [mod[i:i + 1] for i in range(6)]

    g_in, g_rcw, g_fcw = exchange([(bf(w_in), False), (rnn_conv_w[0], False), (ffn_conv_w[0], False)], "gather_w_in")
    w_in_f = jnp.transpose(g_in, (1, 0, 2)).reshape(d, D_IN)
    rcw = jnp.transpose(g_rcw, (1, 0, 2)).reshape(RNN_CONV, D_RNN)
    fcw = jnp.transpose(g_fcw, (1, 0, 2)).reshape(FFN_CONV, 2 * D_FF)
    wa_b, wi_b = bf(w_rg_a), bf(w_rg_i)
    sinks = attn_sinks[0]

    u = prenorm(xs, norm1, scale1, shift1, "prenorm1")
    proj, g_oa, g_or, g_out, g_down = matmul(
        u, w_in_f, "nn", BF16, "mm_in", 1024, 512, 2048,
        comm=[(bf(w_o_attn), False), (bf(w_o_rnn), False), (bf(w_out), False), (bf(w_down), False)])
    w_oa_f, w_or_f = g_oa.reshape(D_ATTN, d), g_or.reshape(D_RNN, d)
    w_out_f, w_down_f = g_out.reshape(d, d), g_down.reshape(D_FF, d)
    bias = band_bias(rel_bias)
    att, g_up = attention_forward(proj, bias, sinks, [(bf(w_up), False)])
    w_up_f = jnp.transpose(g_up, (1, 0, 2)).reshape(d, 2 * D_FF)
    z, hr = rnn_forward(proj, rcw, rnn_conv_b, wa_b, b_rg_a, wi_b, b_rg_i, rg_lambda)
    y_attn = matmul(att, w_oa_f, "nn", BF16, "mm_o_attn", 1024, 1024, 2048)
    y_rnn = matmul(z, w_or_f, "nn", BF16, "mm_o_rnn", 1024, 1024, 2560)
    merged = merge_forward(proj, y_attn, y_rnn)
    h1, mo = matmul(merged, w_out_f, "nn", BF16, "mm_out", 512, 1024, 2048, res=xs, gate=gate1)
    u2 = prenorm(h1, norm2, scale2, shift2, "prenorm2")
    upp = matmul(u2, w_up_f, "nn", BF16, "mm_up", 1024, 1024, 2048)
    act, up_g, up_v = ffn_act_forward(upp, fcw, ffn_conv_b)
    h2, dn = matmul(act, w_down_f, "nn", BF16, "mm_down", 512, 1024, 2048, res=h1, gate=gate2)

    dh2, d_dn, loss_cols, d_norm_f, d_gate2 = loss_head(h2, tgt, dn, norm_f.reshape(1, d), gate2)
    loss = lax.psum(jnp.sum(loss_cols), ("x", "y", "c"))

    d_act = matmul(d_dn, w_down_f, "nt", BF16, "mm_down_dx", 1024, 1536, 2048)
    g_w_down = matmul(act, d_dn, "tn", BF16, "mm_down_dw", 1536, 1024, 2048)
    d_g, d_v, d_fcb_g, d_fcb_v = ffn_act_backward(d_act, up_g, up_v)
    d_upp, d_fcw = conv_backward(d_g, d_v, upp, fcw)
    d_fcb = jnp.concatenate([d_fcb_g, d_fcb_v], axis=1)
    d_u2, p_down = matmul(d_upp, w_up_f, "nt", BF16, "mm_up_dx", 1024, 1024, 3072,
                          comm=[(g_w_down.reshape(N_DEV, D_FF // N_DEV, d), True)])
    g_w_up = matmul(u2, d_upp, "tn", BF16, "mm_up_dw", 1024, 1024, 4096)
    g_up_blk = jnp.transpose(g_w_up.reshape(d, N_DEV, 2 * D_FF // N_DEV), (1, 0, 2))
    dh1, d_shift2, d_scale2, d_norm2, d_mo, d_gate1 = norm_backward(d_u2, h1, dh2, norm2, scale2, "norm2_backward",
                                                                    mo=mo, gate=gate1)
    d_merged = matmul(d_mo, w_out_f, "nt", BF16, "mm_out_dx", 1024, 1024, 2048)
    g_w_out = matmul(merged, d_mo, "tn", BF16, "mm_out_dw", 2048, 1024, 2048)
    d_ga, d_gl, d_ya, d_yr = merge_backward(d_merged, proj, y_attn, y_rnn)
    d_att = matmul(d_ya, w_oa_f, "nt", BF16, "mm_o_attn_dx", 1024, 1024, 2048)
    g_w_oa = matmul(att, d_ya, "tn", BF16, "mm_o_attn_dw", 2048, 1024, 2048)
    d_z = matmul(d_yr, w_or_f, "nt", BF16, "mm_o_rnn_dx", 1024, 1280, 2048)
    g_w_or = matmul(z, d_yr, "tn", BF16, "mm_o_rnn_dw", 1280, 1024, 2048)
    d_xr, d_gr, d_wa, d_wi, d_rsmall = rnn_backward(proj, hr, d_z, rcw, rnn_conv_b, wa_b, b_rg_a, wi_b, b_rg_i, rg_lambda)
    d_q, d_k, d_v_, d_bias, d_sink, p_up = attention_backward(proj, d_att, bias, sinks, [(g_up_blk, True)])
    d_rel = rel_bias_grad(d_bias)
    d_proj = jnp.concatenate([d_q, d_k, d_v_, d_xr, d_gr, d_ga, d_gl], axis=1)

    def rsmall_of(ba_, bi_, lam_, cb_):
        return jnp.stack([ba_[0].reshape(N_RNN_BLOCKS, LANES), bi_[0].reshape(N_RNN_BLOCKS, LANES),
                          lam_[0].reshape(N_RNN_BLOCKS, LANES), cb_[0].reshape(N_RNN_BLOCKS, LANES)]
                         + [jnp.zeros((N_RNN_BLOCKS, LANES), F32)] * 4, axis=1)

    pack = lambda t: jnp.concatenate([_rows128(q) for q in t], axis=0)
    late_n = 2
    g_early = [d_norm2, d_norm_f, d_rsmall, d_wa, d_wi, d_sink[:, 0], d_rel[:, :N_HEADS], d_fcb, d_fcw]

    g_w_in, p_out, p_oa, p_or, small_early = matmul(
        u, d_proj, "tn", BF16, "mm_in_dw", 2048, 512, 2048,
        comm=[(g_w_out.reshape(N_DEV, d // N_DEV, d), True), (g_w_oa.reshape(N_DEV, D_ATTN // N_DEV, d), True),
              (g_w_or.reshape(N_DEV, D_RNN // N_DEV, d), True), (pack(g_early), False)])
    g_in_blk = jnp.transpose(g_w_in.reshape(d, N_DEV, D_IN // N_DEV), (1, 0, 2))
    d_u, p_in = matmul(d_proj, w_in_f, "nt", BF16, "mm_in_dx", 1024, 1024, 2944, comm=[(g_in_blk, True)])
    grad_x, d_shift1, d_scale1, d_norm1 = norm_backward(d_u, xs, dh1, norm1, scale1, "norm1_backward")
    d_mod = jnp.concatenate([d_shift1, d_scale1, d_gate1, d_shift2, d_scale2, d_gate2], axis=1)
    small_late = exchange([(pack([d_mod, d_norm1]), False)], "gather_late_grads")[0]

    g_list = [d_mod, d_norm1] + g_early
    w_list = [b_ada, norm1, norm2, norm_f, rsmall_of(b_rg_a, b_rg_i, rg_lambda, rnn_conv_b), w_rg_a, w_rg_i,
              attn_sinks, rel_bias, ffn_conv_b, jnp.zeros_like(d_fcw)]
    m_list = [m_b_ada, m_norm1, m_norm2, m_norm_f, rsmall_of(m_b_rg_a, m_b_rg_i, m_rg_lambda, m_rnn_conv_b), m_w_rg_a,
              m_w_rg_i, m_attn_sinks, m_rel_bias, m_ffn_conv_b, jnp.zeros_like(d_fcw)]
    v_list = [v_b_ada, v_norm1, v_norm2, v_norm_f, rsmall_of(v_b_rg_a, v_b_rg_i, v_rg_lambda, v_rnn_conv_b), v_w_rg_a,
              v_w_rg_i, v_attn_sinks, v_rel_bias, v_ffn_conv_b, jnp.ones_like(d_fcw)]
    sizes = [_rows128(q).shape[0] for q in g_list]
    offs = np.concatenate([[0], np.cumsum(sizes)]).tolist()
    r_late = offs[late_n]
    late = adamw(small_late, pack(w_list[:late_n]), pack(m_list[:late_n]), pack(v_list[:late_n]), "adamw_small_late",
                 rows=r_late)
    early = adamw(small_early, pack(w_list[late_n:]), pack(m_list[late_n:]), pack(v_list[late_n:]), "adamw_small_early",
                  rows=(offs[-1] - r_late) // 7)

    def seg(packed, idx, like):
        n_el = int(np.prod(like.shape))
        return packed[offs[idx]:offs[idx + 1]].reshape(-1)[:n_el].reshape(like.shape)

    def unpack(kind):
        packed = jnp.concatenate([late[kind], early[kind]], axis=0)
        rs = seg(packed, 4, d_rsmall)
        out = dict(
            b_ada=seg(packed, 0, b_ada), norm1=seg(packed, 1, norm1), norm2=seg(packed, 2, norm2),
            norm_f=seg(packed, 3, norm_f), b_rg_a=rs[:, 0].reshape(1, D_RNN), b_rg_i=rs[:, 1].reshape(1, D_RNN),
            rg_lambda=rs[:, 2].reshape(1, D_RNN), rnn_conv_b=rs[:, 3].reshape(1, D_RNN),
            w_rg_a=seg(packed, 5, w_rg_a), w_rg_i=seg(packed, 6, w_rg_i), attn_sinks=seg(packed, 7, attn_sinks),
            rel_bias=seg(packed, 8, rel_bias), ffn_conv_b=seg(packed, 9, ffn_conv_b))
        out["rnn_conv_w_full"] = jnp.transpose(rs[:, 4:8], (1, 0, 2)).reshape(RNN_CONV, D_RNN)
        out["ffn_conv_w_full"] = seg(packed, 10, d_fcw)
        return out

    small = [unpack(kind) for kind in range(4)]

    rcw_cols = D_RNN // N_DEV
    fcw_cols = 2 * D_FF // N_DEV
    g_rcw_ = lax.dynamic_slice(small[0]["rnn_conv_w_full"], (0, me * rcw_cols), (RNN_CONV, rcw_cols))
    g_fcw_ = lax.dynamic_slice(small[0]["ffn_conv_w_full"], (0, me * fcw_cols), (FFN_CONV, fcw_cols))
    r_rcw = adamw(g_rcw_[None], rnn_conv_w[0], m_rnn_conv_w[0], v_rnn_conv_w[0], "adamw_rnn_conv_w")
    r_fcw = adamw(g_fcw_[None], ffn_conv_w[0], m_ffn_conv_w[0], v_ffn_conv_w[0], "adamw_ffn_conv_w")

    ada_cols = 6 * d // N_DEV
    dmod_all = small_late[:, offs[0]:offs[1]].reshape(N_DEV, 6 * d)
    dmod_cols = lax.dynamic_slice(dmod_all, (0, me * ada_cols), (N_DEV, ada_cols))
    g_ada = ada_weight_grad(jnp.transpose(c_all.reshape(N_DEV, d)), dmod_cols)
    r_ada = adamw(g_ada[None], w_ada[0], m_w_ada[0], v_w_ada[0], "adamw_w_ada")

    r_in = adamw(p_in, w_in[0], m_w_in[0], v_w_in[0], "adamw_w_in")
    r_up = adamw(p_up, w_up[0], m_w_up[0], v_w_up[0], "adamw_w_up")
    r_or = adamw(p_or, w_o_rnn[0], m_w_o_rnn[0], v_w_o_rnn[0], "adamw_w_o_rnn", rows=160)
    r_oa = adamw(p_oa, w_o_attn[0], m_w_o_attn[0], v_w_o_attn[0], "adamw_w_o_attn")
    r_out = adamw(p_out, w_out[0], m_w_out[0], v_w_out[0], "adamw_w_out")
    r_down = adamw(p_down, w_down[0], m_w_down[0], v_w_down[0], "adamw_w_down")

    def res(kind):
        sm = small[kind]
        return [r_ada[kind][None], sm["b_ada"], sm["norm1"], r_in[kind][None], r_rcw[kind][None], sm["rnn_conv_b"],
                sm["w_rg_a"], sm["b_rg_a"], sm["w_rg_i"], sm["b_rg_i"], sm["rg_lambda"], r_or[kind][None],
                r_oa[kind][None], sm["attn_sinks"], sm["rel_bias"], r_out[kind][None], sm["norm2"], r_up[kind][None],
                r_fcw[kind][None], sm["ffn_conv_b"], r_down[kind][None], sm["norm_f"]]

    return (loss, grad_x[None], *res(0), *res(1), *res(2), *res(3))
```

```python
import functools
import math

import numpy as np
import jax
import jax.numpy as jnp
from jax import lax
from jax.experimental import pallas as pl
from jax.experimental.pallas import tpu as pltpu

F32, BF16 = jnp.float32, jnp.bfloat16

N_DEV = 8
D_MODEL = 2048
N_HEADS, HEAD_DIM, N_KV = 32, 64, 4
GROUP = N_HEADS // N_KV
D_ATTN, D_KV = N_HEADS * HEAD_DIM, N_KV * HEAD_DIM
BLOCK = 128
NUM_BUCKETS, MAX_DISTANCE = 32, 128
D_RNN, N_RNN_BLOCKS, RNN_BLOCK = 2560, 20, 128
RNN_CONV, FFN_CONV = 4, 3
RG_C = 8.0
D_FF = 3 * D_MODEL
D_IN = D_ATTN + 2 * D_KV + 2 * D_RNN + 2 * D_MODEL
EPS = 1e-6
NEG_INF = -1e30
ADAM_LR, ADAM_B1, ADAM_B2, ADAM_EPS, ADAM_WD, ADAM_STEP = 0.001, 0.9, 0.999, 1e-08, 0.01, 10

LANES = 128
HALO = 16
VMEM_LIMIT = 56 * 1024 * 1024
MESH = pl.DeviceIdType.MESH
RELAY_AT_NUM, RELAY_AT_DEN = 2, 3

Q0, K0, V0, XR0, GR0, GA0, GL0 = 0, 2048, 2304, 2560, 5120, 7680, 9728


def _cparams(n_axes):
    return pltpu.CompilerParams(dimension_semantics=("arbitrary",) * n_axes, vmem_limit_bytes=VMEM_LIMIT)


def _gelu(x):
    k = math.sqrt(2.0 / math.pi)
    return 0.5 * x * (1.0 + jnp.tanh(k * (x + 0.044715 * x * x * x)))


def _gelu_and_grad(x):
    k = math.sqrt(2.0 / math.pi)
    t = jnp.tanh(k * (x + 0.044715 * x * x * x))
    g = 0.5 * x * (1.0 + t)
    dg = 0.5 * (1.0 + t) + 0.5 * x * (1.0 - t * t) * k * (1.0 + 3.0 * 0.044715 * x * x)
    return g, dg


def _sigmoid(x):
    return 1.0 / (1.0 + jnp.exp(-x))


def _shift_down(prev, x, j):
    if j == 0:
        return x
    xe = jnp.concatenate([prev, x], axis=0)
    return pltpu.roll(xe, j, axis=0)[HALO:, :]


def _shift_up(x, nxt, j):
    if j == 0:
        return x
    xe = jnp.concatenate([x, nxt], axis=0)
    n = xe.shape[0]
    return pltpu.roll(xe, n - j, axis=0)[: x.shape[0], :]


def _my_coords():
    return lax.axis_index("x"), lax.axis_index("y"), lax.axis_index("c")


def _peer(x, y, c, k):
    kx, ky, kc = (k >> 2) & 1, (k >> 1) & 1, k & 1
    px, py, pc = (x + kx) % 2, (y + ky) % 2, (c + kc) % 2
    return (px, py, pc), 4 * px + 2 * py + pc


def _exchange_shapes(items):
    return [jax.ShapeDtypeStruct((N_DEV,) + tuple(arr.shape[1:] if s else arr.shape), arr.dtype) for arr, s in items]


def _exchange_sems(n):
    return [pltpu.SemaphoreType.DMA((n, N_DEV - 1)), pltpu.SemaphoreType.DMA((n, N_DEV - 1)),
            pltpu.SemaphoreType.DMA((n,))]


def _exchange_copies(srcs, dsts, scat, send_sems, recv_sems, loc_sems):
    x, y, c = _my_coords()
    me = 4 * x + 2 * y + c
    copies = []
    for a in range(len(srcs)):
        mine = srcs[a].at[me] if scat[a] else srcs[a]
        copies.append(pltpu.make_async_copy(mine, dsts[a].at[me], loc_sems.at[a]))
    for k in range(1, N_DEV):
        peer, p = _peer(x, y, c, k)
        for a in range(len(srcs)):
            src = srcs[a].at[p] if scat[a] else srcs[a]
            copies.append(pltpu.make_async_remote_copy(
                src_ref=src, dst_ref=dsts[a].at[me], send_sem=send_sems.at[a, k - 1],
                recv_sem=recv_sems.at[a, k - 1], device_id=peer, device_id_type=MESH))
    return copies


def _exchange_steps(phase, srcs, dsts, scat, send_sems, recv_sems, loc_sems):
    if any(scat):
        if phase == 1:
            return []
        copies = _exchange_copies(srcs, dsts, scat, send_sems, recv_sems, loc_sems)
        return [cp.start if phase == 0 else cp.wait for cp in copies]
    x, y, c = _my_coords()
    me = 4 * x + 2 * y + c
    sibling, _ = _peer(x, y, c, 1)

    def remote(a, src, dst, k, to):
        return pltpu.make_async_remote_copy(src_ref=src, dst_ref=dst, send_sem=send_sems.at[a, k - 1],
                                            recv_sem=recv_sems.at[a, k - 1], device_id=to, device_id_type=MESH)

    acts = []
    for a in range(len(srcs)):
        if phase != 1:
            loc = pltpu.make_async_copy(srcs[a], dsts[a].at[me], loc_sems.at[a])
            sib = remote(a, srcs[a], dsts[a].at[me], 1, sibling)
            acts += [loc.start, sib.start] if phase == 0 else [loc.wait, sib.wait]
        for k in (2, 4, 6):
            peer, p = _peer(x, y, c, k)
            out = remote(a, srcs[a], dsts[a].at[me], k, peer)
            if phase == 0:
                acts.append(out.start)
            else:
                onward = remote(a, dsts[a].at[p], dsts[a].at[p], k + 1, sibling)
                acts += [out.wait_recv, onward.start] if phase == 1 else [out.wait_send, onward.wait]
    return acts


def exchange(items, name):
    n = len(items)
    scat = [s for _, s in items]

    def body(*refs):
        for phase in range(3):
            for act in _exchange_steps(phase, refs[:n], refs[n:2 * n], scat, *refs[2 * n:]):
                act()

    any_spec = pl.BlockSpec(memory_space=pl.ANY)
    return pl.pallas_call(
        body, name=name, out_shape=tuple(_exchange_shapes(items)),
        in_specs=[any_spec] * n, out_specs=tuple([any_spec] * n), scratch_shapes=_exchange_sems(n),
    )(*[a for a, _ in items])


def mod_forward(c, w_ada, b_ada):
    d, ncol = w_ada.shape

    def body(c_ref, w_ref, b_ref, mod_ref, call_ref, cols_ref, s1, r1, s2, r2):
        x, y, c_ = _my_coords()
        me = 4 * x + 2 * y + c_
        call_ref[me] = c_ref[0]
        sends = []
        for k in range(1, N_DEV):
            peer, p = _peer(x, y, c_, k)
            cp = pltpu.make_async_remote_copy(src_ref=c_ref.at[0], dst_ref=call_ref.at[me], send_sem=s1.at[k - 1],
                                              recv_sem=r1.at[k - 1], device_id=peer, device_id_type=MESH)
            cp.start()
            sends.append(cp)
        for cp in sends:
            cp.wait()
        rows = lax.broadcasted_iota(jnp.int32, (N_DEV, d), 0)
        cmat = jnp.zeros((N_DEV, d), F32)
        for b in range(N_DEV):
            cmat = jnp.where(rows == b, call_ref[b], cmat)
        cs = cmat * _sigmoid(cmat)
        bias = b_ref[:, pl.ds(pl.multiple_of(me * ncol, LANES), ncol)]
        mc = jnp.dot(cs, w_ref[...], preferred_element_type=F32, precision=lax.Precision.HIGHEST) + bias
        for b in range(N_DEV):
            cols_ref[b] = mc[b:b + 1, :]
        mod_ref[me] = cols_ref[me]
        sends = []
        for k in range(1, N_DEV):
            peer, p = _peer(x, y, c_, k)
            cp = pltpu.make_async_remote_copy(src_ref=cols_ref.at[p], dst_ref=mod_ref.at[me], send_sem=s2.at[k - 1],
                                              recv_sem=r2.at[k - 1], device_id=peer, device_id_type=MESH)
            cp.start()
            sends.append(cp)
        for cp in sends:
            cp.wait()

    vm = pl.BlockSpec(memory_space=pltpu.VMEM)
    return pl.pallas_call(
        body, name="mod_forward",
        out_shape=(jax.ShapeDtypeStruct((N_DEV, 1, ncol), F32), jax.ShapeDtypeStruct((N_DEV, 1, d), F32)),
        in_specs=[vm, vm, vm], out_specs=(vm, vm),
        scratch_shapes=[pltpu.VMEM((N_DEV, 1, ncol), F32)] + [pltpu.SemaphoreType.DMA((N_DEV - 1,))] * 4,
        compiler_params=pltpu.CompilerParams(vmem_limit_bytes=VMEM_LIMIT),
    )(c, w_ada, b_ada)


def matmul(a, b, mode, out_dtype, name, tm, tn, tk, res=None, gate=None, comm=None):
    if mode == "nn":
        (m, kk), (_, n) = a.shape, b.shape
    elif mode == "nt":
        (m, kk), (n, _) = a.shape, b.shape
    else:
        (kk, m), (_, n) = a.shape, b.shape
    tm, tn, tk = min(tm, m), min(tn, n), min(tk, kk)
    assert m % tm == 0 and n % tn == 0 and kk % tk == 0, (name, m, n, kk, tm, tn, tk)
    if mode == "nn":
        a_spec = pl.BlockSpec((tm, tk), lambda j, i, k: (i, k))
        b_spec = pl.BlockSpec((tk, tn), lambda j, i, k: (k, j))
        dims = (((1,), (0,)), ((), ()))
    elif mode == "nt":
        a_spec = pl.BlockSpec((tm, tk), lambda j, i, k: (i, k))
        b_spec = pl.BlockSpec((tn, tk), lambda j, i, k: (j, k))
        dims = (((1,), (1,)), ((), ()))
    else:
        a_spec = pl.BlockSpec((tk, tm), lambda j, i, k: (k, i))
        b_spec = pl.BlockSpec((tk, tn), lambda j, i, k: (k, j))
        dims = (((0,), (0,)), ((), ()))
    nj, ni, nk = n // tn, m // tm, kk // tk
    fused = res is not None
    items = list(comm or [])
    nc = len(items)
    scat = [s_ for _, s_ in items]
    n_in = (4 if fused else 2) + nc
    n_out = (2 if fused else 1) + nc
    relay_step = (RELAY_AT_NUM * nj * ni * nk) // RELAY_AT_DEN
    o_spec = pl.BlockSpec((tm, tn), lambda j, i, k: (i, j))

    def body(*refs):
        ins, outs, scratch = refs[:n_in], refs[n_in:n_in + n_out], refs[n_in + n_out:]
        a_ref, b_ref = ins[:2]
        o_ref = outs[0]
        acc_ref = scratch[0] if nk > 1 else None
        j, i, k = pl.program_id(0), pl.program_id(1), pl.program_id(2)

        def run(phase):
            for act in _exchange_steps(phase, ins[n_in - nc:], outs[n_out - nc:], scat, *scratch[len(scratch) - 3:]):
                act()

        step = (j * ni + i) * nk + k
        if nc:
            @pl.when(step == 0)
            def _():
                run(0)

            @pl.when(step == relay_step)
            def _():
                run(1)

        def finish(acc):
            if fused:
                o_ref[...] = ins[2][...] + ins[3][...] * acc
                outs[1][...] = acc.astype(outs[1].dtype)
            else:
                o_ref[...] = acc.astype(o_ref.dtype)

        prod = lax.dot_general(a_ref[...], b_ref[...], dims, preferred_element_type=F32)
        if nk == 1:
            finish(prod)
        else:
            @pl.when(k == 0)
            def _():
                acc_ref[...] = prod

            @pl.when(k > 0)
            def _():
                acc_ref[...] += prod

            @pl.when(k == nk - 1)
            def _():
                finish(acc_ref[...])

        if nc:
            @pl.when(step == nj * ni * nk - 1)
            def _():
                run(2)

    any_spec = pl.BlockSpec(memory_space=pl.ANY)
    in_specs, args = [a_spec, b_spec], [a, b]
    if fused:
        in_specs += [o_spec, pl.BlockSpec((1, tn), lambda j, i, k: (0, j))]
        args += [res, gate]
        out_shape = [jax.ShapeDtypeStruct((m, n), F32), jax.ShapeDtypeStruct((m, n), out_dtype)]
        out_specs = [o_spec, o_spec]
    else:
        out_shape = [jax.ShapeDtypeStruct((m, n), out_dtype)]
        out_specs = [o_spec]
    in_specs += [any_spec] * nc
    args += [arr for arr, _ in items]
    out_shape += _exchange_shapes(items)
    out_specs += [any_spec] * nc
    scratch = ([pltpu.VMEM((tm, tn), F32)] if nk > 1 else []) + (_exchange_sems(nc) if nc else [])
    outs = pl.pallas_call(
        body, name=name, grid=(nj, ni, nk), in_specs=in_specs, out_specs=tuple(out_specs), out_shape=tuple(out_shape),
        scratch_shapes=scratch, compiler_params=_cparams(3),
    )(*args)
    return outs[0] if len(outs) == 1 else outs


def _row_tile(s, want):
    t = min(want, s)
    assert s % t == 0 and t % HALO == 0
    return t


def prenorm(x, nw, scale, shift, name):
    s, d = x.shape
    tm = _row_tile(s, 512)

    def body(x_ref, nw_ref, sc_ref, sh_ref, o_ref):
        xv = x_ref[...]
        r = lax.rsqrt(jnp.mean(xv * xv, axis=-1, keepdims=True) + EPS)
        o_ref[...] = ((xv * r) * nw_ref[...] * (1.0 + sc_ref[...]) + sh_ref[...]).astype(BF16)

    row = pl.BlockSpec((tm, d), lambda i: (i, 0))
    vec = pl.BlockSpec((1, d), lambda i: (0, 0))
    return pl.pallas_call(body, name=name, grid=(s // tm,), in_specs=[row, vec, vec, vec], out_specs=row,
                          out_shape=jax.ShapeDtypeStruct((s, d), BF16), compiler_params=_cparams(1))(x, nw, scale, shift)


def norm_backward(du, xin, dres, nw, scale, name, mo=None, gate=None):
    s, d = xin.shape
    tm = _row_tile(s, 256)
    gated = mo is not None

    def body(*refs):
        if gated:
            du_ref, x_ref, dr_ref, nw_ref, sc_ref, mo_ref, g_ref, dx_ref, dsh_ref, dsc_ref, dnw_ref, dmo_ref, dg_ref = refs
        else:
            du_ref, x_ref, dr_ref, nw_ref, sc_ref, dx_ref, dsh_ref, dsc_ref, dnw_ref = refs
        i = pl.program_id(0)
        xv = x_ref[...]
        r = lax.rsqrt(jnp.mean(xv * xv, axis=-1, keepdims=True) + EPS)
        xn = xv * r
        duv = du_ref[...].astype(F32)
        nwv, scv = nw_ref[...], sc_ref[...]
        dxn = duv * (nwv * (1.0 + scv))
        dx = dr_ref[...] + r * (dxn - xn * jnp.mean(dxn * xn, axis=-1, keepdims=True))
        dx_ref[...] = dx
        sums = [jnp.sum(duv, axis=0, keepdims=True), jnp.sum(duv * xn * nwv, axis=0, keepdims=True),
                jnp.sum(duv * xn * (1.0 + scv), axis=0, keepdims=True)]
        accs = [dsh_ref, dsc_ref, dnw_ref]
        if gated:
            dmo_ref[...] = (dx * g_ref[...]).astype(BF16)
            sums.append(jnp.sum(dx * mo_ref[...].astype(F32), axis=0, keepdims=True))
            accs.append(dg_ref)

        @pl.when(i == 0)
        def _():
            for acc, sm in zip(accs, sums):
                acc[...] = sm

        @pl.when(i > 0)
        def _():
            for acc, sm in zip(accs, sums):
                acc[...] += sm

    row = pl.BlockSpec((tm, d), lambda i: (i, 0))
    vec = pl.BlockSpec((1, d), lambda i: (0, 0))
    vshape = jax.ShapeDtypeStruct((1, d), F32)
    in_specs, args = [row, row, row, vec, vec], [du, xin, dres, nw, scale]
    out_specs, out_shape = [row, vec, vec, vec], [jax.ShapeDtypeStruct((s, d), F32), vshape, vshape, vshape]
    if gated:
        in_specs += [row, vec]
        args += [mo, gate]
        out_specs += [row, vec]
        out_shape += [jax.ShapeDtypeStruct((s, d), BF16), vshape]
    return pl.pallas_call(body, name=name, grid=(s // tm,), in_specs=in_specs, out_specs=tuple(out_specs),
                          out_shape=tuple(out_shape), compiler_params=_cparams(1))(*args)


def merge_forward(proj, y_attn, y_rnn):
    s, d = y_attn.shape
    tm, cw = _row_tile(s, 1024), 512

    def body(ga_ref, gl_ref, ya_ref, yr_ref, o_ref):
        o_ref[...] = (_sigmoid(ga_ref[...].astype(F32)) * ya_ref[...].astype(F32)
                      + _sigmoid(gl_ref[...].astype(F32)) * yr_ref[...].astype(F32)).astype(BF16)

    def at(off):
        return pl.BlockSpec((tm, cw), lambda j, i: (i, off // cw + j))

    return pl.pallas_call(body, name="merge_forward", grid=(d // cw, s // tm),
                          in_specs=[at(GA0), at(GL0), at(0), at(0)], out_specs=at(0),
                          out_shape=jax.ShapeDtypeStruct((s, d), BF16), compiler_params=_cparams(2))(proj, proj, y_attn, y_rnn)


def merge_backward(dmerged, proj, y_attn, y_rnn):
    s, d = y_attn.shape
    tm, cw = _row_tile(s, 1024), 512

    def body(dm_ref, ga_ref, gl_ref, ya_ref, yr_ref, dga_ref, dgl_ref, dya_ref, dyr_ref):
        dm = dm_ref[...].astype(F32)
        sa, sl = _sigmoid(ga_ref[...].astype(F32)), _sigmoid(gl_ref[...].astype(F32))
        dga_ref[...] = (dm * ya_ref[...].astype(F32) * sa * (1.0 - sa)).astype(BF16)
        dgl_ref[...] = (dm * yr_ref[...].astype(F32) * sl * (1.0 - sl)).astype(BF16)
        dya_ref[...] = (dm * sa).astype(BF16)
        dyr_ref[...] = (dm * sl).astype(BF16)

    def at(off):
        return pl.BlockSpec((tm, cw), lambda j, i: (i, off // cw + j))

    o = jax.ShapeDtypeStruct((s, d), BF16)
    return pl.pallas_call(body, name="merge_backward", grid=(d // cw, s // tm),
                          in_specs=[at(0), at(GA0), at(GL0), at(0), at(0)], out_specs=(at(0),) * 4,
                          out_shape=(o, o, o, o), compiler_params=_cparams(2))(dmerged, proj, proj, y_attn, y_rnn)


def _prev_spec(tm, cw, off_blocks):
    r = tm // HALO
    return pl.BlockSpec((HALO, cw), lambda j, i: (jnp.maximum(i * r - 1, 0), off_blocks + j))


def ffn_act_forward(upp, cw_full, cb_full):
    s, f2 = upp.shape
    f = f2 // 2
    tm, cw = _row_tile(s, 512), 1536
    nj = f // cw

    def body(g_ref, gp_ref, v_ref, vp_ref, wg_ref, wv_ref, bg_ref, bv_ref, o_ref, og_ref, ov_ref):
        i = pl.program_id(1)
        first = i == 0

        def conv(x_ref, p_ref, w_ref, b_ref):
            xv = x_ref[...].astype(F32)
            pv = jnp.where(first, 0.0, p_ref[...].astype(F32))
            acc = b_ref[...] + w_ref[FFN_CONV - 1:FFN_CONV, :] * xv
            for k in range(FFN_CONV - 1):
                acc = acc + w_ref[k:k + 1, :] * _shift_down(pv, xv, FFN_CONV - 1 - k)
            return acc

        g = conv(g_ref, gp_ref, wg_ref, bg_ref)
        v = conv(v_ref, vp_ref, wv_ref, bv_ref)
        og_ref[...] = g.astype(BF16)
        ov_ref[...] = v.astype(BF16)
        o_ref[...] = (_gelu(g) * v).astype(BF16)

    def tile(ob):
        return pl.BlockSpec((tm, cw), lambda j, i: (i, ob + j))

    def par(rows, ob):
        return pl.BlockSpec((rows, cw), lambda j, i: (0, ob + j))

    o = jax.ShapeDtypeStruct((s, f), BF16)
    return pl.pallas_call(
        body, name="ffn_act_forward", grid=(nj, s // tm),
        in_specs=[tile(0), _prev_spec(tm, cw, 0), tile(nj), _prev_spec(tm, cw, nj),
                  par(FFN_CONV, 0), par(FFN_CONV, nj), par(1, 0), par(1, nj)],
        out_specs=(tile(0), tile(0), tile(0)), out_shape=(o, o, o), compiler_params=_cparams(2),
    )(upp, upp, upp, upp, cw_full, cw_full, cb_full, cb_full)


ROW_CHUNK = 16
LANE_CHUNK = 512


def ffn_act_backward(dact, up_g, up_v):
    s, f = dact.shape
    tm, cw = _row_tile(s, 512), 1536
    nr = tm // ROW_CHUNK

    def body(da_ref, g_ref, v_ref, dg_ref, dv_ref, sg_ref, sv_ref, acc_ref):
        i = pl.program_id(1)
        acc_ref[...] = jnp.zeros_like(acc_ref)

        def chunk(r, carry):
            rows = pl.ds(pl.multiple_of(r * ROW_CHUNK, ROW_CHUNK), ROW_CHUNK)
            for c0 in range(0, cw, LANE_CHUNK):
                cols = pl.ds(c0, LANE_CHUNK)
                da = da_ref[rows, cols].astype(F32)
                ge, dge = _gelu_and_grad(g_ref[rows, cols].astype(F32))
                dg = da * v_ref[rows, cols].astype(F32) * dge
                dv = da * ge
                dg_ref[rows, cols] = dg.astype(BF16)
                dv_ref[rows, cols] = dv.astype(BF16)
                acc_ref[0, :, cols] += dg[0:8] + dg[8:16]
                acc_ref[1, :, cols] += dv[0:8] + dv[8:16]
            return carry

        lax.fori_loop(0, nr, chunk, 0)
        sg = jnp.sum(acc_ref[0], axis=0, keepdims=True)
        sv = jnp.sum(acc_ref[1], axis=0, keepdims=True)

        @pl.when(i == 0)
        def _():
            sg_ref[...] = sg
            sv_ref[...] = sv

        @pl.when(i > 0)
        def _():
            sg_ref[...] += sg
            sv_ref[...] += sv

    tile = pl.BlockSpec((tm, cw), lambda j, i: (i, j))
    vec = pl.BlockSpec((1, cw), lambda j, i: (0, j))
    o = jax.ShapeDtypeStruct((s, f), BF16)
    v1 = jax.ShapeDtypeStruct((1, f), F32)
    return pl.pallas_call(
        body, name="ffn_act_backward", grid=(f // cw, s // tm), in_specs=[tile, tile, tile],
        out_specs=(tile, tile, vec, vec), out_shape=(o, o, v1, v1),
        scratch_shapes=[pltpu.VMEM((2, 8, cw), F32)], compiler_params=_cparams(2),
    )(dact, up_g, up_v)


def conv_backward(d_g, d_v, upp, cw_full):
    s, f = d_g.shape
    f2 = 2 * f
    tm, cw = _row_tile(s, 512), 1536
    nt, nj = s // tm, f // cw
    r_halo = tm // HALO
    nr = tm // ROW_CHUNK

    def body(xg_ref, ng_ref, xv_ref, nv_ref, u_ref, w_ref, o_ref, dw_ref, acc_ref):
        j, i = pl.program_id(0), pl.program_id(1)
        acc_ref[...] = jnp.zeros_like(acc_ref)

        def run(x_ref, n_ref):
            def chunk(r, carry):
                rows = pl.ds(pl.multiple_of(r * ROW_CHUNK, ROW_CHUNK), ROW_CHUNK)
                nrows = pl.ds(pl.multiple_of(jnp.minimum(r + 1, nr - 1) * ROW_CHUNK, ROW_CHUNK), ROW_CHUNK)
                for c0 in range(0, cw, LANE_CHUNK):
                    cols = pl.ds(c0, LANE_CHUNK)
                    cur = x_ref[rows, cols].astype(F32)
                    halo = jnp.where(i == nt - 1, 0.0, n_ref[:, cols].astype(F32))
                    nxt = jnp.where(r == nr - 1, halo, x_ref[nrows, cols].astype(F32))
                    uv = u_ref[rows, cols].astype(F32)
                    acc = None
                    for k in range(FFN_CONV):
                        sh = _shift_up(cur, nxt, FFN_CONV - 1 - k)
                        term = w_ref[k:k + 1, cols] * sh
                        acc = term if acc is None else acc + term
                        pr = uv * sh
                        acc_ref[k, :, cols] += pr[0:8] + pr[8:16]
                    o_ref[rows, cols] = acc.astype(BF16)
                return carry

            lax.fori_loop(0, nr, chunk, 0)

        @pl.when(j < nj)
        def _():
            run(xg_ref, ng_ref)

        @pl.when(j >= nj)
        def _():
            run(xv_ref, nv_ref)

        sums = jnp.concatenate([jnp.sum(acc_ref[k], axis=0, keepdims=True) for k in range(FFN_CONV)], axis=0)

        @pl.when(i == 0)
        def _():
            dw_ref[...] = sums

        @pl.when(i > 0)
        def _():
            dw_ref[...] += sums

    half = lambda j: j % nj
    tile_h = pl.BlockSpec((tm, cw), lambda j, i: (i, half(j)))
    next_h = pl.BlockSpec((HALO, cw), lambda j, i: (jnp.minimum((i + 1) * r_halo, s // HALO - 1), half(j)))
    tile = pl.BlockSpec((tm, cw), lambda j, i: (i, j))
    par = pl.BlockSpec((FFN_CONV, cw), lambda j, i: (0, j))
    return pl.pallas_call(
        body, name="ffn_conv_backward", grid=(2 * nj, nt), in_specs=[tile_h, next_h, tile_h, next_h, tile, par],
        out_specs=(tile, par), out_shape=(jax.ShapeDtypeStruct((s, f2), BF16), jax.ShapeDtypeStruct((FFN_CONV, f2), F32)),
        scratch_shapes=[pltpu.VMEM((FFN_CONV, 8, cw), F32)], compiler_params=_cparams(2),
    )(d_g, d_g, d_v, d_v, upp, cw_full)


def loss_head(h2, target, dn, norm_f, gate2):
    s, d = h2.shape
    tm = _row_tile(s, 256)

    def body(h_ref, t_ref, dn_ref, nf_ref, g_ref, dh_ref, ddn_ref, loss_ref, dnf_ref, dg_ref):
        i = pl.program_id(0)
        hv = h_ref[...]
        r = lax.rsqrt(jnp.mean(hv * hv, axis=-1, keepdims=True) + EPS)
        yh = hv * r
        nf = nf_ref[...]
        err = yh * nf - t_ref[...]
        dy = err * (1.0 / d)
        dyh = dy * nf
        dh = r * (dyh - yh * jnp.mean(dyh * yh, axis=-1, keepdims=True))
        dh_ref[...] = dh
        ddn_ref[...] = (dh * g_ref[...]).astype(BF16)
        sums = [jnp.sum(err * err, axis=0, keepdims=True) * (0.5 / d), jnp.sum(dy * yh, axis=0, keepdims=True),
                jnp.sum(dh * dn_ref[...].astype(F32), axis=0, keepdims=True)]
        accs = [loss_ref, dnf_ref, dg_ref]

        @pl.when(i == 0)
        def _():
            for acc, sm in zip(accs, sums):
                acc[...] = sm

        @pl.when(i > 0)
        def _():
            for acc, sm in zip(accs, sums):
                acc[...] += sm

    row = pl.BlockSpec((tm, d), lambda i: (i, 0))
    vec = pl.BlockSpec((1, d), lambda i: (0, 0))
    v = jax.ShapeDtypeStruct((1, d), F32)
    return pl.pallas_call(
        body, name="loss_head", grid=(s // tm,), in_specs=[row, row, row, vec, vec], out_specs=(row, row, vec, vec, vec),
        out_shape=(jax.ShapeDtypeStruct((s, d), F32), jax.ShapeDtypeStruct((s, d), BF16), v, v, v),
        compiler_params=_cparams(1))(h2, target, dn, norm_f, gate2)


def _t5_buckets():
    qi = np.arange(BLOCK)[:, None]
    kj = np.arange(2 * BLOCK)[None, :]
    dist = qi + BLOCK - kj
    dd = np.maximum(dist, 0)
    max_exact = NUM_BUCKETS // 2
    dflt = np.maximum(dd, 1).astype(np.float32)
    large = max_exact + (np.log(dflt / max_exact) / math.log(MAX_DISTANCE / max_exact)
                         * (NUM_BUCKETS - max_exact)).astype(np.int32)
    large = np.minimum(large, NUM_BUCKETS - 1)
    bucket = np.where(dd < max_exact, dd, large).astype(np.int32)
    in_window = (dist >= 0) & (dist < BLOCK)
    return bucket, in_window


def band_bias(rel_bias):
    bucket, in_window = _t5_buckets()
    bucket_t = jnp.asarray(np.where(in_window, bucket, -1).astype(np.int32).T)

    def body(rb_ref, bk_ref, o_ref):
        bk = bk_ref[...]
        for h in range(N_HEADS):
            acc = jnp.full((2 * BLOCK, BLOCK), NEG_INF, F32)
            for b in range(NUM_BUCKETS):
                acc = jnp.where(bk == b, rb_ref[b, h], acc)
            o_ref[h] = acc

    return pl.pallas_call(
        body, name="band_bias", out_shape=jax.ShapeDtypeStruct((N_HEADS, 2 * BLOCK, BLOCK), F32),
        in_specs=[pl.BlockSpec(memory_space=pltpu.SMEM), pl.BlockSpec(memory_space=pltpu.VMEM)],
        out_specs=pl.BlockSpec(memory_space=pltpu.VMEM))(rel_bias, bucket_t)


def rel_bias_grad(dbias):
    bucket, in_window = _t5_buckets()
    bucket_t = jnp.asarray(np.where(in_window, bucket, -1).astype(np.int32).T)

    def body(db_ref, bk_ref, o_ref):
        bk = bk_ref[...]
        rows = lax.broadcasted_iota(jnp.int32, (NUM_BUCKETS, LANES), 0)
        lanes = lax.broadcasted_iota(jnp.int32, (NUM_BUCKETS, LANES), 1)
        acc = jnp.zeros((NUM_BUCKETS, LANES), F32)
        for h in range(N_HEADS):
            dv = db_ref[h]
            for b in range(NUM_BUCKETS):
                sm = jnp.sum(jnp.where(bk == b, dv, 0.0))
                acc = jnp.where((rows == b) & (lanes == h), sm, acc)
        o_ref[...] = acc

    vm = pl.BlockSpec(memory_space=pltpu.VMEM)
    return pl.pallas_call(body, name="rel_bias_grad", out_shape=jax.ShapeDtypeStruct((NUM_BUCKETS, LANES), F32),
                          in_specs=[vm, vm], out_specs=vm)(dbias, bucket_t)


HP = 2
Q_PER_HP = D_ATTN // HP
H_PER_HP = N_HEADS // HP
NT_DIMS = (((1,), (1,)), ((), ()))
TN_DIMS = (((0,), (0,)), ((), ()))


def _stack_heads(ref, hh):
    lane = lax.broadcasted_iota(jnp.int32, (BLOCK, LANES), 1)
    lo = lane < HEAD_DIM
    parts = []
    for s_ in range(GROUP // 2):
        c0 = hh * (GROUP * HEAD_DIM) + s_ * LANES
        slab = ref[:, c0:c0 + LANES]
        parts.append(jnp.where(lo, slab, jnp.zeros_like(slab)))
        parts.append(jnp.where(lo, jnp.zeros_like(slab), slab))
    return jnp.concatenate(parts, axis=0)


def _attn_probs(hh, q_ref, kp_ref, kc_ref, vp_ref, vc_ref, bias_ref, sink_ref, hp, n):
    lane = lax.broadcasted_iota(jnp.int32, (2 * BLOCK, LANES), 1)
    own = (lane >= HEAD_DIM) if hh == 1 else (lane < HEAD_DIM)
    kband = jnp.concatenate([kp_ref[...], kc_ref[...]], axis=0)
    vband = jnp.concatenate([vp_ref[...], vc_ref[...]], axis=0)
    kk = jnp.where(own, kband, pltpu.roll(kband, HEAD_DIM, axis=1))
    vv = jnp.where(own, vband, pltpu.roll(vband, HEAD_DIM, axis=1))
    qs = _stack_heads(q_ref, hh)
    sc = lax.dot_general(kk, qs, NT_DIMS, preferred_element_type=F32) * (HEAD_DIM ** -0.5)
    sc = sc + jnp.concatenate([bias_ref[hh * GROUP + g] for g in range(GROUP)], axis=1)
    krow = lax.broadcasted_iota(jnp.int32, sc.shape, 0)
    sc = jnp.where((n == 0) & (krow < BLOCK), NEG_INF, sc)
    sink = jnp.concatenate([jnp.full((1, BLOCK), sink_ref[hp * H_PER_HP + hh * GROUP + g], F32) for g in range(GROUP)], axis=1)
    m = jnp.maximum(jnp.max(sc, axis=0, keepdims=True), sink)
    p = jnp.exp(sc - m)
    es = jnp.exp(sink - m)
    inv = 1.0 / (jnp.sum(p, axis=0, keepdims=True) + es)
    return qs, kk, vv, p * inv, es * inv


def _unstack(o, dtype):
    lane = lax.broadcasted_iota(jnp.int32, (BLOCK, LANES), 1)
    lo = lane < HEAD_DIM
    slabs = []
    for s_ in range(GROUP // 2):
        ev = o[(2 * s_) * BLOCK:(2 * s_ + 1) * BLOCK]
        od = o[(2 * s_ + 1) * BLOCK:(2 * s_ + 2) * BLOCK]
        slabs.append(jnp.where(lo, ev, od).astype(dtype))
    return slabs


def attention_forward(proj, bias, sinks, comm):
    s = proj.shape[0]
    nb = s // BLOCK
    kb, vb = K0 // LANES, V0 // LANES
    nc = len(comm)
    scat = [s_ for _, s_ in comm]

    def body(*refs):
        q_ref, kp_ref, kc_ref, vp_ref, vc_ref, bias_ref, sink_ref = refs[:7]
        srcs, o_ref, dsts, sems = refs[7:7 + nc], refs[7 + nc], refs[8 + nc:8 + 2 * nc], refs[8 + 2 * nc:]
        hp, n = pl.program_id(0), pl.program_id(1)

        step = hp * nb + n

        def run(phase):
            for act in _exchange_steps(phase, srcs, dsts, scat, *sems):
                act()

        @pl.when(step == 0)
        def _():
            run(0)

        @pl.when(step == HP * nb - HP * nb // 16)
        def _():
            run(1)

        for hh in range(2):
            qs, kk, vv, probs, _ = _attn_probs(hh, q_ref, kp_ref, kc_ref, vp_ref, vc_ref, bias_ref, sink_ref, hp, n)
            o = lax.dot_general(probs.astype(BF16), vv, TN_DIMS, preferred_element_type=F32)
            for s_, slab in enumerate(_unstack(o, BF16)):
                c0 = hh * (GROUP * HEAD_DIM) + s_ * LANES
                o_ref[:, c0:c0 + LANES] = slab

        @pl.when(step == HP * nb - 1)
        def _():
            run(2)

    qspec = pl.BlockSpec((BLOCK, Q_PER_HP), lambda hp, n: (n, hp))
    any_spec = pl.BlockSpec(memory_space=pl.ANY)

    def kv(base, prev):
        if prev:
            return pl.BlockSpec((BLOCK, LANES), lambda hp, n: (jnp.maximum(n - 1, 0), base + hp))
        return pl.BlockSpec((BLOCK, LANES), lambda hp, n: (n, base + hp))

    return pl.pallas_call(
        body, name="attention_forward", grid=(HP, nb),
        in_specs=[qspec, kv(kb, True), kv(kb, False), kv(vb, True), kv(vb, False),
                  pl.BlockSpec((H_PER_HP, 2 * BLOCK, BLOCK), lambda hp, n: (hp, 0, 0)),
                  pl.BlockSpec(memory_space=pltpu.SMEM)] + [any_spec] * nc,
        out_specs=(qspec,) + (any_spec,) * nc,
        out_shape=(jax.ShapeDtypeStruct((s, D_ATTN), BF16),) + tuple(_exchange_shapes(comm)),
        scratch_shapes=_exchange_sems(nc), compiler_params=_cparams(2),
    )(proj, proj, proj, proj, proj, bias, sinks, *[arr for arr, _ in comm])


def attention_backward(proj, datt, bias, sinks, comm):
    s = proj.shape[0]
    nb = s // BLOCK
    kb, vb = K0 // LANES, V0 // LANES
    nc = len(comm)
    scat = [s_ for _, s_ in comm]

    def body(*refs):
        q_ref, kp_ref, kc_ref, vp_ref, vc_ref, do_ref, bias_ref, sink_ref = refs[:8]
        srcs = refs[8:8 + nc]
        dq_ref, dk_ref, dv_ref, dbias_ref, dsink_ref = refs[8 + nc:13 + nc]
        dsts = refs[13 + nc:13 + 2 * nc]
        kcar_ref, vcar_ref, sacc_ref = refs[13 + 2 * nc:16 + 2 * nc]
        sems = refs[16 + 2 * nc:]
        hp, n = pl.program_id(0), pl.program_id(1)

        @pl.when((hp == 0) & (n == 0))
        def _():
            for cp in _exchange_copies(srcs, dsts, scat, *sems):
                cp.start()

        @pl.when(n == 0)
        def _():
            kcar_ref[...] = jnp.zeros_like(kcar_ref)
            vcar_ref[...] = jnp.zeros_like(vcar_ref)
            dbias_ref[...] = jnp.zeros_like(dbias_ref)
            sacc_ref[...] = jnp.zeros_like(sacc_ref)

        @pl.when(n < nb)
        def _():
            lane2 = lax.broadcasted_iota(jnp.int32, (2 * BLOCK, LANES), 1)
            dk_band = jnp.zeros((2 * BLOCK, LANES), F32)
            dv_band = jnp.zeros((2 * BLOCK, LANES), F32)
            for hh in range(2):
                qs, kk, vv, probs, psink = _attn_probs(hh, q_ref, kp_ref, kc_ref, vp_ref, vc_ref, bias_ref, sink_ref, hp, n)
                dos = _stack_heads(do_ref, hh)
                dp = lax.dot_general(vv, dos, NT_DIMS, preferred_element_type=F32)
                dsum = jnp.sum(probs * dp, axis=0, keepdims=True)
                ds = probs * (dp - dsum)
                for g in range(GROUP):
                    dbias_ref[hh * GROUP + g] += ds[:, g * BLOCK:(g + 1) * BLOCK]
                sacc_ref[hh:hh + 1, :] += -psink * dsum
                dsb = (ds * (HEAD_DIM ** -0.5)).astype(BF16)
                pb = probs.astype(BF16)
                dq = lax.dot_general(dsb, kk, TN_DIMS, preferred_element_type=F32)
                for s_, slab in enumerate(_unstack(dq, BF16)):
                    c0 = hh * (GROUP * HEAD_DIM) + s_ * LANES
                    dq_ref[:, c0:c0 + LANES] = slab
                dkh = jnp.dot(dsb, qs, preferred_element_type=F32)
                dvh = jnp.dot(pb, dos, preferred_element_type=F32)
                own = (lane2 >= HEAD_DIM) if hh == 1 else (lane2 < HEAD_DIM)
                dk_band = dk_band + jnp.where(own, dkh + pltpu.roll(dkh, HEAD_DIM, axis=1), 0.0)
                dv_band = dv_band + jnp.where(own, dvh + pltpu.roll(dvh, HEAD_DIM, axis=1), 0.0)
            dk_ref[...] = (kcar_ref[...] + dk_band[:BLOCK]).astype(BF16)
            dv_ref[...] = (vcar_ref[...] + dv_band[:BLOCK]).astype(BF16)
            kcar_ref[...] = dk_band[BLOCK:]
            vcar_ref[...] = dv_band[BLOCK:]

        @pl.when(n == nb)
        def _():
            dk_ref[...] = kcar_ref[...].astype(BF16)
            dv_ref[...] = vcar_ref[...].astype(BF16)
            rows = [jnp.full((1, LANES), jnp.sum(sacc_ref[hh:hh + 1, g * BLOCK:(g + 1) * BLOCK]), F32)
                    for hh in range(2) for g in range(GROUP)]
            dsink_ref[...] = jnp.concatenate(rows, axis=0)

        @pl.when((hp == HP - 1) & (n == nb))
        def _():
            for cp in _exchange_copies(srcs, dsts, scat, *sems):
                cp.wait()

    qspec = pl.BlockSpec((BLOCK, Q_PER_HP), lambda hp, n: (jnp.minimum(n, nb - 1), hp))
    any_spec = pl.BlockSpec(memory_space=pl.ANY)

    def kv(base, prev):
        if prev:
            return pl.BlockSpec((BLOCK, LANES), lambda hp, n: (jnp.maximum(jnp.minimum(n, nb - 1) - 1, 0), base + hp))
        return pl.BlockSpec((BLOCK, LANES), lambda hp, n: (jnp.minimum(n, nb - 1), base + hp))

    dkv_spec = pl.BlockSpec((BLOCK, LANES), lambda hp, n: (jnp.maximum(n - 1, 0), hp))
    return pl.pallas_call(
        body, name="attention_backward", grid=(HP, nb + 1),
        in_specs=[qspec, kv(kb, True), kv(kb, False), kv(vb, True), kv(vb, False), qspec,
                  pl.BlockSpec((H_PER_HP, 2 * BLOCK, BLOCK), lambda hp, n: (hp, 0, 0)),
                  pl.BlockSpec(memory_space=pltpu.SMEM)] + [any_spec] * nc,
        out_specs=(qspec, dkv_spec, dkv_spec,
                   pl.BlockSpec((H_PER_HP, 2 * BLOCK, BLOCK), lambda hp, n: (hp, 0, 0)),
                   pl.BlockSpec((H_PER_HP, LANES), lambda hp, n: (hp, 0))) + (any_spec,) * nc,
        out_shape=(jax.ShapeDtypeStruct((s, D_ATTN), BF16), jax.ShapeDtypeStruct((s, D_KV), BF16),
                   jax.ShapeDtypeStruct((s, D_KV), BF16), jax.ShapeDtypeStruct((N_HEADS, 2 * BLOCK, BLOCK), F32),
                   jax.ShapeDtypeStruct((N_HEADS, LANES), F32)) + tuple(_exchange_shapes(comm)),
        scratch_shapes=[pltpu.VMEM((BLOCK, LANES), F32), pltpu.VMEM((BLOCK, LANES), F32),
                        pltpu.VMEM((8, GROUP * BLOCK), F32)] + _exchange_sems(nc),
        compiler_params=_cparams(2),
    )(proj, proj, proj, proj, proj, datt, bias, sinks, *[arr for arr, _ in comm])


def _neg_expm1(x):
    series = -(x * (1.0 + x * (1.0 / 2 + x * (1.0 / 6 + x * (1.0 / 24 + x * (1.0 / 120 + x * (1.0 / 720)))))))
    return jnp.where(x > -0.25, series, 1.0 - jnp.exp(x))


def _softplus_neg(lam):
    u = jnp.exp(-jnp.abs(lam))
    w = 1.0 + u
    log1p = jnp.where(w == 1.0, u, jnp.log(w) * u / jnp.where(w == 1.0, 1.0, w - 1.0))
    sp = jnp.maximum(-lam, 0.0) + log1p
    return sp, -_sigmoid(-lam)


def _rnn_gates(x_ref, xp_ref, cw_ref, cb_ref, wa_ref, ba_ref, wi_ref, bi_ref, lam_ref, first, row0):
    xv = x_ref[...].astype(F32)
    pv = jnp.where(first, 0.0, xp_ref[...].astype(F32))
    xs = [_shift_down(pv, xv, RNN_CONV - 1 - k) for k in range(RNN_CONV)]
    xc = cb_ref[...]
    for k in range(RNN_CONV):
        xc = xc + cw_ref[k:k + 1, :] * xs[k]
    xcb = xc.astype(BF16)
    ra = _sigmoid(jnp.dot(xcb, wa_ref[...], preferred_element_type=F32) + ba_ref[...])
    ri = _sigmoid(jnp.dot(xcb, wi_ref[...], preferred_element_type=F32) + bi_ref[...])
    sp, dsp = _softplus_neg(lam_ref[...])
    la = (-RG_C) * ra * sp
    a = jnp.exp(la)
    t = row0 + lax.broadcasted_iota(jnp.int32, xv.shape, 0)
    start = t == 0
    mult = jnp.where(start, 1.0, jnp.sqrt(_neg_expm1(2.0 * la)))
    return xs, xc, xcb, ra, ri, sp, dsp, a, mult, start


def _rnn_specs(t_rows, s, rev):
    nt = s // t_rows
    r = t_rows // HALO
    xb, gb = XR0 // LANES, GR0 // LANES
    ti = (lambda i: nt - 1 - i) if rev else (lambda i: i)
    tile = lambda base: pl.BlockSpec((t_rows, LANES), lambda n, i: (ti(i), base + n))
    prev = lambda base: pl.BlockSpec((HALO, LANES), lambda n, i: (jnp.maximum(ti(i) * r - 1, 0), base + n))
    par = lambda rows: pl.BlockSpec((rows, LANES), lambda n, i: (0, n))
    mat = pl.BlockSpec((None, RNN_BLOCK, RNN_BLOCK), lambda n, i: (n, 0, 0))
    return nt, ti, tile, prev, par, mat, xb, gb


def rnn_forward(proj, cw, cb, wa, ba, wi, bi, lam):
    s = proj.shape[0]
    t_rows = _row_tile(s, 512)
    nt, ti, tile, prev, par, mat, xb, gb = _rnn_specs(t_rows, s, False)

    def body(x_ref, xp_ref, g_ref, cw_ref, cb_ref, wa_ref, ba_ref, wi_ref, bi_ref, lam_ref, z_ref, h_ref, car_ref):
        i = pl.program_id(1)
        first = i == 0
        _, xc, _, _, ri, _, _, a, mult, _ = _rnn_gates(x_ref, xp_ref, cw_ref, cb_ref, wa_ref, ba_ref, wi_ref, bi_ref,
                                                       lam_ref, first, i * t_rows)
        aa, bb = a, mult * ri * xc
        rows = lax.broadcasted_iota(jnp.int32, aa.shape, 0)
        d = 1
        while d < t_rows:
            keep = rows >= d
            a_s, b_s = pltpu.roll(aa, d, axis=0), pltpu.roll(bb, d, axis=0)
            bb = jnp.where(keep, aa * b_s + bb, bb)
            aa = jnp.where(keep, aa * a_s, aa)
            d *= 2
        carry = jnp.where(first, 0.0, car_ref[0:1, :])
        h = aa * carry + bb
        car_ref[...] = jnp.broadcast_to(h[t_rows - 1:t_rows, :], car_ref.shape)
        h_ref[...] = h.astype(BF16)
        z_ref[...] = (h * _gelu(g_ref[...].astype(F32))).astype(BF16)

    o = jax.ShapeDtypeStruct((s, D_RNN), BF16)
    out_tile = pl.BlockSpec((t_rows, LANES), lambda n, i: (i, n))
    return pl.pallas_call(
        body, name="rnn_forward", grid=(N_RNN_BLOCKS, nt),
        in_specs=[tile(xb), prev(xb), tile(gb), par(RNN_CONV), par(1), mat, par(1), mat, par(1), par(1)],
        out_specs=(out_tile, out_tile), out_shape=(o, o), scratch_shapes=[pltpu.VMEM((8, LANES), F32)],
        compiler_params=_cparams(2),
    )(proj, proj, proj, cw, cb, wa, ba, wi, bi, lam)


def rnn_backward(proj, h, dz, cw, cb, wa, ba, wi, bi, lam, comm):
    s = proj.shape[0]
    t_rows = _row_tile(s, 512)
    nt, ti, tile, prev, par, mat, xb, gb = _rnn_specs(t_rows, s, True)
    r = t_rows // HALO
    nc = len(comm)
    scat = [s_ for _, s_ in comm]

    def body(*refs):
        (x_ref, xp_ref, g_ref, h_ref, hp_ref, dz_ref, cw_ref, cb_ref, wa_ref, ba_ref, wi_ref, bi_ref,
         lam_ref) = refs[:13]
        srcs = refs[13:13 + nc]
        dx_ref, dg_ref, dwa_ref, dwi_ref, sm_ref = refs[13 + nc:18 + nc]
        dsts = refs[18 + nc:18 + 2 * nc]
        gcar_ref, xcar_ref = refs[18 + 2 * nc:20 + 2 * nc]
        sems = refs[20 + 2 * nc:]
        i = pl.program_id(1)
        step = pl.program_id(0) * nt + i

        def run(phase):
            for act in _exchange_steps(phase, srcs, dsts, scat, *sems):
                act()

        @pl.when(step == 0)
        def _():
            run(0)

        @pl.when(step == (RELAY_AT_NUM * N_RNN_BLOCKS * nt) // RELAY_AT_DEN)
        def _():
            run(1)

        it = nt - 1 - i
        first, last = it == 0, it == nt - 1
        xs, xc, xcb, ra, ri, sp, dsp, a, mult, start = _rnn_gates(
            x_ref, xp_ref, cw_ref, cb_ref, wa_ref, ba_ref, wi_ref, bi_ref, lam_ref, first, it * t_rows)
        hf = h_ref[...].astype(F32)
        hprev = _shift_down(jnp.where(first, 0.0, hp_ref[...].astype(F32)), hf, 1)
        ge, dge = _gelu_and_grad(g_ref[...].astype(F32))
        dz = dz_ref[...].astype(F32)
        dg_ref[...] = (dz * hf * dge).astype(BF16)
        rows = lax.broadcasted_iota(jnp.int32, hf.shape, 0)
        tail = rows == t_rows - 1
        carry = jnp.where(last, 0.0, gcar_ref[0:1, :])
        bb = dz * ge + jnp.where(tail, carry, 0.0)
        aa = jnp.where(tail, 0.0, pltpu.roll(a, t_rows - 1, axis=0))
        d = 1
        while d < t_rows:
            keep = rows < t_rows - d
            a_s, b_s = pltpu.roll(aa, t_rows - d, axis=0), pltpu.roll(bb, t_rows - d, axis=0)
            bb = jnp.where(keep, bb + aa * b_s, bb)
            aa = jnp.where(keep, aa * a_s, aa)
            d *= 2
        gg = bb
        gcar_ref[...] = jnp.broadcast_to(a[0:1, :] * gg[0:1, :], gcar_ref.shape)
        da = gg * hprev
        dmult = jnp.where(start, 0.0, gg * ri * xc)
        dri = gg * mult * xc
        dxc = gg * mult * ri
        safe_mult = jnp.where(start, 1.0, mult)
        dla = da * a - dmult * (a * a) / safe_mult
        dra = dla * ((-RG_C) * sp)
        dlam = jnp.sum(dla * ((-RG_C) * ra), axis=0, keepdims=True) * dsp
        dpa = dra * ra * (1.0 - ra)
        dpi = dri * ri * (1.0 - ri)
        dpab, dpib = dpa.astype(BF16), dpi.astype(BF16)
        nt_dims = (((1,), (1,)), ((), ()))
        tn_dims = (((0,), (0,)), ((), ()))
        dxc = dxc + lax.dot_general(dpab, wa_ref[...], nt_dims, preferred_element_type=F32) \
            + lax.dot_general(dpib, wi_ref[...], nt_dims, preferred_element_type=F32)
        dwa = lax.dot_general(xcb, dpab, tn_dims, preferred_element_type=F32)
        dwi = lax.dot_general(xcb, dpib, tn_dims, preferred_element_type=F32)
        nxt = jnp.where(last, 0.0, xcar_ref[...])
        dx = cw_ref[RNN_CONV - 1:RNN_CONV, :] * dxc
        for k in range(RNN_CONV - 1):
            dx = dx + cw_ref[k:k + 1, :] * _shift_up(dxc, nxt, RNN_CONV - 1 - k)
        dx_ref[...] = dx.astype(BF16)
        xcar_ref[...] = dxc[0:HALO, :]
        small = jnp.concatenate(
            [jnp.sum(dpa, axis=0, keepdims=True), jnp.sum(dpi, axis=0, keepdims=True), dlam,
             jnp.sum(dxc, axis=0, keepdims=True)]
            + [jnp.sum(dxc * xs[k], axis=0, keepdims=True) for k in range(RNN_CONV)], axis=0)

        @pl.when(i == 0)
        def _():
            dwa_ref[...] = dwa
            dwi_ref[...] = dwi
            sm_ref[...] = small

        @pl.when(i > 0)
        def _():
            dwa_ref[...] += dwa
            dwi_ref[...] += dwi
            sm_ref[...] += small

        @pl.when(step == N_RNN_BLOCKS * nt - 1)
        def _():
            run(2)

    o = jax.ShapeDtypeStruct((s, D_RNN), BF16)
    any_spec = pl.BlockSpec(memory_space=pl.ANY)
    plain = pl.BlockSpec((t_rows, LANES), lambda n, i: (ti(i), n))
    plain_prev = pl.BlockSpec((HALO, LANES), lambda n, i: (jnp.maximum(ti(i) * r - 1, 0), n))
    return pl.pallas_call(
        body, name="rnn_backward", grid=(N_RNN_BLOCKS, nt),
        in_specs=[tile(xb), prev(xb), tile(gb), plain, plain_prev, plain,
                  par(RNN_CONV), par(1), mat, par(1), mat, par(1), par(1)] + [any_spec] * nc,
        out_specs=(plain, plain, mat, mat, pl.BlockSpec((None, 8, LANES), lambda n, i: (n, 0, 0))) + (any_spec,) * nc,
        out_shape=(o, o, jax.ShapeDtypeStruct((N_RNN_BLOCKS, RNN_BLOCK, RNN_BLOCK), F32),
                   jax.ShapeDtypeStruct((N_RNN_BLOCKS, RNN_BLOCK, RNN_BLOCK), F32),
                   jax.ShapeDtypeStruct((N_RNN_BLOCKS, 8, LANES), F32)) + tuple(_exchange_shapes(comm)),
        scratch_shapes=[pltpu.VMEM((8, LANES), F32), pltpu.VMEM((HALO, LANES), F32)] + _exchange_sems(nc),
        compiler_params=_cparams(2),
    )(proj, proj, proj, h, h, dz, cw, cb, wa, ba, wi, bi, lam, *[arr for arr, _ in comm])


def adamw(parts, w, m, v, name, rows=256):
    p, r, c = parts.shape
    tr = min(rows, r)
    assert r % tr == 0

    def body(p_ref, w_ref, m_ref, v_ref, g_ref, d_ref, nm_ref, nv_ref):
        g = p_ref[0].astype(F32)
        for q in range(1, p):
            g = g + p_ref[q].astype(F32)
        nm = ADAM_B1 * m_ref[...] + (1.0 - ADAM_B1) * g
        nv = ADAM_B2 * v_ref[...] + (1.0 - ADAM_B2) * (g * g)
        mh = nm / (1.0 - ADAM_B1 ** ADAM_STEP)
        vh = nv / (1.0 - ADAM_B2 ** ADAM_STEP)
        g_ref[...] = g
        d_ref[...] = (-ADAM_LR) * (mh / (jnp.sqrt(vh) + ADAM_EPS) + ADAM_WD * w_ref[...])
        nm_ref[...] = nm
        nv_ref[...] = nv

    pspec = pl.BlockSpec((p, tr, c), lambda i: (0, i, 0))
    spec = pl.BlockSpec((tr, c), lambda i: (i, 0))
    o = jax.ShapeDtypeStruct((r, c), F32)
    return pl.pallas_call(body, name=name, grid=(r // tr,), in_specs=[pspec, spec, spec, spec],
                          out_specs=(spec,) * 4, out_shape=(o, o, o, o), compiler_params=_cparams(1))(parts, w, m, v)


def ada_weight_grad(c_t, dmod):
    d, nb = c_t.shape
    c = dmod.shape[1]
    tr = 512

    def body(c_ref, dm_ref, o_ref):
        cv = c_ref[...]
        cs = cv * _sigmoid(cv)
        acc = cs[:, 0:1] * dm_ref[0:1, :]
        for b in range(1, nb):
            acc = acc + cs[:, b:b + 1] * dm_ref[b:b + 1, :]
        o_ref[...] = acc

    return pl.pallas_call(body, name="ada_weight_grad", grid=(d // tr,),
                          in_specs=[pl.BlockSpec((tr, nb), lambda i: (i, 0)), pl.BlockSpec((nb, c), lambda i: (0, 0))],
                          out_specs=pl.BlockSpec((tr, c), lambda i: (i, 0)),
                          out_shape=jax.ShapeDtypeStruct((d, c), F32), compiler_params=_cparams(1))(c_t, dmod)


def _rows128(a):
    flat = a.reshape(-1).astype(F32)
    pad = (-flat.shape[0]) % (8 * LANES)
    if pad:
        flat = jnp.concatenate([flat, jnp.zeros((pad,), F32)])
    return flat.reshape(-1, LANES)


def kernel(x, c, w_ada, b_ada, norm1, w_in, rnn_conv_w, rnn_conv_b, w_rg_a, b_rg_a, w_rg_i, b_rg_i, rg_lambda, w_o_rnn, w_o_attn, attn_sinks, rel_bias, w_out, norm2, w_up, ffn_conv_w, ffn_conv_b, w_down, norm_f, loss_target, m_w_ada, m_b_ada, m_norm1, m_w_in, m_rnn_conv_w, m_rnn_conv_b, m_w_rg_a, m_b_rg_a, m_w_rg_i, m_b_rg_i, m_rg_lambda, m_w_o_rnn, m_w_o_attn, m_attn_sinks, m_rel_bias, m_w_out, m_norm2, m_w_up, m_ffn_conv_w, m_ffn_conv_b, m_w_down, m_norm_f, v_w_ada, v_b_ada, v_norm1, v_w_in, v_rnn_conv_w, v_rnn_conv_b, v_w_rg_a, v_b_rg_a, v_w_rg_i, v_b_rg_i, v_rg_lambda, v_w_o_rnn, v_w_o_attn, v_attn_sinks, v_rel_bias, v_w_out, v_norm2, v_w_up, v_ffn_conv_w, v_ffn_conv_b, v_w_down, v_norm_f):
    me = 4 * lax.axis_index("x") + 2 * lax.axis_index("y") + lax.axis_index("c")
    xs = x[0]
    tgt = loss_target[0]
    s, d = xs.shape
    bf = lambda w: w[0].astype(BF16)

    mod, c_all = mod_forward(c.reshape(1, 1, d), w_ada[0], b_ada)
    mod = mod.reshape(6, d)
    shift1, scale1, gate1, shift2, scale2, gate2 = [mod[i:i + 1] for i in range(6)]

    g_in, g_rcw, g_fcw = exchange([(bf(w_in), False), (rnn_conv_w[0], False), (ffn_conv_w[0], False)], "gather_w_in")
    w_in_f = jnp.transpose(g_in, (1, 0, 2)).reshape(d, D_IN)
    rcw = jnp.transpose(g_rcw, (1, 0, 2)).reshape(RNN_CONV, D_RNN)
    fcw = jnp.transpose(g_fcw, (1, 0, 2)).reshape(FFN_CONV, 2 * D_FF)
    wa_b, wi_b = bf(w_rg_a), bf(w_rg_i)
    sinks = attn_sinks[0]

    u = prenorm(xs, norm1, scale1, shift1, "prenorm1")
    proj, g_oa, g_or, g_out, g_down = matmul(
        u, w_in_f, "nn", BF16, "mm_in", 1024, 512, 2048,
        comm=[(bf(w_o_attn), False), (bf(w_o_rnn), False), (bf(w_out), False), (bf(w_down), False)])
    w_oa_f, w_or_f = g_oa.reshape(D_ATTN, d), g_or.reshape(D_RNN, d)
    w_out_f, w_down_f = g_out.reshape(d, d), g_down.reshape(D_FF, d)
    bias = band_bias(rel_bias)
    att, g_up = attention_forward(proj, bias, sinks, [(bf(w_up), False)])
    w_up_f = jnp.transpose(g_up, (1, 0, 2)).reshape(d, 2 * D_FF)
    z, hr = rnn_forward(proj, rcw, rnn_conv_b, wa_b, b_rg_a, wi_b, b_rg_i, rg_lambda)
    y_attn = matmul(att, w_oa_f, "nn", BF16, "mm_o_attn", 1024, 1024, 2048)
    y_rnn = matmul(z, w_or_f, "nn", BF16, "mm_o_rnn", 1024, 1024, 2560)
    merged = merge_forward(proj, y_attn, y_rnn)
    h1, mo = matmul(merged, w_out_f, "nn", BF16, "mm_out", 512, 1024, 2048, res=xs, gate=gate1)
    u2 = prenorm(h1, norm2, scale2, shift2, "prenorm2")
    upp = matmul(u2, w_up_f, "nn", BF16, "mm_up", 1024, 1024, 2048)
    act, up_g, up_v = ffn_act_forward(upp, fcw, ffn_conv_b)
    h2, dn = matmul(act, w_down_f, "nn", BF16, "mm_down", 512, 1024, 2048, res=h1, gate=gate2)

    dh2, d_dn, loss_cols, d_norm_f, d_gate2 = loss_head(h2, tgt, dn, norm_f.reshape(1, d), gate2)
    loss = lax.psum(jnp.sum(loss_cols), ("x", "y", "c"))

    d_act = matmul(d_dn, w_down_f, "nt", BF16, "mm_down_dx", 1024, 1536, 2048)
    g_w_down = matmul(act, d_dn, "tn", BF16, "mm_down_dw", 1536, 1024, 2048)
    d_g, d_v, d_fcb_g, d_fcb_v = ffn_act_backward(d_act, up_g, up_v)
    d_upp, d_fcw = conv_backward(d_g, d_v, upp, fcw)
    d_fcb = jnp.concatenate([d_fcb_g, d_fcb_v], axis=1)
    d_u2, p_down = matmul(d_upp, w_up_f, "nt", BF16, "mm_up_dx", 1024, 1024, 3072,
                          comm=[(g_w_down.reshape(N_DEV, D_FF // N_DEV, d), True)])
    g_w_up = matmul(u2, d_upp, "tn", BF16, "mm_up_dw", 1024, 1024, 4096)
    g_up_blk = jnp.transpose(g_w_up.reshape(d, N_DEV, 2 * D_FF // N_DEV), (1, 0, 2))
    dh1, d_shift2, d_scale2, d_norm2, d_mo, d_gate1 = norm_backward(d_u2, h1, dh2, norm2, scale2, "norm2_backward",
                                                                    mo=mo, gate=gate1)
    d_merged = matmul(d_mo, w_out_f, "nt", BF16, "mm_out_dx", 1024, 1024, 2048)
    g_w_out = matmul(merged, d_mo, "tn", BF16, "mm_out_dw", 2048, 1024, 2048)
    d_ga, d_gl, d_ya, d_yr = merge_backward(d_merged, proj, y_attn, y_rnn)
    d_att = matmul(d_ya, w_oa_f, "nt", BF16, "mm_o_attn_dx", 1024, 1024, 2048)
    g_w_oa = matmul(att, d_ya, "tn", BF16, "mm_o_attn_dw", 2048, 1024, 2048)
    d_z = matmul(d_yr, w_or_f, "nt", BF16, "mm_o_rnn_dx", 1024, 1280, 2048)
    g_w_or = matmul(z, d_yr, "tn", BF16, "mm_o_rnn_dw", 1280, 1024, 2048)
    d_xr, d_gr, d_wa, d_wi, d_rsmall, p_up = rnn_backward(proj, hr, d_z, rcw, rnn_conv_b, wa_b, b_rg_a, wi_b, b_rg_i,
                                                          rg_lambda, [(g_up_blk, True)])
    d_q, d_k, d_v_, d_bias, d_sink, p_out, p_oa, p_or = attention_backward(
        proj, d_att, bias, sinks,
        [(g_w_out.reshape(N_DEV, d // N_DEV, d), True), (g_w_oa.reshape(N_DEV, D_ATTN // N_DEV, d), True),
         (g_w_or.reshape(N_DEV, D_RNN // N_DEV, d), True)])
    d_rel = rel_bias_grad(d_bias)
    d_proj = jnp.concatenate([d_q, d_k, d_v_, d_xr, d_gr, d_ga, d_gl], axis=1)

    def rsmall_of(ba_, bi_, lam_, cb_):
        return jnp.stack([ba_[0].reshape(N_RNN_BLOCKS, LANES), bi_[0].reshape(N_RNN_BLOCKS, LANES),
                          lam_[0].reshape(N_RNN_BLOCKS, LANES), cb_[0].reshape(N_RNN_BLOCKS, LANES)]
                         + [jnp.zeros((N_RNN_BLOCKS, LANES), F32)] * 4, axis=1)

    pack = lambda t: jnp.concatenate([_rows128(q) for q in t], axis=0)
    late_n = 2
    g_early = [d_norm2, d_norm_f, d_rsmall, d_wa, d_wi, d_sink[:, 0], d_rel[:, :N_HEADS], d_fcb, d_fcw]

    g_w_in, small_early = matmul(u, d_proj, "tn", BF16, "mm_in_dw", 2048, 512, 2048, comm=[(pack(g_early), False)])
    g_in_blk = jnp.transpose(g_w_in.reshape(d, N_DEV, D_IN // N_DEV), (1, 0, 2))
    d_u, p_in = matmul(d_proj, w_in_f, "nt", BF16, "mm_in_dx", 1024, 1024, 2944, comm=[(g_in_blk, True)])
    grad_x, d_shift1, d_scale1, d_norm1 = norm_backward(d_u, xs, dh1, norm1, scale1, "norm1_backward")
    d_mod = jnp.concatenate([d_shift1, d_scale1, d_gate1, d_shift2, d_scale2, d_gate2], axis=1)
    small_late = exchange([(pack([d_mod, d_norm1]), False)], "gather_late_grads")[0]

    g_list = [d_mod, d_norm1] + g_early
    w_list = [b_ada, norm1, norm2, norm_f, rsmall_of(b_rg_a, b_rg_i, rg_lambda, rnn_conv_b), w_rg_a, w_rg_i,
              attn_sinks, rel_bias, ffn_conv_b, jnp.zeros_like(d_fcw)]
    m_list = [m_b_ada, m_norm1, m_norm2, m_norm_f, rsmall_of(m_b_rg_a, m_b_rg_i, m_rg_lambda, m_rnn_conv_b), m_w_rg_a,
              m_w_rg_i, m_attn_sinks, m_rel_bias, m_ffn_conv_b, jnp.zeros_like(d_fcw)]
    v_list = [v_b_ada, v_norm1, v_norm2, v_norm_f, rsmall_of(v_b_rg_a, v_b_rg_i, v_rg_lambda, v_rnn_conv_b), v_w_rg_a,
              v_w_rg_i, v_attn_sinks, v_rel_bias, v_ffn_conv_b, jnp.ones_like(d_fcw)]
    sizes = [_rows128(q).shape[0] for q in g_list]
    offs = np.concatenate([[0], np.cumsum(sizes)]).tolist()
    r_late = offs[late_n]
    late = adamw(small_late, pack(w_list[:late_n]), pack(m_list[:late_n]), pack(v_list[:late_n]), "adamw_small_late",
                 rows=r_late)
    early = adamw(small_early, pack(w_list[late_n:]), pack(m_list[late_n:]), pack(v_list[late_n:]), "adamw_small_early",
                  rows=(offs[-1] - r_late) // 7)

    def seg(packed, idx, like):
        n_el = int(np.prod(like.shape))
        return packed[offs[idx]:offs[idx + 1]].reshape(-1)[:n_el].reshape(like.shape)

    def unpack(kind):
        packed = jnp.concatenate([late[kind], early[kind]], axis=0)
        rs = seg(packed, 4, d_rsmall)
        out = dict(
            b_ada=seg(packed, 0, b_ada), norm1=seg(packed, 1, norm1), norm2=seg(packed, 2, norm2),
            norm_f=seg(packed, 3, norm_f), b_rg_a=rs[:, 0].reshape(1, D_RNN), b_rg_i=rs[:, 1].reshape(1, D_RNN),
            rg_lambda=rs[:, 2].reshape(1, D_RNN), rnn_conv_b=rs[:, 3].reshape(1, D_RNN),
            w_rg_a=seg(packed, 5, w_rg_a), w_rg_i=seg(packed, 6, w_rg_i), attn_sinks=seg(packed, 7, attn_sinks),
            rel_bias=seg(packed, 8, rel_bias), ffn_conv_b=seg(packed, 9, ffn_conv_b))
        out["rnn_conv_w_full"] = jnp.transpose(rs[:, 4:8], (1, 0, 2)).reshape(RNN_CONV, D_RNN)
        out["ffn_conv_w_full"] = seg(packed, 10, d_fcw)
        return out

    small = [unpack(kind) for kind in range(4)]

    rcw_cols = D_RNN // N_DEV
    fcw_cols = 2 * D_FF // N_DEV
    g_rcw_ = lax.dynamic_slice(small[0]["rnn_conv_w_full"], (0, me * rcw_cols), (RNN_CONV, rcw_cols))
    g_fcw_ = lax.dynamic_slice(small[0]["ffn_conv_w_full"], (0, me * fcw_cols), (FFN_CONV, fcw_cols))
    r_rcw = adamw(g_rcw_[None], rnn_conv_w[0], m_rnn_conv_w[0], v_rnn_conv_w[0], "adamw_rnn_conv_w")
    r_fcw = adamw(g_fcw_[None], ffn_conv_w[0], m_ffn_conv_w[0], v_ffn_conv_w[0], "adamw_ffn_conv_w")

    ada_cols = 6 * d // N_DEV
    dmod_all = small_late[:, offs[0]:offs[1]].reshape(N_DEV, 6 * d)
    dmod_cols = lax.dynamic_slice(dmod_all, (0, me * ada_cols), (N_DEV, ada_cols))
    g_ada = ada_weight_grad(jnp.transpose(c_all.reshape(N_DEV, d)), dmod_cols)
    r_ada = adamw(g_ada[None], w_ada[0], m_w_ada[0], v_w_ada[0], "adamw_w_ada")

    r_in = adamw(p_in, w_in[0], m_w_in[0], v_w_in[0], "adamw_w_in")
    r_up = adamw(p_up, w_up[0], m_w_up[0], v_w_up[0], "adamw_w_up")
    r_or = adamw(p_or, w_o_rnn[0], m_w_o_rnn[0], v_w_o_rnn[0], "adamw_w_o_rnn", rows=160)
    r_oa = adamw(p_oa, w_o_attn[0], m_w_o_attn[0], v_w_o_attn[0], "adamw_w_o_attn")
    r_out = adamw(p_out, w_out[0], m_w_out[0], v_w_out[0], "adamw_w_out")
    r_down = adamw(p_down, w_down[0], m_w_down[0], v_w_down[0], "adamw_w_down")

    def res(kind):
        sm = small[kind]
        return [r_ada[kind][None], sm["b_ada"], sm["norm1"], r_in[kind][None], r_rcw[kind][None], sm["rnn_conv_b"],
                sm["w_rg_a"], sm["b_rg_a"], sm["w_rg_i"], sm["b_rg_i"], sm["rg_lambda"], r_or[kind][None],
                r_oa[kind][None], sm["attn_sinks"], sm["rel_bias"], r_out[kind][None], sm["norm2"], r_up[kind][None],
                r_fcw[kind][None], sm["ffn_conv_b"], r_down[kind][None], sm["norm_f"]]

    return (loss, grad_x[None], *res(0), *res(1), *res(2), *res(3))
```

```python
import functools
import math

import numpy as np
import jax
import jax.numpy as jnp
from jax import lax
from jax.experimental import pallas as pl
from jax.experimental.pallas import tpu as pltpu

F32, BF16 = jnp.float32, jnp.bfloat16

N_DEV = 8
D_MODEL = 2048
N_HEADS, HEAD_DIM, N_KV = 32, 64, 4
GROUP = N_HEADS // N_KV
D_ATTN, D_KV = N_HEADS * HEAD_DIM, N_KV * HEAD_DIM
BLOCK = 128
NUM_BUCKETS, MAX_DISTANCE = 32, 128
D_RNN, N_RNN_BLOCKS, RNN_BLOCK = 2560, 20, 128
RNN_CONV, FFN_CONV = 4, 3
RG_C = 8.0
D_FF = 3 * D_MODEL
D_IN = D_ATTN + 2 * D_KV + 2 * D_RNN + 2 * D_MODEL
EPS = 1e-6
NEG_INF = -1e30
ADAM_LR, ADAM_B1, ADAM_B2, ADAM_EPS, ADAM_WD, ADAM_STEP = 0.001, 0.9, 0.999, 1e-08, 0.01, 10

LANES = 128
HALO = 16
VMEM_LIMIT = 56 * 1024 * 1024
MESH = pl.DeviceIdType.MESH
RELAY_AT_NUM, RELAY_AT_DEN = 2, 3

Q0, K0, V0, XR0, GR0, GA0, GL0 = 0, 2048, 2304, 2560, 5120, 7680, 9728


def _cparams(n_axes):
    return pltpu.CompilerParams(dimension_semantics=("arbitrary",) * n_axes, vmem_limit_bytes=VMEM_LIMIT)


def _gelu(x):
    k = math.sqrt(2.0 / math.pi)
    return 0.5 * x * (1.0 + jnp.tanh(k * (x + 0.044715 * x * x * x)))


def _gelu_and_grad(x):
    k = math.sqrt(2.0 / math.pi)
    t = jnp.tanh(k * (x + 0.044715 * x * x * x))
    g = 0.5 * x * (1.0 + t)
    dg = 0.5 * (1.0 + t) + 0.5 * x * (1.0 - t * t) * k * (1.0 + 3.0 * 0.044715 * x * x)
    return g, dg


def _sigmoid(x):
    return 1.0 / (1.0 + jnp.exp(-x))


def _shift_down(prev, x, j):
    if j == 0:
        return x
    xe = jnp.concatenate([prev, x], axis=0)
    return pltpu.roll(xe, j, axis=0)[HALO:, :]


def _shift_up(x, nxt, j):
    if j == 0:
        return x
    xe = jnp.concatenate([x, nxt], axis=0)
    n = xe.shape[0]
    return pltpu.roll(xe, n - j, axis=0)[: x.shape[0], :]


def _my_coords():
    return lax.axis_index("x"), lax.axis_index("y"), lax.axis_index("c")


def _peer(x, y, c, k):
    kx, ky, kc = (k >> 2) & 1, (k >> 1) & 1, k & 1
    px, py, pc = (x + kx) % 2, (y + ky) % 2, (c + kc) % 2
    return (px, py, pc), 4 * px + 2 * py + pc


def _exchange_shapes(items):
    return [jax.ShapeDtypeStruct((N_DEV,) + tuple(arr.shape[1:] if s else arr.shape), arr.dtype) for arr, s in items]


def _exchange_sems(n):
    return [pltpu.SemaphoreType.DMA((n, N_DEV - 1)), pltpu.SemaphoreType.DMA((n, N_DEV - 1)),
            pltpu.SemaphoreType.DMA((n,))]


def _exchange_copies(srcs, dsts, scat, send_sems, recv_sems, loc_sems):
    x, y, c = _my_coords()
    me = 4 * x + 2 * y + c
    copies = []
    for a in range(len(srcs)):
        mine = srcs[a].at[me] if scat[a] else srcs[a]
        copies.append(pltpu.make_async_copy(mine, dsts[a].at[me], loc_sems.at[a]))
    for k in range(1, N_DEV):
        peer, p = _peer(x, y, c, k)
        for a in range(len(srcs)):
            src = srcs[a].at[p] if scat[a] else srcs[a]
            copies.append(pltpu.make_async_remote_copy(
                src_ref=src, dst_ref=dsts[a].at[me], send_sem=send_sems.at[a, k - 1],
                recv_sem=recv_sems.at[a, k - 1], device_id=peer, device_id_type=MESH))
    return copies


def _exchange_steps(phase, srcs, dsts, scat, send_sems, recv_sems, loc_sems):
    if any(scat):
        if phase == 1:
            return []
        copies = _exchange_copies(srcs, dsts, scat, send_sems, recv_sems, loc_sems)
        return [cp.start if phase == 0 else cp.wait for cp in copies]
    x, y, c = _my_coords()
    me = 4 * x + 2 * y + c
    sibling, _ = _peer(x, y, c, 1)

    def remote(a, src, dst, k, to):
        return pltpu.make_async_remote_copy(src_ref=src, dst_ref=dst, send_sem=send_sems.at[a, k - 1],
                                            recv_sem=recv_sems.at[a, k - 1], device_id=to, device_id_type=MESH)

    acts = []
    for a in range(len(srcs)):
        if phase != 1:
            loc = pltpu.make_async_copy(srcs[a], dsts[a].at[me], loc_sems.at[a])
            sib = remote(a, srcs[a], dsts[a].at[me], 1, sibling)
            acts += [loc.start, sib.start] if phase == 0 else [loc.wait, sib.wait]
        for k in (2, 4, 6):
            peer, p = _peer(x, y, c, k)
            out = remote(a, srcs[a], dsts[a].at[me], k, peer)
            if phase == 0:
                acts.append(out.start)
            else:
                onward = remote(a, dsts[a].at[p], dsts[a].at[p], k + 1, sibling)
                acts += [out.wait_recv, onward.start] if phase == 1 else [out.wait_send, onward.wait]
    return acts


def exchange(items, name):
    n = len(items)
    scat = [s for _, s in items]

    def body(*refs):
        for phase in range(3):
            for act in _exchange_steps(phase, refs[:n], refs[n:2 * n], scat, *refs[2 * n:]):
                act()

    any_spec = pl.BlockSpec(memory_space=pl.ANY)
    return pl.pallas_call(
        body, name=name, out_shape=tuple(_exchange_shapes(items)),
        in_specs=[any_spec] * n, out_specs=tuple([any_spec] * n), scratch_shapes=_exchange_sems(n),
    )(*[a for a, _ in items])


def mod_forward(c, w_ada, b_ada):
    d, ncol = w_ada.shape

    def body(c_ref, w_ref, b_ref, mod_ref, call_ref, cols_ref, s1, r1, s2, r2):
        x, y, c_ = _my_coords()
        me = 4 * x + 2 * y + c_
        call_ref[me] = c_ref[0]
        sends = []
        for k in range(1, N_DEV):
            peer, p = _peer(x, y, c_, k)
            cp = pltpu.make_async_remote_copy(src_ref=c_ref.at[0], dst_ref=call_ref.at[me], send_sem=s1.at[k - 1],
                                              recv_sem=r1.at[k - 1], device_id=peer, device_id_type=MESH)
            cp.start()
            sends.append(cp)
        for cp in sends:
            cp.wait()
        rows = lax.broadcasted_iota(jnp.int32, (N_DEV, d), 0)
        cmat = jnp.zeros((N_DEV, d), F32)
        for b in range(N_DEV):
            cmat = jnp.where(rows == b, call_ref[b], cmat)
        cs = cmat * _sigmoid(cmat)
        bias = b_ref[:, pl.ds(pl.multiple_of(me * ncol, LANES), ncol)]
        mc = jnp.dot(cs, w_ref[...], preferred_element_type=F32, precision=lax.Precision.HIGHEST) + bias
        for b in range(N_DEV):
            cols_ref[b] = mc[b:b + 1, :]
        mod_ref[me] = cols_ref[me]
        sends = []
        for k in range(1, N_DEV):
            peer, p = _peer(x, y, c_, k)
            cp = pltpu.make_async_remote_copy(src_ref=cols_ref.at[p], dst_ref=mod_ref.at[me], send_sem=s2.at[k - 1],
                                              recv_sem=r2.at[k - 1], device_id=peer, device_id_type=MESH)
            cp.start()
            sends.append(cp)
        for cp in sends:
            cp.wait()

    vm = pl.BlockSpec(memory_space=pltpu.VMEM)
    return pl.pallas_call(
        body, name="mod_forward",
        out_shape=(jax.ShapeDtypeStruct((N_DEV, 1, ncol), F32), jax.ShapeDtypeStruct((N_DEV, 1, d), F32)),
        in_specs=[vm, vm, vm], out_specs=(vm, vm),
        scratch_shapes=[pltpu.VMEM((N_DEV, 1, ncol), F32)] + [pltpu.SemaphoreType.DMA((N_DEV - 1,))] * 4,
        compiler_params=pltpu.CompilerParams(vmem_limit_bytes=VMEM_LIMIT),
    )(c, w_ada, b_ada)


def matmul(a, b, mode, out_dtype, name, tm, tn, tk, res=None, gate=None, comm=None):
    if mode == "nn":
        (m, kk), (_, n) = a.shape, b.shape
    elif mode == "nt":
        (m, kk), (n, _) = a.shape, b.shape
    else:
        (kk, m), (_, n) = a.shape, b.shape
    tm, tn, tk = min(tm, m), min(tn, n), min(tk, kk)
    assert m % tm == 0 and n % tn == 0 and kk % tk == 0, (name, m, n, kk, tm, tn, tk)
    if mode == "nn":
        a_spec = pl.BlockSpec((tm, tk), lambda j, i, k: (i, k))
        b_spec = pl.BlockSpec((tk, tn), lambda j, i, k: (k, j))
        dims = (((1,), (0,)), ((), ()))
    elif mode == "nt":
        a_spec = pl.BlockSpec((tm, tk), lambda j, i, k: (i, k))
        b_spec = pl.BlockSpec((tn, tk), lambda j, i, k: (j, k))
        dims = (((1,), (1,)), ((), ()))
    else:
        a_spec = pl.BlockSpec((tk, tm), lambda j, i, k: (k, i))
        b_spec = pl.BlockSpec((tk, tn), lambda j, i, k: (k, j))
        dims = (((0,), (0,)), ((), ()))
    nj, ni, nk = n // tn, m // tm, kk // tk
    fused = res is not None
    items = list(comm or [])
    nc = len(items)
    scat = [s_ for _, s_ in items]
    n_in = (4 if fused else 2) + nc
    n_out = (2 if fused else 1) + nc
    relay_step = (RELAY_AT_NUM * nj * ni * nk) // RELAY_AT_DEN
    o_spec = pl.BlockSpec((tm, tn), lambda j, i, k: (i, j))

    def body(*refs):
        ins, outs, scratch = refs[:n_in], refs[n_in:n_in + n_out], refs[n_in + n_out:]
        a_ref, b_ref = ins[:2]
        o_ref = outs[0]
        acc_ref = scratch[0] if nk > 1 else None
        j, i, k = pl.program_id(0), pl.program_id(1), pl.program_id(2)

        def run(phase):
            for act in _exchange_steps(phase, ins[n_in - nc:], outs[n_out - nc:], scat, *scratch[len(scratch) - 3:]):
                act()

        step = (j * ni + i) * nk + k
        if nc:
            @pl.when(step == 0)
            def _():
                run(0)

            @pl.when(step == relay_step)
            def _():
                run(1)

        def finish(acc):
            if fused:
                o_ref[...] = ins[2][...] + ins[3][...] * acc
                outs[1][...] = acc.astype(outs[1].dtype)
            else:
                o_ref[...] = acc.astype(o_ref.dtype)

        prod = lax.dot_general(a_ref[...], b_ref[...], dims, preferred_element_type=F32)
        if nk == 1:
            finish(prod)
        else:
            @pl.when(k == 0)
            def _():
                acc_ref[...] = prod

            @pl.when(k > 0)
            def _():
                acc_ref[...] += prod

            @pl.when(k == nk - 1)
            def _():
                finish(acc_ref[...])

        if nc:
            @pl.when(step == nj * ni * nk - 1)
            def _():
                run(2)

    any_spec = pl.BlockSpec(memory_space=pl.ANY)
    in_specs, args = [a_spec, b_spec], [a, b]
    if fused:
        in_specs += [o_spec, pl.BlockSpec((1, tn), lambda j, i, k: (0, j))]
        args += [res, gate]
        out_shape = [jax.ShapeDtypeStruct((m, n), F32), jax.ShapeDtypeStruct((m, n), out_dtype)]
        out_specs = [o_spec, o_spec]
    else:
        out_shape = [jax.ShapeDtypeStruct((m, n), out_dtype)]
        out_specs = [o_spec]
    in_specs += [any_spec] * nc
    args += [arr for arr, _ in items]
    out_shape += _exchange_shapes(items)
    out_specs += [any_spec] * nc
    scratch = ([pltpu.VMEM((tm, tn), F32)] if nk > 1 else []) + (_exchange_sems(nc) if nc else [])
    outs = pl.pallas_call(
        body, name=name, grid=(nj, ni, nk), in_specs=in_specs, out_specs=tuple(out_specs), out_shape=tuple(out_shape),
        scratch_shapes=scratch, compiler_params=_cparams(3),
    )(*args)
    return outs[0] if len(outs) == 1 else outs


def _row_tile(s, want):
    t = min(want, s)
    assert s % t == 0 and t % HALO == 0
    return t


def prenorm(x, nw, scale, shift, name):
    s, d = x.shape
    tm = _row_tile(s, 512)

    def body(x_ref, nw_ref, sc_ref, sh_ref, o_ref):
        xv = x_ref[...]
        r = lax.rsqrt(jnp.mean(xv * xv, axis=-1, keepdims=True) + EPS)
        o_ref[...] = ((xv * r) * nw_ref[...] * (1.0 + sc_ref[...]) + sh_ref[...]).astype(BF16)

    row = pl.BlockSpec((tm, d), lambda i: (i, 0))
    vec = pl.BlockSpec((1, d), lambda i: (0, 0))
    return pl.pallas_call(body, name=name, grid=(s // tm,), in_specs=[row, vec, vec, vec], out_specs=row,
                          out_shape=jax.ShapeDtypeStruct((s, d), BF16), compiler_params=_cparams(1))(x, nw, scale, shift)


def norm_backward(du, xin, dres, nw, scale, name, mo=None, gate=None):
    s, d = xin.shape
    tm = _row_tile(s, 256)
    gated = mo is not None

    def body(*refs):
        if gated:
            du_ref, x_ref, dr_ref, nw_ref, sc_ref, mo_ref, g_ref, dx_ref, dsh_ref, dsc_ref, dnw_ref, dmo_ref, dg_ref = refs
        else:
            du_ref, x_ref, dr_ref, nw_ref, sc_ref, dx_ref, dsh_ref, dsc_ref, dnw_ref = refs
        i = pl.program_id(0)
        xv = x_ref[...]
        r = lax.rsqrt(jnp.mean(xv * xv, axis=-1, keepdims=True) + EPS)
        xn = xv * r
        duv = du_ref[...].astype(F32)
        nwv, scv = nw_ref[...], sc_ref[...]
        dxn = duv * (nwv * (1.0 + scv))
        dx = dr_ref[...] + r * (dxn - xn * jnp.mean(dxn * xn, axis=-1, keepdims=True))
        dx_ref[...] = dx
        sums = [jnp.sum(duv, axis=0, keepdims=True), jnp.sum(duv * xn * nwv, axis=0, keepdims=True),
                jnp.sum(duv * xn * (1.0 + scv), axis=0, keepdims=True)]
        accs = [dsh_ref, dsc_ref, dnw_ref]
        if gated:
            dmo_ref[...] = (dx * g_ref[...]).astype(BF16)
            sums.append(jnp.sum(dx * mo_ref[...].astype(F32), axis=0, keepdims=True))
            accs.append(dg_ref)

        @pl.when(i == 0)
        def _():
            for acc, sm in zip(accs, sums):
                acc[...] = sm

        @pl.when(i > 0)
        def _():
            for acc, sm in zip(accs, sums):
                acc[...] += sm

    row = pl.BlockSpec((tm, d), lambda i: (i, 0))
    vec = pl.BlockSpec((1, d), lambda i: (0, 0))
    vshape = jax.ShapeDtypeStruct((1, d), F32)
    in_specs, args = [row, row, row, vec, vec], [du, xin, dres, nw, scale]
    out_specs, out_shape = [row, vec, vec, vec], [jax.ShapeDtypeStruct((s, d), F32), vshape, vshape, vshape]
    if gated:
        in_specs += [row, vec]
        args += [mo, gate]
        out_specs += [row, vec]
        out_shape += [jax.ShapeDtypeStruct((s, d), BF16), vshape]
    return pl.pallas_call(body, name=name, grid=(s // tm,), in_specs=in_specs, out_specs=tuple(out_specs),
                          out_shape=tuple(out_shape), compiler_params=_cparams(1))(*args)


def merge_forward(proj, y_attn, y_rnn):
    s, d = y_attn.shape
    tm, cw = _row_tile(s, 1024), 512

    def body(ga_ref, gl_ref, ya_ref, yr_ref, o_ref):
        o_ref[...] = (_sigmoid(ga_ref[...].astype(F32)) * ya_ref[...].astype(F32)
                      + _sigmoid(gl_ref[...].astype(F32)) * yr_ref[...].astype(F32)).astype(BF16)

    def at(off):
        return pl.BlockSpec((tm, cw), lambda j, i: (i, off // cw + j))

    return pl.pallas_call(body, name="merge_forward", grid=(d // cw, s // tm),
                          in_specs=[at(GA0), at(GL0), at(0), at(0)], out_specs=at(0),
                          out_shape=jax.ShapeDtypeStruct((s, d), BF16), compiler_params=_cparams(2))(proj, proj, y_attn, y_rnn)


def merge_backward(dmerged, proj, y_attn, y_rnn):
    s, d = y_attn.shape
    tm, cw = _row_tile(s, 1024), 512

    def body(dm_ref, ga_ref, gl_ref, ya_ref, yr_ref, dga_ref, dgl_ref, dya_ref, dyr_ref):
        dm = dm_ref[...].astype(F32)
        sa, sl = _sigmoid(ga_ref[...].astype(F32)), _sigmoid(gl_ref[...].astype(F32))
        dga_ref[...] = (dm * ya_ref[...].astype(F32) * sa * (1.0 - sa)).astype(BF16)
        dgl_ref[...] = (dm * yr_ref[...].astype(F32) * sl * (1.0 - sl)).astype(BF16)
        dya_ref[...] = (dm * sa).astype(BF16)
        dyr_ref[...] = (dm * sl).astype(BF16)

    def at(off):
        return pl.BlockSpec((tm, cw), lambda j, i: (i, off // cw + j))

    o = jax.ShapeDtypeStruct((s, d), BF16)
    return pl.pallas_call(body, name="merge_backward", grid=(d // cw, s // tm),
                          in_specs=[at(0), at(GA0), at(GL0), at(0), at(0)], out_specs=(at(0),) * 4,
                          out_shape=(o, o, o, o), compiler_params=_cparams(2))(dmerged, proj, proj, y_attn, y_rnn)


def _prev_spec(tm, cw, off_blocks):
    r = tm // HALO
    return pl.BlockSpec((HALO, cw), lambda j, i: (jnp.maximum(i * r - 1, 0), off_blocks + j))


def ffn_act_forward(upp, cw_full, cb_full):
    s, f2 = upp.shape
    f = f2 // 2
    tm, cw = _row_tile(s, 512), 1536
    nj = f // cw

    def body(g_ref, gp_ref, v_ref, vp_ref, wg_ref, wv_ref, bg_ref, bv_ref, o_ref, og_ref, ov_ref):
        i = pl.program_id(1)
        first = i == 0

        def conv(x_ref, p_ref, w_ref, b_ref):
            xv = x_ref[...].astype(F32)
            pv = jnp.where(first, 0.0, p_ref[...].astype(F32))
            acc = b_ref[...] + w_ref[FFN_CONV - 1:FFN_CONV, :] * xv
            for k in range(FFN_CONV - 1):
                acc = acc + w_ref[k:k + 1, :] * _shift_down(pv, xv, FFN_CONV - 1 - k)
            return acc

        g = conv(g_ref, gp_ref, wg_ref, bg_ref)
        v = conv(v_ref, vp_ref, wv_ref, bv_ref)
        og_ref[...] = g.astype(BF16)
        ov_ref[...] = v.astype(BF16)
        o_ref[...] = (_gelu(g) * v).astype(BF16)

    def tile(ob):
        return pl.BlockSpec((tm, cw), lambda j, i: (i, ob + j))

    def par(rows, ob):
        return pl.BlockSpec((rows, cw), lambda j, i: (0, ob + j))

    o = jax.ShapeDtypeStruct((s, f), BF16)
    return pl.pallas_call(
        body, name="ffn_act_forward", grid=(nj, s // tm),
        in_specs=[tile(0), _prev_spec(tm, cw, 0), tile(nj), _prev_spec(tm, cw, nj),
                  par(FFN_CONV, 0), par(FFN_CONV, nj), par(1, 0), par(1, nj)],
        out_specs=(tile(0), tile(0), tile(0)), out_shape=(o, o, o), compiler_params=_cparams(2),
    )(upp, upp, upp, upp, cw_full, cw_full, cb_full, cb_full)


ROW_CHUNK = 16
LANE_CHUNK = 512


def ffn_act_backward(dact, up_g, up_v):
    s, f = dact.shape
    tm, cw = _row_tile(s, 512), 1536
    nr = tm // ROW_CHUNK

    def body(da_ref, g_ref, v_ref, dg_ref, dv_ref, sg_ref, sv_ref, acc_ref):
        i = pl.program_id(1)
        acc_ref[...] = jnp.zeros_like(acc_ref)

        def chunk(r, carry):
            rows = pl.ds(pl.multiple_of(r * ROW_CHUNK, ROW_CHUNK), ROW_CHUNK)
            for c0 in range(0, cw, LANE_CHUNK):
                cols = pl.ds(c0, LANE_CHUNK)
                da = da_ref[rows, cols].astype(F32)
                ge, dge = _gelu_and_grad(g_ref[rows, cols].astype(F32))
                dg = da * v_ref[rows, cols].astype(F32) * dge
                dv = da * ge
                dg_ref[rows, cols] = dg.astype(BF16)
                dv_ref[rows, cols] = dv.astype(BF16)
                acc_ref[0, :, cols] += dg[0:8] + dg[8:16]
                acc_ref[1, :, cols] += dv[0:8] + dv[8:16]
            return carry

        lax.fori_loop(0, nr, chunk, 0)
        sg = jnp.sum(acc_ref[0], axis=0, keepdims=True)
        sv = jnp.sum(acc_ref[1], axis=0, keepdims=True)

        @pl.when(i == 0)
        def _():
            sg_ref[...] = sg
            sv_ref[...] = sv

        @pl.when(i > 0)
        def _():
            sg_ref[...] += sg
            sv_ref[...] += sv

    tile = pl.BlockSpec((tm, cw), lambda j, i: (i, j))
    vec = pl.BlockSpec((1, cw), lambda j, i: (0, j))
    o = jax.ShapeDtypeStruct((s, f), BF16)
    v1 = jax.ShapeDtypeStruct((1, f), F32)
    return pl.pallas_call(
        body, name="ffn_act_backward", grid=(f // cw, s // tm), in_specs=[tile, tile, tile],
        out_specs=(tile, tile, vec, vec), out_shape=(o, o, v1, v1),
        scratch_shapes=[pltpu.VMEM((2, 8, cw), F32)], compiler_params=_cparams(2),
    )(dact, up_g, up_v)


def conv_backward(d_g, d_v, upp, cw_full):
    s, f = d_g.shape
    f2 = 2 * f
    tm, cw = _row_tile(s, 512), 1536
    nt, nj = s // tm, f // cw
    r_halo = tm // HALO
    nr = tm // ROW_CHUNK

    def body(xg_ref, ng_ref, xv_ref, nv_ref, u_ref, w_ref, o_ref, dw_ref, acc_ref):
        j, i = pl.program_id(0), pl.program_id(1)
        acc_ref[...] = jnp.zeros_like(acc_ref)

        def run(x_ref, n_ref):
            def chunk(r, carry):
                rows = pl.ds(pl.multiple_of(r * ROW_CHUNK, ROW_CHUNK), ROW_CHUNK)
                nrows = pl.ds(pl.multiple_of(jnp.minimum(r + 1, nr - 1) * ROW_CHUNK, ROW_CHUNK), ROW_CHUNK)
                for c0 in range(0, cw, LANE_CHUNK):
                    cols = pl.ds(c0, LANE_CHUNK)
                    cur = x_ref[rows, cols].astype(F32)
                    halo = jnp.where(i == nt - 1, 0.0, n_ref[:, cols].astype(F32))
                    nxt = jnp.where(r == nr - 1, halo, x_ref[nrows, cols].astype(F32))
                    uv = u_ref[rows, cols].astype(F32)
                    acc = None
                    for k in range(FFN_CONV):
                        sh = _shift_up(cur, nxt, FFN_CONV - 1 - k)
                        term = w_ref[k:k + 1, cols] * sh
                        acc = term if acc is None else acc + term
                        pr = uv * sh
                        acc_ref[k, :, cols] += pr[0:8] + pr[8:16]
                    o_ref[rows, cols] = acc.astype(BF16)
                return carry

            lax.fori_loop(0, nr, chunk, 0)

        @pl.when(j < nj)
        def _():
            run(xg_ref, ng_ref)

        @pl.when(j >= nj)
        def _():
            run(xv_ref, nv_ref)

        sums = jnp.concatenate([jnp.sum(acc_ref[k], axis=0, keepdims=True) for k in range(FFN_CONV)], axis=0)

        @pl.when(i == 0)
        def _():
            dw_ref[...] = sums

        @pl.when(i > 0)
        def _():
            dw_ref[...] += sums

    def tile_h(used):
        return pl.BlockSpec((tm, cw), lambda j, i: (jnp.where(used(j), i, 0), jnp.where(used(j), j % nj, 0)))

    def next_h(used):
        return pl.BlockSpec((HALO, cw), lambda j, i: (
            jnp.where(used(j), jnp.minimum((i + 1) * r_halo, s // HALO - 1), 0), jnp.where(used(j), j % nj, 0)))

    is_g = lambda j: j < nj
    is_v = lambda j: j >= nj
    tile = pl.BlockSpec((tm, cw), lambda j, i: (i, j))
    par = pl.BlockSpec((FFN_CONV, cw), lambda j, i: (0, j))
    return pl.pallas_call(
        body, name="ffn_conv_backward", grid=(2 * nj, nt),
        in_specs=[tile_h(is_g), next_h(is_g), tile_h(is_v), next_h(is_v), tile, par],
        out_specs=(tile, par), out_shape=(jax.ShapeDtypeStruct((s, f2), BF16), jax.ShapeDtypeStruct((FFN_CONV, f2), F32)),
        scratch_shapes=[pltpu.VMEM((FFN_CONV, 8, cw), F32)], compiler_params=_cparams(2),
    )(d_g, d_g, d_v, d_v, upp, cw_full)


def loss_head(h2, target, dn, norm_f, gate2):
    s, d = h2.shape
    tm = _row_tile(s, 256)

    def body(h_ref, t_ref, dn_ref, nf_ref, g_ref, dh_ref, ddn_ref, loss_ref, dnf_ref, dg_ref):
        i = pl.program_id(0)
        hv = h_ref[...]
        r = lax.rsqrt(jnp.mean(hv * hv, axis=-1, keepdims=True) + EPS)
        yh = hv * r
        nf = nf_ref[...]
        err = yh * nf - t_ref[...]
        dy = err * (1.0 / d)
        dyh = dy * nf
        dh = r * (dyh - yh * jnp.mean(dyh * yh, axis=-1, keepdims=True))
        dh_ref[...] = dh
        ddn_ref[...] = (dh * g_ref[...]).astype(BF16)
        sums = [jnp.sum(err * err, axis=0, keepdims=True) * (0.5 / d), jnp.sum(dy * yh, axis=0, keepdims=True),
                jnp.sum(dh * dn_ref[...].astype(F32), axis=0, keepdims=True)]
        accs = [loss_ref, dnf_ref, dg_ref]

        @pl.when(i == 0)
        def _():
            for acc, sm in zip(accs, sums):
                acc[...] = sm

        @pl.when(i > 0)
        def _():
            for acc, sm in zip(accs, sums):
                acc[...] += sm

    row = pl.BlockSpec((tm, d), lambda i: (i, 0))
    vec = pl.BlockSpec((1, d), lambda i: (0, 0))
    v = jax.ShapeDtypeStruct((1, d), F32)
    return pl.pallas_call(
        body, name="loss_head", grid=(s // tm,), in_specs=[row, row, row, vec, vec], out_specs=(row, row, vec, vec, vec),
        out_shape=(jax.ShapeDtypeStruct((s, d), F32), jax.ShapeDtypeStruct((s, d), BF16), v, v, v),
        compiler_params=_cparams(1))(h2, target, dn, norm_f, gate2)


def _t5_buckets():
    qi = np.arange(BLOCK)[:, None]
    kj = np.arange(2 * BLOCK)[None, :]
    dist = qi + BLOCK - kj
    dd = np.maximum(dist, 0)
    max_exact = NUM_BUCKETS // 2
    dflt = np.maximum(dd, 1).astype(np.float32)
    large = max_exact + (np.log(dflt / max_exact) / math.log(MAX_DISTANCE / max_exact)
                         * (NUM_BUCKETS - max_exact)).astype(np.int32)
    large = np.minimum(large, NUM_BUCKETS - 1)
    bucket = np.where(dd < max_exact, dd, large).astype(np.int32)
    in_window = (dist >= 0) & (dist < BLOCK)
    return bucket, in_window


def band_bias(rel_bias):
    bucket, in_window = _t5_buckets()
    bucket_t = jnp.asarray(np.where(in_window, bucket, -1).astype(np.int32).T)

    def body(rb_ref, bk_ref, o_ref):
        bk = bk_ref[...]
        for h in range(N_HEADS):
            acc = jnp.full((2 * BLOCK, BLOCK), NEG_INF, F32)
            for b in range(NUM_BUCKETS):
                acc = jnp.where(bk == b, rb_ref[b, h], acc)
            o_ref[h] = acc

    return pl.pallas_call(
        body, name="band_bias", out_shape=jax.ShapeDtypeStruct((N_HEADS, 2 * BLOCK, BLOCK), F32),
        in_specs=[pl.BlockSpec(memory_space=pltpu.SMEM), pl.BlockSpec(memory_space=pltpu.VMEM)],
        out_specs=pl.BlockSpec(memory_space=pltpu.VMEM))(rel_bias, bucket_t)


def rel_bias_grad(dbias):
    bucket, in_window = _t5_buckets()
    bucket_t = jnp.asarray(np.where(in_window, bucket, -1).astype(np.int32).T)

    def body(db_ref, bk_ref, o_ref):
        bk = bk_ref[...]
        rows = lax.broadcasted_iota(jnp.int32, (NUM_BUCKETS, LANES), 0)
        lanes = lax.broadcasted_iota(jnp.int32, (NUM_BUCKETS, LANES), 1)
        acc = jnp.zeros((NUM_BUCKETS, LANES), F32)
        for h in range(N_HEADS):
            dv = db_ref[h]
            for b in range(NUM_BUCKETS):
                sm = jnp.sum(jnp.where(bk == b, dv, 0.0))
                acc = jnp.where((rows == b) & (lanes == h), sm, acc)
        o_ref[...] = acc

    vm = pl.BlockSpec(memory_space=pltpu.VMEM)
    return pl.pallas_call(body, name="rel_bias_grad", out_shape=jax.ShapeDtypeStruct((NUM_BUCKETS, LANES), F32),
                          in_specs=[vm, vm], out_specs=vm)(dbias, bucket_t)


HP = 2
Q_PER_HP = D_ATTN // HP
H_PER_HP = N_HEADS // HP
NT_DIMS = (((1,), (1,)), ((), ()))
TN_DIMS = (((0,), (0,)), ((), ()))


def _stack_heads(ref, hh):
    lane = lax.broadcasted_iota(jnp.int32, (BLOCK, LANES), 1)
    lo = lane < HEAD_DIM
    parts = []
    for s_ in range(GROUP // 2):
        c0 = hh * (GROUP * HEAD_DIM) + s_ * LANES
        slab = ref[:, c0:c0 + LANES]
        parts.append(jnp.where(lo, slab, jnp.zeros_like(slab)))
        parts.append(jnp.where(lo, jnp.zeros_like(slab), slab))
    return jnp.concatenate(parts, axis=0)


def _attn_probs(hh, q_ref, kp_ref, kc_ref, vp_ref, vc_ref, bias_ref, sink_ref, hp, n):
    lane = lax.broadcasted_iota(jnp.int32, (2 * BLOCK, LANES), 1)
    own = (lane >= HEAD_DIM) if hh == 1 else (lane < HEAD_DIM)
    kband = jnp.concatenate([kp_ref[...], kc_ref[...]], axis=0)
    vband = jnp.concatenate([vp_ref[...], vc_ref[...]], axis=0)
    kk = jnp.where(own, kband, pltpu.roll(kband, HEAD_DIM, axis=1))
    vv = jnp.where(own, vband, pltpu.roll(vband, HEAD_DIM, axis=1))
    qs = _stack_heads(q_ref, hh)
    sc = lax.dot_general(kk, qs, NT_DIMS, preferred_element_type=F32) * (HEAD_DIM ** -0.5)
    sc = sc + jnp.concatenate([bias_ref[hh * GROUP + g] for g in range(GROUP)], axis=1)
    krow = lax.broadcasted_iota(jnp.int32, sc.shape, 0)
    sc = jnp.where((n == 0) & (krow < BLOCK), NEG_INF, sc)
    sink = jnp.concatenate([jnp.full((1, BLOCK), sink_ref[hp * H_PER_HP + hh * GROUP + g], F32) for g in range(GROUP)], axis=1)
    m = jnp.maximum(jnp.max(sc, axis=0, keepdims=True), sink)
    p = jnp.exp(sc - m)
    es = jnp.exp(sink - m)
    inv = 1.0 / (jnp.sum(p, axis=0, keepdims=True) + es)
    return qs, kk, vv, p * inv, es * inv


def _unstack(o, dtype):
    lane = lax.broadcasted_iota(jnp.int32, (BLOCK, LANES), 1)
    lo = lane < HEAD_DIM
    slabs = []
    for s_ in range(GROUP // 2):
        ev = o[(2 * s_) * BLOCK:(2 * s_ + 1) * BLOCK]
        od = o[(2 * s_ + 1) * BLOCK:(2 * s_ + 2) * BLOCK]
        slabs.append(jnp.where(lo, ev, od).astype(dtype))
    return slabs


def attention_forward(proj, bias, sinks, comm):
    s = proj.shape[0]
    nb = s // BLOCK
    kb, vb = K0 // LANES, V0 // LANES
    nc = len(comm)
    scat = [s_ for _, s_ in comm]

    def body(*refs):
        q_ref, kp_ref, kc_ref, vp_ref, vc_ref, bias_ref, sink_ref = refs[:7]
        srcs, o_ref, dsts, sems = refs[7:7 + nc], refs[7 + nc], refs[8 + nc:8 + 2 * nc], refs[8 + 2 * nc:]
        hp, n = pl.program_id(0), pl.program_id(1)

        step = hp * nb + n

        def run(phase):
            for act in _exchange_steps(phase, srcs, dsts, scat, *sems):
                act()

        @pl.when(step == 0)
        def _():
            run(0)

        @pl.when(step == HP * nb - HP * nb // 16)
        def _():
            run(1)

        for hh in range(2):
            qs, kk, vv, probs, _ = _attn_probs(hh, q_ref, kp_ref, kc_ref, vp_ref, vc_ref, bias_ref, sink_ref, hp, n)
            o = lax.dot_general(probs.astype(BF16), vv, TN_DIMS, preferred_element_type=F32)
            for s_, slab in enumerate(_unstack(o, BF16)):
                c0 = hh * (GROUP * HEAD_DIM) + s_ * LANES
                o_ref[:, c0:c0 + LANES] = slab

        @pl.when(step == HP * nb - 1)
        def _():
            run(2)

    qspec = pl.BlockSpec((BLOCK, Q_PER_HP), lambda hp, n: (n, hp))
    any_spec = pl.BlockSpec(memory_space=pl.ANY)

    def kv(base, prev):
        if prev:
            return pl.BlockSpec((BLOCK, LANES), lambda hp, n: (jnp.maximum(n - 1, 0), base + hp))
        return pl.BlockSpec((BLOCK, LANES), lambda hp, n: (n, base + hp))

    return pl.pallas_call(
        body, name="attention_forward", grid=(HP, nb),
        in_specs=[qspec, kv(kb, True), kv(kb, False), kv(vb, True), kv(vb, False),
                  pl.BlockSpec((H_PER_HP, 2 * BLOCK, BLOCK), lambda hp, n: (hp, 0, 0)),
                  pl.BlockSpec(memory_space=pltpu.SMEM)] + [any_spec] * nc,
        out_specs=(qspec,) + (any_spec,) * nc,
        out_shape=(jax.ShapeDtypeStruct((s, D_ATTN), BF16),) + tuple(_exchange_shapes(comm)),
        scratch_shapes=_exchange_sems(nc), compiler_params=_cparams(2),
    )(proj, proj, proj, proj, proj, bias, sinks, *[arr for arr, _ in comm])


def attention_backward(proj, datt, bias, sinks, comm):
    s = proj.shape[0]
    nb = s // BLOCK
    kb, vb = K0 // LANES, V0 // LANES
    nc = len(comm)
    scat = [s_ for _, s_ in comm]

    def body(*refs):
        q_ref, kp_ref, kc_ref, vp_ref, vc_ref, do_ref, bias_ref, sink_ref = refs[:8]
        srcs = refs[8:8 + nc]
        dq_ref, dk_ref, dv_ref, dbias_ref, dsink_ref = refs[8 + nc:13 + nc]
        dsts = refs[13 + nc:13 + 2 * nc]
        kcar_ref, vcar_ref, sacc_ref = refs[13 + 2 * nc:16 + 2 * nc]
        sems = refs[16 + 2 * nc:]
        hp, n = pl.program_id(0), pl.program_id(1)

        @pl.when((hp == 0) & (n == 0))
        def _():
            for cp in _exchange_copies(srcs, dsts, scat, *sems):
                cp.start()

        @pl.when(n == 0)
        def _():
            kcar_ref[...] = jnp.zeros_like(kcar_ref)
            vcar_ref[...] = jnp.zeros_like(vcar_ref)
            dbias_ref[...] = jnp.zeros_like(dbias_ref)
            sacc_ref[...] = jnp.zeros_like(sacc_ref)

        @pl.when(n < nb)
        def _():
            lane2 = lax.broadcasted_iota(jnp.int32, (2 * BLOCK, LANES), 1)
            dk_band = jnp.zeros((2 * BLOCK, LANES), F32)
            dv_band = jnp.zeros((2 * BLOCK, LANES), F32)
            for hh in range(2):
                qs, kk, vv, probs, psink = _attn_probs(hh, q_ref, kp_ref, kc_ref, vp_ref, vc_ref, bias_ref, sink_ref, hp, n)
                dos = _stack_heads(do_ref, hh)
                dp = lax.dot_general(vv, dos, NT_DIMS, preferred_element_type=F32)
                dsum = jnp.sum(probs * dp, axis=0, keepdims=True)
                ds = probs * (dp - dsum)
                for g in range(GROUP):
                    dbias_ref[hh * GROUP + g] += ds[:, g * BLOCK:(g + 1) * BLOCK]
                sacc_ref[hh:hh + 1, :] += -psink * dsum
                dsb = (ds * (HEAD_DIM ** -0.5)).astype(BF16)
                pb = probs.astype(BF16)
                dq = lax.dot_general(dsb, kk, TN_DIMS, preferred_element_type=F32)
                for s_, slab in enumerate(_unstack(dq, BF16)):
                    c0 = hh * (GROUP * HEAD_DIM) + s_ * LANES
                    dq_ref[:, c0:c0 + LANES] = slab
                dkh = jnp.dot(dsb, qs, preferred_element_type=F32)
                dvh = jnp.dot(pb, dos, preferred_element_type=F32)
                own = (lane2 >= HEAD_DIM) if hh == 1 else (lane2 < HEAD_DIM)
                dk_band = dk_band + jnp.where(own, dkh + pltpu.roll(dkh, HEAD_DIM, axis=1), 0.0)
                dv_band = dv_band + jnp.where(own, dvh + pltpu.roll(dvh, HEAD_DIM, axis=1), 0.0)
            dk_ref[...] = (kcar_ref[...] + dk_band[:BLOCK]).astype(BF16)
            dv_ref[...] = (vcar_ref[...] + dv_band[:BLOCK]).astype(BF16)
            kcar_ref[...] = dk_band[BLOCK:]
            vcar_ref[...] = dv_band[BLOCK:]

        @pl.when(n == nb)
        def _():
            dk_ref[...] = kcar_ref[...].astype(BF16)
            dv_ref[...] = vcar_ref[...].astype(BF16)
            rows = [jnp.full((1, LANES), jnp.sum(sacc_ref[hh:hh + 1, g * BLOCK:(g + 1) * BLOCK]), F32)
                    for hh in range(2) for g in range(GROUP)]
            dsink_ref[...] = jnp.concatenate(rows, axis=0)

        @pl.when((hp == HP - 1) & (n == nb))
        def _():
            for cp in _exchange_copies(srcs, dsts, scat, *sems):
                cp.wait()

    qspec = pl.BlockSpec((BLOCK, Q_PER_HP), lambda hp, n: (jnp.minimum(n, nb - 1), hp))
    any_spec = pl.BlockSpec(memory_space=pl.ANY)

    def kv(base, prev):
        if prev:
            return pl.BlockSpec((BLOCK, LANES), lambda hp, n: (jnp.maximum(jnp.minimum(n, nb - 1) - 1, 0), base + hp))
        return pl.BlockSpec((BLOCK, LANES), lambda hp, n: (jnp.minimum(n, nb - 1), base + hp))

    dkv_spec = pl.BlockSpec((BLOCK, LANES), lambda hp, n: (jnp.maximum(n - 1, 0), hp))
    return pl.pallas_call(
        body, name="attention_backward", grid=(HP, nb + 1),
        in_specs=[qspec, kv(kb, True), kv(kb, False), kv(vb, True), kv(vb, False), qspec,
                  pl.BlockSpec((H_PER_HP, 2 * BLOCK, BLOCK), lambda hp, n: (hp, 0, 0)),
                  pl.BlockSpec(memory_space=pltpu.SMEM)] + [any_spec] * nc,
        out_specs=(qspec, dkv_spec, dkv_spec,
                   pl.BlockSpec((H_PER_HP, 2 * BLOCK, BLOCK), lambda hp, n: (hp, 0, 0)),
                   pl.BlockSpec((H_PER_HP, LANES), lambda hp, n: (hp, 0))) + (any_spec,) * nc,
        out_shape=(jax.ShapeDtypeStruct((s, D_ATTN), BF16), jax.ShapeDtypeStruct((s, D_KV), BF16),
                   jax.ShapeDtypeStruct((s, D_KV), BF16), jax.ShapeDtypeStruct((N_HEADS, 2 * BLOCK, BLOCK), F32),
                   jax.ShapeDtypeStruct((N_HEADS, LANES), F32)) + tuple(_exchange_shapes(comm)),
        scratch_shapes=[pltpu.VMEM((BLOCK, LANES), F32), pltpu.VMEM((BLOCK, LANES), F32),
                        pltpu.VMEM((8, GROUP * BLOCK), F32)] + _exchange_sems(nc),
        compiler_params=_cparams(2),
    )(proj, proj, proj, proj, proj, datt, bias, sinks, *[arr for arr, _ in comm])


def _neg_expm1(x):
    series = -(x * (1.0 + x * (1.0 / 2 + x * (1.0 / 6 + x * (1.0 / 24 + x * (1.0 / 120 + x * (1.0 / 720)))))))
    return jnp.where(x > -0.25, series, 1.0 - jnp.exp(x))


def _softplus_neg(lam):
    u = jnp.exp(-jnp.abs(lam))
    w = 1.0 + u
    log1p = jnp.where(w == 1.0, u, jnp.log(w) * u / jnp.where(w == 1.0, 1.0, w - 1.0))
    sp = jnp.maximum(-lam, 0.0) + log1p
    return sp, -_sigmoid(-lam)


def _rnn_gates(x_ref, xp_ref, cw_ref, cb_ref, wa_ref, ba_ref, wi_ref, bi_ref, lam_ref, first, row0):
    xv = x_ref[...].astype(F32)
    pv = jnp.where(first, 0.0, xp_ref[...].astype(F32))
    xs = [_shift_down(pv, xv, RNN_CONV - 1 - k) for k in range(RNN_CONV)]
    xc = cb_ref[...]
    for k in range(RNN_CONV):
        xc = xc + cw_ref[k:k + 1, :] * xs[k]
    xcb = xc.astype(BF16)
    ra = _sigmoid(jnp.dot(xcb, wa_ref[...], preferred_element_type=F32) + ba_ref[...])
    ri = _sigmoid(jnp.dot(xcb, wi_ref[...], preferred_element_type=F32) + bi_ref[...])
    sp, dsp = _softplus_neg(lam_ref[...])
    la = (-RG_C) * ra * sp
    a = jnp.exp(la)
    t = row0 + lax.broadcasted_iota(jnp.int32, xv.shape, 0)
    start = t == 0
    mult = jnp.where(start, 1.0, jnp.sqrt(_neg_expm1(2.0 * la)))
    return xs, xc, xcb, ra, ri, sp, dsp, a, mult, start


def _rnn_specs(t_rows, s, rev):
    nt = s // t_rows
    r = t_rows // HALO
    xb, gb = XR0 // LANES, GR0 // LANES
    ti = (lambda i: nt - 1 - i) if rev else (lambda i: i)
    tile = lambda base: pl.BlockSpec((t_rows, LANES), lambda n, i: (ti(i), base + n))
    prev = lambda base: pl.BlockSpec((HALO, LANES), lambda n, i: (jnp.maximum(ti(i) * r - 1, 0), base + n))
    par = lambda rows: pl.BlockSpec((rows, LANES), lambda n, i: (0, n))
    mat = pl.BlockSpec((None, RNN_BLOCK, RNN_BLOCK), lambda n, i: (n, 0, 0))
    return nt, ti, tile, prev, par, mat, xb, gb


def rnn_forward(proj, cw, cb, wa, ba, wi, bi, lam):
    s = proj.shape[0]
    t_rows = _row_tile(s, 512)
    nt, ti, tile, prev, par, mat, xb, gb = _rnn_specs(t_rows, s, False)

    def body(x_ref, xp_ref, g_ref, cw_ref, cb_ref, wa_ref, ba_ref, wi_ref, bi_ref, lam_ref, z_ref, h_ref, car_ref):
        i = pl.program_id(1)
        first = i == 0
        _, xc, _, _, ri, _, _, a, mult, _ = _rnn_gates(x_ref, xp_ref, cw_ref, cb_ref, wa_ref, ba_ref, wi_ref, bi_ref,
                                                       lam_ref, first, i * t_rows)
        aa, bb = a, mult * ri * xc
        rows = lax.broadcasted_iota(jnp.int32, aa.shape, 0)
        d = 1
        while d < t_rows:
            keep = rows >= d
            a_s, b_s = pltpu.roll(aa, d, axis=0), pltpu.roll(bb, d, axis=0)
            bb = jnp.where(keep, aa * b_s + bb, bb)
            aa = jnp.where(keep, aa * a_s, aa)
            d *= 2
        carry = jnp.where(first, 0.0, car_ref[0:1, :])
        h = aa * carry + bb
        car_ref[...] = jnp.broadcast_to(h[t_rows - 1:t_rows, :], car_ref.shape)
        h_ref[...] = h.astype(BF16)
        z_ref[...] = (h * _gelu(g_ref[...].astype(F32))).astype(BF16)

    o = jax.ShapeDtypeStruct((s, D_RNN), BF16)
    out_tile = pl.BlockSpec((t_rows, LANES), lambda n, i: (i, n))
    return pl.pallas_call(
        body, name="rnn_forward", grid=(N_RNN_BLOCKS, nt),
        in_specs=[tile(xb), prev(xb), tile(gb), par(RNN_CONV), par(1), mat, par(1), mat, par(1), par(1)],
        out_specs=(out_tile, out_tile), out_shape=(o, o), scratch_shapes=[pltpu.VMEM((8, LANES), F32)],
        compiler_params=_cparams(2),
    )(proj, proj, proj, cw, cb, wa, ba, wi, bi, lam)


def rnn_backward(proj, h, dz, cw, cb, wa, ba, wi, bi, lam, comm):
    s = proj.shape[0]
    t_rows = _row_tile(s, 512)
    nt, ti, tile, prev, par, mat, xb, gb = _rnn_specs(t_rows, s, True)
    r = t_rows // HALO
    nc = len(comm)
    scat = [s_ for _, s_ in comm]

    def body(*refs):
        (x_ref, xp_ref, g_ref, h_ref, hp_ref, dz_ref, cw_ref, cb_ref, wa_ref, ba_ref, wi_ref, bi_ref,
         lam_ref) = refs[:13]
        srcs = refs[13:13 + nc]
        dx_ref, dg_ref, dwa_ref, dwi_ref, sm_ref = refs[13 + nc:18 + nc]
        dsts = refs[18 + nc:18 + 2 * nc]
        gcar_ref, xcar_ref = refs[18 + 2 * nc:20 + 2 * nc]
        sems = refs[20 + 2 * nc:]
        i = pl.program_id(1)
        step = pl.program_id(0) * nt + i

        def run(phase):
            for act in _exchange_steps(phase, srcs, dsts, scat, *sems):
                act()

        @pl.when(step == 0)
        def _():
            run(0)

        @pl.when(step == (RELAY_AT_NUM * N_RNN_BLOCKS * nt) // RELAY_AT_DEN)
        def _():
            run(1)

        it = nt - 1 - i
        first, last = it == 0, it == nt - 1
        xs, xc, xcb, ra, ri, sp, dsp, a, mult, start = _rnn_gates(
            x_ref, xp_ref, cw_ref, cb_ref, wa_ref, ba_ref, wi_ref, bi_ref, lam_ref, first, it * t_rows)
        hf = h_ref[...].astype(F32)
        hprev = _shift_down(jnp.where(first, 0.0, hp_ref[...].astype(F32)), hf, 1)
        ge, dge = _gelu_and_grad(g_ref[...].astype(F32))
        dz = dz_ref[...].astype(F32)
        dg_ref[...] = (dz * hf * dge).astype(BF16)
        rows = lax.broadcasted_iota(jnp.int32, hf.shape, 0)
        tail = rows == t_rows - 1
        carry = jnp.where(last, 0.0, gcar_ref[0:1, :])
        bb = dz * ge + jnp.where(tail, carry, 0.0)
        aa = jnp.where(tail, 0.0, pltpu.roll(a, t_rows - 1, axis=0))
        d = 1
        while d < t_rows:
            keep = rows < t_rows - d
            a_s, b_s = pltpu.roll(aa, t_rows - d, axis=0), pltpu.roll(bb, t_rows - d, axis=0)
            bb = jnp.where(keep, bb + aa * b_s, bb)
            aa = jnp.where(keep, aa * a_s, aa)
            d *= 2
        gg = bb
        gcar_ref[...] = jnp.broadcast_to(a[0:1, :] * gg[0:1, :], gcar_ref.shape)
        da = gg * hprev
        dmult = jnp.where(start, 0.0, gg * ri * xc)
        dri = gg * mult * xc
        dxc = gg * mult * ri
        safe_mult = jnp.where(start, 1.0, mult)
        dla = da * a - dmult * (a * a) / safe_mult
        dra = dla * ((-RG_C) * sp)
        dlam = jnp.sum(dla * ((-RG_C) * ra), axis=0, keepdims=True) * dsp
        dpa = dra * ra * (1.0 - ra)
        dpi = dri * ri * (1.0 - ri)
        dpab, dpib = dpa.astype(BF16), dpi.astype(BF16)
        nt_dims = (((1,), (1,)), ((), ()))
        tn_dims = (((0,), (0,)), ((), ()))
        dxc = dxc + lax.dot_general(dpab, wa_ref[...], nt_dims, preferred_element_type=F32) \
            + lax.dot_general(dpib, wi_ref[...], nt_dims, preferred_element_type=F32)
        dwa = lax.dot_general(xcb, dpab, tn_dims, preferred_element_type=F32)
        dwi = lax.dot_general(xcb, dpib, tn_dims, preferred_element_type=F32)
        nxt = jnp.where(last, 0.0, xcar_ref[...])
        dx = cw_ref[RNN_CONV - 1:RNN_CONV, :] * dxc
        for k in range(RNN_CONV - 1):
            dx = dx + cw_ref[k:k + 1, :] * _shift_up(dxc, nxt, RNN_CONV - 1 - k)
        dx_ref[...] = dx.astype(BF16)
        xcar_ref[...] = dxc[0:HALO, :]
        small = jnp.concatenate(
            [jnp.sum(dpa, axis=0, keepdims=True), jnp.sum(dpi, axis=0, keepdims=True), dlam,
             jnp.sum(dxc, axis=0, keepdims=True)]
            + [jnp.sum(dxc * xs[k], axis=0, keepdims=True) for k in range(RNN_CONV)], axis=0)

        @pl.when(i == 0)
        def _():
            dwa_ref[...] = dwa
            dwi_ref[...] = dwi
            sm_ref[...] = small

        @pl.when(i > 0)
        def _():
            dwa_ref[...] += dwa
            dwi_ref[...] += dwi
            sm_ref[...] += small

        @pl.when(step == N_RNN_BLOCKS * nt - 1)
        def _():
            run(2)

    o = jax.ShapeDtypeStruct((s, D_RNN), BF16)
    any_spec = pl.BlockSpec(memory_space=pl.ANY)
    plain = pl.BlockSpec((t_rows, LANES), lambda n, i: (ti(i), n))
    plain_prev = pl.BlockSpec((HALO, LANES), lambda n, i: (jnp.maximum(ti(i) * r - 1, 0), n))
    return pl.pallas_call(
        body, name="rnn_backward", grid=(N_RNN_BLOCKS, nt),
        in_specs=[tile(xb), prev(xb), tile(gb), plain, plain_prev, plain,
                  par(RNN_CONV), par(1), mat, par(1), mat, par(1), par(1)] + [any_spec] * nc,
        out_specs=(plain, plain, mat, mat, pl.BlockSpec((None, 8, LANES), lambda n, i: (n, 0, 0))) + (any_spec,) * nc,
        out_shape=(o, o, jax.ShapeDtypeStruct((N_RNN_BLOCKS, RNN_BLOCK, RNN_BLOCK), F32),
                   jax.ShapeDtypeStruct((N_RNN_BLOCKS, RNN_BLOCK, RNN_BLOCK), F32),
                   jax.ShapeDtypeStruct((N_RNN_BLOCKS, 8, LANES), F32)) + tuple(_exchange_shapes(comm)),
        scratch_shapes=[pltpu.VMEM((8, LANES), F32), pltpu.VMEM((HALO, LANES), F32)] + _exchange_sems(nc),
        compiler_params=_cparams(2),
    )(proj, proj, proj, h, h, dz, cw, cb, wa, ba, wi, bi, lam, *[arr for arr, _ in comm])


def adamw(parts, w, m, v, name, rows=256):
    p, r, c = parts.shape
    tr = min(rows, r)
    assert r % tr == 0

    def body(p_ref, w_ref, m_ref, v_ref, g_ref, d_ref, nm_ref, nv_ref):
        g = p_ref[0].astype(F32)
        for q in range(1, p):
            g = g + p_ref[q].astype(F32)
        nm = ADAM_B1 * m_ref[...] + (1.0 - ADAM_B1) * g
        nv = ADAM_B2 * v_ref[...] + (1.0 - ADAM_B2) * (g * g)
        mh = nm / (1.0 - ADAM_B1 ** ADAM_STEP)
        vh = nv / (1.0 - ADAM_B2 ** ADAM_STEP)
        g_ref[...] = g
        d_ref[...] = (-ADAM_LR) * (mh / (jnp.sqrt(vh) + ADAM_EPS) + ADAM_WD * w_ref[...])
        nm_ref[...] = nm
        nv_ref[...] = nv

    pspec = pl.BlockSpec((p, tr, c), lambda i: (0, i, 0))
    spec = pl.BlockSpec((tr, c), lambda i: (i, 0))
    o = jax.ShapeDtypeStruct((r, c), F32)
    return pl.pallas_call(body, name=name, grid=(r // tr,), in_specs=[pspec, spec, spec, spec],
                          out_specs=(spec,) * 4, out_shape=(o, o, o, o), compiler_params=_cparams(1))(parts, w, m, v)


def ada_weight_grad(c_t, dmod):
    d, nb = c_t.shape
    c = dmod.shape[1]
    tr = 512

    def body(c_ref, dm_ref, o_ref):
        cv = c_ref[...]
        cs = cv * _sigmoid(cv)
        acc = cs[:, 0:1] * dm_ref[0:1, :]
        for b in range(1, nb):
            acc = acc + cs[:, b:b + 1] * dm_ref[b:b + 1, :]
        o_ref[...] = acc

    return pl.pallas_call(body, name="ada_weight_grad", grid=(d // tr,),
                          in_specs=[pl.BlockSpec((tr, nb), lambda i: (i, 0)), pl.BlockSpec((nb, c), lambda i: (0, 0))],
                          out_specs=pl.BlockSpec((tr, c), lambda i: (i, 0)),
                          out_shape=jax.ShapeDtypeStruct((d, c), F32), compiler_params=_cparams(1))(c_t, dmod)


def _rows128(a):
    flat = a.reshape(-1).astype(F32)
    pad = (-flat.shape[0]) % (8 * LANES)
    if pad:
        flat = jnp.concatenate([flat, jnp.zeros((pad,), F32)])
    return flat.reshape(-1, LANES)


def kernel(x, c, w_ada, b_ada, norm1, w_in, rnn_conv_w, rnn_conv_b, w_rg_a, b_rg_a, w_rg_i, b_rg_i, rg_lambda, w_o_rnn, w_o_attn, attn_sinks, rel_bias, w_out, norm2, w_up, ffn_conv_w, ffn_conv_b, w_down, norm_f, loss_target, m_w_ada, m_b_ada, m_norm1, m_w_in, m_rnn_conv_w, m_rnn_conv_b, m_w_rg_a, m_b_rg_a, m_w_rg_i, m_b_rg_i, m_rg_lambda, m_w_o_rnn, m_w_o_attn, m_attn_sinks, m_rel_bias, m_w_out, m_norm2, m_w_up, m_ffn_conv_w, m_ffn_conv_b, m_w_down, m_norm_f, v_w_ada, v_b_ada, v_norm1, v_w_in, v_rnn_conv_w, v_rnn_conv_b, v_w_rg_a, v_b_rg_a, v_w_rg_i, v_b_rg_i, v_rg_lambda, v_w_o_rnn, v_w_o_attn, v_attn_sinks, v_rel_bias, v_w_out, v_norm2, v_w_up, v_ffn_conv_w, v_ffn_conv_b, v_w_down, v_norm_f):
    me = 4 * lax.axis_index("x") + 2 * lax.axis_index("y") + lax.axis_index("c")
    xs = x[0]
    tgt = loss_target[0]
    s, d = xs.shape
    bf = lambda w: w[0].astype(BF16)

    mod, c_all = mod_forward(c.reshape(1, 1, d), w_ada[0], b_ada)
    mod = mod.reshape(6, d)
    shift1, scale1, gate1, shift2, scale2, gate2 = [mod[i:i + 1] for i in range(6)]

    g_in, g_rcw, g_fcw = exchange([(bf(w_in), False), (rnn_conv_w[0], False), (ffn_conv_w[0], False)], "gather_w_in")
    w_in_f = jnp.transpose(g_in, (1, 0, 2)).reshape(d, D_IN)
    rcw = jnp.transpose(g_rcw, (1, 0, 2)).reshape(RNN_CONV, D_RNN)
    fcw = jnp.transpose(g_fcw, (1, 0, 2)).reshape(FFN_CONV, 2 * D_FF)
    wa_b, wi_b = bf(w_rg_a), bf(w_rg_i)
    sinks = attn_sinks[0]

    u = prenorm(xs, norm1, scale1, shift1, "prenorm1")
    proj, g_oa, g_or, g_out, g_down = matmul(
        u, w_in_f, "nn", BF16, "mm_in", 2048, 512, 2048,
        comm=[(bf(w_o_attn), False), (bf(w_o_rnn), False), (bf(w_out), False), (bf(w_down), False)])
    w_oa_f, w_or_f = g_oa.reshape(D_ATTN, d), g_or.reshape(D_RNN, d)
    w_out_f, w_down_f = g_out.reshape(d, d), g_down.reshape(D_FF, d)
    bias = band_bias(rel_bias)
    att, g_up = attention_forward(proj, bias, sinks, [(bf(w_up), False)])
    w_up_f = jnp.transpose(g_up, (1, 0, 2)).reshape(d, 2 * D_FF)
    z, hr = rnn_forward(proj, rcw, rnn_conv_b, wa_b, b_rg_a, wi_b, b_rg_i, rg_lambda)
    y_attn = matmul(att, w_oa_f, "nn", BF16, "mm_o_attn", 1024, 1024, 2048)
    y_rnn = matmul(z, w_or_f, "nn", BF16, "mm_o_rnn", 1024, 1024, 2560)
    merged = merge_forward(proj, y_attn, y_rnn)
    h1, mo = matmul(merged, w_out_f, "nn", BF16, "mm_out", 512, 1024, 2048, res=xs, gate=gate1)
    u2 = prenorm(h1, norm2, scale2, shift2, "prenorm2")
    upp = matmul(u2, w_up_f, "nn", BF16, "mm_up", 1024, 1024, 2048)
    act, up_g, up_v = ffn_act_forward(upp, fcw, ffn_conv_b)
    h2, dn = matmul(act, w_down_f, "nn", BF16, "mm_down", 512, 1024, 6144, res=h1, gate=gate2)

    dh2, d_dn, loss_cols, d_norm_f, d_gate2 = loss_head(h2, tgt, dn, norm_f.reshape(1, d), gate2)
    loss = lax.psum(jnp.sum(loss_cols), ("x", "y", "c"))

    d_act = matmul(d_dn, w_down_f, "nt", BF16, "mm_down_dx", 1024, 1536, 2048)
    g_w_down = matmul(act, d_dn, "tn", BF16, "mm_down_dw", 1536, 1024, 2048)
    d_g, d_v, d_fcb_g, d_fcb_v = ffn_act_backward(d_act, up_g, up_v)
    d_upp, d_fcw = conv_backward(d_g, d_v, upp, fcw)
    d_fcb = jnp.concatenate([d_fcb_g, d_fcb_v], axis=1)
    d_u2, p_down = matmul(d_upp, w_up_f, "nt", BF16, "mm_up_dx", 1024, 1024, 3072,
                          comm=[(g_w_down.reshape(N_DEV, D_FF // N_DEV, d), True)])
    g_w_up = matmul(u2, d_upp, "tn", BF16, "mm_up_dw", 1024, 1024, 4096)
    g_up_blk = jnp.transpose(g_w_up.reshape(d, N_DEV, 2 * D_FF // N_DEV), (1, 0, 2))
    dh1, d_shift2, d_scale2, d_norm2, d_mo, d_gate1 = norm_backward(d_u2, h1, dh2, norm2, scale2, "norm2_backward",
                                                                    mo=mo, gate=gate1)
    d_merged = matmul(d_mo, w_out_f, "nt", BF16, "mm_out_dx", 1024, 1024, 2048)
    g_w_out = matmul(merged, d_mo, "tn", BF16, "mm_out_dw", 2048, 1024, 2048)
    d_ga, d_gl, d_ya, d_yr = merge_backward(d_merged, proj, y_attn, y_rnn)
    d_att = matmul(d_ya, w_oa_f, "nt", BF16, "mm_o_attn_dx", 1024, 1024, 2048)
    g_w_oa = matmul(att, d_ya, "tn", BF16, "mm_o_attn_dw", 2048, 1024, 2048)
    d_z = matmul(d_yr, w_or_f, "nt", BF16, "mm_o_rnn_dx", 1024, 1280, 2048)
    g_w_or = matmul(z, d_yr, "tn", BF16, "mm_o_rnn_dw", 1280, 1024, 2048)
    d_xr, d_gr, d_wa, d_wi, d_rsmall, p_up = rnn_backward(proj, hr, d_z, rcw, rnn_conv_b, wa_b, b_rg_a, wi_b, b_rg_i,
                                                          rg_lambda, [(g_up_blk, True)])
    d_q, d_k, d_v_, d_bias, d_sink, p_out, p_oa, p_or = attention_backward(
        proj, d_att, bias, sinks,
        [(g_w_out.reshape(N_DEV, d // N_DEV, d), True), (g_w_oa.reshape(N_DEV, D_ATTN // N_DEV, d), True),
         (g_w_or.reshape(N_DEV, D_RNN // N_DEV, d), True)])
    d_rel = rel_bias_grad(d_bias)
    d_proj = jnp.concatenate([d_q, d_k, d_v_, d_xr, d_gr, d_ga, d_gl], axis=1)

    def rsmall_of(ba_, bi_, lam_, cb_):
        return jnp.stack([ba_[0].reshape(N_RNN_BLOCKS, LANES), bi_[0].reshape(N_RNN_BLOCKS, LANES),
                          lam_[0].reshape(N_RNN_BLOCKS, LANES), cb_[0].reshape(N_RNN_BLOCKS, LANES)]
                         + [jnp.zeros((N_RNN_BLOCKS, LANES), F32)] * 4, axis=1)

    pack = lambda t: jnp.concatenate([_rows128(q) for q in t], axis=0)
    late_n = 2
    g_early = [d_norm2, d_norm_f, d_rsmall, d_wa, d_wi, d_sink[:, 0], d_rel[:, :N_HEADS], d_fcb, d_fcw]

    g_w_in, small_early = matmul(u, d_proj, "tn", BF16, "mm_in_dw", 1024, 512, 4096, comm=[(pack(g_early), False)])
    g_in_blk = jnp.transpose(g_w_in.reshape(d, N_DEV, D_IN // N_DEV), (1, 0, 2))
    d_u, p_in = matmul(d_proj, w_in_f, "nt", BF16, "mm_in_dx", 1024, 1024, 2944, comm=[(g_in_blk, True)])
    grad_x, d_shift1, d_scale1, d_norm1 = norm_backward(d_u, xs, dh1, norm1, scale1, "norm1_backward")
    d_mod = jnp.concatenate([d_shift1, d_scale1, d_gate1, d_shift2, d_scale2, d_gate2], axis=1)
    small_late = exchange([(pack([d_mod, d_norm1]), False)], "gather_late_grads")[0]

    g_list = [d_mod, d_norm1] + g_early
    w_list = [b_ada, norm1, norm2, norm_f, rsmall_of(b_rg_a, b_rg_i, rg_lambda, rnn_conv_b), w_rg_a, w_rg_i,
              attn_sinks, rel_bias, ffn_conv_b, jnp.zeros_like(d_fcw)]
    m_list = [m_b_ada, m_norm1, m_norm2, m_norm_f, rsmall_of(m_b_rg_a, m_b_rg_i, m_rg_lambda, m_rnn_conv_b), m_w_rg_a,
              m_w_rg_i, m_attn_sinks, m_rel_bias, m_ffn_conv_b, jnp.zeros_like(d_fcw)]
    v_list = [v_b_ada, v_norm1, v_norm2, v_norm_f, rsmall_of(v_b_rg_a, v_b_rg_i, v_rg_lambda, v_rnn_conv_b), v_w_rg_a,
              v_w_rg_i, v_attn_sinks, v_rel_bias, v_ffn_conv_b, jnp.ones_like(d_fcw)]
    sizes = [_rows128(q).shape[0] for q in g_list]
    offs = np.concatenate([[0], np.cumsum(sizes)]).tolist()
    r_late = offs[late_n]
    late = adamw(small_late, pack(w_list[:late_n]), pack(m_list[:late_n]), pack(v_list[:late_n]), "adamw_small_late",
                 rows=r_late)
    early = adamw(small_early, pack(w_list[late_n:]), pack(m_list[late_n:]), pack(v_list[late_n:]), "adamw_small_early",
                  rows=(offs[-1] - r_late) // 7)

    def seg(packed, idx, like):
        n_el = int(np.prod(like.shape))
        return packed[offs[idx]:offs[idx + 1]].reshape(-1)[:n_el].reshape(like.shape)

    def unpack(kind):
        packed = jnp.concatenate([late[kind], early[kind]], axis=0)
        rs = seg(packed, 4, d_rsmall)
        out = dict(
            b_ada=seg(packed, 0, b_ada), norm1=seg(packed, 1, norm1), norm2=seg(packed, 2, norm2),
            norm_f=seg(packed, 3, norm_f), b_rg_a=rs[:, 0].reshape(1, D_RNN), b_rg_i=rs[:, 1].reshape(1, D_RNN),
            rg_lambda=rs[:, 2].reshape(1, D_RNN), rnn_conv_b=rs[:, 3].reshape(1, D_RNN),
            w_rg_a=seg(packed, 5, w_rg_a), w_rg_i=seg(packed, 6, w_rg_i), attn_sinks=seg(packed, 7, attn_sinks),
            rel_bias=seg(packed, 8, rel_bias), ffn_conv_b=seg(packed, 9, ffn_conv_b))
        out["rnn_conv_w_full"] = jnp.transpose(rs[:, 4:8], (1, 0, 2)).reshape(RNN_CONV, D_RNN)
        out["ffn_conv_w_full"] = seg(packed, 10, d_fcw)
        return out

    small = [unpack(kind) for kind in range(4)]

    rcw_cols = D_RNN // N_DEV
    fcw_cols = 2 * D_FF // N_DEV
    g_rcw_ = lax.dynamic_slice(small[0]["rnn_conv_w_full"], (0, me * rcw_cols), (RNN_CONV, rcw_cols))
    g_fcw_ = lax.dynamic_slice(small[0]["ffn_conv_w_full"], (0, me * fcw_cols), (FFN_CONV, fcw_cols))
    r_rcw = adamw(g_rcw_[None], rnn_conv_w[0], m_rnn_conv_w[0], v_rnn_conv_w[0], "adamw_rnn_conv_w")
    r_fcw = adamw(g_fcw_[None], ffn_conv_w[0], m_ffn_conv_w[0], v_ffn_conv_w[0], "adamw_ffn_conv_w")

    ada_cols = 6 * d // N_DEV
    dmod_all = small_late[:, offs[0]:offs[1]].reshape(N_DEV, 6 * d)
    dmod_cols = lax.dynamic_slice(dmod_all, (0, me * ada_cols), (N_DEV, ada_cols))
    g_ada = ada_weight_grad(jnp.transpose(c_all.reshape(N_DEV, d)), dmod_cols)
    r_ada = adamw(g_ada[None], w_ada[0], m_w_ada[0], v_w_ada[0], "adamw_w_ada")

    r_in = adamw(p_in, w_in[0], m_w_in[0], v_w_in[0], "adamw_w_in")
    r_up = adamw(p_up, w_up[0], m_w_up[0], v_w_up[0], "adamw_w_up")
    r_or = adamw(p_or, w_o_rnn[0], m_w_o_rnn[0], v_w_o_rnn[0], "adamw_w_o_rnn", rows=160)
    r_oa = adamw(p_oa, w_o_attn[0], m_w_o_attn[0], v_w_o_attn[0], "adamw_w_o_attn")
    r_out = adamw(p_out, w_out[0], m_w_out[0], v_w_out[0], "adamw_w_out")
    r_down = adamw(p_down, w_down[0], m_w_down[0], v_w_down[0], "adamw_w_down")

    def res(kind):
        sm = small[kind]
        return [r_ada[kind][None], sm["b_ada"], sm["norm1"], r_in[kind][None], r_rcw[kind][None], sm["rnn_conv_b"],
                sm["w_rg_a"], sm["b_rg_a"], sm["w_rg_i"], sm["b_rg_i"], sm["rg_lambda"], r_or[kind][None],
                r_oa[kind][None], sm["attn_sinks"], sm["rel_bias"], r_out[kind][None], sm["norm2"], r_up[kind][None],
                r_fcw[kind][None], sm["ffn_conv_b"], r_down[kind][None], sm["norm_f"]]

    return (loss, grad_x[None], *res(0), *res(1), *res(2), *res(3))
```

```python
import functools
import math

import numpy as np
import jax
import jax.numpy as jnp
from jax import lax
from jax.experimental import pallas as pl
from jax.experimental.pallas import tpu as pltpu

F32, BF16 = jnp.float32, jnp.bfloat16

N_DEV = 8
D_MODEL = 2048
N_HEADS, HEAD_DIM, N_KV = 32, 64, 4
GROUP = N_HEADS // N_KV
D_ATTN, D_KV = N_HEADS * HEAD_DIM, N_KV * HEAD_DIM
BLOCK = 128
NUM_BUCKETS, MAX_DISTANCE = 32, 128
D_RNN, N_RNN_BLOCKS, RNN_BLOCK = 2560, 20, 128
RNN_CONV, FFN_CONV = 4, 3
RG_C = 8.0
D_FF = 3 * D_MODEL
D_IN = D_ATTN + 2 * D_KV + 2 * D_RNN + 2 * D_MODEL
EPS = 1e-6
NEG_INF = -1e30
ADAM_LR, ADAM_B1, ADAM_B2, ADAM_EPS, ADAM_WD, ADAM_STEP = 0.001, 0.9, 0.999, 1e-08, 0.01, 10

LANES = 128
HALO = 16
VMEM_LIMIT = 56 * 1024 * 1024
MESH = pl.DeviceIdType.MESH
RELAY_AT_NUM, RELAY_AT_DEN = 2, 3

Q0, K0, V0, XR0, GR0, GA0, GL0 = 0, 2048, 2304, 2560, 5120, 7680, 9728


def _cparams(n_axes):
    return pltpu.CompilerParams(dimension_semantics=("arbitrary",) * n_axes, vmem_limit_bytes=VMEM_LIMIT)


def _gelu(x):
    k = math.sqrt(2.0 / math.pi)
    return 0.5 * x * (1.0 + jnp.tanh(k * (x + 0.044715 * x * x * x)))


def _gelu_and_grad(x):
    k = math.sqrt(2.0 / math.pi)
    t = jnp.tanh(k * (x + 0.044715 * x * x * x))
    g = 0.5 * x * (1.0 + t)
    dg = 0.5 * (1.0 + t) + 0.5 * x * (1.0 - t * t) * k * (1.0 + 3.0 * 0.044715 * x * x)
    return g, dg


def _sigmoid(x):
    return 1.0 / (1.0 + jnp.exp(-x))


def _shift_down(prev, x, j):
    if j == 0:
        return x
    xe = jnp.concatenate([prev, x], axis=0)
    return pltpu.roll(xe, j, axis=0)[HALO:, :]


def _shift_up(x, nxt, j):
    if j == 0:
        return x
    xe = jnp.concatenate([x, nxt], axis=0)
    n = xe.shape[0]
    return pltpu.roll(xe, n - j, axis=0)[: x.shape[0], :]


def _my_coords():
    return lax.axis_index("x"), lax.axis_index("y"), lax.axis_index("c")


def _peer(x, y, c, k):
    kx, ky, kc = (k >> 2) & 1, (k >> 1) & 1, k & 1
    px, py, pc = (x + kx) % 2, (y + ky) % 2, (c + kc) % 2
    return (px, py, pc), 4 * px + 2 * py + pc


def _exchange_shapes(items):
    return [jax.ShapeDtypeStruct((N_DEV,) + tuple(arr.shape[1:] if s else arr.shape), arr.dtype) for arr, s in items]


def _exchange_sems(n):
    return [pltpu.SemaphoreType.DMA((n, N_DEV - 1)), pltpu.SemaphoreType.DMA((n, N_DEV - 1)),
            pltpu.SemaphoreType.DMA((n,))]


def _exchange_copies(srcs, dsts, scat, send_sems, recv_sems, loc_sems):
    x, y, c = _my_coords()
    me = 4 * x + 2 * y + c
    copies = []
    for a in range(len(srcs)):
        mine = srcs[a].at[me] if scat[a] else srcs[a]
        copies.append(pltpu.make_async_copy(mine, dsts[a].at[me], loc_sems.at[a]))
    for k in range(1, N_DEV):
        peer, p = _peer(x, y, c, k)
        for a in range(len(srcs)):
            src = srcs[a].at[p] if scat[a] else srcs[a]
            copies.append(pltpu.make_async_remote_copy(
                src_ref=src, dst_ref=dsts[a].at[me], send_sem=send_sems.at[a, k - 1],
                recv_sem=recv_sems.at[a, k - 1], device_id=peer, device_id_type=MESH))
    return copies


def _exchange_steps(phase, srcs, dsts, scat, send_sems, recv_sems, loc_sems):
    if any(scat):
        if phase == 1:
            return []
        copies = _exchange_copies(srcs, dsts, scat, send_sems, recv_sems, loc_sems)
        return [cp.start if phase == 0 else cp.wait for cp in copies]
    x, y, c = _my_coords()
    me = 4 * x + 2 * y + c
    sibling, _ = _peer(x, y, c, 1)

    def remote(a, src, dst, k, to):
        return pltpu.make_async_remote_copy(src_ref=src, dst_ref=dst, send_sem=send_sems.at[a, k - 1],
                                            recv_sem=recv_sems.at[a, k - 1], device_id=to, device_id_type=MESH)

    acts = []
    for a in range(len(srcs)):
        if phase != 1:
            loc = pltpu.make_async_copy(srcs[a], dsts[a].at[me], loc_sems.at[a])
            sib = remote(a, srcs[a], dsts[a].at[me], 1, sibling)
            acts += [loc.start, sib.start] if phase == 0 else [loc.wait, sib.wait]
        for k in (2, 4, 6):
            peer, p = _peer(x, y, c, k)
            out = remote(a, srcs[a], dsts[a].at[me], k, peer)
            if phase == 0:
                acts.append(out.start)
            else:
                onward = remote(a, dsts[a].at[p], dsts[a].at[p], k + 1, sibling)
                acts += [out.wait_recv, onward.start] if phase == 1 else [out.wait_send, onward.wait]
    return acts


def exchange(items, name):
    n = len(items)
    scat = [s for _, s in items]

    def body(*refs):
        for phase in range(3):
            for act in _exchange_steps(phase, refs[:n], refs[n:2 * n], scat, *refs[2 * n:]):
                act()

    any_spec = pl.BlockSpec(memory_space=pl.ANY)
    return pl.pallas_call(
        body, name=name, out_shape=tuple(_exchange_shapes(items)),
        in_specs=[any_spec] * n, out_specs=tuple([any_spec] * n), scratch_shapes=_exchange_sems(n),
    )(*[a for a, _ in items])


def mod_forward(c, w_ada, b_ada):
    d, ncol = w_ada.shape

    def body(c_ref, w_ref, b_ref, mod_ref, call_ref, cols_ref, s1, r1, s2, r2):
        x, y, c_ = _my_coords()
        me = 4 * x + 2 * y + c_
        call_ref[me] = c_ref[0]
        sends = []
        for k in range(1, N_DEV):
            peer, p = _peer(x, y, c_, k)
            cp = pltpu.make_async_remote_copy(src_ref=c_ref.at[0], dst_ref=call_ref.at[me], send_sem=s1.at[k - 1],
                                              recv_sem=r1.at[k - 1], device_id=peer, device_id_type=MESH)
            cp.start()
            sends.append(cp)
        for cp in sends:
            cp.wait()
        rows = lax.broadcasted_iota(jnp.int32, (N_DEV, d), 0)
        cmat = jnp.zeros((N_DEV, d), F32)
        for b in range(N_DEV):
            cmat = jnp.where(rows == b, call_ref[b], cmat)
        cs = cmat * _sigmoid(cmat)
        bias = b_ref[:, pl.ds(pl.multiple_of(me * ncol, LANES), ncol)]
        mc = jnp.dot(cs, w_ref[...], preferred_element_type=F32, precision=lax.Precision.HIGHEST) + bias
        for b in range(N_DEV):
            cols_ref[b] = mc[b:b + 1, :]
        mod_ref[me] = cols_ref[me]
        sends = []
        for k in range(1, N_DEV):
            peer, p = _peer(x, y, c_, k)
            cp = pltpu.make_async_remote_copy(src_ref=cols_ref.at[p], dst_ref=mod_ref.at[me], send_sem=s2.at[k - 1],
                                              recv_sem=r2.at[k - 1], device_id=peer, device_id_type=MESH)
            cp.start()
            sends.append(cp)
        for cp in sends:
            cp.wait()

    vm = pl.BlockSpec(memory_space=pltpu.VMEM)
    return pl.pallas_call(
        body, name="mod_forward",
        out_shape=(jax.ShapeDtypeStruct((N_DEV, 1, ncol), F32), jax.ShapeDtypeStruct((N_DEV, 1, d), F32)),
        in_specs=[vm, vm, vm], out_specs=(vm, vm),
        scratch_shapes=[pltpu.VMEM((N_DEV, 1, ncol), F32)] + [pltpu.SemaphoreType.DMA((N_DEV - 1,))] * 4,
        compiler_params=pltpu.CompilerParams(vmem_limit_bytes=VMEM_LIMIT),
    )(c, w_ada, b_ada)


def matmul(a, b, mode, out_dtype, name, tm, tn, tk, res=None, gate=None, comm=None):
    if mode == "nn":
        (m, kk), (_, n) = a.shape, b.shape
    elif mode == "nt":
        (m, kk), (n, _) = a.shape, b.shape
    else:
        (kk, m), (_, n) = a.shape, b.shape
    tm, tn, tk = min(tm, m), min(tn, n), min(tk, kk)
    assert m % tm == 0 and n % tn == 0 and kk % tk == 0, (name, m, n, kk, tm, tn, tk)
    if mode == "nn":
        a_spec = pl.BlockSpec((tm, tk), lambda j, i, k: (i, k))
        b_spec = pl.BlockSpec((tk, tn), lambda j, i, k: (k, j))
        dims = (((1,), (0,)), ((), ()))
    elif mode == "nt":
        a_spec = pl.BlockSpec((tm, tk), lambda j, i, k: (i, k))
        b_spec = pl.BlockSpec((tn, tk), lambda j, i, k: (j, k))
        dims = (((1,), (1,)), ((), ()))
    else:
        a_spec = pl.BlockSpec((tk, tm), lambda j, i, k: (k, i))
        b_spec = pl.BlockSpec((tk, tn), lambda j, i, k: (k, j))
        dims = (((0,), (0,)), ((), ()))
    nj, ni, nk = n // tn, m // tm, kk // tk
    fused = res is not None
    items = list(comm or [])
    nc = len(items)
    scat = [s_ for _, s_ in items]
    n_in = (4 if fused else 2) + nc
    n_out = (2 if fused else 1) + nc
    relay_step = (RELAY_AT_NUM * nj * ni * nk) // RELAY_AT_DEN
    o_spec = pl.BlockSpec((tm, tn), lambda j, i, k: (i, j))

    def body(*refs):
        ins, outs, scratch = refs[:n_in], refs[n_in:n_in + n_out], refs[n_in + n_out:]
        a_ref, b_ref = ins[:2]
        o_ref = outs[0]
        acc_ref = scratch[0] if nk > 1 else None
        j, i, k = pl.program_id(0), pl.program_id(1), pl.program_id(2)

        def run(phase):
            for act in _exchange_steps(phase, ins[n_in - nc:], outs[n_out - nc:], scat, *scratch[len(scratch) - 3:]):
                act()

        step = (j * ni + i) * nk + k
        if nc:
            @pl.when(step == 0)
            def _():
                run(0)

            @pl.when(step == relay_step)
            def _():
                run(1)

        def finish(acc):
            if fused:
                o_ref[...] = ins[2][...] + ins[3][...] * acc
                outs[1][...] = acc.astype(outs[1].dtype)
            else:
                o_ref[...] = acc.astype(o_ref.dtype)

        prod = lax.dot_general(a_ref[...], b_ref[...], dims, preferred_element_type=F32)
        if nk == 1:
            finish(prod)
        else:
            @pl.when(k == 0)
            def _():
                acc_ref[...] = prod

            @pl.when(k > 0)
            def _():
                acc_ref[...] += prod

            @pl.when(k == nk - 1)
            def _():
                finish(acc_ref[...])

        if nc:
            @pl.when(step == nj * ni * nk - 1)
            def _():
                run(2)

    any_spec = pl.BlockSpec(memory_space=pl.ANY)
    in_specs, args = [a_spec, b_spec], [a, b]
    if fused:
        in_specs += [o_spec, pl.BlockSpec((1, tn), lambda j, i, k: (0, j))]
        args += [res, gate]
        out_shape = [jax.ShapeDtypeStruct((m, n), F32), jax.ShapeDtypeStruct((m, n), out_dtype)]
        out_specs = [o_spec, o_spec]
    else:
        out_shape = [jax.ShapeDtypeStruct((m, n), out_dtype)]
        out_specs = [o_spec]
    in_specs += [any_spec] * nc
    args += [arr for arr, _ in items]
    out_shape += _exchange_shapes(items)
    out_specs += [any_spec] * nc
    scratch = ([pltpu.VMEM((tm, tn), F32)] if nk > 1 else []) + (_exchange_sems(nc) if nc else [])
    outs = pl.pallas_call(
        body, name=name, grid=(nj, ni, nk), in_specs=in_specs, out_specs=tuple(out_specs), out_shape=tuple(out_shape),
        scratch_shapes=scratch, compiler_params=_cparams(3),
    )(*args)
    return outs[0] if len(outs) == 1 else outs


def _row_tile(s, want):
    t = min(want, s)
    assert s % t == 0 and t % HALO == 0
    return t


def prenorm(x, nw, scale, shift, name):
    s, d = x.shape
    tm = _row_tile(s, 512)

    def body(x_ref, nw_ref, sc_ref, sh_ref, o_ref):
        xv = x_ref[...]
        r = lax.rsqrt(jnp.mean(xv * xv, axis=-1, keepdims=True) + EPS)
        o_ref[...] = ((xv * r) * nw_ref[...] * (1.0 + sc_ref[...]) + sh_ref[...]).astype(BF16)

    row = pl.BlockSpec((tm, d), lambda i: (i, 0))
    vec = pl.BlockSpec((1, d), lambda i: (0, 0))
    return pl.pallas_call(body, name=name, grid=(s // tm,), in_specs=[row, vec, vec, vec], out_specs=row,
                          out_shape=jax.ShapeDtypeStruct((s, d), BF16), compiler_params=_cparams(1))(x, nw, scale, shift)


def norm_backward(du, xin, dres, nw, scale, name, mo=None, gate=None):
    s, d = xin.shape
    tm = _row_tile(s, 256)
    gated = mo is not None

    def body(*refs):
        if gated:
            du_ref, x_ref, dr_ref, nw_ref, sc_ref, mo_ref, g_ref, dx_ref, dsh_ref, dsc_ref, dnw_ref, dmo_ref, dg_ref = refs
        else:
            du_ref, x_ref, dr_ref, nw_ref, sc_ref, dx_ref, dsh_ref, dsc_ref, dnw_ref = refs
        i = pl.program_id(0)
        xv = x_ref[...]
        r = lax.rsqrt(jnp.mean(xv * xv, axis=-1, keepdims=True) + EPS)
        xn = xv * r
        duv = du_ref[...].astype(F32)
        nwv, scv = nw_ref[...], sc_ref[...]
        dxn = duv * (nwv * (1.0 + scv))
        dx = dr_ref[...] + r * (dxn - xn * jnp.mean(dxn * xn, axis=-1, keepdims=True))
        dx_ref[...] = dx
        sums = [jnp.sum(duv, axis=0, keepdims=True), jnp.sum(duv * xn * nwv, axis=0, keepdims=True),
                jnp.sum(duv * xn * (1.0 + scv), axis=0, keepdims=True)]
        accs = [dsh_ref, dsc_ref, dnw_ref]
        if gated:
            dmo_ref[...] = (dx * g_ref[...]).astype(BF16)
            sums.append(jnp.sum(dx * mo_ref[...].astype(F32), axis=0, keepdims=True))
            accs.append(dg_ref)

        @pl.when(i == 0)
        def _():
            for acc, sm in zip(accs, sums):
                acc[...] = sm

        @pl.when(i > 0)
        def _():
            for acc, sm in zip(accs, sums):
                acc[...] += sm

    row = pl.BlockSpec((tm, d), lambda i: (i, 0))
    vec = pl.BlockSpec((1, d), lambda i: (0, 0))
    vshape = jax.ShapeDtypeStruct((1, d), F32)
    in_specs, args = [row, row, row, vec, vec], [du, xin, dres, nw, scale]
    out_specs, out_shape = [row, vec, vec, vec], [jax.ShapeDtypeStruct((s, d), F32), vshape, vshape, vshape]
    if gated:
        in_specs += [row, vec]
        args += [mo, gate]
        out_specs += [row, vec]
        out_shape += [jax.ShapeDtypeStruct((s, d), BF16), vshape]
    return pl.pallas_call(body, name=name, grid=(s // tm,), in_specs=in_specs, out_specs=tuple(out_specs),
                          out_shape=tuple(out_shape), compiler_params=_cparams(1))(*args)


def merge_forward(proj, y_attn, y_rnn):
    s, d = y_attn.shape
    tm, cw = _row_tile(s, 1024), 512

    def body(ga_ref, gl_ref, ya_ref, yr_ref, o_ref):
        o_ref[...] = (_sigmoid(ga_ref[...].astype(F32)) * ya_ref[...].astype(F32)
                      + _sigmoid(gl_ref[...].astype(F32)) * yr_ref[...].astype(F32)).astype(BF16)

    def at(off):
        return pl.BlockSpec((tm, cw), lambda j, i: (i, off // cw + j))

    return pl.pallas_call(body, name="merge_forward", grid=(d // cw, s // tm),
                          in_specs=[at(GA0), at(GL0), at(0), at(0)], out_specs=at(0),
                          out_shape=jax.ShapeDtypeStruct((s, d), BF16), compiler_params=_cparams(2))(proj, proj, y_attn, y_rnn)


def merge_backward(dmerged, proj, y_attn, y_rnn):
    s, d = y_attn.shape
    tm, cw = _row_tile(s, 1024), 512

    def body(dm_ref, ga_ref, gl_ref, ya_ref, yr_ref, dga_ref, dgl_ref, dya_ref, dyr_ref):
        dm = dm_ref[...].astype(F32)
        sa, sl = _sigmoid(ga_ref[...].astype(F32)), _sigmoid(gl_ref[...].astype(F32))
        dga_ref[...] = (dm * ya_ref[...].astype(F32) * sa * (1.0 - sa)).astype(BF16)
        dgl_ref[...] = (dm * yr_ref[...].astype(F32) * sl * (1.0 - sl)).astype(BF16)
        dya_ref[...] = (dm * sa).astype(BF16)
        dyr_ref[...] = (dm * sl).astype(BF16)

    def at(off):
        return pl.BlockSpec((tm, cw), lambda j, i: (i, off // cw + j))

    o = jax.ShapeDtypeStruct((s, d), BF16)
    return pl.pallas_call(body, name="merge_backward", grid=(d // cw, s // tm),
                          in_specs=[at(0), at(GA0), at(GL0), at(0), at(0)], out_specs=(at(0),) * 4,
                          out_shape=(o, o, o, o), compiler_params=_cparams(2))(dmerged, proj, proj, y_attn, y_rnn)


def _prev_spec(tm, cw, off_blocks):
    r = tm // HALO
    return pl.BlockSpec((HALO, cw), lambda j, i: (jnp.maximum(i * r - 1, 0), off_blocks + j))


def ffn_act_forward(upp, cw_full, cb_full):
    s, f2 = upp.shape
    f = f2 // 2
    tm, cw = _row_tile(s, 512), 1536
    nj = f // cw

    def body(g_ref, gp_ref, v_ref, vp_ref, wg_ref, wv_ref, bg_ref, bv_ref, o_ref, og_ref, ov_ref):
        i = pl.program_id(1)
        first = i == 0

        def conv(x_ref, p_ref, w_ref, b_ref):
            xv = x_ref[...].astype(F32)
            pv = jnp.where(first, 0.0, p_ref[...].astype(F32))
            acc = b_ref[...] + w_ref[FFN_CONV - 1:FFN_CONV, :] * xv
            for k in range(FFN_CONV - 1):
                acc = acc + w_ref[k:k + 1, :] * _shift_down(pv, xv, FFN_CONV - 1 - k)
            return acc

        g = conv(g_ref, gp_ref, wg_ref, bg_ref)
        v = conv(v_ref, vp_ref, wv_ref, bv_ref)
        og_ref[...] = g.astype(BF16)
        ov_ref[...] = v.astype(BF16)
        o_ref[...] = (_gelu(g) * v).astype(BF16)

    def tile(ob):
        return pl.BlockSpec((tm, cw), lambda j, i: (i, ob + j))

    def par(rows, ob):
        return pl.BlockSpec((rows, cw), lambda j, i: (0, ob + j))

    o = jax.ShapeDtypeStruct((s, f), BF16)
    return pl.pallas_call(
        body, name="ffn_act_forward", grid=(nj, s // tm),
        in_specs=[tile(0), _prev_spec(tm, cw, 0), tile(nj), _prev_spec(tm, cw, nj),
                  par(FFN_CONV, 0), par(FFN_CONV, nj), par(1, 0), par(1, nj)],
        out_specs=(tile(0), tile(0), tile(0)), out_shape=(o, o, o), compiler_params=_cparams(2),
    )(upp, upp, upp, upp, cw_full, cw_full, cb_full, cb_full)


ROW_CHUNK = 16
LANE_CHUNK = 512


def ffn_act_backward(dact, up_g, up_v):
    s, f = dact.shape
    tm, cw = _row_tile(s, 512), 1536
    nr = tm // ROW_CHUNK

    def body(da_ref, g_ref, v_ref, dg_ref, dv_ref, sg_ref, sv_ref, acc_ref):
        i = pl.program_id(1)
        acc_ref[...] = jnp.zeros_like(acc_ref)

        def chunk(r, carry):
            rows = pl.ds(pl.multiple_of(r * ROW_CHUNK, ROW_CHUNK), ROW_CHUNK)
            for c0 in range(0, cw, LANE_CHUNK):
                cols = pl.ds(c0, LANE_CHUNK)
                da = da_ref[rows, cols].astype(F32)
                ge, dge = _gelu_and_grad(g_ref[rows, cols].astype(F32))
                dg = da * v_ref[rows, cols].astype(F32) * dge
                dv = da * ge
                dg_ref[rows, cols] = dg.astype(BF16)
                dv_ref[rows, cols] = dv.astype(BF16)
                acc_ref[0, :, cols] += dg[0:8] + dg[8:16]
                acc_ref[1, :, cols] += dv[0:8] + dv[8:16]
            return carry

        lax.fori_loop(0, nr, chunk, 0)
        sg = jnp.sum(acc_ref[0], axis=0, keepdims=True)
        sv = jnp.sum(acc_ref[1], axis=0, keepdims=True)

        @pl.when(i == 0)
        def _():
            sg_ref[...] = sg
            sv_ref[...] = sv

        @pl.when(i > 0)
        def _():
            sg_ref[...] += sg
            sv_ref[...] += sv

    tile = pl.BlockSpec((tm, cw), lambda j, i: (i, j))
    vec = pl.BlockSpec((1, cw), lambda j, i: (0, j))
    o = jax.ShapeDtypeStruct((s, f), BF16)
    v1 = jax.ShapeDtypeStruct((1, f), F32)
    return pl.pallas_call(
        body, name="ffn_act_backward", grid=(f // cw, s // tm), in_specs=[tile, tile, tile],
        out_specs=(tile, tile, vec, vec), out_shape=(o, o, v1, v1),
        scratch_shapes=[pltpu.VMEM((2, 8, cw), F32)], compiler_params=_cparams(2),
    )(dact, up_g, up_v)


def conv_backward(d_g, d_v, upp, cw_full):
    s, f = d_g.shape
    f2 = 2 * f
    tm, cw = _row_tile(s, 512), 1536
    nt, nj = s // tm, f // cw
    r_halo = tm // HALO
    nr = tm // ROW_CHUNK

    def body(xg_ref, ng_ref, xv_ref, nv_ref, u_ref, w_ref, o_ref, dw_ref, acc_ref):
        j, i = pl.program_id(0), pl.program_id(1)
        acc_ref[...] = jnp.zeros_like(acc_ref)

        def run(x_ref, n_ref):
            def chunk(r, carry):
                rows = pl.ds(pl.multiple_of(r * ROW_CHUNK, ROW_CHUNK), ROW_CHUNK)
                nrows = pl.ds(pl.multiple_of(jnp.minimum(r + 1, nr - 1) * ROW_CHUNK, ROW_CHUNK), ROW_CHUNK)
                for c0 in range(0, cw, LANE_CHUNK):
                    cols = pl.ds(c0, LANE_CHUNK)
                    cur = x_ref[rows, cols].astype(F32)
                    halo = jnp.where(i == nt - 1, 0.0, n_ref[:, cols].astype(F32))
                    nxt = jnp.where(r == nr - 1, halo, x_ref[nrows, cols].astype(F32))
                    uv = u_ref[rows, cols].astype(F32)
                    acc = None
                    for k in range(FFN_CONV):
                        sh = _shift_up(cur, nxt, FFN_CONV - 1 - k)
                        term = w_ref[k:k + 1, cols] * sh
                        acc = term if acc is None else acc + term
                        pr = uv * sh
                        acc_ref[k, :, cols] += pr[0:8] + pr[8:16]
                    o_ref[rows, cols] = acc.astype(BF16)
                return carry

            lax.fori_loop(0, nr, chunk, 0)

        @pl.when(j < nj)
        def _():
            run(xg_ref, ng_ref)

        @pl.when(j >= nj)
        def _():
            run(xv_ref, nv_ref)

        sums = jnp.concatenate([jnp.sum(acc_ref[k], axis=0, keepdims=True) for k in range(FFN_CONV)], axis=0)

        @pl.when(i == 0)
        def _():
            dw_ref[...] = sums

        @pl.when(i > 0)
        def _():
            dw_ref[...] += sums

    def tile_h(used):
        return pl.BlockSpec((tm, cw), lambda j, i: (jnp.where(used(j), i, 0), jnp.where(used(j), j % nj, 0)))

    def next_h(used):
        return pl.BlockSpec((HALO, cw), lambda j, i: (
            jnp.where(used(j), jnp.minimum((i + 1) * r_halo, s // HALO - 1), 0), jnp.where(used(j), j % nj, 0)))

    is_g = lambda j: j < nj
    is_v = lambda j: j >= nj
    tile = pl.BlockSpec((tm, cw), lambda j, i: (i, j))
    par = pl.BlockSpec((FFN_CONV, cw), lambda j, i: (0, j))
    return pl.pallas_call(
        body, name="ffn_conv_backward", grid=(2 * nj, nt),
        in_specs=[tile_h(is_g), next_h(is_g), tile_h(is_v), next_h(is_v), tile, par],
        out_specs=(tile, par), out_shape=(jax.ShapeDtypeStruct((s, f2), BF16), jax.ShapeDtypeStruct((FFN_CONV, f2), F32)),
        scratch_shapes=[pltpu.VMEM((FFN_CONV, 8, cw), F32)], compiler_params=_cparams(2),
    )(d_g, d_g, d_v, d_v, upp, cw_full)


def loss_head(h2, target, dn, norm_f, gate2):
    s, d = h2.shape
    tm = _row_tile(s, 256)

    def body(h_ref, t_ref, dn_ref, nf_ref, g_ref, dh_ref, ddn_ref, loss_ref, dnf_ref, dg_ref):
        i = pl.program_id(0)
        hv = h_ref[...]
        r = lax.rsqrt(jnp.mean(hv * hv, axis=-1, keepdims=True) + EPS)
        yh = hv * r
        nf = nf_ref[...]
        err = yh * nf - t_ref[...]
        dy = err * (1.0 / d)
        dyh = dy * nf
        dh = r * (dyh - yh * jnp.mean(dyh * yh, axis=-1, keepdims=True))
        dh_ref[...] = dh
        ddn_ref[...] = (dh * g_ref[...]).astype(BF16)
        sums = [jnp.sum(err * err, axis=0, keepdims=True) * (0.5 / d), jnp.sum(dy * yh, axis=0, keepdims=True),
                jnp.sum(dh * dn_ref[...].astype(F32), axis=0, keepdims=True)]
        accs = [loss_ref, dnf_ref, dg_ref]

        @pl.when(i == 0)
        def _():
            for acc, sm in zip(accs, sums):
                acc[...] = sm

        @pl.when(i > 0)
        def _():
            for acc, sm in zip(accs, sums):
                acc[...] += sm

    row = pl.BlockSpec((tm, d), lambda i: (i, 0))
    vec = pl.BlockSpec((1, d), lambda i: (0, 0))
    v = jax.ShapeDtypeStruct((1, d), F32)
    return pl.pallas_call(
        body, name="loss_head", grid=(s // tm,), in_specs=[row, row, row, vec, vec], out_specs=(row, row, vec, vec, vec),
        out_shape=(jax.ShapeDtypeStruct((s, d), F32), jax.ShapeDtypeStruct((s, d), BF16), v, v, v),
        compiler_params=_cparams(1))(h2, target, dn, norm_f, gate2)


def _t5_buckets():
    qi = np.arange(BLOCK)[:, None]
    kj = np.arange(2 * BLOCK)[None, :]
    dist = qi + BLOCK - kj
    dd = np.maximum(dist, 0)
    max_exact = NUM_BUCKETS // 2
    dflt = np.maximum(dd, 1).astype(np.float32)
    large = max_exact + (np.log(dflt / max_exact) / math.log(MAX_DISTANCE / max_exact)
                         * (NUM_BUCKETS - max_exact)).astype(np.int32)
    large = np.minimum(large, NUM_BUCKETS - 1)
    bucket = np.where(dd < max_exact, dd, large).astype(np.int32)
    in_window = (dist >= 0) & (dist < BLOCK)
    return bucket, in_window


def band_bias(rel_bias):
    bucket, in_window = _t5_buckets()
    bucket_t = jnp.asarray(np.where(in_window, bucket, -1).astype(np.int32).T)

    def body(rb_ref, bk_ref, o_ref):
        bk = bk_ref[...]
        for h in range(N_HEADS):
            acc = jnp.full((2 * BLOCK, BLOCK), NEG_INF, F32)
            for b in range(NUM_BUCKETS):
                acc = jnp.where(bk == b, rb_ref[b, h], acc)
            o_ref[h] = acc

    return pl.pallas_call(
        body, name="band_bias", out_shape=jax.ShapeDtypeStruct((N_HEADS, 2 * BLOCK, BLOCK), F32),
        in_specs=[pl.BlockSpec(memory_space=pltpu.SMEM), pl.BlockSpec(memory_space=pltpu.VMEM)],
        out_specs=pl.BlockSpec(memory_space=pltpu.VMEM))(rel_bias, bucket_t)


def rel_bias_grad(dbias):
    bucket, in_window = _t5_buckets()
    bucket_t = jnp.asarray(np.where(in_window, bucket, -1).astype(np.int32).T)

    def body(db_ref, bk_ref, o_ref):
        bk = bk_ref[...]
        rows = lax.broadcasted_iota(jnp.int32, (NUM_BUCKETS, LANES), 0)
        lanes = lax.broadcasted_iota(jnp.int32, (NUM_BUCKETS, LANES), 1)
        acc = jnp.zeros((NUM_BUCKETS, LANES), F32)
        for h in range(N_HEADS):
            dv = db_ref[h]
            for b in range(NUM_BUCKETS):
                sm = jnp.sum(jnp.where(bk == b, dv, 0.0))
                acc = jnp.where((rows == b) & (lanes == h), sm, acc)
        o_ref[...] = acc

    vm = pl.BlockSpec(memory_space=pltpu.VMEM)
    return pl.pallas_call(body, name="rel_bias_grad", out_shape=jax.ShapeDtypeStruct((NUM_BUCKETS, LANES), F32),
                          in_specs=[vm, vm], out_specs=vm)(dbias, bucket_t)


HP = 2
Q_PER_HP = D_ATTN // HP
H_PER_HP = N_HEADS // HP
NT_DIMS = (((1,), (1,)), ((), ()))
TN_DIMS = (((0,), (0,)), ((), ()))


def _stack_heads(ref, hh):
    lane = lax.broadcasted_iota(jnp.int32, (BLOCK, LANES), 1)
    lo = lane < HEAD_DIM
    parts = []
    for s_ in range(GROUP // 2):
        c0 = hh * (GROUP * HEAD_DIM) + s_ * LANES
        slab = ref[:, c0:c0 + LANES]
        parts.append(jnp.where(lo, slab, jnp.zeros_like(slab)))
        parts.append(jnp.where(lo, jnp.zeros_like(slab), slab))
    return jnp.concatenate(parts, axis=0)


def _attn_probs(hh, q_ref, kp_ref, kc_ref, vp_ref, vc_ref, bias_ref, sink_ref, hp, n):
    lane = lax.broadcasted_iota(jnp.int32, (2 * BLOCK, LANES), 1)
    own = (lane >= HEAD_DIM) if hh == 1 else (lane < HEAD_DIM)
    kband = jnp.concatenate([kp_ref[...], kc_ref[...]], axis=0)
    vband = jnp.concatenate([vp_ref[...], vc_ref[...]], axis=0)
    kk = jnp.where(own, kband, pltpu.roll(kband, HEAD_DIM, axis=1))
    vv = jnp.where(own, vband, pltpu.roll(vband, HEAD_DIM, axis=1))
    qs = _stack_heads(q_ref, hh)
    sc = lax.dot_general(kk, qs, NT_DIMS, preferred_element_type=F32) * (HEAD_DIM ** -0.5)
    sc = sc + jnp.concatenate([bias_ref[hh * GROUP + g] for g in range(GROUP)], axis=1)
    krow = lax.broadcasted_iota(jnp.int32, sc.shape, 0)
    sc = jnp.where((n == 0) & (krow < BLOCK), NEG_INF, sc)
    sink = jnp.concatenate([jnp.full((1, BLOCK), sink_ref[hp * H_PER_HP + hh * GROUP + g], F32) for g in range(GROUP)], axis=1)
    m = jnp.maximum(jnp.max(sc, axis=0, keepdims=True), sink)
    p = jnp.exp(sc - m)
    es = jnp.exp(sink - m)
    inv = 1.0 / (jnp.sum(p, axis=0, keepdims=True) + es)
    return qs, kk, vv, p * inv, es * inv


def _unstack(o, dtype):
    lane = lax.broadcasted_iota(jnp.int32, (BLOCK, LANES), 1)
    lo = lane < HEAD_DIM
    slabs = []
    for s_ in range(GROUP // 2):
        ev = o[(2 * s_) * BLOCK:(2 * s_ + 1) * BLOCK]
        od = o[(2 * s_ + 1) * BLOCK:(2 * s_ + 2) * BLOCK]
        slabs.append(jnp.where(lo, ev, od).astype(dtype))
    return slabs


def attention_forward(proj, bias, sinks, comm):
    s = proj.shape[0]
    nb = s // BLOCK
    kb, vb = K0 // LANES, V0 // LANES
    nc = len(comm)
    scat = [s_ for _, s_ in comm]

    def body(*refs):
        q_ref, kp_ref, kc_ref, vp_ref, vc_ref, bias_ref, sink_ref = refs[:7]
        srcs, o_ref, dsts, sems = refs[7:7 + nc], refs[7 + nc], refs[8 + nc:8 + 2 * nc], refs[8 + 2 * nc:]
        hp, n = pl.program_id(0), pl.program_id(1)

        step = hp * nb + n

        def run(phase):
            for act in _exchange_steps(phase, srcs, dsts, scat, *sems):
                act()

        @pl.when(step == 0)
        def _():
            run(0)

        @pl.when(step == HP * nb - HP * nb // 16)
        def _():
            run(1)

        for hh in range(2):
            qs, kk, vv, probs, _ = _attn_probs(hh, q_ref, kp_ref, kc_ref, vp_ref, vc_ref, bias_ref, sink_ref, hp, n)
            o = lax.dot_general(probs.astype(BF16), vv, TN_DIMS, preferred_element_type=F32)
            for s_, slab in enumerate(_unstack(o, BF16)):
                c0 = hh * (GROUP * HEAD_DIM) + s_ * LANES
                o_ref[:, c0:c0 + LANES] = slab

        @pl.when(step == HP * nb - 1)
        def _():
            run(2)

    qspec = pl.BlockSpec((BLOCK, Q_PER_HP), lambda hp, n: (n, hp))
    any_spec = pl.BlockSpec(memory_space=pl.ANY)

    def kv(base, prev):
        if prev:
            return pl.BlockSpec((BLOCK, LANES), lambda hp, n: (jnp.maximum(n - 1, 0), base + hp))
        return pl.BlockSpec((BLOCK, LANES), lambda hp, n: (n, base + hp))

    return pl.pallas_call(
        body, name="attention_forward", grid=(HP, nb),
        in_specs=[qspec, kv(kb, True), kv(kb, False), kv(vb, True), kv(vb, False),
                  pl.BlockSpec((H_PER_HP, 2 * BLOCK, BLOCK), lambda hp, n: (hp, 0, 0)),
                  pl.BlockSpec(memory_space=pltpu.SMEM)] + [any_spec] * nc,
        out_specs=(qspec,) + (any_spec,) * nc,
        out_shape=(jax.ShapeDtypeStruct((s, D_ATTN), BF16),) + tuple(_exchange_shapes(comm)),
        scratch_shapes=_exchange_sems(nc), compiler_params=_cparams(2),
    )(proj, proj, proj, proj, proj, bias, sinks, *[arr for arr, _ in comm])


def attention_backward(proj, datt, bias, sinks, comm):
    s = proj.shape[0]
    nb = s // BLOCK
    kb, vb = K0 // LANES, V0 // LANES
    nc = len(comm)
    scat = [s_ for _, s_ in comm]

    def body(*refs):
        q_ref, kp_ref, kc_ref, vp_ref, vc_ref, do_ref, bias_ref, sink_ref = refs[:8]
        srcs = refs[8:8 + nc]
        dq_ref, dk_ref, dv_ref, dbias_ref, dsink_ref = refs[8 + nc:13 + nc]
        dsts = refs[13 + nc:13 + 2 * nc]
        kcar_ref, vcar_ref, sacc_ref = refs[13 + 2 * nc:16 + 2 * nc]
        sems = refs[16 + 2 * nc:]
        hp, n = pl.program_id(0), pl.program_id(1)

        @pl.when((hp == 0) & (n == 0))
        def _():
            for cp in _exchange_copies(srcs, dsts, scat, *sems):
                cp.start()

        @pl.when(n == 0)
        def _():
            kcar_ref[...] = jnp.zeros_like(kcar_ref)
            vcar_ref[...] = jnp.zeros_like(vcar_ref)
            dbias_ref[...] = jnp.zeros_like(dbias_ref)
            sacc_ref[...] = jnp.zeros_like(sacc_ref)

        @pl.when(n < nb)
        def _():
            lane2 = lax.broadcasted_iota(jnp.int32, (2 * BLOCK, LANES), 1)
            dk_band = jnp.zeros((2 * BLOCK, LANES), F32)
            dv_band = jnp.zeros((2 * BLOCK, LANES), F32)
            for hh in range(2):
                qs, kk, vv, probs, psink = _attn_probs(hh, q_ref, kp_ref, kc_ref, vp_ref, vc_ref, bias_ref, sink_ref, hp, n)
                dos = _stack_heads(do_ref, hh)
                dp = lax.dot_general(vv, dos, NT_DIMS, preferred_element_type=F32)
                dsum = jnp.sum(probs * dp, axis=0, keepdims=True)
                ds = probs * (dp - dsum)
                for g in range(GROUP):
                    dbias_ref[hh * GROUP + g] += ds[:, g * BLOCK:(g + 1) * BLOCK]
                sacc_ref[hh:hh + 1, :] += -psink * dsum
                dsb = (ds * (HEAD_DIM ** -0.5)).astype(BF16)
                pb = probs.astype(BF16)
                dq = lax.dot_general(dsb, kk, TN_DIMS, preferred_element_type=F32)
                for s_, slab in enumerate(_unstack(dq, BF16)):
                    c0 = hh * (GROUP * HEAD_DIM) + s_ * LANES
                    dq_ref[:, c0:c0 + LANES] = slab
                dkh = jnp.dot(dsb, qs, preferred_element_type=F32)
                dvh = jnp.dot(pb, dos, preferred_element_type=F32)
                own = (lane2 >= HEAD_DIM) if hh == 1 else (lane2 < HEAD_DIM)
                dk_band = dk_band + jnp.where(own, dkh + pltpu.roll(dkh, HEAD_DIM, axis=1), 0.0)
                dv_band = dv_band + jnp.where(own, dvh + pltpu.roll(dvh, HEAD_DIM, axis=1), 0.0)
            dk_ref[...] = (kcar_ref[...] + dk_band[:BLOCK]).astype(BF16)
            dv_ref[...] = (vcar_ref[...] + dv_band[:BLOCK]).astype(BF16)
            kcar_ref[...] = dk_band[BLOCK:]
            vcar_ref[...] = dv_band[BLOCK:]

        @pl.when(n == nb)
        def _():
            dk_ref[...] = kcar_ref[...].astype(BF16)
            dv_ref[...] = vcar_ref[...].astype(BF16)
            rows = [jnp.full((1, LANES), jnp.sum(sacc_ref[hh:hh + 1, g * BLOCK:(g + 1) * BLOCK]), F32)
                    for hh in range(2) for g in range(GROUP)]
            dsink_ref[...] = jnp.concatenate(rows, axis=0)

        @pl.when((hp == HP - 1) & (n == nb))
        def _():
            for cp in _exchange_copies(srcs, dsts, scat, *sems):
                cp.wait()

    qspec = pl.BlockSpec((BLOCK, Q_PER_HP), lambda hp, n: (jnp.minimum(n, nb - 1), hp))
    any_spec = pl.BlockSpec(memory_space=pl.ANY)

    def kv(base, prev):
        if prev:
            return pl.BlockSpec((BLOCK, LANES), lambda hp, n: (jnp.maximum(jnp.minimum(n, nb - 1) - 1, 0), base + hp))
        return pl.BlockSpec((BLOCK, LANES), lambda hp, n: (jnp.minimum(n, nb - 1), base + hp))

    dkv_spec = pl.BlockSpec((BLOCK, LANES), lambda hp, n: (jnp.maximum(n - 1, 0), hp))
    return pl.pallas_call(
        body, name="attention_backward", grid=(HP, nb + 1),
        in_specs=[qspec, kv(kb, True), kv(kb, False), kv(vb, True), kv(vb, False), qspec,
                  pl.BlockSpec((H_PER_HP, 2 * BLOCK, BLOCK), lambda hp, n: (hp, 0, 0)),
                  pl.BlockSpec(memory_space=pltpu.SMEM)] + [any_spec] * nc,
        out_specs=(qspec, dkv_spec, dkv_spec,
                   pl.BlockSpec((H_PER_HP, 2 * BLOCK, BLOCK), lambda hp, n: (hp, 0, 0)),
                   pl.BlockSpec((H_PER_HP, LANES), lambda hp, n: (hp, 0))) + (any_spec,) * nc,
        out_shape=(jax.ShapeDtypeStruct((s, D_ATTN), BF16), jax.ShapeDtypeStruct((s, D_KV), BF16),
                   jax.ShapeDtypeStruct((s, D_KV), BF16), jax.ShapeDtypeStruct((N_HEADS, 2 * BLOCK, BLOCK), F32),
                   jax.ShapeDtypeStruct((N_HEADS, LANES), F32)) + tuple(_exchange_shapes(comm)),
        scratch_shapes=[pltpu.VMEM((BLOCK, LANES), F32), pltpu.VMEM((BLOCK, LANES), F32),
                        pltpu.VMEM((8, GROUP * BLOCK), F32)] + _exchange_sems(nc),
        compiler_params=_cparams(2),
    )(proj, proj, proj, proj, proj, datt, bias, sinks, *[arr for arr, _ in comm])


def _neg_expm1(x):
    series = -(x * (1.0 + x * (1.0 / 2 + x * (1.0 / 6 + x * (1.0 / 24 + x * (1.0 / 120 + x * (1.0 / 720)))))))
    return jnp.where(x > -0.25, series, 1.0 - jnp.exp(x))


def _softplus_neg(lam):
    u = jnp.exp(-jnp.abs(lam))
    w = 1.0 + u
    log1p = jnp.where(w == 1.0, u, jnp.log(w) * u / jnp.where(w == 1.0, 1.0, w - 1.0))
    sp = jnp.maximum(-lam, 0.0) + log1p
    return sp, -_sigmoid(-lam)


def _rnn_gates(x_ref, xp_ref, cw_ref, cb_ref, wa_ref, ba_ref, wi_ref, bi_ref, lam_ref, first, row0):
    xv = x_ref[...].astype(F32)
    pv = jnp.where(first, 0.0, xp_ref[...].astype(F32))
    xs = [_shift_down(pv, xv, RNN_CONV - 1 - k) for k in range(RNN_CONV)]
    xc = cb_ref[...]
    for k in range(RNN_CONV):
        xc = xc + cw_ref[k:k + 1, :] * xs[k]
    xcb = xc.astype(BF16)
    ra = _sigmoid(jnp.dot(xcb, wa_ref[...], preferred_element_type=F32) + ba_ref[...])
    ri = _sigmoid(jnp.dot(xcb, wi_ref[...], preferred_element_type=F32) + bi_ref[...])
    sp, dsp = _softplus_neg(lam_ref[...])
    la = (-RG_C) * ra * sp
    a = jnp.exp(la)
    t = row0 + lax.broadcasted_iota(jnp.int32, xv.shape, 0)
    start = t == 0
    mult = jnp.where(start, 1.0, jnp.sqrt(_neg_expm1(2.0 * la)))
    return xs, xc, xcb, ra, ri, sp, dsp, a, mult, start


def _rnn_specs(t_rows, s, rev):
    nt = s // t_rows
    r = t_rows // HALO
    xb, gb = XR0 // LANES, GR0 // LANES
    ti = (lambda i: nt - 1 - i) if rev else (lambda i: i)
    tile = lambda base: pl.BlockSpec((t_rows, LANES), lambda n, i: (ti(i), base + n))
    prev = lambda base: pl.BlockSpec((HALO, LANES), lambda n, i: (jnp.maximum(ti(i) * r - 1, 0), base + n))
    par = lambda rows: pl.BlockSpec((rows, LANES), lambda n, i: (0, n))
    mat = pl.BlockSpec((None, RNN_BLOCK, RNN_BLOCK), lambda n, i: (n, 0, 0))
    return nt, ti, tile, prev, par, mat, xb, gb


def rnn_forward(proj, cw, cb, wa, ba, wi, bi, lam):
    s = proj.shape[0]
    t_rows = _row_tile(s, 512)
    nt, ti, tile, prev, par, mat, xb, gb = _rnn_specs(t_rows, s, False)

    def body(x_ref, xp_ref, g_ref, cw_ref, cb_ref, wa_ref, ba_ref, wi_ref, bi_ref, lam_ref, z_ref, h_ref, car_ref):
        i = pl.program_id(1)
        first = i == 0
        _, xc, _, _, ri, _, _, a, mult, _ = _rnn_gates(x_ref, xp_ref, cw_ref, cb_ref, wa_ref, ba_ref, wi_ref, bi_ref,
                                                       lam_ref, first, i * t_rows)
        aa, bb = a, mult * ri * xc
        rows = lax.broadcasted_iota(jnp.int32, aa.shape, 0)
        d = 1
        while d < t_rows:
            keep = rows >= d
            a_s, b_s = pltpu.roll(aa, d, axis=0), pltpu.roll(bb, d, axis=0)
            bb = jnp.where(keep, aa * b_s + bb, bb)
            aa = jnp.where(keep, aa * a_s, aa)
            d *= 2
        carry = jnp.where(first, 0.0, car_ref[0:1, :])
        h = aa * carry + bb
        car_ref[...] = jnp.broadcast_to(h[t_rows - 1:t_rows, :], car_ref.shape)
        h_ref[...] = h.astype(BF16)
        z_ref[...] = (h * _gelu(g_ref[...].astype(F32))).astype(BF16)

    o = jax.ShapeDtypeStruct((s, D_RNN), BF16)
    out_tile = pl.BlockSpec((t_rows, LANES), lambda n, i: (i, n))
    return pl.pallas_call(
        body, name="rnn_forward", grid=(N_RNN_BLOCKS, nt),
        in_specs=[tile(xb), prev(xb), tile(gb), par(RNN_CONV), par(1), mat, par(1), mat, par(1), par(1)],
        out_specs=(out_tile, out_tile), out_shape=(o, o), scratch_shapes=[pltpu.VMEM((8, LANES), F32)],
        compiler_params=_cparams(2),
    )(proj, proj, proj, cw, cb, wa, ba, wi, bi, lam)


def rnn_backward(proj, h, dz, cw, cb, wa, ba, wi, bi, lam, comm):
    s = proj.shape[0]
    t_rows = _row_tile(s, 512)
    nt, ti, tile, prev, par, mat, xb, gb = _rnn_specs(t_rows, s, True)
    r = t_rows // HALO
    nc = len(comm)
    scat = [s_ for _, s_ in comm]

    def body(*refs):
        (x_ref, xp_ref, g_ref, h_ref, hp_ref, dz_ref, cw_ref, cb_ref, wa_ref, ba_ref, wi_ref, bi_ref,
         lam_ref) = refs[:13]
        srcs = refs[13:13 + nc]
        dx_ref, dg_ref, dwa_ref, dwi_ref, sm_ref = refs[13 + nc:18 + nc]
        dsts = refs[18 + nc:18 + 2 * nc]
        gcar_ref, xcar_ref = refs[18 + 2 * nc:20 + 2 * nc]
        sems = refs[20 + 2 * nc:]
        i = pl.program_id(1)
        step = pl.program_id(0) * nt + i

        def run(phase):
            for act in _exchange_steps(phase, srcs, dsts, scat, *sems):
                act()

        @pl.when(step == 0)
        def _():
            run(0)

        @pl.when(step == (RELAY_AT_NUM * N_RNN_BLOCKS * nt) // RELAY_AT_DEN)
        def _():
            run(1)

        it = nt - 1 - i
        first, last = it == 0, it == nt - 1
        xs, xc, xcb, ra, ri, sp, dsp, a, mult, start = _rnn_gates(
            x_ref, xp_ref, cw_ref, cb_ref, wa_ref, ba_ref, wi_ref, bi_ref, lam_ref, first, it * t_rows)
        hf = h_ref[...].astype(F32)
        hprev = _shift_down(jnp.where(first, 0.0, hp_ref[...].astype(F32)), hf, 1)
        ge, dge = _gelu_and_grad(g_ref[...].astype(F32))
        dz = dz_ref[...].astype(F32)
        dg_ref[...] = (dz * hf * dge).astype(BF16)
        rows = lax.broadcasted_iota(jnp.int32, hf.shape, 0)
        tail = rows == t_rows - 1
        carry = jnp.where(last, 0.0, gcar_ref[0:1, :])
        bb = dz * ge + jnp.where(tail, carry, 0.0)
        aa = jnp.where(tail, 0.0, pltpu.roll(a, t_rows - 1, axis=0))
        d = 1
        while d < t_rows:
            keep = rows < t_rows - d
            a_s, b_s = pltpu.roll(aa, t_rows - d, axis=0), pltpu.roll(bb, t_rows - d, axis=0)
            bb = jnp.where(keep, bb + aa * b_s, bb)
            aa = jnp.where(keep, aa * a_s, aa)
            d *= 2
        gg = bb
        gcar_ref[...] = jnp.broadcast_to(a[0:1, :] * gg[0:1, :], gcar_ref.shape)
        da = gg * hprev
        dmult = jnp.where(start, 0.0, gg * ri * xc)
        dri = gg * mult * xc
        dxc = gg * mult * ri
        safe_mult = jnp.where(start, 1.0, mult)
        dla = da * a - dmult * (a * a) / safe_mult
        dra = dla * ((-RG_C) * sp)
        dlam = jnp.sum(dla * ((-RG_C) * ra), axis=0, keepdims=True) * dsp
        dpa = dra * ra * (1.0 - ra)
        dpi = dri * ri * (1.0 - ri)
        dpab, dpib = dpa.astype(BF16), dpi.astype(BF16)
        nt_dims = (((1,), (1,)), ((), ()))
        tn_dims = (((0,), (0,)), ((), ()))
        dxc = dxc + lax.dot_general(dpab, wa_ref[...], nt_dims, preferred_element_type=F32) \
            + lax.dot_general(dpib, wi_ref[...], nt_dims, preferred_element_type=F32)
        dwa = lax.dot_general(xcb, dpab, tn_dims, preferred_element_type=F32)
        dwi = lax.dot_general(xcb, dpib, tn_dims, preferred_element_type=F32)
        nxt = jnp.where(last, 0.0, xcar_ref[...])
        dx = cw_ref[RNN_CONV - 1:RNN_CONV, :] * dxc
        for k in range(RNN_CONV - 1):
            dx = dx + cw_ref[k:k + 1, :] * _shift_up(dxc, nxt, RNN_CONV - 1 - k)
        dx_ref[...] = dx.astype(BF16)
        xcar_ref[...] = dxc[0:HALO, :]
        small = jnp.concatenate(
            [jnp.sum(dpa, axis=0, keepdims=True), jnp.sum(dpi, axis=0, keepdims=True), dlam,
             jnp.sum(dxc, axis=0, keepdims=True)]
            + [jnp.sum(dxc * xs[k], axis=0, keepdims=True) for k in range(RNN_CONV)], axis=0)

        @pl.when(i == 0)
        def _():
            dwa_ref[...] = dwa
            dwi_ref[...] = dwi
            sm_ref[...] = small

        @pl.when(i > 0)
        def _():
            dwa_ref[...] += dwa
            dwi_ref[...] += dwi
            sm_ref[...] += small

        @pl.when(step == N_RNN_BLOCKS * nt - 1)
        def _():
            run(2)

    o = jax.ShapeDtypeStruct((s, D_RNN), BF16)
    any_spec = pl.BlockSpec(memory_space=pl.ANY)
    plain = pl.BlockSpec((t_rows, LANES), lambda n, i: (ti(i), n))
    plain_prev = pl.BlockSpec((HALO, LANES), lambda n, i: (jnp.maximum(ti(i) * r - 1, 0), n))
    return pl.pallas_call(
        body, name="rnn_backward", grid=(N_RNN_BLOCKS, nt),
        in_specs=[tile(xb), prev(xb), tile(gb), plain, plain_prev, plain,
                  par(RNN_CONV), par(1), mat, par(1), mat, par(1), par(1)] + [any_spec] * nc,
        out_specs=(plain, plain, mat, mat, pl.BlockSpec((None, 8, LANES), lambda n, i: (n, 0, 0))) + (any_spec,) * nc,
        out_shape=(o, o, jax.ShapeDtypeStruct((N_RNN_BLOCKS, RNN_BLOCK, RNN_BLOCK), F32),
                   jax.ShapeDtypeStruct((N_RNN_BLOCKS, RNN_BLOCK, RNN_BLOCK), F32),
                   jax.ShapeDtypeStruct((N_RNN_BLOCKS, 8, LANES), F32)) + tuple(_exchange_shapes(comm)),
        scratch_shapes=[pltpu.VMEM((8, LANES), F32), pltpu.VMEM((HALO, LANES), F32)] + _exchange_sems(nc),
        compiler_params=_cparams(2),
    )(proj, proj, proj, h, h, dz, cw, cb, wa, ba, wi, bi, lam, *[arr for arr, _ in comm])


def adamw(parts, w, m, v, name, rows=256):
    p, r, c = parts.shape
    tr = min(rows, r)
    assert r % tr == 0

    def body(p_ref, w_ref, m_ref, v_ref, g_ref, d_ref, nm_ref, nv_ref):
        g = p_ref[0].astype(F32)
        for q in range(1, p):
            g = g + p_ref[q].astype(F32)
        nm = ADAM_B1 * m_ref[...] + (1.0 - ADAM_B1) * g
        nv = ADAM_B2 * v_ref[...] + (1.0 - ADAM_B2) * (g * g)
        mh = nm / (1.0 - ADAM_B1 ** ADAM_STEP)
        vh = nv / (1.0 - ADAM_B2 ** ADAM_STEP)
        g_ref[...] = g
        d_ref[...] = (-ADAM_LR) * (mh / (jnp.sqrt(vh) + ADAM_EPS) + ADAM_WD * w_ref[...])
        nm_ref[...] = nm
        nv_ref[...] = nv

    pspec = pl.BlockSpec((p, tr, c), lambda i: (0, i, 0))
    spec = pl.BlockSpec((tr, c), lambda i: (i, 0))
    o = jax.ShapeDtypeStruct((r, c), F32)
    return pl.pallas_call(body, name=name, grid=(r // tr,), in_specs=[pspec, spec, spec, spec],
                          out_specs=(spec,) * 4, out_shape=(o, o, o, o), compiler_params=_cparams(1))(parts, w, m, v)


def ada_weight_grad(c_t, dmod):
    d, nb = c_t.shape
    c = dmod.shape[1]
    tr = 512

    def body(c_ref, dm_ref, o_ref):
        cv = c_ref[...]
        cs = cv * _sigmoid(cv)
        acc = cs[:, 0:1] * dm_ref[0:1, :]
        for b in range(1, nb):
            acc = acc + cs[:, b:b + 1] * dm_ref[b:b + 1, :]
        o_ref[...] = acc

    return pl.pallas_call(body, name="ada_weight_grad", grid=(d // tr,),
                          in_specs=[pl.BlockSpec((tr, nb), lambda i: (i, 0)), pl.BlockSpec((nb, c), lambda i: (0, 0))],
                          out_specs=pl.BlockSpec((tr, c), lambda i: (i, 0)),
                          out_shape=jax.ShapeDtypeStruct((d, c), F32), compiler_params=_cparams(1))(c_t, dmod)


def _rows128(a):
    flat = a.reshape(-1).astype(F32)
    pad = (-flat.shape[0]) % (8 * LANES)
    if pad:
        flat = jnp.concatenate([flat, jnp.zeros((pad,), F32)])
    return flat.reshape(-1, LANES)


def kernel(x, c, w_ada, b_ada, norm1, w_in, rnn_conv_w, rnn_conv_b, w_rg_a, b_rg_a, w_rg_i, b_rg_i, rg_lambda, w_o_rnn, w_o_attn, attn_sinks, rel_bias, w_out, norm2, w_up, ffn_conv_w, ffn_conv_b, w_down, norm_f, loss_target, m_w_ada, m_b_ada, m_norm1, m_w_in, m_rnn_conv_w, m_rnn_conv_b, m_w_rg_a, m_b_rg_a, m_w_rg_i, m_b_rg_i, m_rg_lambda, m_w_o_rnn, m_w_o_attn, m_attn_sinks, m_rel_bias, m_w_out, m_norm2, m_w_up, m_ffn_conv_w, m_ffn_conv_b, m_w_down, m_norm_f, v_w_ada, v_b_ada, v_norm1, v_w_in, v_rnn_conv_w, v_rnn_conv_b, v_w_rg_a, v_b_rg_a, v_w_rg_i, v_b_rg_i, v_rg_lambda, v_w_o_rnn, v_w_o_attn, v_attn_sinks, v_rel_bias, v_w_out, v_norm2, v_w_up, v_ffn_conv_w, v_ffn_conv_b, v_w_down, v_norm_f):
    me = 4 * lax.axis_index("x") + 2 * lax.axis_index("y") + lax.axis_index("c")
    xs = x[0]
    tgt = loss_target[0]
    s, d = xs.shape
    bf = lambda w: w[0].astype(BF16)

    mod, c_all = mod_forward(c.reshape(1, 1, d), w_ada[0], b_ada)
    mod = mod.reshape(6, d)
    shift1, scale1, gate1, shift2, scale2, gate2 = [mod[i:i + 1] for i in range(6)]

    g_in, g_rcw, g_fcw = exchange([(bf(w_in), False), (rnn_conv_w[0], False), (ffn_conv_w[0], False)], "gather_w_in")
    w_in_f = jnp.transpose(g_in, (1, 0, 2)).reshape(d, D_IN)
    rcw = jnp.transpose(g_rcw, (1, 0, 2)).reshape(RNN_CONV, D_RNN)
    fcw = jnp.transpose(g_fcw, (1, 0, 2)).reshape(FFN_CONV, 2 * D_FF)
    wa_b, wi_b = bf(w_rg_a), bf(w_rg_i)
    sinks = attn_sinks[0]

    u = prenorm(xs, norm1, scale1, shift1, "prenorm1")
    proj, g_oa, g_or, g_out, g_down = matmul(
        u, w_in_f, "nn", BF16, "mm_in", 2048, 512, 2048,
        comm=[(bf(w_o_attn), False), (bf(w_o_rnn), False), (bf(w_out), False), (bf(w_down), False)])
    w_oa_f, w_or_f = g_oa.reshape(D_ATTN, d), g_or.reshape(D_RNN, d)
    w_out_f, w_down_f = g_out.reshape(d, d), g_down.reshape(D_FF, d)
    bias = band_bias(rel_bias)
    att, g_up = attention_forward(proj, bias, sinks, [(bf(w_up), False)])
    w_up_f = jnp.transpose(g_up, (1, 0, 2)).reshape(d, 2 * D_FF)
    z, hr = rnn_forward(proj, rcw, rnn_conv_b, wa_b, b_rg_a, wi_b, b_rg_i, rg_lambda)
    y_attn = matmul(att, w_oa_f, "nn", BF16, "mm_o_attn", 1024, 1024, 2048)
    y_rnn = matmul(z, w_or_f, "nn", BF16, "mm_o_rnn", 1024, 1024, 2560)
    merged = merge_forward(proj, y_attn, y_rnn)
    h1, mo = matmul(merged, w_out_f, "nn", BF16, "mm_out", 512, 1024, 2048, res=xs, gate=gate1)
    u2 = prenorm(h1, norm2, scale2, shift2, "prenorm2")
    upp = matmul(u2, w_up_f, "nn", BF16, "mm_up", 1024, 1024, 2048)
    act, up_g, up_v = ffn_act_forward(upp, fcw, ffn_conv_b)
    h2, dn = matmul(act, w_down_f, "nn", BF16, "mm_down", 512, 1024, 6144, res=h1, gate=gate2)

    dh2, d_dn, loss_cols, d_norm_f, d_gate2 = loss_head(h2, tgt, dn, norm_f.reshape(1, d), gate2)
    loss = lax.psum(jnp.sum(loss_cols), ("x", "y", "c"))

    d_act = matmul(d_dn, w_down_f, "nt", BF16, "mm_down_dx", 1024, 1536, 2048)
    g_w_down = matmul(act, d_dn, "tn", BF16, "mm_down_dw", 1536, 1024, 2048)
    d_g, d_v, d_fcb_g, d_fcb_v = ffn_act_backward(d_act, up_g, up_v)
    d_upp, d_fcw = conv_backward(d_g, d_v, upp, fcw)
    d_fcb = jnp.concatenate([d_fcb_g, d_fcb_v], axis=1)
    d_u2, p_down = matmul(d_upp, w_up_f, "nt", BF16, "mm_up_dx", 1024, 1024, 3072,
                          comm=[(g_w_down.reshape(N_DEV, D_FF // N_DEV, d), True)])
    g_w_up = matmul(u2, d_upp, "tn", BF16, "mm_up_dw", 1024, 512, 8192)
    g_up_blk = jnp.transpose(g_w_up.reshape(d, N_DEV, 2 * D_FF // N_DEV), (1, 0, 2))
    dh1, d_shift2, d_scale2, d_norm2, d_mo, d_gate1 = norm_backward(d_u2, h1, dh2, norm2, scale2, "norm2_backward",
                                                                    mo=mo, gate=gate1)
    d_merged = matmul(d_mo, w_out_f, "nt", BF16, "mm_out_dx", 1024, 1024, 2048)
    g_w_out = matmul(merged, d_mo, "tn", BF16, "mm_out_dw", 2048, 1024, 2048)
    d_ga, d_gl, d_ya, d_yr = merge_backward(d_merged, proj, y_attn, y_rnn)
    d_att = matmul(d_ya, w_oa_f, "nt", BF16, "mm_o_attn_dx", 1024, 1024, 2048)
    g_w_oa = matmul(att, d_ya, "tn", BF16, "mm_o_attn_dw", 2048, 1024, 2048)
    d_z = matmul(d_yr, w_or_f, "nt", BF16, "mm_o_rnn_dx", 1024, 1280, 2048)
    g_w_or = matmul(z, d_yr, "tn", BF16, "mm_o_rnn_dw", 1280, 1024, 2048)
    d_xr, d_gr, d_wa, d_wi, d_rsmall, p_up = rnn_backward(proj, hr, d_z, rcw, rnn_conv_b, wa_b, b_rg_a, wi_b, b_rg_i,
                                                          rg_lambda, [(g_up_blk, True)])
    d_q, d_k, d_v_, d_bias, d_sink, p_out, p_oa, p_or = attention_backward(
        proj, d_att, bias, sinks,
        [(g_w_out.reshape(N_DEV, d // N_DEV, d), True), (g_w_oa.reshape(N_DEV, D_ATTN // N_DEV, d), True),
         (g_w_or.reshape(N_DEV, D_RNN // N_DEV, d), True)])
    d_rel = rel_bias_grad(d_bias)
    d_proj = jnp.concatenate([d_q, d_k, d_v_, d_xr, d_gr, d_ga, d_gl], axis=1)

    def rsmall_of(ba_, bi_, lam_, cb_):
        return jnp.stack([ba_[0].reshape(N_RNN_BLOCKS, LANES), bi_[0].reshape(N_RNN_BLOCKS, LANES),
                          lam_[0].reshape(N_RNN_BLOCKS, LANES), cb_[0].reshape(N_RNN_BLOCKS, LANES)]
                         + [jnp.zeros((N_RNN_BLOCKS, LANES), F32)] * 4, axis=1)

    pack = lambda t: jnp.concatenate([_rows128(q) for q in t], axis=0)
    late_n = 2
    g_early = [d_norm2, d_norm_f, d_rsmall, d_wa, d_wi, d_sink[:, 0], d_rel[:, :N_HEADS], d_fcb, d_fcw]

    g_w_in, small_early = matmul(u, d_proj, "tn", BF16, "mm_in_dw", 1024, 512, 8192, comm=[(pack(g_early), False)])
    g_in_blk = jnp.transpose(g_w_in.reshape(d, N_DEV, D_IN // N_DEV), (1, 0, 2))
    d_u, p_in = matmul(d_proj, w_in_f, "nt", BF16, "mm_in_dx", 1024, 1024, 2944, comm=[(g_in_blk, True)])
    grad_x, d_shift1, d_scale1, d_norm1 = norm_backward(d_u, xs, dh1, norm1, scale1, "norm1_backward")
    d_mod = jnp.concatenate([d_shift1, d_scale1, d_gate1, d_shift2, d_scale2, d_gate2], axis=1)
    small_late = exchange([(pack([d_mod, d_norm1]), False)], "gather_late_grads")[0]

    g_list = [d_mod, d_norm1] + g_early
    w_list = [b_ada, norm1, norm2, norm_f, rsmall_of(b_rg_a, b_rg_i, rg_lambda, rnn_conv_b), w_rg_a, w_rg_i,
              attn_sinks, rel_bias, ffn_conv_b, jnp.zeros_like(d_fcw)]
    m_list = [m_b_ada, m_norm1, m_norm2, m_norm_f, rsmall_of(m_b_rg_a, m_b_rg_i, m_rg_lambda, m_rnn_conv_b), m_w_rg_a,
              m_w_rg_i, m_attn_sinks, m_rel_bias, m_ffn_conv_b, jnp.zeros_like(d_fcw)]
    v_list = [v_b_ada, v_norm1, v_norm2, v_norm_f, rsmall_of(v_b_rg_a, v_b_rg_i, v_rg_lambda, v_rnn_conv_b), v_w_rg_a,
              v_w_rg_i, v_attn_sinks, v_rel_bias, v_ffn_conv_b, jnp.ones_like(d_fcw)]
    sizes = [_rows128(q).shape[0] for q in g_list]
    offs = np.concatenate([[0], np.cumsum(sizes)]).tolist()
    r_late = offs[late_n]
    late = adamw(small_late, pack(w_list[:late_n]), pack(m_list[:late_n]), pack(v_list[:late_n]), "adamw_small_late",
                 rows=r_late)
    early = adamw(small_early, pack(w_list[late_n:]), pack(m_list[late_n:]), pack(v_list[late_n:]), "adamw_small_early",
                  rows=(offs[-1] - r_late) // 7)

    def seg(packed, idx, like):
        n_el = int(np.prod(like.shape))
        return packed[offs[idx]:offs[idx + 1]].reshape(-1)[:n_el].reshape(like.shape)

    def unpack(kind):
        packed = jnp.concatenate([late[kind], early[kind]], axis=0)
        rs = seg(packed, 4, d_rsmall)
        out = dict(
            b_ada=seg(packed, 0, b_ada), norm1=seg(packed, 1, norm1), norm2=seg(packed, 2, norm2),
            norm_f=seg(packed, 3, norm_f), b_rg_a=rs[:, 0].reshape(1, D_RNN), b_rg_i=rs[:, 1].reshape(1, D_RNN),
            rg_lambda=rs[:, 2].reshape(1, D_RNN), rnn_conv_b=rs[:, 3].reshape(1, D_RNN),
            w_rg_a=seg(packed, 5, w_rg_a), w_rg_i=seg(packed, 6, w_rg_i), attn_sinks=seg(packed, 7, attn_sinks),
            rel_bias=seg(packed, 8, rel_bias), ffn_conv_b=seg(packed, 9, ffn_conv_b))
        out["rnn_conv_w_full"] = jnp.transpose(rs[:, 4:8], (1, 0, 2)).reshape(RNN_CONV, D_RNN)
        out["ffn_conv_w_full"] = seg(packed, 10, d_fcw)
        return out

    small = [unpack(kind) for kind in range(4)]

    rcw_cols = D_RNN // N_DEV
    fcw_cols = 2 * D_FF // N_DEV
    g_rcw_ = lax.dynamic_slice(small[0]["rnn_conv_w_full"], (0, me * rcw_cols), (RNN_CONV, rcw_cols))
    g_fcw_ = lax.dynamic_slice(small[0]["ffn_conv_w_full"], (0, me * fcw_cols), (FFN_CONV, fcw_cols))
    r_rcw = adamw(g_rcw_[None], rnn_conv_w[0], m_rnn_conv_w[0], v_rnn_conv_w[0], "adamw_rnn_conv_w")
    r_fcw = adamw(g_fcw_[None], ffn_conv_w[0], m_ffn_conv_w[0], v_ffn_conv_w[0], "adamw_ffn_conv_w")

    ada_cols = 6 * d // N_DEV
    dmod_all = small_late[:, offs[0]:offs[1]].reshape(N_DEV, 6 * d)
    dmod_cols = lax.dynamic_slice(dmod_all, (0, me * ada_cols), (N_DEV, ada_cols))
    g_ada = ada_weight_grad(jnp.transpose(c_all.reshape(N_DEV, d)), dmod_cols)
    r_ada = adamw(g_ada[None], w_ada[0], m_w_ada[0], v_w_ada[0], "adamw_w_ada")

    r_in = adamw(p_in, w_in[0], m_w_in[0], v_w_in[0], "adamw_w_in")
    r_up = adamw(p_up, w_up[0], m_w_up[0], v_w_up[0], "adamw_w_up")
    r_or = adamw(p_or, w_o_rnn[0], m_w_o_rnn[0], v_w_o_rnn[0], "adamw_w_o_rnn", rows=160)
    r_oa = adamw(p_oa, w_o_attn[0], m_w_o_attn[0], v_w_o_attn[0], "adamw_w_o_attn")
    r_out = adamw(p_out, w_out[0], m_w_out[0], v_w_out[0], "adamw_w_out")
    r_down = adamw(p_down, w_down[0], m_w_down[0], v_w_down[0], "adamw_w_down")

    def res(kind):
        sm = small[kind]
        return [r_ada[kind][None], sm["b_ada"], sm["norm1"], r_in[kind][None], r_rcw[kind][None], sm["rnn_conv_b"],
                sm["w_rg_a"], sm["b_rg_a"], sm["w_rg_i"], sm["b_rg_i"], sm["rg_lambda"], r_or[kind][None],
                r_oa[kind][None], sm["attn_sinks"], sm["rel_bias"], r_out[kind][None], sm["norm2"], r_up[kind][None],
                r_fcw[kind][None], sm["ffn_conv_b"], r_down[kind][None], sm["norm_f"]]

    return (loss, grad_x[None], *res(0), *res(1), *res(2), *res(3))
```

```python
import functools
import math

import numpy as np
import jax
import jax.numpy as jnp
from jax import lax
from jax.experimental import pallas as pl
from jax.experimental.pallas import tpu as pltpu

F32, BF16 = jnp.float32, jnp.bfloat16

N_DEV = 8
D_MODEL = 2048
N_HEADS, HEAD_DIM, N_KV = 32, 64, 4
GROUP = N_HEADS // N_KV
D_ATTN, D_KV = N_HEADS * HEAD_DIM, N_KV * HEAD_DIM
BLOCK = 128
NUM_BUCKETS, MAX_DISTANCE = 32, 128
D_RNN, N_RNN_BLOCKS, RNN_BLOCK = 2560, 20, 128
RNN_CONV, FFN_CONV = 4, 3
RG_C = 8.0
D_FF = 3 * D_MODEL
D_IN = D_ATTN + 2 * D_KV + 2 * D_RNN + 2 * D_MODEL
EPS = 1e-6
NEG_INF = -1e30
ADAM_LR, ADAM_B1, ADAM_B2, ADAM_EPS, ADAM_WD, ADAM_STEP = 0.001, 0.9, 0.999, 1e-08, 0.01, 10

LANES = 128
HALO = 16
VMEM_LIMIT = 56 * 1024 * 1024
MESH = pl.DeviceIdType.MESH
RELAY_AT_NUM, RELAY_AT_DEN = 2, 3

Q0, K0, V0, XR0, GR0, GA0, GL0 = 0, 2048, 2304, 2560, 5120, 7680, 9728


def _cparams(n_axes):
    return pltpu.CompilerParams(dimension_semantics=("arbitrary",) * n_axes, vmem_limit_bytes=VMEM_LIMIT)


def _gelu(x):
    k = math.sqrt(2.0 / math.pi)
    return 0.5 * x * (1.0 + jnp.tanh(k * (x + 0.044715 * x * x * x)))


def _gelu_and_grad(x):
    k = math.sqrt(2.0 / math.pi)
    t = jnp.tanh(k * (x + 0.044715 * x * x * x))
    g = 0.5 * x * (1.0 + t)
    dg = 0.5 * (1.0 + t) + 0.5 * x * (1.0 - t * t) * k * (1.0 + 3.0 * 0.044715 * x * x)
    return g, dg


def _sigmoid(x):
    return 1.0 / (1.0 + jnp.exp(-x))


def _shift_down(prev, x, j):
    if j == 0:
        return x
    xe = jnp.concatenate([prev, x], axis=0)
    return pltpu.roll(xe, j, axis=0)[HALO:, :]


def _shift_up(x, nxt, j):
    if j == 0:
        return x
    xe = jnp.concatenate([x, nxt], axis=0)
    n = xe.shape[0]
    return pltpu.roll(xe, n - j, axis=0)[: x.shape[0], :]


def _my_coords():
    return lax.axis_index("x"), lax.axis_index("y"), lax.axis_index("c")


def _peer(x, y, c, k):
    kx, ky, kc = (k >> 2) & 1, (k >> 1) & 1, k & 1
    px, py, pc = (x + kx) % 2, (y + ky) % 2, (c + kc) % 2
    return (px, py, pc), 4 * px + 2 * py + pc


def _exchange_shapes(items):
    return [jax.ShapeDtypeStruct((N_DEV,) + tuple(arr.shape[1:] if s else arr.shape), arr.dtype) for arr, s in items]


def _exchange_sems(n):
    return [pltpu.SemaphoreType.DMA((n, N_DEV - 1)), pltpu.SemaphoreType.DMA((n, N_DEV - 1)),
            pltpu.SemaphoreType.DMA((n,))]


def _exchange_copies(srcs, dsts, scat, send_sems, recv_sems, loc_sems):
    x, y, c = _my_coords()
    me = 4 * x + 2 * y + c
    copies = []
    for a in range(len(srcs)):
        mine = srcs[a].at[me] if scat[a] else srcs[a]
        copies.append(pltpu.make_async_copy(mine, dsts[a].at[me], loc_sems.at[a]))
    for k in range(1, N_DEV):
        peer, p = _peer(x, y, c, k)
        for a in range(len(srcs)):
            src = srcs[a].at[p] if scat[a] else srcs[a]
            copies.append(pltpu.make_async_remote_copy(
                src_ref=src, dst_ref=dsts[a].at[me], send_sem=send_sems.at[a, k - 1],
                recv_sem=recv_sems.at[a, k - 1], device_id=peer, device_id_type=MESH))
    return copies


def _exchange_steps(phase, srcs, dsts, scat, send_sems, recv_sems, loc_sems):
    if any(scat):
        if phase == 1:
            return []
        copies = _exchange_copies(srcs, dsts, scat, send_sems, recv_sems, loc_sems)
        return [cp.start if phase == 0 else cp.wait for cp in copies]
    x, y, c = _my_coords()
    me = 4 * x + 2 * y + c
    sibling, _ = _peer(x, y, c, 1)

    def remote(a, src, dst, k, to):
        return pltpu.make_async_remote_copy(src_ref=src, dst_ref=dst, send_sem=send_sems.at[a, k - 1],
                                            recv_sem=recv_sems.at[a, k - 1], device_id=to, device_id_type=MESH)

    acts = []
    for a in range(len(srcs)):
        if phase != 1:
            loc = pltpu.make_async_copy(srcs[a], dsts[a].at[me], loc_sems.at[a])
            sib = remote(a, srcs[a], dsts[a].at[me], 1, sibling)
            acts += [loc.start, sib.start] if phase == 0 else [loc.wait, sib.wait]
        for k in (2, 4, 6):
            peer, p = _peer(x, y, c, k)
            out = remote(a, srcs[a], dsts[a].at[me], k, peer)
            if phase == 0:
                acts.append(out.start)
            else:
                onward = remote(a, dsts[a].at[p], dsts[a].at[p], k + 1, sibling)
                acts += [out.wait_recv, onward.start] if phase == 1 else [out.wait_send, onward.wait]
    return acts


def exchange(items, name):
    n = len(items)
    scat = [s for _, s in items]

    def body(*refs):
        for phase in range(3):
            for act in _exchange_steps(phase, refs[:n], refs[n:2 * n], scat, *refs[2 * n:]):
                act()

    any_spec = pl.BlockSpec(memory_space=pl.ANY)
    return pl.pallas_call(
        body, name=name, out_shape=tuple(_exchange_shapes(items)),
        in_specs=[any_spec] * n, out_specs=tuple([any_spec] * n), scratch_shapes=_exchange_sems(n),
    )(*[a for a, _ in items])


def mod_forward(c, w_ada, b_ada):
    d, ncol = w_ada.shape

    def body(c_ref, w_ref, b_ref, mod_ref, call_ref, cols_ref, s1, r1, s2, r2):
        x, y, c_ = _my_coords()
        me = 4 * x + 2 * y + c_
        call_ref[me] = c_ref[0]
        sends = []
        for k in range(1, N_DEV):
            peer, p = _peer(x, y, c_, k)
            cp = pltpu.make_async_remote_copy(src_ref=c_ref.at[0], dst_ref=call_ref.at[me], send_sem=s1.at[k - 1],
                                              recv_sem=r1.at[k - 1], device_id=peer, device_id_type=MESH)
            cp.start()
            sends.append(cp)
        for cp in sends:
            cp.wait()
        rows = lax.broadcasted_iota(jnp.int32, (N_DEV, d), 0)
        cmat = jnp.zeros((N_DEV, d), F32)
        for b in range(N_DEV):
            cmat = jnp.where(rows == b, call_ref[b], cmat)
        cs = cmat * _sigmoid(cmat)
        bias = b_ref[:, pl.ds(pl.multiple_of(me * ncol, LANES), ncol)]
        mc = jnp.dot(cs, w_ref[...], preferred_element_type=F32, precision=lax.Precision.HIGHEST) + bias
        for b in range(N_DEV):
            cols_ref[b] = mc[b:b + 1, :]
        mod_ref[me] = cols_ref[me]
        sends = []
        for k in range(1, N_DEV):
            peer, p = _peer(x, y, c_, k)
            cp = pltpu.make_async_remote_copy(src_ref=cols_ref.at[p], dst_ref=mod_ref.at[me], send_sem=s2.at[k - 1],
                                              recv_sem=r2.at[k - 1], device_id=peer, device_id_type=MESH)
            cp.start()
            sends.append(cp)
        for cp in sends:
            cp.wait()

    vm = pl.BlockSpec(memory_space=pltpu.VMEM)
    return pl.pallas_call(
        body, name="mod_forward",
        out_shape=(jax.ShapeDtypeStruct((N_DEV, 1, ncol), F32), jax.ShapeDtypeStruct((N_DEV, 1, d), F32)),
        in_specs=[vm, vm, vm], out_specs=(vm, vm),
        scratch_shapes=[pltpu.VMEM((N_DEV, 1, ncol), F32)] + [pltpu.SemaphoreType.DMA((N_DEV - 1,))] * 4,
        compiler_params=pltpu.CompilerParams(vmem_limit_bytes=VMEM_LIMIT),
    )(c, w_ada, b_ada)


def matmul(a, b, mode, out_dtype, name, tm, tn, tk, res=None, gate=None, comm=None):
    if mode == "nn":
        (m, kk), (_, n) = a.shape, b.shape
    elif mode == "nt":
        (m, kk), (n, _) = a.shape, b.shape
    else:
        (kk, m), (_, n) = a.shape, b.shape
    tm, tn, tk = min(tm, m), min(tn, n), min(tk, kk)
    assert m % tm == 0 and n % tn == 0 and kk % tk == 0, (name, m, n, kk, tm, tn, tk)
    if mode == "nn":
        a_spec = pl.BlockSpec((tm, tk), lambda j, i, k: (i, k))
        b_spec = pl.BlockSpec((tk, tn), lambda j, i, k: (k, j))
        dims = (((1,), (0,)), ((), ()))
    elif mode == "nt":
        a_spec = pl.BlockSpec((tm, tk), lambda j, i, k: (i, k))
        b_spec = pl.BlockSpec((tn, tk), lambda j, i, k: (j, k))
        dims = (((1,), (1,)), ((), ()))
    else:
        a_spec = pl.BlockSpec((tk, tm), lambda j, i, k: (k, i))
        b_spec = pl.BlockSpec((tk, tn), lambda j, i, k: (k, j))
        dims = (((0,), (0,)), ((), ()))
    nj, ni, nk = n // tn, m // tm, kk // tk
    fused = res is not None
    items = list(comm or [])
    nc = len(items)
    scat = [s_ for _, s_ in items]
    n_in = (4 if fused else 2) + nc
    n_out = (2 if fused else 1) + nc
    relay_step = (RELAY_AT_NUM * nj * ni * nk) // RELAY_AT_DEN
    o_spec = pl.BlockSpec((tm, tn), lambda j, i, k: (i, j))

    def body(*refs):
        ins, outs, scratch = refs[:n_in], refs[n_in:n_in + n_out], refs[n_in + n_out:]
        a_ref, b_ref = ins[:2]
        o_ref = outs[0]
        acc_ref = scratch[0] if nk > 1 else None
        j, i, k = pl.program_id(0), pl.program_id(1), pl.program_id(2)

        def run(phase):
            for act in _exchange_steps(phase, ins[n_in - nc:], outs[n_out - nc:], scat, *scratch[len(scratch) - 3:]):
                act()

        step = (j * ni + i) * nk + k
        if nc:
            @pl.when(step == 0)
            def _():
                run(0)

            @pl.when(step == relay_step)
            def _():
                run(1)

        def finish(acc):
            if fused:
                o_ref[...] = ins[2][...] + ins[3][...] * acc
                outs[1][...] = acc.astype(outs[1].dtype)
            else:
                o_ref[...] = acc.astype(o_ref.dtype)

        prod = lax.dot_general(a_ref[...], b_ref[...], dims, preferred_element_type=F32)
        if nk == 1:
            finish(prod)
        else:
            @pl.when(k == 0)
            def _():
                acc_ref[...] = prod

            @pl.when(k > 0)
            def _():
                acc_ref[...] += prod

            @pl.when(k == nk - 1)
            def _():
                finish(acc_ref[...])

        if nc:
            @pl.when(step == nj * ni * nk - 1)
            def _():
                run(2)

    any_spec = pl.BlockSpec(memory_space=pl.ANY)
    in_specs, args = [a_spec, b_spec], [a, b]
    if fused:
        in_specs += [o_spec, pl.BlockSpec((1, tn), lambda j, i, k: (0, j))]
        args += [res, gate]
        out_shape = [jax.ShapeDtypeStruct((m, n), F32), jax.ShapeDtypeStruct((m, n), out_dtype)]
        out_specs = [o_spec, o_spec]
    else:
        out_shape = [jax.ShapeDtypeStruct((m, n), out_dtype)]
        out_specs = [o_spec]
    in_specs += [any_spec] * nc
    args += [arr for arr, _ in items]
    out_shape += _exchange_shapes(items)
    out_specs += [any_spec] * nc
    scratch = ([pltpu.VMEM((tm, tn), F32)] if nk > 1 else []) + (_exchange_sems(nc) if nc else [])
    outs = pl.pallas_call(
        body, name=name, grid=(nj, ni, nk), in_specs=in_specs, out_specs=tuple(out_specs), out_shape=tuple(out_shape),
        scratch_shapes=scratch, compiler_params=_cparams(3),
    )(*args)
    return outs[0] if len(outs) == 1 else outs


def _row_tile(s, want):
    t = min(want, s)
    assert s % t == 0 and t % HALO == 0
    return t


def prenorm(x, nw, scale, shift, name):
    s, d = x.shape
    tm = _row_tile(s, 512)

    def body(x_ref, nw_ref, sc_ref, sh_ref, o_ref):
        xv = x_ref[...]
        r = lax.rsqrt(jnp.mean(xv * xv, axis=-1, keepdims=True) + EPS)
        o_ref[...] = ((xv * r) * nw_ref[...] * (1.0 + sc_ref[...]) + sh_ref[...]).astype(BF16)

    row = pl.BlockSpec((tm, d), lambda i: (i, 0))
    vec = pl.BlockSpec((1, d), lambda i: (0, 0))
    return pl.pallas_call(body, name=name, grid=(s // tm,), in_specs=[row, vec, vec, vec], out_specs=row,
                          out_shape=jax.ShapeDtypeStruct((s, d), BF16), compiler_params=_cparams(1))(x, nw, scale, shift)


def norm_backward(du, xin, dres, nw, scale, name, mo=None, gate=None):
    s, d = xin.shape
    tm = _row_tile(s, 256)
    gated = mo is not None

    def body(*refs):
        if gated:
            du_ref, x_ref, dr_ref, nw_ref, sc_ref, mo_ref, g_ref, dx_ref, dsh_ref, dsc_ref, dnw_ref, dmo_ref, dg_ref = refs
        else:
            du_ref, x_ref, dr_ref, nw_ref, sc_ref, dx_ref, dsh_ref, dsc_ref, dnw_ref = refs
        i = pl.program_id(0)
        xv = x_ref[...]
        r = lax.rsqrt(jnp.mean(xv * xv, axis=-1, keepdims=True) + EPS)
        xn = xv * r
        duv = du_ref[...].astype(F32)
        nwv, scv = nw_ref[...], sc_ref[...]
        dxn = duv * (nwv * (1.0 + scv))
        dx = dr_ref[...] + r * (dxn - xn * jnp.mean(dxn * xn, axis=-1, keepdims=True))
        dx_ref[...] = dx
        sums = [jnp.sum(duv, axis=0, keepdims=True), jnp.sum(duv * xn * nwv, axis=0, keepdims=True),
                jnp.sum(duv * xn * (1.0 + scv), axis=0, keepdims=True)]
        accs = [dsh_ref, dsc_ref, dnw_ref]
        if gated:
            dmo_ref[...] = (dx * g_ref[...]).astype(BF16)
            sums.append(jnp.sum(dx * mo_ref[...].astype(F32), axis=0, keepdims=True))
            accs.append(dg_ref)

        @pl.when(i == 0)
        def _():
            for acc, sm in zip(accs, sums):
                acc[...] = sm

        @pl.when(i > 0)
        def _():
            for acc, sm in zip(accs, sums):
                acc[...] += sm

    row = pl.BlockSpec((tm, d), lambda i: (i, 0))
    vec = pl.BlockSpec((1, d), lambda i: (0, 0))
    vshape = jax.ShapeDtypeStruct((1, d), F32)
    in_specs, args = [row, row, row, vec, vec], [du, xin, dres, nw, scale]
    out_specs, out_shape = [row, vec, vec, vec], [jax.ShapeDtypeStruct((s, d), F32), vshape, vshape, vshape]
    if gated:
        in_specs += [row, vec]
        args += [mo, gate]
        out_specs += [row, vec]
        out_shape += [jax.ShapeDtypeStruct((s, d), BF16), vshape]
    return pl.pallas_call(body, name=name, grid=(s // tm,), in_specs=in_specs, out_specs=tuple(out_specs),
                          out_shape=tuple(out_shape), compiler_params=_cparams(1))(*args)


def merge_forward(proj, y_attn, y_rnn):
    s, d = y_attn.shape
    tm, cw = _row_tile(s, 1024), 512

    def body(ga_ref, gl_ref, ya_ref, yr_ref, o_ref):
        o_ref[...] = (_sigmoid(ga_ref[...].astype(F32)) * ya_ref[...].astype(F32)
                      + _sigmoid(gl_ref[...].astype(F32)) * yr_ref[...].astype(F32)).astype(BF16)

    def at(off):
        return pl.BlockSpec((tm, cw), lambda j, i: (i, off // cw + j))

    return pl.pallas_call(body, name="merge_forward", grid=(d // cw, s // tm),
                          in_specs=[at(GA0), at(GL0), at(0), at(0)], out_specs=at(0),
                          out_shape=jax.ShapeDtypeStruct((s, d), BF16), compiler_params=_cparams(2))(proj, proj, y_attn, y_rnn)


def merge_backward(dmerged, proj, y_attn, y_rnn):
    s, d = y_attn.shape
    tm, cw = _row_tile(s, 1024), 512

    def body(dm_ref, ga_ref, gl_ref, ya_ref, yr_ref, dga_ref, dgl_ref, dya_ref, dyr_ref):
        dm = dm_ref[...].astype(F32)
        sa, sl = _sigmoid(ga_ref[...].astype(F32)), _sigmoid(gl_ref[...].astype(F32))
        dga_ref[...] = (dm * ya_ref[...].astype(F32) * sa * (1.0 - sa)).astype(BF16)
        dgl_ref[...] = (dm * yr_ref[...].astype(F32) * sl * (1.0 - sl)).astype(BF16)
        dya_ref[...] = (dm * sa).astype(BF16)
        dyr_ref[...] = (dm * sl).astype(BF16)

    def at(off):
        return pl.BlockSpec((tm, cw), lambda j, i: (i, off // cw + j))

    o = jax.ShapeDtypeStruct((s, d), BF16)
    return pl.pallas_call(body, name="merge_backward", grid=(d // cw, s // tm),
                          in_specs=[at(0), at(GA0), at(GL0), at(0), at(0)], out_specs=(at(0),) * 4,
                          out_shape=(o, o, o, o), compiler_params=_cparams(2))(dmerged, proj, proj, y_attn, y_rnn)


def _prev_spec(tm, cw, off_blocks):
    r = tm // HALO
    return pl.BlockSpec((HALO, cw), lambda j, i: (jnp.maximum(i * r - 1, 0), off_blocks + j))


def ffn_act_forward(upp, cw_full, cb_full):
    s, f2 = upp.shape
    f = f2 // 2
    tm, cw = _row_tile(s, 512), 1536
    nj = f // cw

    def body(g_ref, gp_ref, v_ref, vp_ref, wg_ref, wv_ref, bg_ref, bv_ref, o_ref, og_ref, ov_ref):
        i = pl.program_id(1)
        first = i == 0

        def conv(x_ref, p_ref, w_ref, b_ref):
            xv = x_ref[...].astype(F32)
            pv = jnp.where(first, 0.0, p_ref[...].astype(F32))
            acc = b_ref[...] + w_ref[FFN_CONV - 1:FFN_CONV, :] * xv
            for k in range(FFN_CONV - 1):
                acc = acc + w_ref[k:k + 1, :] * _shift_down(pv, xv, FFN_CONV - 1 - k)
            return acc

        g = conv(g_ref, gp_ref, wg_ref, bg_ref)
        v = conv(v_ref, vp_ref, wv_ref, bv_ref)
        og_ref[...] = g.astype(BF16)
        ov_ref[...] = v.astype(BF16)
        o_ref[...] = (_gelu(g) * v).astype(BF16)

    def tile(ob):
        return pl.BlockSpec((tm, cw), lambda j, i: (i, ob + j))

    def par(rows, ob):
        return pl.BlockSpec((rows, cw), lambda j, i: (0, ob + j))

    o = jax.ShapeDtypeStruct((s, f), BF16)
    return pl.pallas_call(
        body, name="ffn_act_forward", grid=(nj, s // tm),
        in_specs=[tile(0), _prev_spec(tm, cw, 0), tile(nj), _prev_spec(tm, cw, nj),
                  par(FFN_CONV, 0), par(FFN_CONV, nj), par(1, 0), par(1, nj)],
        out_specs=(tile(0), tile(0), tile(0)), out_shape=(o, o, o), compiler_params=_cparams(2),
    )(upp, upp, upp, upp, cw_full, cw_full, cb_full, cb_full)


ROW_CHUNK = 16
LANE_CHUNK = 512


def ffn_act_backward(dact, up_g, up_v):
    s, f = dact.shape
    tm, cw = _row_tile(s, 512), 1536
    nr = tm // ROW_CHUNK

    def body(da_ref, g_ref, v_ref, dg_ref, dv_ref, sg_ref, sv_ref, acc_ref):
        i = pl.program_id(1)
        acc_ref[...] = jnp.zeros_like(acc_ref)

        def chunk(r, carry):
            rows = pl.ds(pl.multiple_of(r * ROW_CHUNK, ROW_CHUNK), ROW_CHUNK)
            for c0 in range(0, cw, LANE_CHUNK):
                cols = pl.ds(c0, LANE_CHUNK)
                da = da_ref[rows, cols].astype(F32)
                ge, dge = _gelu_and_grad(g_ref[rows, cols].astype(F32))
                dg = da * v_ref[rows, cols].astype(F32) * dge
                dv = da * ge
                dg_ref[rows, cols] = dg.astype(BF16)
                dv_ref[rows, cols] = dv.astype(BF16)
                acc_ref[0, :, cols] += dg[0:8] + dg[8:16]
                acc_ref[1, :, cols] += dv[0:8] + dv[8:16]
            return carry

        lax.fori_loop(0, nr, chunk, 0)
        sg = jnp.sum(acc_ref[0], axis=0, keepdims=True)
        sv = jnp.sum(acc_ref[1], axis=0, keepdims=True)

        @pl.when(i == 0)
        def _():
            sg_ref[...] = sg
            sv_ref[...] = sv

        @pl.when(i > 0)
        def _():
            sg_ref[...] += sg
            sv_ref[...] += sv

    tile = pl.BlockSpec((tm, cw), lambda j, i: (i, j))
    vec = pl.BlockSpec((1, cw), lambda j, i: (0, j))
    o = jax.ShapeDtypeStruct((s, f), BF16)
    v1 = jax.ShapeDtypeStruct((1, f), F32)
    return pl.pallas_call(
        body, name="ffn_act_backward", grid=(f // cw, s // tm), in_specs=[tile, tile, tile],
        out_specs=(tile, tile, vec, vec), out_shape=(o, o, v1, v1),
        scratch_shapes=[pltpu.VMEM((2, 8, cw), F32)], compiler_params=_cparams(2),
    )(dact, up_g, up_v)


def conv_backward(d_g, d_v, upp, cw_full):
    s, f = d_g.shape
    f2 = 2 * f
    tm, cw = _row_tile(s, 512), 1536
    nt, nj = s // tm, f // cw
    r_halo = tm // HALO
    nr = tm // ROW_CHUNK

    def body(xg_ref, ng_ref, xv_ref, nv_ref, u_ref, w_ref, o_ref, dw_ref, acc_ref):
        j, i = pl.program_id(0), pl.program_id(1)
        acc_ref[...] = jnp.zeros_like(acc_ref)

        def run(x_ref, n_ref):
            def chunk(r, carry):
                rows = pl.ds(pl.multiple_of(r * ROW_CHUNK, ROW_CHUNK), ROW_CHUNK)
                nrows = pl.ds(pl.multiple_of(jnp.minimum(r + 1, nr - 1) * ROW_CHUNK, ROW_CHUNK), ROW_CHUNK)
                for c0 in range(0, cw, LANE_CHUNK):
                    cols = pl.ds(c0, LANE_CHUNK)
                    cur = x_ref[rows, cols].astype(F32)
                    halo = jnp.where(i == nt - 1, 0.0, n_ref[:, cols].astype(F32))
                    nxt = jnp.where(r == nr - 1, halo, x_ref[nrows, cols].astype(F32))
                    uv = u_ref[rows, cols].astype(F32)
                    acc = None
                    for k in range(FFN_CONV):
                        sh = _shift_up(cur, nxt, FFN_CONV - 1 - k)
                        term = w_ref[k:k + 1, cols] * sh
                        acc = term if acc is None else acc + term
                        pr = uv * sh
                        acc_ref[k, :, cols] += pr[0:8] + pr[8:16]
                    o_ref[rows, cols] = acc.astype(BF16)
                return carry

            lax.fori_loop(0, nr, chunk, 0)

        @pl.when(j < nj)
        def _():
            run(xg_ref, ng_ref)

        @pl.when(j >= nj)
        def _():
            run(xv_ref, nv_ref)

        sums = jnp.concatenate([jnp.sum(acc_ref[k], axis=0, keepdims=True) for k in range(FFN_CONV)], axis=0)

        @pl.when(i == 0)
        def _():
            dw_ref[...] = sums

        @pl.when(i > 0)
        def _():
            dw_ref[...] += sums

    def tile_h(used):
        return pl.BlockSpec((tm, cw), lambda j, i: (jnp.where(used(j), i, 0), jnp.where(used(j), j % nj, 0)))

    def next_h(used):
        return pl.BlockSpec((HALO, cw), lambda j, i: (
            jnp.where(used(j), jnp.minimum((i + 1) * r_halo, s // HALO - 1), 0), jnp.where(used(j), j % nj, 0)))

    is_g = lambda j: j < nj
    is_v = lambda j: j >= nj
    tile = pl.BlockSpec((tm, cw), lambda j, i: (i, j))
    par = pl.BlockSpec((FFN_CONV, cw), lambda j, i: (0, j))
    return pl.pallas_call(
        body, name="ffn_conv_backward", grid=(2 * nj, nt),
        in_specs=[tile_h(is_g), next_h(is_g), tile_h(is_v), next_h(is_v), tile, par],
        out_specs=(tile, par), out_shape=(jax.ShapeDtypeStruct((s, f2), BF16), jax.ShapeDtypeStruct((FFN_CONV, f2), F32)),
        scratch_shapes=[pltpu.VMEM((FFN_CONV, 8, cw), F32)], compiler_params=_cparams(2),
    )(d_g, d_g, d_v, d_v, upp, cw_full)


def loss_head(h2, target, dn, norm_f, gate2):
    s, d = h2.shape
    tm = _row_tile(s, 256)

    def body(h_ref, t_ref, dn_ref, nf_ref, g_ref, dh_ref, ddn_ref, loss_ref, dnf_ref, dg_ref):
        i = pl.program_id(0)
        hv = h_ref[...]
        r = lax.rsqrt(jnp.mean(hv * hv, axis=-1, keepdims=True) + EPS)
        yh = hv * r
        nf = nf_ref[...]
        err = yh * nf - t_ref[...]
        dy = err * (1.0 / d)
        dyh = dy * nf
        dh = r * (dyh - yh * jnp.mean(dyh * yh, axis=-1, keepdims=True))
        dh_ref[...] = dh
        ddn_ref[...] = (dh * g_ref[...]).astype(BF16)
        sums = [jnp.sum(err * err, axis=0, keepdims=True) * (0.5 / d), jnp.sum(dy * yh, axis=0, keepdims=True),
                jnp.sum(dh * dn_ref[...].astype(F32), axis=0, keepdims=True)]
        accs = [loss_ref, dnf_ref, dg_ref]

        @pl.when(i == 0)
        def _():
            for acc, sm in zip(accs, sums):
                acc[...] = sm

        @pl.when(i > 0)
        def _():
            for acc, sm in zip(accs, sums):
                acc[...] += sm

    row = pl.BlockSpec((tm, d), lambda i: (i, 0))
    vec = pl.BlockSpec((1, d), lambda i: (0, 0))
    v = jax.ShapeDtypeStruct((1, d), F32)
    return pl.pallas_call(
        body, name="loss_head", grid=(s // tm,), in_specs=[row, row, row, vec, vec], out_specs=(row, row, vec, vec, vec),
        out_shape=(jax.ShapeDtypeStruct((s, d), F32), jax.ShapeDtypeStruct((s, d), BF16), v, v, v),
        compiler_params=_cparams(1))(h2, target, dn, norm_f, gate2)


def _t5_buckets():
    qi = np.arange(BLOCK)[:, None]
    kj = np.arange(2 * BLOCK)[None, :]
    dist = qi + BLOCK - kj
    dd = np.maximum(dist, 0)
    max_exact = NUM_BUCKETS // 2
    dflt = np.maximum(dd, 1).astype(np.float32)
    large = max_exact + (np.log(dflt / max_exact) / math.log(MAX_DISTANCE / max_exact)
                         * (NUM_BUCKETS - max_exact)).astype(np.int32)
    large = np.minimum(large, NUM_BUCKETS - 1)
    bucket = np.where(dd < max_exact, dd, large).astype(np.int32)
    in_window = (dist >= 0) & (dist < BLOCK)
    return bucket, in_window


def band_bias(rel_bias):
    bucket, in_window = _t5_buckets()
    bucket_t = jnp.asarray(np.where(in_window, bucket, -1).astype(np.int32).T)

    def body(rb_ref, bk_ref, o_ref):
        bk = bk_ref[...]
        for h in range(N_HEADS):
            acc = jnp.full((2 * BLOCK, BLOCK), NEG_INF, F32)
            for b in range(NUM_BUCKETS):
                acc = jnp.where(bk == b, rb_ref[b, h], acc)
            o_ref[h] = acc

    return pl.pallas_call(
        body, name="band_bias", out_shape=jax.ShapeDtypeStruct((N_HEADS, 2 * BLOCK, BLOCK), F32),
        in_specs=[pl.BlockSpec(memory_space=pltpu.SMEM), pl.BlockSpec(memory_space=pltpu.VMEM)],
        out_specs=pl.BlockSpec(memory_space=pltpu.VMEM))(rel_bias, bucket_t)


def rel_bias_grad(dbias):
    bucket, in_window = _t5_buckets()
    bucket_t = jnp.asarray(np.where(in_window, bucket, -1).astype(np.int32).T)

    def body(db_ref, bk_ref, o_ref):
        bk = bk_ref[...]
        rows = lax.broadcasted_iota(jnp.int32, (NUM_BUCKETS, LANES), 0)
        lanes = lax.broadcasted_iota(jnp.int32, (NUM_BUCKETS, LANES), 1)
        acc = jnp.zeros((NUM_BUCKETS, LANES), F32)
        for h in range(N_HEADS):
            dv = db_ref[h]
            for b in range(NUM_BUCKETS):
                sm = jnp.sum(jnp.where(bk == b, dv, 0.0))
                acc = jnp.where((rows == b) & (lanes == h), sm, acc)
        o_ref[...] = acc

    vm = pl.BlockSpec(memory_space=pltpu.VMEM)
    return pl.pallas_call(body, name="rel_bias_grad", out_shape=jax.ShapeDtypeStruct((NUM_BUCKETS, LANES), F32),
                          in_specs=[vm, vm], out_specs=vm)(dbias, bucket_t)


HP = 2
Q_PER_HP = D_ATTN // HP
H_PER_HP = N_HEADS // HP
NT_DIMS = (((1,), (1,)), ((), ()))
TN_DIMS = (((0,), (0,)), ((), ()))


def _stack_heads(ref, hh):
    lane = lax.broadcasted_iota(jnp.int32, (BLOCK, LANES), 1)
    lo = lane < HEAD_DIM
    parts = []
    for s_ in range(GROUP // 2):
        c0 = hh * (GROUP * HEAD_DIM) + s_ * LANES
        slab = ref[:, c0:c0 + LANES]
        parts.append(jnp.where(lo, slab, jnp.zeros_like(slab)))
        parts.append(jnp.where(lo, jnp.zeros_like(slab), slab))
    return jnp.concatenate(parts, axis=0)


def _attn_probs(hh, q_ref, kp_ref, kc_ref, vp_ref, vc_ref, bias_ref, sink_ref, hp, n):
    lane = lax.broadcasted_iota(jnp.int32, (2 * BLOCK, LANES), 1)
    own = (lane >= HEAD_DIM) if hh == 1 else (lane < HEAD_DIM)
    kband = jnp.concatenate([kp_ref[...], kc_ref[...]], axis=0)
    vband = jnp.concatenate([vp_ref[...], vc_ref[...]], axis=0)
    kk = jnp.where(own, kband, pltpu.roll(kband, HEAD_DIM, axis=1))
    vv = jnp.where(own, vband, pltpu.roll(vband, HEAD_DIM, axis=1))
    qs = _stack_heads(q_ref, hh)
    sc = lax.dot_general(kk, qs, NT_DIMS, preferred_element_type=F32) * (HEAD_DIM ** -0.5)
    sc = sc + jnp.concatenate([bias_ref[hh * GROUP + g] for g in range(GROUP)], axis=1)
    krow = lax.broadcasted_iota(jnp.int32, sc.shape, 0)
    sc = jnp.where((n == 0) & (krow < BLOCK), NEG_INF, sc)
    sink = jnp.concatenate([jnp.full((1, BLOCK), sink_ref[hp * H_PER_HP + hh * GROUP + g], F32) for g in range(GROUP)], axis=1)
    m = jnp.maximum(jnp.max(sc, axis=0, keepdims=True), sink)
    p = jnp.exp(sc - m)
    es = jnp.exp(sink - m)
    inv = 1.0 / (jnp.sum(p, axis=0, keepdims=True) + es)
    return qs, kk, vv, p * inv, es * inv


def _unstack(o, dtype):
    lane = lax.broadcasted_iota(jnp.int32, (BLOCK, LANES), 1)
    lo = lane < HEAD_DIM
    slabs = []
    for s_ in range(GROUP // 2):
        ev = o[(2 * s_) * BLOCK:(2 * s_ + 1) * BLOCK]
        od = o[(2 * s_ + 1) * BLOCK:(2 * s_ + 2) * BLOCK]
        slabs.append(jnp.where(lo, ev, od).astype(dtype))
    return slabs


def attention_forward(proj, bias, sinks, comm):
    s = proj.shape[0]
    nb = s // BLOCK
    kb, vb = K0 // LANES, V0 // LANES
    nc = len(comm)
    scat = [s_ for _, s_ in comm]

    def body(*refs):
        q_ref, kp_ref, kc_ref, vp_ref, vc_ref, bias_ref, sink_ref = refs[:7]
        srcs, o_ref, dsts, sems = refs[7:7 + nc], refs[7 + nc], refs[8 + nc:8 + 2 * nc], refs[8 + 2 * nc:]
        hp, n = pl.program_id(0), pl.program_id(1)

        step = hp * nb + n

        def run(phase):
            for act in _exchange_steps(phase, srcs, dsts, scat, *sems):
                act()

        @pl.when(step == 0)
        def _():
            run(0)

        @pl.when(step == HP * nb - HP * nb // 16)
        def _():
            run(1)

        for hh in range(2):
            qs, kk, vv, probs, _ = _attn_probs(hh, q_ref, kp_ref, kc_ref, vp_ref, vc_ref, bias_ref, sink_ref, hp, n)
            o = lax.dot_general(probs.astype(BF16), vv, TN_DIMS, preferred_element_type=F32)
            for s_, slab in enumerate(_unstack(o, BF16)):
                c0 = hh * (GROUP * HEAD_DIM) + s_ * LANES
                o_ref[:, c0:c0 + LANES] = slab

        @pl.when(step == HP * nb - 1)
        def _():
            run(2)

    qspec = pl.BlockSpec((BLOCK, Q_PER_HP), lambda hp, n: (n, hp))
    any_spec = pl.BlockSpec(memory_space=pl.ANY)

    def kv(base, prev):
        if prev:
            return pl.BlockSpec((BLOCK, LANES), lambda hp, n: (jnp.maximum(n - 1, 0), base + hp))
        return pl.BlockSpec((BLOCK, LANES), lambda hp, n: (n, base + hp))

    return pl.pallas_call(
        body, name="attention_forward", grid=(HP, nb),
        in_specs=[qspec, kv(kb, True), kv(kb, False), kv(vb, True), kv(vb, False),
                  pl.BlockSpec((H_PER_HP, 2 * BLOCK, BLOCK), lambda hp, n: (hp, 0, 0)),
                  pl.BlockSpec(memory_space=pltpu.SMEM)] + [any_spec] * nc,
        out_specs=(qspec,) + (any_spec,) * nc,
        out_shape=(jax.ShapeDtypeStruct((s, D_ATTN), BF16),) + tuple(_exchange_shapes(comm)),
        scratch_shapes=_exchange_sems(nc), compiler_params=_cparams(2),
    )(proj, proj, proj, proj, proj, bias, sinks, *[arr for arr, _ in comm])


def attention_backward(proj, datt, bias, sinks, comm):
    s = proj.shape[0]
    nb = s // BLOCK
    kb, vb = K0 // LANES, V0 // LANES
    nc = len(comm)
    scat = [s_ for _, s_ in comm]

    def body(*refs):
        q_ref, kp_ref, kc_ref, vp_ref, vc_ref, do_ref, bias_ref, sink_ref = refs[:8]
        srcs = refs[8:8 + nc]
        dq_ref, dk_ref, dv_ref, dbias_ref, dsink_ref = refs[8 + nc:13 + nc]
        dsts = refs[13 + nc:13 + 2 * nc]
        kcar_ref, vcar_ref, sacc_ref = refs[13 + 2 * nc:16 + 2 * nc]
        sems = refs[16 + 2 * nc:]
        hp, n = pl.program_id(0), pl.program_id(1)

        @pl.when((hp == 0) & (n == 0))
        def _():
            for cp in _exchange_copies(srcs, dsts, scat, *sems):
                cp.start()

        @pl.when(n == 0)
        def _():
            kcar_ref[...] = jnp.zeros_like(kcar_ref)
            vcar_ref[...] = jnp.zeros_like(vcar_ref)
            dbias_ref[...] = jnp.zeros_like(dbias_ref)
            sacc_ref[...] = jnp.zeros_like(sacc_ref)

        @pl.when(n < nb)
        def _():
            lane2 = lax.broadcasted_iota(jnp.int32, (2 * BLOCK, LANES), 1)
            dk_band = jnp.zeros((2 * BLOCK, LANES), F32)
            dv_band = jnp.zeros((2 * BLOCK, LANES), F32)
            for hh in range(2):
                qs, kk, vv, probs, psink = _attn_probs(hh, q_ref, kp_ref, kc_ref, vp_ref, vc_ref, bias_ref, sink_ref, hp, n)
                dos = _stack_heads(do_ref, hh)
                dp = lax.dot_general(vv, dos, NT_DIMS, preferred_element_type=F32)
                dsum = jnp.sum(probs * dp, axis=0, keepdims=True)
                ds = probs * (dp - dsum)
                for g in range(GROUP):
                    dbias_ref[hh * GROUP + g] += ds[:, g * BLOCK:(g + 1) * BLOCK]
                sacc_ref[hh:hh + 1, :] += -psink * dsum
                dsb = (ds * (HEAD_DIM ** -0.5)).astype(BF16)
                pb = probs.astype(BF16)
                dq = lax.dot_general(dsb, kk, TN_DIMS, preferred_element_type=F32)
                for s_, slab in enumerate(_unstack(dq, BF16)):
                    c0 = hh * (GROUP * HEAD_DIM) + s_ * LANES
                    dq_ref[:, c0:c0 + LANES] = slab
                dkh = jnp.dot(dsb, qs, preferred_element_type=F32)
                dvh = jnp.dot(pb, dos, preferred_element_type=F32)
                own = (lane2 >= HEAD_DIM) if hh == 1 else (lane2 < HEAD_DIM)
                dk_band = dk_band + jnp.where(own, dkh + pltpu.roll(dkh, HEAD_DIM, axis=1), 0.0)
                dv_band = dv_band + jnp.where(own, dvh + pltpu.roll(dvh, HEAD_DIM, axis=1), 0.0)
            dk_ref[...] = (kcar_ref[...] + dk_band[:BLOCK]).astype(BF16)
            dv_ref[...] = (vcar_ref[...] + dv_band[:BLOCK]).astype(BF16)
            kcar_ref[...] = dk_band[BLOCK:]
            vcar_ref[...] = dv_band[BLOCK:]

        @pl.when(n == nb)
        def _():
            dk_ref[...] = kcar_ref[...].astype(BF16)
            dv_ref[...] = vcar_ref[...].astype(BF16)
            rows = [jnp.full((1, LANES), jnp.sum(sacc_ref[hh:hh + 1, g * BLOCK:(g + 1) * BLOCK]), F32)
                    for hh in range(2) for g in range(GROUP)]
            dsink_ref[...] = jnp.concatenate(rows, axis=0)

        @pl.when((hp == HP - 1) & (n == nb))
        def _():
            for cp in _exchange_copies(srcs, dsts, scat, *sems):
                cp.wait()

    qspec = pl.BlockSpec((BLOCK, Q_PER_HP), lambda hp, n: (jnp.minimum(n, nb - 1), hp))
    any_spec = pl.BlockSpec(memory_space=pl.ANY)

    def kv(base, prev):
        if prev:
            return pl.BlockSpec((BLOCK, LANES), lambda hp, n: (jnp.maximum(jnp.minimum(n, nb - 1) - 1, 0), base + hp))
        return pl.BlockSpec((BLOCK, LANES), lambda hp, n: (jnp.minimum(n, nb - 1), base + hp))

    dkv_spec = pl.BlockSpec((BLOCK, LANES), lambda hp, n: (jnp.maximum(n - 1, 0), hp))
    return pl.pallas_call(
        body, name="attention_backward", grid=(HP, nb + 1),
        in_specs=[qspec, kv(kb, True), kv(kb, False), kv(vb, True), kv(vb, False), qspec,
                  pl.BlockSpec((H_PER_HP, 2 * BLOCK, BLOCK), lambda hp, n: (hp, 0, 0)),
                  pl.BlockSpec(memory_space=pltpu.SMEM)] + [any_spec] * nc,
        out_specs=(qspec, dkv_spec, dkv_spec,
                   pl.BlockSpec((H_PER_HP, 2 * BLOCK, BLOCK), lambda hp, n: (hp, 0, 0)),
                   pl.BlockSpec((H_PER_HP, LANES), lambda hp, n: (hp, 0))) + (any_spec,) * nc,
        out_shape=(jax.ShapeDtypeStruct((s, D_ATTN), BF16), jax.ShapeDtypeStruct((s, D_KV), BF16),
                   jax.ShapeDtypeStruct((s, D_KV), BF16), jax.ShapeDtypeStruct((N_HEADS, 2 * BLOCK, BLOCK), F32),
                   jax.ShapeDtypeStruct((N_HEADS, LANES), F32)) + tuple(_exchange_shapes(comm)),
        scratch_shapes=[pltpu.VMEM((BLOCK, LANES), F32), pltpu.VMEM((BLOCK, LANES), F32),
                        pltpu.VMEM((8, GROUP * BLOCK), F32)] + _exchange_sems(nc),
        compiler_params=_cparams(2),
    )(proj, proj, proj, proj, proj, datt, bias, sinks, *[arr for arr, _ in comm])


def _neg_expm1(x):
    series = -(x * (1.0 + x * (1.0 / 2 + x * (1.0 / 6 + x * (1.0 / 24 + x * (1.0 / 120 + x * (1.0 / 720)))))))
    return jnp.where(x > -0.25, series, 1.0 - jnp.exp(x))


def _softplus_neg(lam):
    u = jnp.exp(-jnp.abs(lam))
    w = 1.0 + u
    log1p = jnp.where(w == 1.0, u, jnp.log(w) * u / jnp.where(w == 1.0, 1.0, w - 1.0))
    sp = jnp.maximum(-lam, 0.0) + log1p
    return sp, -_sigmoid(-lam)


def _rnn_gates(x_ref, xp_ref, cw_ref, cb_ref, wa_ref, ba_ref, wi_ref, bi_ref, lam_ref, first, row0):
    xv = x_ref[...].astype(F32)
    pv = jnp.where(first, 0.0, xp_ref[...].astype(F32))
    xs = [_shift_down(pv, xv, RNN_CONV - 1 - k) for k in range(RNN_CONV)]
    xc = cb_ref[...]
    for k in range(RNN_CONV):
        xc = xc + cw_ref[k:k + 1, :] * xs[k]
    xcb = xc.astype(BF16)
    ra = _sigmoid(jnp.dot(xcb, wa_ref[...], preferred_element_type=F32) + ba_ref[...])
    ri = _sigmoid(jnp.dot(xcb, wi_ref[...], preferred_element_type=F32) + bi_ref[...])
    sp, dsp = _softplus_neg(lam_ref[...])
    la = (-RG_C) * ra * sp
    a = jnp.exp(la)
    t = row0 + lax.broadcasted_iota(jnp.int32, xv.shape, 0)
    start = t == 0
    mult = jnp.where(start, 1.0, jnp.sqrt(_neg_expm1(2.0 * la)))
    return xs, xc, xcb, ra, ri, sp, dsp, a, mult, start


def _rnn_specs(t_rows, s, rev):
    nt = s // t_rows
    r = t_rows // HALO
    xb, gb = XR0 // LANES, GR0 // LANES
    ti = (lambda i: nt - 1 - i) if rev else (lambda i: i)
    tile = lambda base: pl.BlockSpec((t_rows, LANES), lambda n, i: (ti(i), base + n))
    prev = lambda base: pl.BlockSpec((HALO, LANES), lambda n, i: (jnp.maximum(ti(i) * r - 1, 0), base + n))
    par = lambda rows: pl.BlockSpec((rows, LANES), lambda n, i: (0, n))
    mat = pl.BlockSpec((None, RNN_BLOCK, RNN_BLOCK), lambda n, i: (n, 0, 0))
    return nt, ti, tile, prev, par, mat, xb, gb


def rnn_forward(proj, cw, cb, wa, ba, wi, bi, lam):
    s = proj.shape[0]
    t_rows = _row_tile(s, 512)
    nt, ti, tile, prev, par, mat, xb, gb = _rnn_specs(t_rows, s, False)

    def body(x_ref, xp_ref, g_ref, cw_ref, cb_ref, wa_ref, ba_ref, wi_ref, bi_ref, lam_ref, z_ref, h_ref, car_ref):
        i = pl.program_id(1)
        first = i == 0
        _, xc, _, _, ri, _, _, a, mult, _ = _rnn_gates(x_ref, xp_ref, cw_ref, cb_ref, wa_ref, ba_ref, wi_ref, bi_ref,
                                                       lam_ref, first, i * t_rows)
        aa, bb = a, mult * ri * xc
        rows = lax.broadcasted_iota(jnp.int32, aa.shape, 0)
        d = 1
        while d < t_rows:
            keep = rows >= d
            a_s, b_s = pltpu.roll(aa, d, axis=0), pltpu.roll(bb, d, axis=0)
            bb = jnp.where(keep, aa * b_s + bb, bb)
            aa = jnp.where(keep, aa * a_s, aa)
            d *= 2
        carry = jnp.where(first, 0.0, car_ref[0:1, :])
        h = aa * carry + bb
        car_ref[...] = jnp.broadcast_to(h[t_rows - 1:t_rows, :], car_ref.shape)
        h_ref[...] = h.astype(BF16)
        z_ref[...] = (h * _gelu(g_ref[...].astype(F32))).astype(BF16)

    o = jax.ShapeDtypeStruct((s, D_RNN), BF16)
    out_tile = pl.BlockSpec((t_rows, LANES), lambda n, i: (i, n))
    return pl.pallas_call(
        body, name="rnn_forward", grid=(N_RNN_BLOCKS, nt),
        in_specs=[tile(xb), prev(xb), tile(gb), par(RNN_CONV), par(1), mat, par(1), mat, par(1), par(1)],
        out_specs=(out_tile, out_tile), out_shape=(o, o), scratch_shapes=[pltpu.VMEM((8, LANES), F32)],
        compiler_params=_cparams(2),
    )(proj, proj, proj, cw, cb, wa, ba, wi, bi, lam)


def rnn_backward(proj, h, dz, cw, cb, wa, ba, wi, bi, lam, comm):
    s = proj.shape[0]
    t_rows = _row_tile(s, 512)
    nt, ti, tile, prev, par, mat, xb, gb = _rnn_specs(t_rows, s, True)
    r = t_rows // HALO
    nc = len(comm)
    scat = [s_ for _, s_ in comm]

    def body(*refs):
        (x_ref, xp_ref, g_ref, h_ref, hp_ref, dz_ref, cw_ref, cb_ref, wa_ref, ba_ref, wi_ref, bi_ref,
         lam_ref) = refs[:13]
        srcs = refs[13:13 + nc]
        dx_ref, dg_ref, dwa_ref, dwi_ref, sm_ref = refs[13 + nc:18 + nc]
        dsts = refs[18 + nc:18 + 2 * nc]
        gcar_ref, xcar_ref = refs[18 + 2 * nc:20 + 2 * nc]
        sems = refs[20 + 2 * nc:]
        i = pl.program_id(1)
        step = pl.program_id(0) * nt + i

        def run(phase):
            for act in _exchange_steps(phase, srcs, dsts, scat, *sems):
                act()

        @pl.when(step == 0)
        def _():
            run(0)

        @pl.when(step == (RELAY_AT_NUM * N_RNN_BLOCKS * nt) // RELAY_AT_DEN)
        def _():
            run(1)

        it = nt - 1 - i
        first, last = it == 0, it == nt - 1
        xs, xc, xcb, ra, ri, sp, dsp, a, mult, start = _rnn_gates(
            x_ref, xp_ref, cw_ref, cb_ref, wa_ref, ba_ref, wi_ref, bi_ref, lam_ref, first, it * t_rows)
        hf = h_ref[...].astype(F32)
        hprev = _shift_down(jnp.where(first, 0.0, hp_ref[...].astype(F32)), hf, 1)
        ge, dge = _gelu_and_grad(g_ref[...].astype(F32))
        dz = dz_ref[...].astype(F32)
        dg_ref[...] = (dz * hf * dge).astype(BF16)
        rows = lax.broadcasted_iota(jnp.int32, hf.shape, 0)
        tail = rows == t_rows - 1
        carry = jnp.where(last, 0.0, gcar_ref[0:1, :])
        bb = dz * ge + jnp.where(tail, carry, 0.0)
        aa = jnp.where(tail, 0.0, pltpu.roll(a, t_rows - 1, axis=0))
        d = 1
        while d < t_rows:
            keep = rows < t_rows - d
            a_s, b_s = pltpu.roll(aa, t_rows - d, axis=0), pltpu.roll(bb, t_rows - d, axis=0)
            bb = jnp.where(keep, bb + aa * b_s, bb)
            aa = jnp.where(keep, aa * a_s, aa)
            d *= 2
        gg = bb
        gcar_ref[...] = jnp.broadcast_to(a[0:1, :] * gg[0:1, :], gcar_ref.shape)
        da = gg * hprev
        dmult = jnp.where(start, 0.0, gg * ri * xc)
        dri = gg * mult * xc
        dxc = gg * mult * ri
        safe_mult = jnp.where(start, 1.0, mult)
        dla = da * a - dmult * (a * a) / safe_mult
        dra = dla * ((-RG_C) * sp)
        dlam = jnp.sum(dla * ((-RG_C) * ra), axis=0, keepdims=True) * dsp
        dpa = dra * ra * (1.0 - ra)
        dpi = dri * ri * (1.0 - ri)
        dpab, dpib = dpa.astype(BF16), dpi.astype(BF16)
        nt_dims = (((1,), (1,)), ((), ()))
        tn_dims = (((0,), (0,)), ((), ()))
        dxc = dxc + lax.dot_general(dpab, wa_ref[...], nt_dims, preferred_element_type=F32) \
            + lax.dot_general(dpib, wi_ref[...], nt_dims, preferred_element_type=F32)
        dwa = lax.dot_general(xcb, dpab, tn_dims, preferred_element_type=F32)
        dwi = lax.dot_general(xcb, dpib, tn_dims, preferred_element_type=F32)
        nxt = jnp.where(last, 0.0, xcar_ref[...])
        dx = cw_ref[RNN_CONV - 1:RNN_CONV, :] * dxc
        for k in range(RNN_CONV - 1):
            dx = dx + cw_ref[k:k + 1, :] * _shift_up(dxc, nxt, RNN_CONV - 1 - k)
        dx_ref[...] = dx.astype(BF16)
        xcar_ref[...] = dxc[0:HALO, :]
        small = jnp.concatenate(
            [jnp.sum(dpa, axis=0, keepdims=True), jnp.sum(dpi, axis=0, keepdims=True), dlam,
             jnp.sum(dxc, axis=0, keepdims=True)]
            + [jnp.sum(dxc * xs[k], axis=0, keepdims=True) for k in range(RNN_CONV)], axis=0)

        @pl.when(i == 0)
        def _():
            dwa_ref[...] = dwa
            dwi_ref[...] = dwi
            sm_ref[...] = small

        @pl.when(i > 0)
        def _():
            dwa_ref[...] += dwa
            dwi_ref[...] += dwi
            sm_ref[...] += small

        @pl.when(step == N_RNN_BLOCKS * nt - 1)
        def _():
            run(2)

    o = jax.ShapeDtypeStruct((s, D_RNN), BF16)
    any_spec = pl.BlockSpec(memory_space=pl.ANY)
    plain = pl.BlockSpec((t_rows, LANES), lambda n, i: (ti(i), n))
    plain_prev = pl.BlockSpec((HALO, LANES), lambda n, i: (jnp.maximum(ti(i) * r - 1, 0), n))
    return pl.pallas_call(
        body, name="rnn_backward", grid=(N_RNN_BLOCKS, nt),
        in_specs=[tile(xb), prev(xb), tile(gb), plain, plain_prev, plain,
                  par(RNN_CONV), par(1), mat, par(1), mat, par(1), par(1)] + [any_spec] * nc,
        out_specs=(plain, plain, mat, mat, pl.BlockSpec((None, 8, LANES), lambda n, i: (n, 0, 0))) + (any_spec,) * nc,
        out_shape=(o, o, jax.ShapeDtypeStruct((N_RNN_BLOCKS, RNN_BLOCK, RNN_BLOCK), F32),
                   jax.ShapeDtypeStruct((N_RNN_BLOCKS, RNN_BLOCK, RNN_BLOCK), F32),
                   jax.ShapeDtypeStruct((N_RNN_BLOCKS, 8, LANES), F32)) + tuple(_exchange_shapes(comm)),
        scratch_shapes=[pltpu.VMEM((8, LANES), F32), pltpu.VMEM((HALO, LANES), F32)] + _exchange_sems(nc),
        compiler_params=_cparams(2),
    )(proj, proj, proj, h, h, dz, cw, cb, wa, ba, wi, bi, lam, *[arr for arr, _ in comm])


def adamw(parts, w, m, v, name, rows=256):
    p, r, c = parts.shape
    tr = min(rows, r)
    assert r % tr == 0

    def body(p_ref, w_ref, m_ref, v_ref, g_ref, d_ref, nm_ref, nv_ref):
        g = p_ref[0].astype(F32)
        for q in range(1, p):
            g = g + p_ref[q].astype(F32)
        nm = ADAM_B1 * m_ref[...] + (1.0 - ADAM_B1) * g
        nv = ADAM_B2 * v_ref[...] + (1.0 - ADAM_B2) * (g * g)
        mh = nm / (1.0 - ADAM_B1 ** ADAM_STEP)
        vh = nv / (1.0 - ADAM_B2 ** ADAM_STEP)
        g_ref[...] = g
        d_ref[...] = (-ADAM_LR) * (mh / (jnp.sqrt(vh) + ADAM_EPS) + ADAM_WD * w_ref[...])
        nm_ref[...] = nm
        nv_ref[...] = nv

    pspec = pl.BlockSpec((p, tr, c), lambda i: (0, i, 0))
    spec = pl.BlockSpec((tr, c), lambda i: (i, 0))
    o = jax.ShapeDtypeStruct((r, c), F32)
    return pl.pallas_call(body, name=name, grid=(r // tr,), in_specs=[pspec, spec, spec, spec],
                          out_specs=(spec,) * 4, out_shape=(o, o, o, o), compiler_params=_cparams(1))(parts, w, m, v)


def ada_weight_grad(c_t, dmod):
    d, nb = c_t.shape
    c = dmod.shape[1]
    tr = 512

    def body(c_ref, dm_ref, o_ref):
        cv = c_ref[...]
        cs = cv * _sigmoid(cv)
        acc = cs[:, 0:1] * dm_ref[0:1, :]
        for b in range(1, nb):
            acc = acc + cs[:, b:b + 1] * dm_ref[b:b + 1, :]
        o_ref[...] = acc

    return pl.pallas_call(body, name="ada_weight_grad", grid=(d // tr,),
                          in_specs=[pl.BlockSpec((tr, nb), lambda i: (i, 0)), pl.BlockSpec((nb, c), lambda i: (0, 0))],
                          out_specs=pl.BlockSpec((tr, c), lambda i: (i, 0)),
                          out_shape=jax.ShapeDtypeStruct((d, c), F32), compiler_params=_cparams(1))(c_t, dmod)


def _rows128(a):
    flat = a.reshape(-1).astype(F32)
    pad = (-flat.shape[0]) % (8 * LANES)
    if pad:
        flat = jnp.concatenate([flat, jnp.zeros((pad,), F32)])
    return flat.reshape(-1, LANES)


def kernel(x, c, w_ada, b_ada, norm1, w_in, rnn_conv_w, rnn_conv_b, w_rg_a, b_rg_a, w_rg_i, b_rg_i, rg_lambda, w_o_rnn, w_o_attn, attn_sinks, rel_bias, w_out, norm2, w_up, ffn_conv_w, ffn_conv_b, w_down, norm_f, loss_target, m_w_ada, m_b_ada, m_norm1, m_w_in, m_rnn_conv_w, m_rnn_conv_b, m_w_rg_a, m_b_rg_a, m_w_rg_i, m_b_rg_i, m_rg_lambda, m_w_o_rnn, m_w_o_attn, m_attn_sinks, m_rel_bias, m_w_out, m_norm2, m_w_up, m_ffn_conv_w, m_ffn_conv_b, m_w_down, m_norm_f, v_w_ada, v_b_ada, v_norm1, v_w_in, v_rnn_conv_w, v_rnn_conv_b, v_w_rg_a, v_b_rg_a, v_w_rg_i, v_b_rg_i, v_rg_lambda, v_w_o_rnn, v_w_o_attn, v_attn_sinks, v_rel_bias, v_w_out, v_norm2, v_w_up, v_ffn_conv_w, v_ffn_conv_b, v_w_down, v_norm_f):
    me = 4 * lax.axis_index("x") + 2 * lax.axis_index("y") + lax.axis_index("c")
    xs = x[0]
    tgt = loss_target[0]
    s, d = xs.shape
    bf = lambda w: w[0].astype(BF16)

    mod, c_all = mod_forward(c.reshape(1, 1, d), w_ada[0], b_ada)
    mod = mod.reshape(6, d)
    shift1, scale1, gate1, shift2, scale2, gate2 = [mod[i:i + 1] for i in range(6)]

    g_in, g_rcw, g_fcw = exchange([(bf(w_in), False), (rnn_conv_w[0], False), (ffn_conv_w[0], False)], "gather_w_in")
    w_in_f = jnp.transpose(g_in, (1, 0, 2)).reshape(d, D_IN)
    rcw = jnp.transpose(g_rcw, (1, 0, 2)).reshape(RNN_CONV, D_RNN)
    fcw = jnp.transpose(g_fcw, (1, 0, 2)).reshape(FFN_CONV, 2 * D_FF)
    wa_b, wi_b = bf(w_rg_a), bf(w_rg_i)
    sinks = attn_sinks[0]

    u = prenorm(xs, norm1, scale1, shift1, "prenorm1")
    proj, g_oa, g_or, g_out, g_down = matmul(
        u, w_in_f, "nn", BF16, "mm_in", 2048, 512, 2048,
        comm=[(bf(w_o_attn), False), (bf(w_o_rnn), False), (bf(w_out), False), (bf(w_down), False)])
    w_oa_f, w_or_f = g_oa.reshape(D_ATTN, d), g_or.reshape(D_RNN, d)
    w_out_f, w_down_f = g_out.reshape(d, d), g_down.reshape(D_FF, d)
    bias = band_bias(rel_bias)
    att, g_up = attention_forward(proj, bias, sinks, [(bf(w_up), False)])
    w_up_f = jnp.transpose(g_up, (1, 0, 2)).reshape(d, 2 * D_FF)
    z, hr = rnn_forward(proj, rcw, rnn_conv_b, wa_b, b_rg_a, wi_b, b_rg_i, rg_lambda)
    y_attn = matmul(att, w_oa_f, "nn", BF16, "mm_o_attn", 1024, 1024, 2048)
    y_rnn = matmul(z, w_or_f, "nn", BF16, "mm_o_rnn", 1024, 1024, 2560)
    merged = merge_forward(proj, y_attn, y_rnn)
    h1, mo = matmul(merged, w_out_f, "nn", BF16, "mm_out", 512, 1024, 2048, res=xs, gate=gate1)
    u2 = prenorm(h1, norm2, scale2, shift2, "prenorm2")
    upp = matmul(u2, w_up_f, "nn", BF16, "mm_up", 1024, 1024, 2048)
    act, up_g, up_v = ffn_act_forward(upp, fcw, ffn_conv_b)
    h2, dn = matmul(act, w_down_f, "nn", BF16, "mm_down", 512, 1024, 6144, res=h1, gate=gate2)

    dh2, d_dn, loss_cols, d_norm_f, d_gate2 = loss_head(h2, tgt, dn, norm_f.reshape(1, d), gate2)
    loss = lax.psum(jnp.sum(loss_cols), ("x", "y", "c"))

    d_act = matmul(d_dn, w_down_f, "nt", BF16, "mm_down_dx", 1024, 1536, 2048)
    g_w_down = matmul(act, d_dn, "tn", BF16, "mm_down_dw", 768, 512, 8192)
    d_g, d_v, d_fcb_g, d_fcb_v = ffn_act_backward(d_act, up_g, up_v)
    d_upp, d_fcw = conv_backward(d_g, d_v, upp, fcw)
    d_fcb = jnp.concatenate([d_fcb_g, d_fcb_v], axis=1)
    d_u2, p_down = matmul(d_upp, w_up_f, "nt", BF16, "mm_up_dx", 1024, 1024, 3072,
                          comm=[(g_w_down.reshape(N_DEV, D_FF // N_DEV, d), True)])
    g_w_up = matmul(u2, d_upp, "tn", BF16, "mm_up_dw", 1024, 512, 8192)
    g_up_blk = jnp.transpose(g_w_up.reshape(d, N_DEV, 2 * D_FF // N_DEV), (1, 0, 2))
    dh1, d_shift2, d_scale2, d_norm2, d_mo, d_gate1 = norm_backward(d_u2, h1, dh2, norm2, scale2, "norm2_backward",
                                                                    mo=mo, gate=gate1)
    d_merged = matmul(d_mo, w_out_f, "nt", BF16, "mm_out_dx", 1024, 1024, 2048)
    g_w_out = matmul(merged, d_mo, "tn", BF16, "mm_out_dw", 1024, 512, 8192)
    d_ga, d_gl, d_ya, d_yr = merge_backward(d_merged, proj, y_attn, y_rnn)
    d_att = matmul(d_ya, w_oa_f, "nt", BF16, "mm_o_attn_dx", 1024, 1024, 2048)
    g_w_oa = matmul(att, d_ya, "tn", BF16, "mm_o_attn_dw", 1024, 512, 8192)
    d_z = matmul(d_yr, w_or_f, "nt", BF16, "mm_o_rnn_dx", 1024, 1280, 2048)
    g_w_or = matmul(z, d_yr, "tn", BF16, "mm_o_rnn_dw", 1280, 1024, 2048)
    d_xr, d_gr, d_wa, d_wi, d_rsmall, p_up = rnn_backward(proj, hr, d_z, rcw, rnn_conv_b, wa_b, b_rg_a, wi_b, b_rg_i,
                                                          rg_lambda, [(g_up_blk, True)])
    d_q, d_k, d_v_, d_bias, d_sink, p_out, p_oa, p_or = attention_backward(
        proj, d_att, bias, sinks,
        [(g_w_out.reshape(N_DEV, d // N_DEV, d), True), (g_w_oa.reshape(N_DEV, D_ATTN // N_DEV, d), True),
         (g_w_or.reshape(N_DEV, D_RNN // N_DEV, d), True)])
    d_rel = rel_bias_grad(d_bias)
    d_proj = jnp.concatenate([d_q, d_k, d_v_, d_xr, d_gr, d_ga, d_gl], axis=1)

    def rsmall_of(ba_, bi_, lam_, cb_):
        return jnp.stack([ba_[0].reshape(N_RNN_BLOCKS, LANES), bi_[0].reshape(N_RNN_BLOCKS, LANES),
                          lam_[0].reshape(N_RNN_BLOCKS, LANES), cb_[0].reshape(N_RNN_BLOCKS, LANES)]
                         + [jnp.zeros((N_RNN_BLOCKS, LANES), F32)] * 4, axis=1)

    pack = lambda t: jnp.concatenate([_rows128(q) for q in t], axis=0)
    late_n = 2
    g_early = [d_norm2, d_norm_f, d_rsmall, d_wa, d_wi, d_sink[:, 0], d_rel[:, :N_HEADS], d_fcb, d_fcw]

    g_w_in, small_early = matmul(u, d_proj, "tn", BF16, "mm_in_dw", 1024, 512, 8192, comm=[(pack(g_early), False)])
    g_in_blk = jnp.transpose(g_w_in.reshape(d, N_DEV, D_IN // N_DEV), (1, 0, 2))
    d_u, p_in = matmul(d_proj, w_in_f, "nt", BF16, "mm_in_dx", 1024, 1024, 2944, comm=[(g_in_blk, True)])
    grad_x, d_shift1, d_scale1, d_norm1 = norm_backward(d_u, xs, dh1, norm1, scale1, "norm1_backward")
    d_mod = jnp.concatenate([d_shift1, d_scale1, d_gate1, d_shift2, d_scale2, d_gate2], axis=1)
    small_late = exchange([(pack([d_mod, d_norm1]), False)], "gather_late_grads")[0]

    g_list = [d_mod, d_norm1] + g_early
    w_list = [b_ada, norm1, norm2, norm_f, rsmall_of(b_rg_a, b_rg_i, rg_lambda, rnn_conv_b), w_rg_a, w_rg_i,
              attn_sinks, rel_bias, ffn_conv_b, jnp.zeros_like(d_fcw)]
    m_list = [m_b_ada, m_norm1, m_norm2, m_norm_f, rsmall_of(m_b_rg_a, m_b_rg_i, m_rg_lambda, m_rnn_conv_b), m_w_rg_a,
              m_w_rg_i, m_attn_sinks, m_rel_bias, m_ffn_conv_b, jnp.zeros_like(d_fcw)]
    v_list = [v_b_ada, v_norm1, v_norm2, v_norm_f, rsmall_of(v_b_rg_a, v_b_rg_i, v_rg_lambda, v_rnn_conv_b), v_w_rg_a,
              v_w_rg_i, v_attn_sinks, v_rel_bias, v_ffn_conv_b, jnp.ones_like(d_fcw)]
    sizes = [_rows128(q).shape[0] for q in g_list]
    offs = np.concatenate([[0], np.cumsum(sizes)]).tolist()
    r_late = offs[late_n]
    late = adamw(small_late, pack(w_list[:late_n]), pack(m_list[:late_n]), pack(v_list[:late_n]), "adamw_small_late",
                 rows=r_late)
    early = adamw(small_early, pack(w_list[late_n:]), pack(m_list[late_n:]), pack(v_list[late_n:]), "adamw_small_early",
                  rows=(offs[-1] - r_late) // 7)

    def seg(packed, idx, like):
        n_el = int(np.prod(like.shape))
        return packed[offs[idx]:offs[idx + 1]].reshape(-1)[:n_el].reshape(like.shape)

    def unpack(kind):
        packed = jnp.concatenate([late[kind], early[kind]], axis=0)
        rs = seg(packed, 4, d_rsmall)
        out = dict(
            b_ada=seg(packed, 0, b_ada), norm1=seg(packed, 1, norm1), norm2=seg(packed, 2, norm2),
            norm_f=seg(packed, 3, norm_f), b_rg_a=rs[:, 0].reshape(1, D_RNN), b_rg_i=rs[:, 1].reshape(1, D_RNN),
            rg_lambda=rs[:, 2].reshape(1, D_RNN), rnn_conv_b=rs[:, 3].reshape(1, D_RNN),
            w_rg_a=seg(packed, 5, w_rg_a), w_rg_i=seg(packed, 6, w_rg_i), attn_sinks=seg(packed, 7, attn_sinks),
            rel_bias=seg(packed, 8, rel_bias), ffn_conv_b=seg(packed, 9, ffn_conv_b))
        out["rnn_conv_w_full"] = jnp.transpose(rs[:, 4:8], (1, 0, 2)).reshape(RNN_CONV, D_RNN)
        out["ffn_conv_w_full"] = seg(packed, 10, d_fcw)
        return out

    small = [unpack(kind) for kind in range(4)]

    rcw_cols = D_RNN // N_DEV
    fcw_cols = 2 * D_FF // N_DEV
    g_rcw_ = lax.dynamic_slice(small[0]["rnn_conv_w_full"], (0, me * rcw_cols), (RNN_CONV, rcw_cols))
    g_fcw_ = lax.dynamic_slice(small[0]["ffn_conv_w_full"], (0, me * fcw_cols), (FFN_CONV, fcw_cols))
    r_rcw = adamw(g_rcw_[None], rnn_conv_w[0], m_rnn_conv_w[0], v_rnn_conv_w[0], "adamw_rnn_conv_w")
    r_fcw = adamw(g_fcw_[None], ffn_conv_w[0], m_ffn_conv_w[0], v_ffn_conv_w[0], "adamw_ffn_conv_w")

    ada_cols = 6 * d // N_DEV
    dmod_all = small_late[:, offs[0]:offs[1]].reshape(N_DEV, 6 * d)
    dmod_cols = lax.dynamic_slice(dmod_all, (0, me * ada_cols), (N_DEV, ada_cols))
    g_ada = ada_weight_grad(jnp.transpose(c_all.reshape(N_DEV, d)), dmod_cols)
    r_ada = adamw(g_ada[None], w_ada[0], m_w_ada[0], v_w_ada[0], "adamw_w_ada")

    r_in = adamw(p_in, w_in[0], m_w_in[0], v_w_in[0], "adamw_w_in")
    r_up = adamw(p_up, w_up[0], m_w_up[0], v_w_up[0], "adamw_w_up")
    r_or = adamw(p_or, w_o_rnn[0], m_w_o_rnn[0], v_w_o_rnn[0], "adamw_w_o_rnn", rows=160)
    r_oa = adamw(p_oa, w_o_attn[0], m_w_o_attn[0], v_w_o_attn[0], "adamw_w_o_attn")
    r_out = adamw(p_out, w_out[0], m_w_out[0], v_w_out[0], "adamw_w_out")
    r_down = adamw(p_down, w_down[0], m_w_down[0], v_w_down[0], "adamw_w_down")

    def res(kind):
        sm = small[kind]
        return [r_ada[kind][None], sm["b_ada"], sm["norm1"], r_in[kind][None], r_rcw[kind][None], sm["rnn_conv_b"],
                sm["w_rg_a"], sm["b_rg_a"], sm["w_rg_i"], sm["b_rg_i"], sm["rg_lambda"], r_or[kind][None],
                r_oa[kind][None], sm["attn_sinks"], sm["rel_bias"], r_out[kind][None], sm["norm2"], r_up[kind][None],
                r_fcw[kind][None], sm["ffn_conv_b"], r_down[kind][None], sm["norm_f"]]

    return (loss, grad_x[None], *res(0), *res(1), *res(2), *res(3))
```
